```python
import math
import jax
import jax.numpy as jnp
from jax import lax
import numpy as np


D_MODEL = 1024
BATCH = 2
SEQ = 8192
DEPTH = 2

GRID_W = 64
CTX_LEN = 256
N_MIXERS = 2
RMS_EPS = 1e-6

S5_GROUP = 16
S5_GROUPS = D_MODEL // S5_GROUP
S5_STATE = 64
S5_DT_MIN = 1e-3
S5_DT_MAX = 1e-1

SSD_EXPAND = 2
SSD_D_INNER = SSD_EXPAND * D_MODEL
SSD_HEAD_DIM = 64
SSD_HEADS = SSD_D_INNER // SSD_HEAD_DIM
SSD_GROUPS = 8
SSD_STATE = 128
SSD_CONV = 5
SSD_CHUNK = 128
SSD_CONV_CH = SSD_D_INNER + 2 * SSD_GROUPS * SSD_STATE
SSD_PROJ = SSD_D_INNER + SSD_CONV_CH + 2 * SSD_HEADS
SSD_DT_MIN = 1e-3
SSD_DT_MAX = 1e-1

N_EXPERTS = 16
N_EXPERT_GROUPS = 4
EXPERTS_PER_GROUP = N_EXPERTS // N_EXPERT_GROUPS
TOP_K = 2
D_EXPERT = 512

F32 = jnp.float32

kernel_name = 'hybrid_s5_ssd_moe_prefix_dit'


def rmsnorm(x, g):
    xf = x.astype(F32)
    y = xf * lax.rsqrt(jnp.mean(xf * xf, axis=-1, keepdims=True) + RMS_EPS)
    return y.astype(x.dtype) * g


def adaln(xn, shift, scale):
    return xn * (1.0 + scale) + shift


def _cmul(ar, ai, br, bi):
    return ar * br - ai * bi, ar * bi + ai * br


def _s5_combine(e1, e2):
    a1r, a1i, b1r, b1i = e1
    a2r, a2i, b2r, b2i = e2
    ar, ai = _cmul(a2r, a2i, a1r, a1i)
    br, bi = _cmul(a2r, a2i, b1r, b1i)
    return ar, ai, br + b2r, bi + b2i


def s5_scan(u, lam_re, lam_im, log_dt, b_re, b_im, c_re, c_im, h0_re, h0_im, with_y):
    bsz, length, d = u.shape
    uf = u.astype(F32).reshape(bsz, length, S5_GROUPS, S5_GROUP)
    lr = lam_re.astype(F32)
    li = lam_im.astype(F32)
    step = jnp.exp(log_dt.astype(F32))[:, None]
    mag = jnp.exp(lr * step)
    abar_r = mag * jnp.cos(li * step)
    abar_i = mag * jnp.sin(li * step)
    den = lr * lr + li * li
    q_r = ((abar_r - 1.0) * lr + abar_i * li) / den
    q_i = (abar_i * lr - (abar_r - 1.0) * li) / den
    bbar_r, bbar_i = _cmul(q_r[..., None], q_i[..., None], b_re.astype(F32), b_im.astype(F32))
    bu_r = jnp.einsum('blgk,gpk->lbgp', uf, bbar_r)
    bu_i = jnp.einsum('blgk,gpk->lbgp', uf, bbar_i)
    i_r, i_i = _cmul(abar_r, abar_i, h0_re.astype(F32), h0_im.astype(F32))
    bu_r = bu_r.at[0].add(i_r)
    bu_i = bu_i.at[0].add(i_i)
    a_r = jnp.broadcast_to(abar_r, (length, 1) + abar_r.shape)
    a_i = jnp.broadcast_to(abar_i, (length, 1) + abar_i.shape)
    _, _, h_r, h_i = lax.associative_scan(_s5_combine, (a_r, a_i, bu_r, bu_i), axis=0)
    final = (h_r[-1], h_i[-1])
    if not with_y:
        return None, final
    y = (jnp.einsum('lbgp,gkp->blgk', h_r, c_re.astype(F32))
         - jnp.einsum('lbgp,gkp->blgk', h_i, c_im.astype(F32)))
    return y.reshape(bsz, length, d).astype(u.dtype), final


def s5_mixer(hn, cn, lam_re, lam_im, log_dt, b_re, b_im, c_re, c_im, d_skip, glu_w, glu_b, need_ctx):
    bsz, _, d = hn.shape
    zero = jnp.zeros((bsz, S5_GROUPS, S5_STATE), F32)

    def run(u, k, h0, with_y):
        return s5_scan(u, lam_re[k], lam_im[k], log_dt[k], b_re, b_im, c_re, c_im, h0[0], h0[1], with_y)

    ycf, hf = run(cn, 0, (zero, zero), need_ctx)
    ycb, hb = run(cn[:, ::-1], 1, (zero, zero), need_ctx)
    ylf, _ = run(hn, 0, hf, True)
    ylb, _ = run(hn[:, ::-1], 1, hb, True)

    def head(yf, yb_rev, u):
        g = jax.nn.gelu(yf + yb_rev[:, ::-1] + d_skip * u)
        z = g @ glu_w + glu_b
        return z[..., :d] * jax.nn.sigmoid(z[..., d:])

    out_lat = head(ylf, ylb, hn)
    out_ctx = head(ycf, ycb, cn) if need_ctx else None
    return out_lat, out_ctx


def dwconv(u, w, b):
    out = lax.conv_general_dilated(
        u, w[:, None, :].astype(u.dtype), (1,), [(SSD_CONV // 2, SSD_CONV // 2)],
        dimension_numbers=('NWC', 'WIO', 'NWC'), feature_group_count=u.shape[-1])
    return out + b


def segsum(v):
    t = v.shape[-1]
    cs = jnp.cumsum(v, axis=-1)
    diff = cs[..., :, None] - cs[..., None, :]
    mask = jnp.tril(jnp.ones((t, t), dtype=bool))
    return jnp.where(mask, diff, -jnp.inf)


def ssd_chunked(xdt, da, bm, cm, h0, with_y):
    bsz, length, nh, p = xdt.shape
    g, n = bm.shape[2], bm.shape[3]
    r = nh // g
    nc = length // SSD_CHUNK
    xc = xdt.reshape(bsz, nc, SSD_CHUNK, g, r, p)
    a = da.reshape(bsz, nc, SSD_CHUNK, g, r).transpose(0, 3, 4, 1, 2)
    bc = bm.reshape(bsz, nc, SSD_CHUNK, g, n)
    cc = cm.reshape(bsz, nc, SSD_CHUNK, g, n)
    a_cs = jnp.cumsum(a, axis=-1)
    decay_states = jnp.exp(a_cs[..., -1:] - a_cs)
    states = jnp.einsum('bcsgn,bgrcs,bcsgrp->bcgrpn', bc, decay_states, xc)
    states = jnp.concatenate([h0.reshape(bsz, 1, g, r, p, n), states], axis=1)
    chunk_tot = jnp.pad(a_cs[..., -1], ((0, 0), (0, 0), (0, 0), (1, 0)))
    decay_chunk = jnp.exp(segsum(chunk_tot))
    new_states = jnp.einsum('bgrzc,bcgrpn->bzgrpn', decay_chunk, states)
    final = new_states[:, -1].reshape(bsz, nh, p, n)
    if not with_y:
        return None, final
    prev = new_states[:, :-1]
    lmat = jnp.exp(segsum(a))
    cb = jnp.einsum('bcqgn,bcsgn->bcgqs', cc, bc)
    y_diag = jnp.einsum('bcgqs,bgrcqs,bcsgrp->bcqgrp', cb, lmat, xc)
    y_off = jnp.einsum('bcqgn,bcgrpn,bgrcq->bcqgrp', cc, prev, jnp.exp(a_cs))
    return (y_diag + y_off).reshape(bsz, length, nh, p), final


def ssd_mixer(hn, cn, in_w, conv_w, conv_b, dt_bias, a_log, d_skip, norm_g, out_w, need_ctx):
    bsz, seq, d = hn.shape
    rows = seq // GRID_W
    a = -jnp.exp(a_log.astype(F32))

    def project(t):
        length = t.shape[1]
        zxbcdt = t @ in_w
        z, xbc, dt_raw = jnp.split(zxbcdt, [SSD_D_INNER, SSD_D_INNER + SSD_CONV_CH], axis=-1)
        xbc = jax.nn.silu(dwconv(xbc, conv_w, conv_b))
        xs, bm, cm = jnp.split(xbc, [SSD_D_INNER, SSD_D_INNER + SSD_GROUPS * SSD_STATE], axis=-1)
        dt = jax.nn.softplus(dt_raw.astype(F32).reshape(bsz, length, 2, SSD_HEADS) + dt_bias.astype(F32))
        return (z, xs.reshape(bsz, length, SSD_HEADS, SSD_HEAD_DIM),
                bm.reshape(bsz, length, SSD_GROUPS, SSD_STATE),
                cm.reshape(bsz, length, SSD_GROUPS, SSD_STATE), dt)

    def scan(xs, bm, cm, dt, k, h0, with_y):
        if k == 1:
            xs, bm, cm, dt = xs[:, ::-1], bm[:, ::-1], cm[:, ::-1], dt[:, ::-1]
        dtk = dt[:, :, k]
        y, hl = ssd_chunked(xs.astype(F32) * dtk[..., None], dtk * a[k],
                            bm.astype(F32), cm.astype(F32), h0, with_y)
        if with_y and k == 1:
            y = y[:, ::-1]
        return y, hl

    def finish(z, xs, yf, yb):
        y = yf + yb + d_skip.astype(F32)[:, None] * xs.astype(F32)
        y = y.reshape(bsz, -1, SSD_D_INNER).astype(z.dtype)
        return rmsnorm(y * jax.nn.silu(z), norm_g) @ out_w

    zero = jnp.zeros((bsz, SSD_HEADS, SSD_HEAD_DIM, SSD_STATE), F32)
    zc, xc, bcx, ccx, dtc = project(cn)
    ycf, hf = scan(xc, bcx, ccx, dtc, 0, zero, need_ctx)
    ycb, hb = scan(xc, bcx, ccx, dtc, 1, zero, need_ctx)
    hp = hn.reshape(bsz, rows, GRID_W, d).transpose(0, 2, 1, 3).reshape(bsz, seq, d)
    zl, xl, bl, cl, dtl = project(hp)
    ylf, _ = scan(xl, bl, cl, dtl, 0, hf, True)
    ylb, _ = scan(xl, bl, cl, dtl, 1, hb, True)
    out = finish(zl, xl, ylf, ylb)
    out = out.reshape(bsz, GRID_W, rows, d).transpose(0, 2, 1, 3).reshape(bsz, seq, d)
    out_ctx = finish(zc, xc, ycf, ycb) if need_ctx else None
    return out, out_ctx


def moe(t, router_w, router_b, w1, w3, w2):
    n = t.shape[0]
    s = jax.nn.sigmoid((t @ router_w).astype(F32))
    sel = (s + router_b.astype(F32)).reshape(n, N_EXPERT_GROUPS, EXPERTS_PER_GROUP)
    group_score = lax.top_k(sel, TOP_K)[0].sum(-1)
    gidx = jnp.argmax(group_score, axis=-1)
    in_group = jnp.take_along_axis(sel, gidx[:, None, None], axis=1)[:, 0]
    _, loc = lax.top_k(in_group, TOP_K)
    eidx = gidx[:, None] * EXPERTS_PER_GROUP + loc
    w = jnp.take_along_axis(s, eidx, axis=1)
    w = w / jnp.sum(w, axis=-1, keepdims=True)
    gates = jnp.sum(jax.nn.one_hot(eidx, N_EXPERTS, dtype=F32) * w[..., None], axis=1)
    y = jnp.zeros_like(t)
    for e in range(N_EXPERTS):
        h = jax.nn.silu(t @ w1[e]) * (t @ w3[e])
        y = y + gates[:, e:e + 1].astype(t.dtype) * (h @ w2[e])
    return y


def setup_inputs(seed: int = 0) -> dict:
    key = jax.random.key(seed)
    ks = iter(jax.random.split(key, 48))

    def nrm(shape, scale):
        return jax.random.normal(next(ks), shape, F32) * scale

    def unif(shape, lo, hi):
        return jax.random.uniform(next(ks), shape, F32, lo, hi)

    d = D_MODEL
    n_a = (DEPTH + N_MIXERS - 1) // N_MIXERS
    n_b = DEPTH // N_MIXERS
    n_idx = jnp.arange(S5_STATE, dtype=F32)
    ssd_dt0 = jnp.exp(unif((n_b, 2, SSD_HEADS), math.log(SSD_DT_MIN), math.log(SSD_DT_MAX)))
    return {
        'x': nrm((BATCH, SEQ, d), 1.0),
        'c': nrm((BATCH, d), 1.0),
        'ctx': nrm((BATCH, CTX_LEN, d), 1.0),
        'c_ctx': nrm((d,), 1.0),
        'mod_w': nrm((DEPTH, d, 6 * d), 0.5 * d ** -0.5),
        'mod_b': nrm((DEPTH, 6 * d), 0.01),
        'norm1_g': 1.0 + nrm((DEPTH, d), 0.1),
        'norm2_g': 1.0 + nrm((DEPTH, d), 0.1),
        'final_g': 1.0 + nrm((d,), 0.1),
        's5_lam_re': -0.5 + nrm((n_a, 2, S5_GROUPS, S5_STATE), 0.01),
        's5_lam_im': jnp.pi * n_idx + nrm((n_a, 2, S5_GROUPS, S5_STATE), 0.01),
        's5_log_dt': unif((n_a, 2, S5_GROUPS), math.log(S5_DT_MIN), math.log(S5_DT_MAX)),
        's5_b_re': nrm((n_a, S5_GROUPS, S5_STATE, S5_GROUP), (2 * S5_GROUP) ** -0.5),
        's5_b_im': nrm((n_a, S5_GROUPS, S5_STATE, S5_GROUP), (2 * S5_GROUP) ** -0.5),
        's5_c_re': nrm((n_a, S5_GROUPS, S5_GROUP, S5_STATE), (2 * S5_STATE) ** -0.5),
        's5_c_im': nrm((n_a, S5_GROUPS, S5_GROUP, S5_STATE), (2 * S5_STATE) ** -0.5),
        's5_d': 1.0 + nrm((n_a, d), 0.1),
        's5_glu_w': nrm((n_a, d, 2 * d), d ** -0.5),
        's5_glu_b': nrm((n_a, 2 * d), 0.01),
        'ssd_in_w': nrm((n_b, d, SSD_PROJ), d ** -0.5),
        'ssd_conv_w': nrm((n_b, SSD_CONV, SSD_CONV_CH), SSD_CONV ** -0.5),
        'ssd_conv_b': nrm((n_b, SSD_CONV_CH), 0.01),
        'ssd_dt_bias': ssd_dt0 + jnp.log(-jnp.expm1(-ssd_dt0)),
        'ssd_a_log': jnp.log(unif((n_b, 2, SSD_HEADS), 1.0, 16.0)),
        'ssd_d': 1.0 + nrm((n_b, SSD_HEADS), 0.1),
        'ssd_norm_g': 1.0 + nrm((n_b, SSD_D_INNER), 0.1),
        'ssd_out_w': nrm((n_b, SSD_D_INNER, d), SSD_D_INNER ** -0.5),
        'router_w': nrm((d, N_EXPERTS), d ** -0.5),
        'router_b': nrm((N_EXPERTS,), 0.01),
        'moe_w1': nrm((DEPTH, N_EXPERTS, d, D_EXPERT), d ** -0.5),
        'moe_w3': nrm((DEPTH, N_EXPERTS, d, D_EXPERT), d ** -0.5),
        'moe_w2': nrm((DEPTH, N_EXPERTS, D_EXPERT, d), D_EXPERT ** -0.5),
    }


def reference(x, c, ctx, c_ctx, mod_w, mod_b, norm1_g, norm2_g, final_g,
              s5_lam_re, s5_lam_im, s5_log_dt, s5_b_re, s5_b_im, s5_c_re, s5_c_im, s5_d, s5_glu_w, s5_glu_b,
              ssd_in_w, ssd_conv_w, ssd_conv_b, ssd_dt_bias, ssd_a_log, ssd_d, ssd_norm_g, ssd_out_w,
              router_w, router_b, moe_w1, moe_w3, moe_w2):
    d = x.shape[-1]
    n_ctx = ctx.shape[1]
    for i in range(DEPTH):
        need_ctx = i < DEPTH - 1
        m_lat = jax.nn.silu(c) @ mod_w[i] + mod_b[i]
        m_ctx = jax.nn.silu(c_ctx) @ mod_w[i] + mod_b[i]
        sh1, sc1, g1, sh2, sc2, g2 = jnp.split(m_lat[:, None, :], 6, axis=-1)
        sh1c, sc1c, g1c, sh2c, sc2c, g2c = jnp.split(m_ctx, 6)

        hn = adaln(rmsnorm(x, norm1_g[i]), sh1, sc1)
        cn = adaln(rmsnorm(ctx, norm1_g[i]), sh1c, sc1c)
        j = i // N_MIXERS
        if i % N_MIXERS == 0:
            y, yc = s5_mixer(hn, cn, s5_lam_re[j], s5_lam_im[j], s5_log_dt[j], s5_b_re[j], s5_b_im[j],
                             s5_c_re[j], s5_c_im[j], s5_d[j], s5_glu_w[j], s5_glu_b[j], need_ctx)
        else:
            y, yc = ssd_mixer(hn, cn, ssd_in_w[j], ssd_conv_w[j], ssd_conv_b[j], ssd_dt_bias[j],
                              ssd_a_log[j], ssd_d[j], ssd_norm_g[j], ssd_out_w[j], need_ctx)
        x = x + g1 * y

        hn2 = adaln(rmsnorm(x, norm2_g[i]), sh2, sc2)
        if need_ctx:
            ctx = ctx + g1c * yc
            cn2 = adaln(rmsnorm(ctx, norm2_g[i]), sh2c, sc2c)
            tokens = jnp.concatenate([cn2, hn2], axis=1)
            out = moe(tokens.reshape(-1, d), router_w, router_b,
                      moe_w1[i], moe_w3[i], moe_w2[i]).reshape(tokens.shape)
            ctx = ctx + g2c * out[:, :n_ctx]
            x = x + g2 * out[:, n_ctx:]
        else:
            out = moe(hn2.reshape(-1, d), router_w, router_b,
                      moe_w1[i], moe_w3[i], moe_w2[i]).reshape(hn2.shape)
            x = x + g2 * out
    return rmsnorm(x, final_g)
```

```python
import functools

import jax
import jax.numpy as jnp
from jax import lax
from jax.experimental import pallas as pl
from jax.experimental.pallas import tpu as pltpu

F32 = jnp.float32
BF16 = jnp.bfloat16
HIGHEST = lax.Precision.HIGHEST

GRID_W = 64
RMS_EPS = 1e-6

S5_GROUP = 16
S5_STATE = 64
S5_T = 16
S5_GB = 8

SSD_HEAD_DIM = 64
SSD_GROUPS = 8
SSD_STATE = 128
SSD_CONV = 5
SSD_CHUNK = 128

N_EXPERT_GROUPS = 4
TOP_K = 2

TOKEN_TILE = 256
MOE_TILE = 256
VMEM_LIMIT_BYTES = 56 * 1024 * 1024


def _cparams(*sem):
    return pltpu.CompilerParams(dimension_semantics=sem, vmem_limit_bytes=VMEM_LIMIT_BYTES)


def _sigmoid(v):
    return 1.0 / (1.0 + jnp.exp(-v))


def _silu(v):
    return v * _sigmoid(v)


def _gelu_tanh(v):
    return 0.5 * v * (1.0 + jnp.tanh(0.7978845608028654 * (v + 0.044715 * (v * v * v))))


def _rms(v, g):
    return v * lax.rsqrt(jnp.mean(v * v, axis=-1, keepdims=True) + RMS_EPS) * g


def _mod_kernel(cc_ref, w_ref, b_ref, o_ref):
    a = _silu(cc_ref[...])
    o_ref[...] = jnp.dot(a, w_ref[...], preferred_element_type=F32, precision=HIGHEST) + b_ref[...]


def _modulation(c, c_ctx, mod_w, mod_b):
    depth, d, d6 = mod_w.shape
    bsz = c.shape[0]
    rows = 8
    cc = jnp.zeros((rows, d), F32).at[:bsz].set(c).at[bsz].set(c_ctx)
    tn = d6 // 4
    out = pl.pallas_call(
        _mod_kernel,
        grid=(depth, d6 // tn),
        in_specs=[
            pl.BlockSpec((rows, d), lambda i, j: (0, 0)),
            pl.BlockSpec((None, d, tn), lambda i, j: (i, 0, j)),
            pl.BlockSpec((None, 1, tn), lambda i, j: (i, 0, j)),
        ],
        out_specs=pl.BlockSpec((None, rows, tn), lambda i, j: (i, 0, j)),
        out_shape=jax.ShapeDtypeStruct((depth, rows, d6), F32),
        compiler_params=_cparams("parallel", "parallel"),
        name="modulation",
    )(cc, mod_w, mod_b.reshape(depth, 1, d6))
    lat = out[:, :bsz].reshape(depth, bsz, 1, 6, d)
    ctx = jnp.broadcast_to(out[:, bsz].reshape(depth, 1, 1, 6, d), (depth, bsz, 1, 6, d))
    return jnp.concatenate([lat, ctx], axis=2).reshape(depth, bsz * 2, 6, d)


def _prenorm_kernel(x_ref, g_ref, mod_ref, o_ref):
    m = mod_ref[...]
    hn = _rms(x_ref[...], g_ref[...]) * (1.0 + m[1:2]) + m[0:1]
    o_ref[...] = hn.astype(o_ref.dtype)


def _prenorm(xall, g, mods, n_lat_tiles):
    bsz, lt, d = xall.shape
    nt = lt // TOKEN_TILE
    return pl.pallas_call(
        _prenorm_kernel,
        grid=(bsz, nt),
        in_specs=[
            pl.BlockSpec((None, TOKEN_TILE, d), lambda b, i: (b, i, 0)),
            pl.BlockSpec((1, d), lambda b, i: (0, 0)),
            pl.BlockSpec((None, 6, d), lambda b, i: (b * 2 + (i >= n_lat_tiles).astype(jnp.int32), 0, 0)),
        ],
        out_specs=pl.BlockSpec((None, TOKEN_TILE, d), lambda b, i: (b, i, 0)),
        out_shape=jax.ShapeDtypeStruct((bsz, lt, d), BF16),
        compiler_params=_cparams("parallel", "parallel"),
        name="prenorm",
    )(xall, g.reshape(1, d), mods)


def _s5_weights(lam_re, lam_im, log_dt, b_re, b_im, c_re, c_im):
    t = S5_T
    ngrp, p = lam_re.shape[1], lam_re.shape[2]
    k16 = b_re.shape[-1]
    tau = jnp.arange(t + 1, dtype=F32)[None, :, None]

    def direction(k):
        lr, li = lam_re[k], lam_im[k]
        step = jnp.exp(log_dt[k])[:, None]
        mag = jnp.exp(lr * step)
        abar_r = mag * jnp.cos(li * step)
        abar_i = mag * jnp.sin(li * step)
        den = lr * lr + li * li
        q_r = ((abar_r - 1.0) * lr + abar_i * li) / den
        q_i = (abar_i * lr - (abar_r - 1.0) * li) / den
        bb_r = q_r[..., None] * b_re - q_i[..., None] * b_im
        bb_i = q_r[..., None] * b_im + q_i[..., None] * b_re
        pmag = jnp.exp((lr * step)[:, None, :] * tau)
        pw_r = pmag * jnp.cos((li * step)[:, None, :] * tau)
        pw_i = pmag * jnp.sin((li * step)[:, None, :] * tau)
        return bb_r, bb_i, pw_r, pw_i

    def cplx_kernel(pw_r, pw_i, bb_r, bb_i):
        m_r = pw_r[:, :t, :, None] * bb_r[:, None] - pw_i[:, :t, :, None] * bb_i[:, None]
        m_i = pw_r[:, :t, :, None] * bb_i[:, None] + pw_i[:, :t, :, None] * bb_r[:, None]
        return (jnp.einsum("gkp,gtpj->gtkj", c_re, m_r, precision=HIGHEST)
                - jnp.einsum("gkp,gtpj->gtkj", c_im, m_i, precision=HIGHEST))

    bf_r, bf_i, pf_r, pf_i = direction(0)
    bb_r, bb_i, pb_r, pb_i = direction(1)
    kf = cplx_kernel(pf_r, pf_i, bf_r, bf_i)
    kb = cplx_kernel(pb_r, pb_i, bb_r, bb_i)

    s_idx = jnp.arange(t)[:, None]
    t_idx = jnp.arange(t)[None, :]
    lag_f = jnp.clip(t_idx - s_idx, 0, t - 1)
    lag_b = jnp.clip(s_idx - t_idx, 0, t - 1)
    tf = jnp.where((t_idx >= s_idx)[None, :, :, None, None], kf[:, lag_f], 0.0)
    tb = jnp.where((s_idx >= t_idx)[None, :, :, None, None], kb[:, lag_b], 0.0)
    wm = (tf + tb).transpose(0, 1, 4, 2, 3).reshape(ngrp, t * k16, t * k16)

    def state_in(pw_r, pw_i, b_r, b_i, order):
        pr, pi = pw_r[:, order], pw_i[:, order]
        w_r = pr[..., None] * b_r[:, None] - pi[..., None] * b_i[:, None]
        w_i = pr[..., None] * b_i[:, None] + pi[..., None] * b_r[:, None]
        tr = lambda w: w.transpose(0, 1, 3, 2).reshape(ngrp, t * k16, p)
        return tr(w_r), tr(w_i)

    sf_r, sf_i = state_in(pf_r, pf_i, bf_r, bf_i, jnp.arange(t - 1, -1, -1))
    sb_r, sb_i = state_in(pb_r, pb_i, bb_r, bb_i, jnp.arange(t))
    ws = jnp.concatenate([sf_r, sb_r, sf_i, sb_i], axis=-1)

    def state_out(pw_r, pw_i, order):
        pr, pi = pw_r[:, order], pw_i[:, order]
        o_r = c_re[:, None] * pr[:, :, None, :] - c_im[:, None] * pi[:, :, None, :]
        o_i = -(c_re[:, None] * pi[:, :, None, :] + c_im[:, None] * pr[:, :, None, :])
        tr = lambda w: w.transpose(0, 3, 1, 2).reshape(ngrp, p, t * k16)
        return tr(o_r), tr(o_i)

    of_r, of_i = state_out(pf_r, pf_i, jnp.arange(1, t + 1))
    ob_r, ob_i = state_out(pb_r, pb_i, jnp.arange(t, 0, -1))
    zero = jnp.zeros_like(of_r)
    w2 = jnp.concatenate([of_r, zero, of_i, zero, zero, ob_r, zero, ob_i], axis=1)

    ar = jnp.concatenate([pf_r[:, t], pb_r[:, t]], axis=-1)
    ai = jnp.concatenate([pf_i[:, t], pb_i[:, t]], axis=-1)
    return ws.astype(BF16), wm.astype(BF16), w2.astype(BF16), ar, ai


def _s5_kernel(u_ref, ws_ref, wm_ref, w2_ref, ar_ref, ai_ref, y_ref,
               sre, sim, hre_f, him_f, hre_b, him_b, *, n_chunks, n_ctx_chunks, pitch):
    nc, ncc = n_chunks, n_ctx_chunks
    ncl = nc - ncc
    p = S5_STATE
    for g in range(S5_GB):
        s = jnp.dot(u_ref[g], ws_ref[g], preferred_element_type=F32)
        sre[pl.ds(g * pitch, nc), :] = s[:, : 2 * p]
        sim[pl.ds(g * pitch, nc), :] = s[:, 2 * p:]

    ar = ar_ref[...]
    ai = ai_ref[...]
    fwd_lane = lax.broadcasted_iota(jnp.int32, (S5_GB, 2 * p), 1) < p

    def step(k, carry):
        h_r, h_i = carry
        cf = jnp.where(k < ncc, ncl + k, k - ncc)
        cb = nc - 1 - k
        rows_f = pl.ds(cf, S5_GB, stride=pitch)
        rows_b = pl.ds(cb, S5_GB, stride=pitch)
        hre_f[rows_f, :] = h_r
        him_f[rows_f, :] = h_i
        hre_b[rows_b, :] = h_r
        him_b[rows_b, :] = h_i
        s_r = jnp.where(fwd_lane, sre[rows_f, :], sre[rows_b, :])
        s_i = jnp.where(fwd_lane, sim[rows_f, :], sim[rows_b, :])
        n_r = ar * h_r - ai * h_i + s_r
        n_i = ar * h_i + ai * h_r + s_i
        return n_r, n_i

    zero = jnp.zeros((S5_GB, 2 * p), F32)
    lax.fori_loop(0, nc, step, (zero, zero))

    for g in range(S5_GB):
        rows = pl.ds(g * pitch, nc)
        hin = jnp.concatenate([hre_f[rows, :], him_f[rows, :], hre_b[rows, :], him_b[rows, :]], axis=1)
        out = jnp.dot(u_ref[g], wm_ref[g], preferred_element_type=F32)
        out = out + jnp.dot(hin.astype(BF16), w2_ref[g], preferred_element_type=F32)
        y_ref[g] = out.astype(y_ref.dtype)


def _s5_scan(hn, n_lat, weights):
    bsz, lt, d = hn.shape
    ngrp = d // S5_GROUP
    t = S5_T
    tk = t * S5_GROUP
    nc = lt // t
    ncc = (lt - n_lat) // t
    ws, wm, w2, ar, ai = weights
    u = hn.reshape(bsz, nc, t, ngrp, S5_GROUP).transpose(3, 0, 1, 2, 4).reshape(ngrp, bsz, nc, tk)
    pitch = nc + 8 if (nc // 8) % 2 == 0 else nc
    kern = functools.partial(_s5_kernel, n_chunks=nc, n_ctx_chunks=ncc, pitch=pitch)
    gb = S5_GB
    y = pl.pallas_call(
        kern,
        grid=(ngrp // gb, bsz),
        in_specs=[
            pl.BlockSpec((gb, None, nc, tk), lambda gi, b: (gi, b, 0, 0)),
            pl.BlockSpec((gb, tk, 4 * S5_STATE), lambda gi, b: (gi, 0, 0)),
            pl.BlockSpec((gb, tk, tk), lambda gi, b: (gi, 0, 0)),
            pl.BlockSpec((gb, 8 * S5_STATE, tk), lambda gi, b: (gi, 0, 0)),
            pl.BlockSpec((gb, 2 * S5_STATE), lambda gi, b: (gi, 0)),
            pl.BlockSpec((gb, 2 * S5_STATE), lambda gi, b: (gi, 0)),
        ],
        out_specs=pl.BlockSpec((gb, None, nc, tk), lambda gi, b: (gi, b, 0, 0)),
        out_shape=jax.ShapeDtypeStruct((ngrp, bsz, nc, tk), BF16),
        scratch_shapes=[pltpu.VMEM((gb * pitch, 2 * S5_STATE), F32) for _ in range(6)],
        compiler_params=_cparams("parallel", "parallel"),
        name="s5_scan",
    )(u, ws, wm, w2, ar, ai)
    return y.reshape(ngrp, bsz, nc, t, S5_GROUP).transpose(1, 2, 3, 0, 4).reshape(bsz, lt, d)


def _router_gates(hn2, rw_ref, rb_ref):
    n_exp = rw_ref.shape[0]
    epg = n_exp // N_EXPERT_GROUPS
    logits = lax.dot_general(rw_ref[...], hn2, (((1,), (1,)), ((), ())),
                             preferred_element_type=F32, precision=HIGHEST)
    s = _sigmoid(logits)
    sel = s + rb_ref[...]
    row = [sel[e:e + 1] for e in range(n_exp)]
    gscore = []
    for gi in range(N_EXPERT_GROUPS):
        a, b, c, dd = row[gi * epg: gi * epg + epg]
        hi1, lo1 = jnp.maximum(a, b), jnp.minimum(a, b)
        hi2, lo2 = jnp.maximum(c, dd), jnp.minimum(c, dd)
        gscore.append(jnp.maximum(hi1, hi2) + jnp.maximum(jnp.minimum(hi1, hi2), jnp.maximum(lo1, lo2)))
    gmax = functools.reduce(jnp.maximum, gscore)
    gates = []
    taken = None
    for gi in range(N_EXPERT_GROUPS):
        is_max = gscore[gi] == gmax
        best = is_max if taken is None else jnp.logical_and(is_max, jnp.logical_not(taken))
        taken = is_max if taken is None else jnp.logical_or(taken, is_max)
        for e in range(gi * epg, gi * epg + epg):
            rank = jnp.zeros_like(row[e])
            for j in range(gi * epg, gi * epg + epg):
                if j == e:
                    continue
                ahead = (row[j] >= row[e]) if j < e else (row[j] > row[e])
                rank = rank + ahead.astype(F32)
            chosen = jnp.logical_and(best, rank < float(TOP_K))
            gates.append(jnp.where(chosen, s[e:e + 1], 0.0))
    g = jnp.concatenate(gates, axis=0)
    return g / jnp.sum(g, axis=0, keepdims=True)


def _glu_kernel(y_ref, u_ref, x_ref, mod_ref, d_ref, w_ref, b_ref, g2_ref, rw_ref, rb_ref,
                x1_ref, hn2_ref, gates_ref):
    d = x_ref.shape[-1]
    m = mod_ref[...]
    u = u_ref[...].astype(F32)
    a = _gelu_tanh(y_ref[...].astype(F32) + d_ref[...] * u)
    z = jnp.dot(a.astype(BF16), w_ref[...], preferred_element_type=F32) + b_ref[...]
    out = z[:, :d] * _sigmoid(z[:, d:])
    x1 = x_ref[...] + m[2:3] * out
    x1_ref[...] = x1
    hn2 = _rms(x1, g2_ref[...]) * (1.0 + m[4:5]) + m[3:4]
    hn2_ref[...] = hn2.astype(hn2_ref.dtype)
    gates_ref[...] = _router_gates(hn2, rw_ref, rb_ref)


def _glu_head(y, hn, xall, mods, d_skip, glu_w, glu_b, norm2_g, router_wt, router_b, n_lat_tiles):
    bsz, lt, d = xall.shape
    nt = lt // TOKEN_TILE
    n_exp = router_wt.shape[0]
    tok = pl.BlockSpec((None, TOKEN_TILE, d), lambda b, i: (b, i, 0))
    vec = lambda n: pl.BlockSpec((1, n), lambda b, i: (0, 0))
    return pl.pallas_call(
        _glu_kernel,
        grid=(bsz, nt),
        in_specs=[
            tok, tok, tok,
            pl.BlockSpec((None, 6, d), lambda b, i: (b * 2 + (i >= n_lat_tiles).astype(jnp.int32), 0, 0)),
            vec(d),
            pl.BlockSpec((d, 2 * d), lambda b, i: (0, 0)),
            vec(2 * d),
            vec(d),
            pl.BlockSpec((n_exp, d), lambda b, i: (0, 0)),
            pl.BlockSpec((n_exp, 1), lambda b, i: (0, 0)),
        ],
        out_specs=[
            tok, tok,
            pl.BlockSpec((n_exp, TOKEN_TILE), lambda b, i: (0, b * nt + i)),
        ],
        out_shape=[
            jax.ShapeDtypeStruct((bsz, lt, d), F32),
            jax.ShapeDtypeStruct((bsz, lt, d), BF16),
            jax.ShapeDtypeStruct((n_exp, bsz * lt), F32),
        ],
        compiler_params=_cparams("parallel", "parallel"),
        name="s5_glu_head",
    )(y, hn, xall, mods, d_skip.reshape(1, d), glu_w.astype(BF16), glu_b.reshape(1, 2 * d),
      norm2_g.reshape(1, d), router_wt, router_b.reshape(n_exp, 1))


def _moe_kernel(t_ref, gates_ref, x_ref, mod_ref, w1_ref, w3_ref, w2_ref, *rest, final, out_cols):
    if final:
        fg_ref, o_ref, acc = rest
    else:
        o_ref, acc = rest
    e = pl.program_id(1)

    @pl.when(e == 0)
    def _():
        acc[...] = jnp.zeros_like(acc)

    t = t_ref[...]
    h = _silu(jnp.dot(t, w1_ref[...], preferred_element_type=F32)) * jnp.dot(t, w3_ref[...], preferred_element_type=F32)
    ye = jnp.dot(h.astype(BF16), w2_ref[...], preferred_element_type=F32)
    gates = gates_ref[...]
    lane = lax.broadcasted_iota(jnp.int32, gates.shape, 1)
    gcol = jnp.sum(jnp.where(lane == e, gates, 0.0), axis=1, keepdims=True)
    acc[...] += gcol * ye

    @pl.when(e == pl.num_programs(1) - 1)
    def _():
        m = mod_ref[...]
        xn = x_ref[...] + m[5:6] * acc[...]
        if final:
            xn = _rms(xn, fg_ref[...])
            for j in range(out_cols):
                o_ref[:, j * xn.shape[1]:(j + 1) * xn.shape[1]] = xn[j * SSD_CHUNK:(j + 1) * SSD_CHUNK]
        else:
            o_ref[...] = xn


def _moe(t, gates, xres, mods, w1, w3, w2, *, tiles_per_batch, n_lat_tiles, final_g=None):
    n, d = t.shape
    n_exp, _, f = w1.shape
    tm = MOE_TILE
    final = final_g is not None
    tok = pl.BlockSpec((tm, d), lambda i, e: (i, 0))

    def mod_map(i, e):
        seg = ((i % tiles_per_batch) >= n_lat_tiles).astype(jnp.int32)
        return ((i // tiles_per_batch) * 2 + seg, 0, 0)

    in_specs = [
        tok,
        pl.BlockSpec((tm, n_exp), lambda i, e: (i, 0)),
        tok,
        pl.BlockSpec((None, 6, d), mod_map),
        pl.BlockSpec((None, d, f), lambda i, e: (e, 0, 0)),
        pl.BlockSpec((None, d, f), lambda i, e: (e, 0, 0)),
        pl.BlockSpec((None, f, d), lambda i, e: (e, 0, 0)),
    ]
    args = [t, gates, xres, mods, w1, w3, w2]
    out_cols = tm // SSD_CHUNK
    if final:
        in_specs.append(pl.BlockSpec((1, d), lambda i, e: (0, 0)))
        args.append(final_g.reshape(1, d))
        bsz = n // (tiles_per_batch * tm)
        out_spec = pl.BlockSpec((None, SSD_CHUNK, out_cols * d),
                                lambda i, e: (i // tiles_per_batch, 0, i % tiles_per_batch))
        out_shape = jax.ShapeDtypeStruct((bsz, SSD_CHUNK, GRID_W * d), F32)
    else:
        out_spec = tok
        out_shape = jax.ShapeDtypeStruct((n, d), F32)
    kern = functools.partial(_moe_kernel, final=final, out_cols=out_cols)
    return pl.pallas_call(
        kern,
        grid=(n // tm, n_exp),
        in_specs=in_specs,
        out_specs=out_spec,
        out_shape=out_shape,
        scratch_shapes=[pltpu.VMEM((tm, d), F32)],
        compiler_params=_cparams("parallel", "arbitrary"),
        name="moe_final" if final else "moe",
    )(*args)


def _ssd_inproj_kernel(xl_ref, xc_ref, g_ref, mod_ref, wz_ref, wx_ref, wdt_ref, wdtt_ref, bias_ref, biast_ref,
                       a_ref, at_ref, z_ref, xbc_ref, dt_ref, cs_ref, cst_ref, xt, *, n_lat_chunks):
    c = pl.program_id(1)

    @pl.when(c < n_lat_chunks)
    def _():
        xt[...] = xl_ref[...]

    @pl.when(c >= n_lat_chunks)
    def _():
        xt[...] = xc_ref[...]

    m = mod_ref[...]
    hn = (_rms(xt[...], g_ref[...]) * (1.0 + m[1:2]) + m[0:1]).astype(BF16)
    z_ref[...] = jnp.dot(hn, wz_ref[...], preferred_element_type=F32).astype(z_ref.dtype)
    xbc_ref[...] = jnp.dot(hn, wx_ref[...], preferred_element_type=F32).astype(xbc_ref.dtype)

    def softplus(v):
        return jnp.maximum(v, 0.0) + jnp.log(1.0 + jnp.exp(-jnp.abs(v)))

    q = SSD_CHUNK
    r_i = lax.broadcasted_iota(jnp.int32, (q, q), 0)
    c_i = lax.broadcasted_iota(jnp.int32, (q, q), 1)
    nh = a_ref.shape[1] // 2
    dt = softplus(jnp.dot(hn, wdt_ref[...], preferred_element_type=F32) + bias_ref[...])
    da = dt * a_ref[...]
    lower = (r_i >= c_i).astype(F32)
    upper = (r_i <= c_i).astype(F32)
    cs_f = jnp.dot(lower, da[:, :nh], preferred_element_type=F32, precision=HIGHEST)
    cs_b = jnp.dot(upper, da[:, nh:], preferred_element_type=F32, precision=HIGHEST)
    cs = jnp.concatenate([cs_f, cs_b], axis=1)
    dtt = softplus(lax.dot_general(wdtt_ref[...], hn, (((1,), (1,)), ((), ())), preferred_element_type=F32)
                   + biast_ref[...])
    dat = dtt * at_ref[...]
    cst_f = jnp.dot(dat[:nh], upper, preferred_element_type=F32, precision=HIGHEST)
    cst_b = jnp.dot(dat[nh:], lower, preferred_element_type=F32, precision=HIGHEST)
    cst = jnp.concatenate([cst_f, cst_b], axis=0)
    r = 4
    for j in range(dt_ref.shape[0]):
        dt_ref[j] = dt[:, j * r:(j + 1) * r]
        cs_ref[j] = cs[:, j * r:(j + 1) * r]
        cst_ref[j] = cst[j * r:(j + 1) * r, :]


def _ssd_inproj(xall, n_lat, norm_g, mods, in_w, dt_bias, a_log):
    bsz, lt, d = xall.shape
    q = SSD_CHUNK
    ncl = n_lat // q
    nc = lt // q
    nh2 = dt_bias.size
    d_inner = (nh2 // 2) * SSD_HEAD_DIM
    conv_ch = in_w.shape[1] - d_inner - nh2
    wz = in_w[:, :d_inner].astype(BF16)
    wx = in_w[:, d_inner:d_inner + conv_ch].astype(BF16)
    wdt = in_w[:, d_inner + conv_ch:].astype(BF16)
    a = -jnp.exp(a_log.astype(F32)).reshape(1, nh2)
    bias = dt_bias.astype(F32).reshape(1, nh2)
    ngr = nh2 // 4
    assert n_lat // GRID_W == q, "one SSD chunk per latent grid column"
    xview = xall.reshape(bsz, lt // GRID_W, GRID_W * d)
    kern = functools.partial(_ssd_inproj_kernel, n_lat_chunks=ncl)
    full = lambda s: pl.BlockSpec(s, lambda b, c: tuple(0 for _ in s))
    return pl.pallas_call(
        kern,
        grid=(bsz, nc),
        in_specs=[
            pl.BlockSpec((None, q, d), lambda b, c: (b, 0, jnp.minimum(c, ncl - 1))),
            pl.BlockSpec((None, q, d), lambda b, c: (b, jnp.maximum(c, ncl), 0)),
            full((1, d)),
            pl.BlockSpec((None, 6, d), lambda b, c: (b * 2 + (c >= ncl).astype(jnp.int32), 0, 0)),
            full((d, d_inner)), full((d, conv_ch)), full((d, nh2)), full((nh2, d)),
            full((1, nh2)), full((nh2, 1)), full((1, nh2)), full((nh2, 1)),
        ],
        out_specs=[
            pl.BlockSpec((None, q, d_inner), lambda b, c: (b, c, 0)),
            pl.BlockSpec((None, q, conv_ch), lambda b, c: (b, c, 0)),
            pl.BlockSpec((None, ngr, q, 4), lambda b, c: (b, 0, c, 0)),
            pl.BlockSpec((None, ngr, q, 4), lambda b, c: (b, 0, c, 0)),
            pl.BlockSpec((None, None, ngr, 4, q), lambda b, c: (b, c, 0, 0, 0)),
        ],
        out_shape=[
            jax.ShapeDtypeStruct((bsz, lt, d_inner), BF16),
            jax.ShapeDtypeStruct((bsz, lt, conv_ch), BF16),
            jax.ShapeDtypeStruct((bsz, ngr, lt, 4), F32),
            jax.ShapeDtypeStruct((bsz, ngr, lt, 4), F32),
            jax.ShapeDtypeStruct((bsz, nc, ngr, 4, q), F32),
        ],
        scratch_shapes=[pltpu.VMEM((q, d), F32)],
        compiler_params=_cparams("parallel", "parallel"),
        name="ssd_inproj",
    )(xview, xall, norm_g.reshape(1, d), mods, wz, wx, wdt, wdt.T, bias, bias.reshape(nh2, 1),
      a, a.reshape(nh2, 1))


def _ssd_conv_kernel(cur_ref, prev_ref, next_ref, w_ref, b_ref, o_ref, ext, *, n_lat_chunks, n_chunks):
    c = pl.program_id(1)
    q = SSD_CHUNK
    halo = prev_ref.shape[0]
    first = jnp.logical_or(c == 0, c == n_lat_chunks)
    last = jnp.logical_or(c == n_lat_chunks - 1, c == n_chunks - 1)
    prev = prev_ref[...].astype(F32)
    nxt = next_ref[...].astype(F32)
    ext[0:8, :] = jnp.where(first, 0.0, prev[halo - 8:halo])
    ext[8:8 + q, :] = cur_ref[...].astype(F32)
    ext[8 + q:16 + q, :] = jnp.where(last, 0.0, nxt[0:8])
    acc = jnp.zeros(o_ref.shape, F32) + b_ref[...]
    w = w_ref[...]
    for k in range(SSD_CONV):
        acc = acc + w[k:k + 1] * ext[pl.ds(8 - SSD_CONV // 2 + k, q), :]
    o_ref[...] = _silu(acc).astype(o_ref.dtype)


def _ssd_conv(xbc, n_lat, conv_w, conv_b):
    bsz, lt, ch = xbc.shape
    q = SSD_CHUNK
    nc, ncl = lt // q, n_lat // q
    halo = 16
    hb = q // halo
    cw = 512
    kern = functools.partial(_ssd_conv_kernel, n_lat_chunks=ncl, n_chunks=nc)
    return pl.pallas_call(
        kern,
        grid=(bsz, nc, ch // cw),
        in_specs=[
            pl.BlockSpec((None, q, cw), lambda b, c, j: (b, c, j)),
            pl.BlockSpec((None, halo, cw), lambda b, c, j: (b, jnp.maximum(c * hb - 1, 0), j)),
            pl.BlockSpec((None, halo, cw), lambda b, c, j: (b, jnp.minimum((c + 1) * hb, nc * hb - 1), j)),
            pl.BlockSpec((SSD_CONV, cw), lambda b, c, j: (0, j)),
            pl.BlockSpec((1, cw), lambda b, c, j: (0, j)),
        ],
        out_specs=pl.BlockSpec((None, q, cw), lambda b, c, j: (b, c, j)),
        out_shape=jax.ShapeDtypeStruct((bsz, lt, ch), BF16),
        scratch_shapes=[pltpu.VMEM((q + 16, cw), F32)],
        compiler_params=_cparams("parallel", "parallel", "parallel"),
        name="ssd_conv",
    )(xbc, xbc, xbc, conv_w.astype(F32), conv_b.astype(F32).reshape(1, ch))


def _ssd_scan_kernel(x_ref, b_ref, c_ref, dt_ref, cs_ref, cst_ref, y_ref, state, *, reverse):
    q = SSD_CHUNK
    hp = SSD_HEAD_DIM
    r = x_ref.shape[1] // hp

    @pl.when(pl.program_id(2) == 0)
    def _():
        state[...] = jnp.zeros_like(state)

    x = x_ref[...].astype(F32)
    bm = b_ref[...]
    cm = c_ref[...]
    dt = dt_ref[...]
    cs = cs_ref[...]
    cst = cst_ref[...]
    cb = lax.dot_general(cm, bm, (((1,), (1,)), ((), ())), preferred_element_type=F32)
    y_off = jnp.dot(cm, state[...].astype(BF16), preferred_element_type=F32)
    r_i = lax.broadcasted_iota(jnp.int32, (q, q), 0)
    c_i = lax.broadcasted_iota(jnp.int32, (q, q), 1)
    mask = (r_i <= c_i) if reverse else (r_i >= c_i)
    end = 0 if reverse else q - 1
    ys, xws, scales = [], [], []
    for h in range(r):
        cs_c = cs[:, h:h + 1]
        seg = jnp.exp(jnp.minimum(cs_c - cst[h:h + 1, :], 0.0))
        gmat = (cb * jnp.where(mask, seg, 0.0)).astype(BF16)
        xdt = x[:, h * hp:(h + 1) * hp] * dt[:, h:h + 1]
        yd = jnp.dot(gmat, xdt.astype(BF16), preferred_element_type=F32)
        ys.append(yd + jnp.exp(cs_c) * y_off[:, h * hp:(h + 1) * hp])
        cs_end = cs[end:end + 1, h:h + 1]
        xws.append((xdt * jnp.exp(cs_end - cs_c)).astype(BF16))
        scales.append(jnp.broadcast_to(jnp.exp(cs_end), (1, hp)))
    y_ref[...] = jnp.concatenate(ys, axis=1).astype(y_ref.dtype)
    upd = lax.dot_general(bm, jnp.concatenate(xws, axis=1), (((0,), (0,)), ((), ())),
                          preferred_element_type=F32)
    state[...] = state[...] * jnp.concatenate(scales, axis=1) + upd


def _ssd_scan(xbc, dt, cs, cst, n_lat, *, reverse):
    bsz, lt, ch = xbc.shape
    q = SSD_CHUNK
    nc, ncl = lt // q, n_lat // q
    ngr = SSD_GROUPS
    n = SSD_STATE
    d_inner = ch - 2 * ngr * n
    gw = d_inner // ngr
    xb, bb, cbk = 0, d_inner // n, d_inner // n + ngr
    if reverse:
        chunk = lambda k: nc - 1 - k
        doff = ngr
    else:
        chunk = lambda k: (k + ncl) % nc
        doff = 0
    kern = functools.partial(_ssd_scan_kernel, reverse=reverse)
    return pl.pallas_call(
        kern,
        grid=(bsz, ngr, nc),
        in_specs=[
            pl.BlockSpec((None, q, gw), lambda b, g, k: (b, chunk(k), g)),
            pl.BlockSpec((None, q, n), lambda b, g, k: (b, chunk(k), bb + g)),
            pl.BlockSpec((None, q, n), lambda b, g, k: (b, chunk(k), cbk + g)),
            pl.BlockSpec((None, None, q, 4), lambda b, g, k: (b, doff + g, chunk(k), 0)),
            pl.BlockSpec((None, None, q, 4), lambda b, g, k: (b, doff + g, chunk(k), 0)),
            pl.BlockSpec((None, None, None, 4, q), lambda b, g, k: (b, chunk(k), doff + g, 0, 0)),
        ],
        out_specs=pl.BlockSpec((None, q, gw), lambda b, g, k: (b, chunk(k), g)),
        out_shape=jax.ShapeDtypeStruct((bsz, lt, d_inner), BF16),
        scratch_shapes=[pltpu.VMEM((n, gw), F32)],
        compiler_params=_cparams("parallel", "parallel", "arbitrary"),
        name="ssd_scan_bwd" if reverse else "ssd_scan_fwd",
    )(xbc, xbc, xbc, dt, cs, cst)


def _ssd_finish_kernel(yf_ref, yb_ref, xs_ref, z_ref, x_ref, mod_ref, dsk_ref, ng_ref, w_ref, g2_ref,
                       rw_ref, rb_ref, x3_ref, hn2_ref, gates_ref):
    m = mod_ref[...]
    y = yf_ref[...].astype(F32) + yb_ref[...].astype(F32) + dsk_ref[...] * xs_ref[...].astype(F32)
    gated = y * _silu(z_ref[...].astype(F32))
    nrm = _rms(gated, ng_ref[...])
    out = jnp.dot(nrm.astype(BF16), w_ref[...], preferred_element_type=F32)
    x3 = x_ref[...] + m[2:3] * out
    x3_ref[...] = x3
    hn2 = _rms(x3, g2_ref[...]) * (1.0 + m[4:5]) + m[3:4]
    hn2_ref[...] = hn2.astype(hn2_ref.dtype)
    gates_ref[...] = _router_gates(hn2, rw_ref, rb_ref)


def _ssd_finish(yf, yb, xbc, z, xall, n_lat, mods, d_skip, norm_g, out_w, norm2_g, router_wt, router_b):
    bsz, lt, d_inner = z.shape
    d = xall.shape[-1]
    q = SSD_CHUNK
    ncl = n_lat // q
    n_exp = router_wt.shape[0]
    xview = xall.reshape(bsz, xall.shape[1] // GRID_W, GRID_W * d)
    inner = pl.BlockSpec((None, q, d_inner), lambda b, c: (b, c, 0))
    tok = pl.BlockSpec((None, q, d), lambda b, c: (b, c, 0))
    full = lambda s: pl.BlockSpec(s, lambda b, c: tuple(0 for _ in s))
    dsk = jnp.repeat(d_skip.astype(F32), SSD_HEAD_DIM).reshape(1, d_inner)
    return pl.pallas_call(
        _ssd_finish_kernel,
        grid=(bsz, ncl),
        in_specs=[
            inner, inner, inner, inner,
            pl.BlockSpec((None, q, d), lambda b, c: (b, 0, c)),
            pl.BlockSpec((None, 6, d), lambda b, c: (b * 2, 0, 0)),
            full((1, d_inner)), full((1, d_inner)), full((d_inner, d)), full((1, d)),
            full((n_exp, d)), full((n_exp, 1)),
        ],
        out_specs=[tok, tok, pl.BlockSpec((n_exp, q), lambda b, c: (0, b * ncl + c))],
        out_shape=[
            jax.ShapeDtypeStruct((bsz, n_lat, d), F32),
            jax.ShapeDtypeStruct((bsz, n_lat, d), BF16),
            jax.ShapeDtypeStruct((n_exp, bsz * n_lat), F32),
        ],
        compiler_params=_cparams("parallel", "parallel"),
        name="ssd_finish",
    )(yf, yb, xbc, z, xview, mods, dsk, norm_g.reshape(1, d_inner), out_w.astype(BF16),
      norm2_g.reshape(1, d), router_wt, router_b.reshape(n_exp, 1))


def kernel(x, c, ctx, c_ctx, mod_w, mod_b, norm1_g, norm2_g, final_g, s5_lam_re, s5_lam_im, s5_log_dt, s5_b_re, s5_b_im, s5_c_re, s5_c_im, s5_d, s5_glu_w, s5_glu_b, ssd_in_w, ssd_conv_w, ssd_conv_b, ssd_dt_bias, ssd_a_log, ssd_d, ssd_norm_g, ssd_out_w, router_w, router_b, moe_w1, moe_w3, moe_w2):
    bsz, n_lat, d = x.shape
    n_ctx = ctx.shape[1]
    lt = n_lat + n_ctx
    n_exp = router_w.shape[1]
    assert n_lat % TOKEN_TILE == 0 and n_ctx % TOKEN_TILE == 0
    assert TOKEN_TILE == MOE_TILE and MOE_TILE % SSD_CHUNK == 0

    mods = _modulation(c, c_ctx, mod_w, mod_b)
    router_wt = router_w.T.astype(F32)
    w1 = moe_w1.astype(BF16)
    w3 = moe_w3.astype(BF16)
    w2 = moe_w2.astype(BF16)
    nlt = n_lat // TOKEN_TILE
    tpb = lt // TOKEN_TILE

    xall = jnp.concatenate([x, ctx], axis=1)
    hn = _prenorm(xall, norm1_g[0], mods[0], nlt)
    s5w = _s5_weights(s5_lam_re[0], s5_lam_im[0], s5_log_dt[0], s5_b_re[0], s5_b_im[0], s5_c_re[0], s5_c_im[0])
    y = _s5_scan(hn, n_lat, s5w)
    x1, hn2, gates_t = _glu_head(y, hn, xall, mods[0], s5_d[0], s5_glu_w[0], s5_glu_b[0], norm2_g[0],
                                 router_wt, router_b, nlt)
    x2 = _moe(hn2.reshape(bsz * lt, d), gates_t.T, x1.reshape(bsz * lt, d), mods[0], w1[0], w3[0], w2[0],
              tiles_per_batch=tpb, n_lat_tiles=nlt).reshape(bsz, lt, d)

    z, xbc_pre, dt, cs, cst = _ssd_inproj(x2, n_lat, norm1_g[1], mods[1], ssd_in_w[0], ssd_dt_bias[0], ssd_a_log[0])
    xbc = _ssd_conv(xbc_pre, n_lat, ssd_conv_w[0], ssd_conv_b[0])
    yf = _ssd_scan(xbc, dt, cs, cst, n_lat, reverse=False)
    yb = _ssd_scan(xbc, dt, cs, cst, n_lat, reverse=True)
    x3, hn3, gates3_t = _ssd_finish(yf, yb, xbc, z, x2, n_lat, mods[1], ssd_d[0], ssd_norm_g[0], ssd_out_w[0],
                                    norm2_g[1], router_wt, router_b)
    out = _moe(hn3.reshape(bsz * n_lat, d), gates3_t.T, x3.reshape(bsz * n_lat, d), mods[1], w1[1], w3[1], w2[1],
               tiles_per_batch=nlt, n_lat_tiles=nlt, final_g=final_g)
    return out.reshape(bsz, n_lat, d)
```

```python
import functools

import jax
import jax.numpy as jnp
from jax import lax
from jax.experimental import pallas as pl
from jax.experimental.pallas import tpu as pltpu

F32 = jnp.float32
BF16 = jnp.bfloat16
HIGHEST = lax.Precision.HIGHEST

GRID_W = 64
RMS_EPS = 1e-6

S5_GROUP = 16
S5_STATE = 64
S5_T = 16
S5_GB = 8

SSD_HEAD_DIM = 64
SSD_GROUPS = 8
SSD_STATE = 128
SSD_CONV = 5
SSD_CHUNK = 128

N_EXPERT_GROUPS = 4
TOP_K = 2

TOKEN_TILE = 256
MOE_TILE = 256
VMEM_LIMIT_BYTES = 56 * 1024 * 1024


def _cparams(*sem):
    return pltpu.CompilerParams(dimension_semantics=sem, vmem_limit_bytes=VMEM_LIMIT_BYTES)


def _sigmoid(v):
    return 1.0 / (1.0 + jnp.exp(-v))


def _silu(v):
    return v * _sigmoid(v)


def _gelu_tanh(v):
    return 0.5 * v * (1.0 + jnp.tanh(0.7978845608028654 * (v + 0.044715 * (v * v * v))))


def _rms(v, g):
    return v * lax.rsqrt(jnp.mean(v * v, axis=-1, keepdims=True) + RMS_EPS) * g


def _mod_kernel(cc_ref, w_ref, b_ref, o_ref):
    a = _silu(cc_ref[...])
    o_ref[...] = jnp.dot(a, w_ref[...], preferred_element_type=F32, precision=HIGHEST) + b_ref[...]


def _modulation(c, c_ctx, mod_w, mod_b):
    depth, d, d6 = mod_w.shape
    bsz = c.shape[0]
    rows = 8
    cc = jnp.zeros((rows, d), F32).at[:bsz].set(c).at[bsz].set(c_ctx)
    tn = d6 // 4
    out = pl.pallas_call(
        _mod_kernel,
        grid=(depth, d6 // tn),
        in_specs=[
            pl.BlockSpec((rows, d), lambda i, j: (0, 0)),
            pl.BlockSpec((None, d, tn), lambda i, j: (i, 0, j)),
            pl.BlockSpec((None, 1, tn), lambda i, j: (i, 0, j)),
        ],
        out_specs=pl.BlockSpec((None, rows, tn), lambda i, j: (i, 0, j)),
        out_shape=jax.ShapeDtypeStruct((depth, rows, d6), F32),
        compiler_params=_cparams("parallel", "parallel"),
        name="modulation",
    )(cc, mod_w, mod_b.reshape(depth, 1, d6))
    lat = out[:, :bsz].reshape(depth, bsz, 1, 6, d)
    ctx = jnp.broadcast_to(out[:, bsz].reshape(depth, 1, 1, 6, d), (depth, bsz, 1, 6, d))
    return jnp.concatenate([lat, ctx], axis=2).reshape(depth, bsz * 2, 6, d)


def _prenorm_kernel(x_ref, g_ref, mod_ref, o_ref):
    m = mod_ref[...]
    hn = _rms(x_ref[...], g_ref[...]) * (1.0 + m[1:2]) + m[0:1]
    o_ref[...] = hn.astype(o_ref.dtype)


def _prenorm(xall, g, mods, n_lat_tiles):
    bsz, lt, d = xall.shape
    nt = lt // TOKEN_TILE
    return pl.pallas_call(
        _prenorm_kernel,
        grid=(bsz, nt),
        in_specs=[
            pl.BlockSpec((None, TOKEN_TILE, d), lambda b, i: (b, i, 0)),
            pl.BlockSpec((1, d), lambda b, i: (0, 0)),
            pl.BlockSpec((None, 6, d), lambda b, i: (b * 2 + (i >= n_lat_tiles).astype(jnp.int32), 0, 0)),
        ],
        out_specs=pl.BlockSpec((None, TOKEN_TILE, d), lambda b, i: (b, i, 0)),
        out_shape=jax.ShapeDtypeStruct((bsz, lt, d), F32),
        compiler_params=_cparams("parallel", "parallel"),
        name="prenorm",
    )(xall, g.reshape(1, d), mods)


def _s5_weights(lam_re, lam_im, log_dt, b_re, b_im, c_re, c_im):
    t = S5_T
    ngrp, p = lam_re.shape[1], lam_re.shape[2]
    k16 = b_re.shape[-1]
    tau = jnp.arange(t + 1, dtype=F32)[None, :, None]

    def direction(k):
        lr, li = lam_re[k], lam_im[k]
        step = jnp.exp(log_dt[k])[:, None]
        mag = jnp.exp(lr * step)
        abar_r = mag * jnp.cos(li * step)
        abar_i = mag * jnp.sin(li * step)
        den = lr * lr + li * li
        q_r = ((abar_r - 1.0) * lr + abar_i * li) / den
        q_i = (abar_i * lr - (abar_r - 1.0) * li) / den
        bb_r = q_r[..., None] * b_re - q_i[..., None] * b_im
        bb_i = q_r[..., None] * b_im + q_i[..., None] * b_re
        pmag = jnp.exp((lr * step)[:, None, :] * tau)
        pw_r = pmag * jnp.cos((li * step)[:, None, :] * tau)
        pw_i = pmag * jnp.sin((li * step)[:, None, :] * tau)
        return bb_r, bb_i, pw_r, pw_i

    def cplx_kernel(pw_r, pw_i, bb_r, bb_i):
        m_r = pw_r[:, :t, :, None] * bb_r[:, None] - pw_i[:, :t, :, None] * bb_i[:, None]
        m_i = pw_r[:, :t, :, None] * bb_i[:, None] + pw_i[:, :t, :, None] * bb_r[:, None]
        return jnp.sum(c_re[:, None, :, :, None] * m_r[:, :, None] - c_im[:, None, :, :, None] * m_i[:, :, None], axis=3)

    bf_r, bf_i, pf_r, pf_i = direction(0)
    bb_r, bb_i, pb_r, pb_i = direction(1)
    kf = cplx_kernel(pf_r, pf_i, bf_r, bf_i)
    kb = cplx_kernel(pb_r, pb_i, bb_r, bb_i)

    nopad = ((0, 0), (0, 0), (0, 0))
    rows = [jnp.pad(kf[:, :t - s], ((0, 0), (s, 0)) + nopad[1:])
            + jnp.pad(kb[:, :s + 1][:, ::-1], ((0, 0), (0, t - 1 - s)) + nopad[1:]) for s in range(t)]
    wm = jnp.stack(rows, axis=1).transpose(0, 1, 4, 2, 3).reshape(ngrp, t * k16, t * k16)

    def state_in(pw_r, pw_i, b_r, b_i, order):
        pr, pi = pw_r[:, order], pw_i[:, order]
        w_r = pr[..., None] * b_r[:, None] - pi[..., None] * b_i[:, None]
        w_i = pr[..., None] * b_i[:, None] + pi[..., None] * b_r[:, None]
        tr = lambda w: w.transpose(0, 1, 3, 2).reshape(ngrp, t * k16, p)
        return tr(w_r), tr(w_i)

    sf_r, sf_i = state_in(pf_r, pf_i, bf_r, bf_i, jnp.arange(t - 1, -1, -1))
    sb_r, sb_i = state_in(pb_r, pb_i, bb_r, bb_i, jnp.arange(t))
    ws = jnp.concatenate([sf_r, sb_r, sf_i, sb_i], axis=-1)

    def state_out(pw_r, pw_i, order):
        pr, pi = pw_r[:, order], pw_i[:, order]
        o_r = c_re[:, None] * pr[:, :, None, :] - c_im[:, None] * pi[:, :, None, :]
        o_i = -(c_re[:, None] * pi[:, :, None, :] + c_im[:, None] * pr[:, :, None, :])
        tr = lambda w: w.transpose(0, 3, 1, 2).reshape(ngrp, p, t * k16)
        return tr(o_r), tr(o_i)

    of_r, of_i = state_out(pf_r, pf_i, jnp.arange(1, t + 1))
    ob_r, ob_i = state_out(pb_r, pb_i, jnp.arange(t, 0, -1))
    zero = jnp.zeros_like(of_r)
    w2 = jnp.concatenate([of_r, zero, of_i, zero, zero, ob_r, zero, ob_i], axis=1)

    ar = jnp.concatenate([pf_r[:, t], pb_r[:, t]], axis=-1)
    ai = jnp.concatenate([pf_i[:, t], pb_i[:, t]], axis=-1)
    return ws.astype(BF16), wm.astype(BF16), w2.astype(BF16), ar, ai


def _s5_row_block(nc):
    return max(rb for rb in range(16, min(nc, 176) + 1, 16) if nc % rb == 0)


def _s5_kernel(hn_ref, ws_ref, wm_ref, w2_ref, ar_ref, ai_ref, yo_ref,
               u_ref, y_ref, sre, sim, hre_f, him_f, hre_b, him_b, *, n_chunks, n_ctx_chunks, pitch):
    nc, ncc = n_chunks, n_ctx_chunks
    ncl = nc - ncc
    p = S5_STATE
    t_len = S5_T
    gl = S5_GROUP
    per_half = 128 // gl
    rb = _s5_row_block(nc)
    lane_slot = lax.broadcasted_iota(jnp.int32, (rb, 128), 1) // gl

    def gather_u(blk, carry):
        r0 = pl.multiple_of(blk * rb, 16)
        a = [hn_ref[pl.ds(r0 * t_len + t, rb, stride=t_len), :] for t in range(t_len)]
        for i in range(S5_GB):
            halves = []
            for hb in range(t_len // per_half):
                acc = None
                for j in range(per_half):
                    src = a[per_half * hb + j]
                    shift = (gl * (j - i)) % 128
                    rolled = src if shift == 0 else pltpu.roll(src, shift, axis=1)
                    acc = rolled if acc is None else jnp.where(lane_slot == j, rolled, acc)
                halves.append(acc)
            u_ref[i, pl.ds(r0, rb), :] = jnp.concatenate(halves, axis=1).astype(u_ref.dtype)
        return carry

    lax.fori_loop(0, nc // rb, gather_u, 0)

    for g in range(S5_GB):
        s = jnp.dot(u_ref[g], ws_ref[g], preferred_element_type=F32)
        sre[pl.ds(g * pitch, nc), :] = s[:, : 2 * p]
        sim[pl.ds(g * pitch, nc), :] = s[:, 2 * p:]

    ar = ar_ref[...]
    ai = ai_ref[...]
    fwd_lane = lax.broadcasted_iota(jnp.int32, (S5_GB, 2 * p), 1) < p

    def step(k, carry):
        h_r, h_i = carry
        cf = jnp.where(k < ncc, ncl + k, k - ncc)
        cb = nc - 1 - k
        rows_f = pl.ds(cf, S5_GB, stride=pitch)
        rows_b = pl.ds(cb, S5_GB, stride=pitch)
        hre_f[rows_f, :] = h_r
        him_f[rows_f, :] = h_i
        hre_b[rows_b, :] = h_r
        him_b[rows_b, :] = h_i
        s_r = jnp.where(fwd_lane, sre[rows_f, :], sre[rows_b, :])
        s_i = jnp.where(fwd_lane, sim[rows_f, :], sim[rows_b, :])
        n_r = ar * h_r - ai * h_i + s_r
        n_i = ar * h_i + ai * h_r + s_i
        return n_r, n_i

    zero = jnp.zeros((S5_GB, 2 * p), F32)
    lax.fori_loop(0, nc, step, (zero, zero))

    for g in range(S5_GB):
        rows = pl.ds(g * pitch, nc)
        hin = jnp.concatenate([hre_f[rows, :], him_f[rows, :], hre_b[rows, :], him_b[rows, :]], axis=1)
        out = jnp.dot(u_ref[g], wm_ref[g], preferred_element_type=F32)
        out = out + jnp.dot(hin.astype(BF16), w2_ref[g], preferred_element_type=F32)
        y_ref[g] = out

    def scatter_y(blk, carry):
        r0 = pl.multiple_of(blk * rb, 16)
        yv = [y_ref[i, pl.ds(r0, rb), :] for i in range(S5_GB)]
        for t in range(t_len):
            hb, j = divmod(t, per_half)
            acc = None
            for i in range(S5_GB):
                src = yv[i][:, hb * 128:(hb + 1) * 128]
                shift = (gl * (i - j)) % 128
                rolled = src if shift == 0 else pltpu.roll(src, shift, axis=1)
                acc = rolled if acc is None else jnp.where(lane_slot == i, rolled, acc)
            yo_ref[pl.ds(r0 * t_len + t, rb, stride=t_len), :] = acc
        return carry

    lax.fori_loop(0, nc // rb, scatter_y, 0)


def _s5_scan(hn, n_lat, weights):
    bsz, lt, d = hn.shape
    ngrp = d // S5_GROUP
    t = S5_T
    tk = t * S5_GROUP
    nc = lt // t
    ncc = (lt - n_lat) // t
    gb = S5_GB
    assert gb * S5_GROUP == 128 and (128 // S5_GROUP) == gb and t % gb == 0
    ws, wm, w2, ar, ai = weights
    pitch = nc + 8 if (nc // 8) % 2 == 0 else nc
    kern = functools.partial(_s5_kernel, n_chunks=nc, n_ctx_chunks=ncc, pitch=pitch)
    return pl.pallas_call(
        kern,
        grid=(ngrp // gb, bsz),
        in_specs=[
            pl.BlockSpec((None, lt, 128), lambda gi, b: (b, 0, gi)),
            pl.BlockSpec((gb, tk, 4 * S5_STATE), lambda gi, b: (gi, 0, 0)),
            pl.BlockSpec((gb, tk, tk), lambda gi, b: (gi, 0, 0)),
            pl.BlockSpec((gb, 8 * S5_STATE, tk), lambda gi, b: (gi, 0, 0)),
            pl.BlockSpec((gb, 2 * S5_STATE), lambda gi, b: (gi, 0)),
            pl.BlockSpec((gb, 2 * S5_STATE), lambda gi, b: (gi, 0)),
        ],
        out_specs=pl.BlockSpec((None, lt, 128), lambda gi, b: (b, 0, gi)),
        out_shape=jax.ShapeDtypeStruct((bsz, lt, d), F32),
        scratch_shapes=[pltpu.VMEM((gb, nc, tk), BF16), pltpu.VMEM((gb, nc, tk), F32)]
        + [pltpu.VMEM((gb * pitch, 2 * S5_STATE), F32) for _ in range(6)],
        compiler_params=_cparams("parallel", "parallel"),
        name="s5_scan",
    )(hn, ws, wm, w2, ar, ai)


def _router_gates(hn2, rw_ref, rb_ref):
    n_exp = rw_ref.shape[0]
    epg = n_exp // N_EXPERT_GROUPS
    logits = lax.dot_general(rw_ref[...], hn2, (((1,), (1,)), ((), ())),
                             preferred_element_type=F32, precision=HIGHEST)
    s = _sigmoid(logits)
    sel = s + rb_ref[...]
    row = [sel[e:e + 1] for e in range(n_exp)]
    gscore = []
    for gi in range(N_EXPERT_GROUPS):
        a, b, c, dd = row[gi * epg: gi * epg + epg]
        hi1, lo1 = jnp.maximum(a, b), jnp.minimum(a, b)
        hi2, lo2 = jnp.maximum(c, dd), jnp.minimum(c, dd)
        gscore.append(jnp.maximum(hi1, hi2) + jnp.maximum(jnp.minimum(hi1, hi2), jnp.maximum(lo1, lo2)))
    gmax = functools.reduce(jnp.maximum, gscore)
    gates = []
    taken = None
    for gi in range(N_EXPERT_GROUPS):
        is_max = gscore[gi] == gmax
        best = is_max if taken is None else jnp.logical_and(is_max, jnp.logical_not(taken))
        taken = is_max if taken is None else jnp.logical_or(taken, is_max)
        for e in range(gi * epg, gi * epg + epg):
            rank = jnp.zeros_like(row[e])
            for j in range(gi * epg, gi * epg + epg):
                if j == e:
                    continue
                ahead = (row[j] >= row[e]) if j < e else (row[j] > row[e])
                rank = rank + ahead.astype(F32)
            chosen = jnp.logical_and(best, rank < float(TOP_K))
            gates.append(jnp.where(chosen, s[e:e + 1], 0.0))
    g = jnp.concatenate(gates, axis=0)
    return g / jnp.sum(g, axis=0, keepdims=True)


def _glu_kernel(y_ref, u_ref, x_ref, mod_ref, d_ref, w_ref, b_ref, g2_ref, rw_ref, rb_ref,
                x1_ref, hn2_ref, gates_ref):
    d = x_ref.shape[-1]
    m = mod_ref[...]
    u = u_ref[...].astype(F32)
    a = _gelu_tanh(y_ref[...].astype(F32) + d_ref[...] * u)
    z = jnp.dot(a.astype(BF16), w_ref[...], preferred_element_type=F32) + b_ref[...]
    out = z[:, :d] * _sigmoid(z[:, d:])
    x1 = x_ref[...] + m[2:3] * out
    x1_ref[...] = x1
    hn2 = _rms(x1, g2_ref[...]) * (1.0 + m[4:5]) + m[3:4]
    hn2_ref[...] = hn2.astype(hn2_ref.dtype)
    gates_ref[...] = _router_gates(hn2, rw_ref, rb_ref)


def _glu_head(y, hn, xall, mods, d_skip, glu_w, glu_b, norm2_g, router_wt, router_b, n_lat_tiles):
    bsz, lt, d = xall.shape
    nt = lt // TOKEN_TILE
    n_exp = router_wt.shape[0]
    tok = pl.BlockSpec((None, TOKEN_TILE, d), lambda b, i: (b, i, 0))
    vec = lambda n: pl.BlockSpec((1, n), lambda b, i: (0, 0))
    return pl.pallas_call(
        _glu_kernel,
        grid=(bsz, nt),
        in_specs=[
            tok, tok, tok,
            pl.BlockSpec((None, 6, d), lambda b, i: (b * 2 + (i >= n_lat_tiles).astype(jnp.int32), 0, 0)),
            vec(d),
            pl.BlockSpec((d, 2 * d), lambda b, i: (0, 0)),
            vec(2 * d),
            vec(d),
            pl.BlockSpec((n_exp, d), lambda b, i: (0, 0)),
            pl.BlockSpec((n_exp, 1), lambda b, i: (0, 0)),
        ],
        out_specs=[
            tok, tok,
            pl.BlockSpec((n_exp, TOKEN_TILE), lambda b, i: (0, b * nt + i)),
        ],
        out_shape=[
            jax.ShapeDtypeStruct((bsz, lt, d), F32),
            jax.ShapeDtypeStruct((bsz, lt, d), BF16),
            jax.ShapeDtypeStruct((n_exp, bsz * lt), F32),
        ],
        compiler_params=_cparams("parallel", "parallel"),
        name="s5_glu_head",
    )(y, hn, xall, mods, d_skip.reshape(1, d), glu_w.astype(BF16), glu_b.reshape(1, 2 * d),
      norm2_g.reshape(1, d), router_wt, router_b.reshape(n_exp, 1))


def _moe_kernel(t_ref, gates_ref, x_ref, mod_ref, w1_ref, w3_ref, w2_ref, *rest, final, out_cols):
    if final:
        fg_ref, o_ref, acc = rest
    else:
        o_ref, acc = rest
    e = pl.program_id(1)

    @pl.when(e == 0)
    def _():
        acc[...] = jnp.zeros_like(acc)

    t = t_ref[...]
    h = _silu(jnp.dot(t, w1_ref[...], preferred_element_type=F32)) * jnp.dot(t, w3_ref[...], preferred_element_type=F32)
    ye = jnp.dot(h.astype(BF16), w2_ref[...], preferred_element_type=F32)
    gates = gates_ref[...]
    lane = lax.broadcasted_iota(jnp.int32, gates.shape, 1)
    gcol = jnp.sum(jnp.where(lane == e, gates, 0.0), axis=1, keepdims=True)
    acc[...] += gcol * ye

    @pl.when(e == pl.num_programs(1) - 1)
    def _():
        m = mod_ref[...]
        xn = x_ref[...] + m[5:6] * acc[...]
        if final:
            xn = _rms(xn, fg_ref[...])
            for j in range(out_cols):
                o_ref[:, j * xn.shape[1]:(j + 1) * xn.shape[1]] = xn[j * SSD_CHUNK:(j + 1) * SSD_CHUNK]
        else:
            o_ref[...] = xn


def _moe(t, gates, xres, mods, w1, w3, w2, *, tiles_per_batch, n_lat_tiles, final_g=None):
    n, d = t.shape
    n_exp, _, f = w1.shape
    tm = MOE_TILE
    final = final_g is not None
    tok = pl.BlockSpec((tm, d), lambda i, e: (i, 0))

    def mod_map(i, e):
        seg = ((i % tiles_per_batch) >= n_lat_tiles).astype(jnp.int32)
        return ((i // tiles_per_batch) * 2 + seg, 0, 0)

    in_specs = [
        tok,
        pl.BlockSpec((tm, n_exp), lambda i, e: (i, 0)),
        tok,
        pl.BlockSpec((None, 6, d), mod_map),
        pl.BlockSpec((None, d, f), lambda i, e: (e, 0, 0)),
        pl.BlockSpec((None, d, f), lambda i, e: (e, 0, 0)),
        pl.BlockSpec((None, f, d), lambda i, e: (e, 0, 0)),
    ]
    args = [t, gates, xres, mods, w1, w3, w2]
    out_cols = tm // SSD_CHUNK
    if final:
        in_specs.append(pl.BlockSpec((1, d), lambda i, e: (0, 0)))
        args.append(final_g.reshape(1, d))
        bsz = n // (tiles_per_batch * tm)
        out_spec = pl.BlockSpec((None, SSD_CHUNK, out_cols * d),
                                lambda i, e: (i // tiles_per_batch, 0, i % tiles_per_batch))
        out_shape = jax.ShapeDtypeStruct((bsz, SSD_CHUNK, GRID_W * d), F32)
    else:
        out_spec = tok
        out_shape = jax.ShapeDtypeStruct((n, d), F32)
    kern = functools.partial(_moe_kernel, final=final, out_cols=out_cols)
    return pl.pallas_call(
        kern,
        grid=(n // tm, n_exp),
        in_specs=in_specs,
        out_specs=out_spec,
        out_shape=out_shape,
        scratch_shapes=[pltpu.VMEM((tm, d), F32)],
        compiler_params=_cparams("parallel", "arbitrary"),
        name="moe_final" if final else "moe",
    )(*args)


def _ssd_inproj_kernel(xl_ref, xc_ref, g_ref, mod_ref, wz_ref, wx_ref, wdt_ref, wdtt_ref, bias_ref, biast_ref,
                       a_ref, at_ref, z_ref, xbc_ref, dt_ref, cs_ref, cst_ref, xt, *, n_lat_chunks):
    c = pl.program_id(1)

    @pl.when(c < n_lat_chunks)
    def _():
        xt[...] = xl_ref[...]

    @pl.when(c >= n_lat_chunks)
    def _():
        xt[...] = xc_ref[...]

    m = mod_ref[...]
    hn = (_rms(xt[...], g_ref[...]) * (1.0 + m[1:2]) + m[0:1]).astype(BF16)
    z_ref[...] = jnp.dot(hn, wz_ref[...], preferred_element_type=F32).astype(z_ref.dtype)
    xbc_ref[...] = jnp.dot(hn, wx_ref[...], preferred_element_type=F32).astype(xbc_ref.dtype)

    def softplus(v):
        return jnp.maximum(v, 0.0) + jnp.log(1.0 + jnp.exp(-jnp.abs(v)))

    q = SSD_CHUNK
    r_i = lax.broadcasted_iota(jnp.int32, (q, q), 0)
    c_i = lax.broadcasted_iota(jnp.int32, (q, q), 1)
    nh = a_ref.shape[1] // 2
    dt = softplus(jnp.dot(hn, wdt_ref[...], preferred_element_type=F32) + bias_ref[...])
    da = dt * a_ref[...]
    lower = (r_i >= c_i).astype(F32)
    upper = (r_i <= c_i).astype(F32)
    cs_f = jnp.dot(lower, da[:, :nh], preferred_element_type=F32, precision=HIGHEST)
    cs_b = jnp.dot(upper, da[:, nh:], preferred_element_type=F32, precision=HIGHEST)
    cs = jnp.concatenate([cs_f, cs_b], axis=1)
    dtt = softplus(lax.dot_general(wdtt_ref[...], hn, (((1,), (1,)), ((), ())), preferred_element_type=F32)
                   + biast_ref[...])
    dat = dtt * at_ref[...]
    cst_f = jnp.dot(dat[:nh], upper, preferred_element_type=F32, precision=HIGHEST)
    cst_b = jnp.dot(dat[nh:], lower, preferred_element_type=F32, precision=HIGHEST)
    cst = jnp.concatenate([cst_f, cst_b], axis=0)
    r = 4
    for j in range(dt_ref.shape[0]):
        dt_ref[j] = dt[:, j * r:(j + 1) * r]
        cs_ref[j] = cs[:, j * r:(j + 1) * r]
        cst_ref[j] = cst[j * r:(j + 1) * r, :]


def _ssd_inproj(xall, n_lat, norm_g, mods, in_w, dt_bias, a_log):
    bsz, lt, d = xall.shape
    q = SSD_CHUNK
    ncl = n_lat // q
    nc = lt // q
    nh2 = dt_bias.size
    d_inner = (nh2 // 2) * SSD_HEAD_DIM
    conv_ch = in_w.shape[1] - d_inner - nh2
    wz = in_w[:, :d_inner].astype(BF16)
    wx = in_w[:, d_inner:d_inner + conv_ch].astype(BF16)
    wdt = in_w[:, d_inner + conv_ch:].astype(BF16)
    a = -jnp.exp(a_log.astype(F32)).reshape(1, nh2)
    bias = dt_bias.astype(F32).reshape(1, nh2)
    ngr = nh2 // 4
    assert n_lat // GRID_W == q, "one SSD chunk per latent grid column"
    xview = xall.reshape(bsz, lt // GRID_W, GRID_W * d)
    kern = functools.partial(_ssd_inproj_kernel, n_lat_chunks=ncl)
    full = lambda s: pl.BlockSpec(s, lambda b, c: tuple(0 for _ in s))
    return pl.pallas_call(
        kern,
        grid=(bsz, nc),
        in_specs=[
            pl.BlockSpec((None, q, d), lambda b, c: (b, 0, jnp.minimum(c, ncl - 1))),
            pl.BlockSpec((None, q, d), lambda b, c: (b, jnp.maximum(c, ncl), 0)),
            full((1, d)),
            pl.BlockSpec((None, 6, d), lambda b, c: (b * 2 + (c >= ncl).astype(jnp.int32), 0, 0)),
            full((d, d_inner)), full((d, conv_ch)), full((d, nh2)), full((nh2, d)),
            full((1, nh2)), full((nh2, 1)), full((1, nh2)), full((nh2, 1)),
        ],
        out_specs=[
            pl.BlockSpec((None, q, d_inner), lambda b, c: (b, c, 0)),
            pl.BlockSpec((None, q, conv_ch), lambda b, c: (b, c, 0)),
            pl.BlockSpec((None, ngr, q, 4), lambda b, c: (b, 0, c, 0)),
            pl.BlockSpec((None, ngr, q, 4), lambda b, c: (b, 0, c, 0)),
            pl.BlockSpec((None, None, ngr, 4, q), lambda b, c: (b, c, 0, 0, 0)),
        ],
        out_shape=[
            jax.ShapeDtypeStruct((bsz, lt, d_inner), BF16),
            jax.ShapeDtypeStruct((bsz, lt, conv_ch), BF16),
            jax.ShapeDtypeStruct((bsz, ngr, lt, 4), F32),
            jax.ShapeDtypeStruct((bsz, ngr, lt, 4), F32),
            jax.ShapeDtypeStruct((bsz, nc, ngr, 4, q), F32),
        ],
        scratch_shapes=[pltpu.VMEM((q, d), F32)],
        compiler_params=_cparams("parallel", "parallel"),
        name="ssd_inproj",
    )(xview, xall, norm_g.reshape(1, d), mods, wz, wx, wdt, wdt.T, bias, bias.reshape(nh2, 1),
      a, a.reshape(nh2, 1))


def _ssd_conv_kernel(cur_ref, prev_ref, next_ref, w_ref, b_ref, o_ref, ext, *, n_lat_chunks, n_chunks):
    c = pl.program_id(1)
    q = SSD_CHUNK
    halo = prev_ref.shape[0]
    first = jnp.logical_or(c == 0, c == n_lat_chunks)
    last = jnp.logical_or(c == n_lat_chunks - 1, c == n_chunks - 1)
    prev = prev_ref[...].astype(F32)
    nxt = next_ref[...].astype(F32)
    ext[0:8, :] = jnp.where(first, 0.0, prev[halo - 8:halo])
    ext[8:8 + q, :] = cur_ref[...].astype(F32)
    ext[8 + q:16 + q, :] = jnp.where(last, 0.0, nxt[0:8])
    acc = jnp.zeros(o_ref.shape, F32) + b_ref[...]
    w = w_ref[...]
    for k in range(SSD_CONV):
        acc = acc + w[k:k + 1] * ext[pl.ds(8 - SSD_CONV // 2 + k, q), :]
    o_ref[...] = _silu(acc).astype(o_ref.dtype)


def _ssd_conv(xbc, n_lat, conv_w, conv_b):
    bsz, lt, ch = xbc.shape
    q = SSD_CHUNK
    nc, ncl = lt // q, n_lat // q
    halo = 16
    hb = q // halo
    cw = ch
    kern = functools.partial(_ssd_conv_kernel, n_lat_chunks=ncl, n_chunks=nc)
    return pl.pallas_call(
        kern,
        grid=(bsz, nc, ch // cw),
        in_specs=[
            pl.BlockSpec((None, q, cw), lambda b, c, j: (b, c, j)),
            pl.BlockSpec((None, halo, cw), lambda b, c, j: (b, jnp.maximum(c * hb - 1, 0), j)),
            pl.BlockSpec((None, halo, cw), lambda b, c, j: (b, jnp.minimum((c + 1) * hb, nc * hb - 1), j)),
            pl.BlockSpec((SSD_CONV, cw), lambda b, c, j: (0, j)),
            pl.BlockSpec((1, cw), lambda b, c, j: (0, j)),
        ],
        out_specs=pl.BlockSpec((None, q, cw), lambda b, c, j: (b, c, j)),
        out_shape=jax.ShapeDtypeStruct((bsz, lt, ch), BF16),
        scratch_shapes=[pltpu.VMEM((q + 16, cw), F32)],
        compiler_params=_cparams("parallel", "parallel", "parallel"),
        name="ssd_conv",
    )(xbc, xbc, xbc, conv_w.astype(F32), conv_b.astype(F32).reshape(1, ch))


def _ssd_scan_dir(x_ref, b_ref, c_ref, dt_ref, cs_ref, cst_ref, y_ref, state, reverse):
    q = SSD_CHUNK
    hp = SSD_HEAD_DIM
    gw = x_ref.shape[1]
    r = gw // hp
    x = x_ref[...].astype(F32)
    bm = b_ref[...]
    cm = c_ref[...]
    dt = dt_ref[...]
    cs = cs_ref[...]
    cst = cst_ref[...]
    v = jnp.concatenate([cs, dt], axis=1)
    v_hi = v.astype(BF16)
    v_r1 = v - v_hi.astype(F32)
    v_mid = v_r1.astype(BF16)
    v_lo = (v_r1 - v_mid.astype(F32)).astype(BF16)
    lhs = jnp.concatenate([v_hi, v_mid, v_lo], axis=1)
    wide = 2 * gw + r * q
    col = lax.broadcasted_iota(jnp.int32, (6 * r, wide), 0) % (2 * r)
    ln = lax.broadcasted_iota(jnp.int32, (6 * r, wide), 1)
    cs_hit = jnp.logical_or(jnp.logical_and(ln < gw, ln // hp == col),
                            jnp.logical_and(ln >= 2 * gw, (ln - 2 * gw) // q == col))
    dt_hit = jnp.logical_and(jnp.logical_and(ln >= gw, ln < 2 * gw), (ln - gw) // hp == col - r)
    hit = jnp.logical_or(jnp.logical_and(col < r, cs_hit), jnp.logical_and(col >= r, dt_hit))
    spread = jnp.where(hit, 1.0, 0.0).astype(BF16)
    spread_out = jnp.dot(lhs, spread, preferred_element_type=F32)
    cs_x = spread_out[:, :gw]
    dt_x = spread_out[:, gw:2 * gw]
    cs_q = spread_out[:, 2 * gw:]
    end = 0 if reverse else q - 1
    cs_end = cs_x[end:end + 1, :]
    xdt = x * dt_x
    xw = (xdt * jnp.exp(cs_end - cs_x)).astype(BF16)
    xdt = xdt.astype(BF16)
    cb = lax.dot_general(cm, bm, (((1,), (1,)), ((), ())), preferred_element_type=F32)
    y_off = jnp.dot(cm, state[...].astype(BF16), preferred_element_type=F32) * jnp.exp(cs_x)
    r_i = lax.broadcasted_iota(jnp.int32, (q, q), 0)
    c_i = lax.broadcasted_iota(jnp.int32, (q, q), 1)
    mask = (r_i <= c_i) if reverse else (r_i >= c_i)
    lane = lax.broadcasted_iota(jnp.int32, (q, 2 * hp), 1)
    ys = []
    for pair in range(r // 2):
        xp = xdt[:, pair * 2 * hp:(pair + 1) * 2 * hp]
        yd = []
        for h in (2 * pair, 2 * pair + 1):
            seg = jnp.exp(jnp.minimum(cs_q[:, h * q:(h + 1) * q] - cst[h:h + 1, :], 0.0))
            gmat = (cb * jnp.where(mask, seg, 0.0)).astype(BF16)
            yd.append(jnp.dot(gmat, xp, preferred_element_type=F32))
        ys.append(jnp.where(lane < hp, yd[0], yd[1]))
    y_ref[...] = (jnp.concatenate(ys, axis=1) + y_off).astype(y_ref.dtype)
    upd = lax.dot_general(bm, xw, (((0,), (0,)), ((), ())), preferred_element_type=F32)
    state[...] = state[...] * jnp.exp(cs_end) + upd


def _ssd_scan_kernel(xf, bf, cf, dtf, csf, cstf, xb, bb, cb, dtb, csb, cstb, yf_ref, yb_ref, state_f, state_b):
    @pl.when(pl.program_id(2) == 0)
    def _():
        state_f[...] = jnp.zeros_like(state_f)
        state_b[...] = jnp.zeros_like(state_b)

    _ssd_scan_dir(xf, bf, cf, dtf, csf, cstf, yf_ref, state_f, False)
    _ssd_scan_dir(xb, bb, cb, dtb, csb, cstb, yb_ref, state_b, True)


def _ssd_scan(xbc, dt, cs, cst, n_lat):
    bsz, lt, ch = xbc.shape
    q = SSD_CHUNK
    nc, ncl = lt // q, n_lat // q
    ngr = SSD_GROUPS
    n = SSD_STATE
    d_inner = ch - 2 * ngr * n
    gw = d_inner // ngr
    bcol, ccol = d_inner // n, d_inner // n + ngr

    def specs(chunk, doff):
        return [
            pl.BlockSpec((None, q, gw), lambda b, g, k: (b, chunk(k), g)),
            pl.BlockSpec((None, q, n), lambda b, g, k: (b, chunk(k), bcol + g)),
            pl.BlockSpec((None, q, n), lambda b, g, k: (b, chunk(k), ccol + g)),
            pl.BlockSpec((None, None, q, 4), lambda b, g, k: (b, doff + g, chunk(k), 0)),
            pl.BlockSpec((None, None, q, 4), lambda b, g, k: (b, doff + g, chunk(k), 0)),
            pl.BlockSpec((None, None, None, 4, q), lambda b, g, k: (b, chunk(k), doff + g, 0, 0)),
        ]

    fwd = lambda k: (k + ncl) % nc
    bwd = lambda k: nc - 1 - k
    out = jax.ShapeDtypeStruct((bsz, lt, d_inner), BF16)
    return pl.pallas_call(
        _ssd_scan_kernel,
        grid=(bsz, ngr, nc),
        in_specs=specs(fwd, 0) + specs(bwd, ngr),
        out_specs=[pl.BlockSpec((None, q, gw), lambda b, g, k: (b, fwd(k), g)),
                   pl.BlockSpec((None, q, gw), lambda b, g, k: (b, bwd(k), g))],
        out_shape=[out, out],
        scratch_shapes=[pltpu.VMEM((n, gw), F32), pltpu.VMEM((n, gw), F32)],
        compiler_params=_cparams("parallel", "parallel", "arbitrary"),
        name="ssd_scan",
    )(xbc, xbc, xbc, dt, cs, cst, xbc, xbc, xbc, dt, cs, cst)


def _ssd_finish_kernel(yf_ref, yb_ref, xs_ref, z_ref, x_ref, mod_ref, dsk_ref, ng_ref, w_ref, g2_ref,
                       rw_ref, rb_ref, x3_ref, hn2_ref, gates_ref):
    m = mod_ref[...]
    y = yf_ref[...].astype(F32) + yb_ref[...].astype(F32) + dsk_ref[...] * xs_ref[...].astype(F32)
    gated = y * _silu(z_ref[...].astype(F32))
    nrm = _rms(gated, ng_ref[...])
    out = jnp.dot(nrm.astype(BF16), w_ref[...], preferred_element_type=F32)
    x3 = x_ref[...] + m[2:3] * out
    x3_ref[...] = x3
    hn2 = _rms(x3, g2_ref[...]) * (1.0 + m[4:5]) + m[3:4]
    hn2_ref[...] = hn2.astype(hn2_ref.dtype)
    gates_ref[...] = _router_gates(hn2, rw_ref, rb_ref)


def _ssd_finish(yf, yb, xbc, z, xall, n_lat, mods, d_skip, norm_g, out_w, norm2_g, router_wt, router_b):
    bsz, lt, d_inner = z.shape
    d = xall.shape[-1]
    q = SSD_CHUNK
    ncl = n_lat // q
    n_exp = router_wt.shape[0]
    xview = xall.reshape(bsz, xall.shape[1] // GRID_W, GRID_W * d)
    inner = pl.BlockSpec((None, q, d_inner), lambda b, c: (b, c, 0))
    tok = pl.BlockSpec((None, q, d), lambda b, c: (b, c, 0))
    full = lambda s: pl.BlockSpec(s, lambda b, c: tuple(0 for _ in s))
    dsk = jnp.repeat(d_skip.astype(F32), SSD_HEAD_DIM).reshape(1, d_inner)
    return pl.pallas_call(
        _ssd_finish_kernel,
        grid=(bsz, ncl),
        in_specs=[
            inner, inner, inner, inner,
            pl.BlockSpec((None, q, d), lambda b, c: (b, 0, c)),
            pl.BlockSpec((None, 6, d), lambda b, c: (b * 2, 0, 0)),
            full((1, d_inner)), full((1, d_inner)), full((d_inner, d)), full((1, d)),
            full((n_exp, d)), full((n_exp, 1)),
        ],
        out_specs=[tok, tok, pl.BlockSpec((n_exp, q), lambda b, c: (0, b * ncl + c))],
        out_shape=[
            jax.ShapeDtypeStruct((bsz, n_lat, d), F32),
            jax.ShapeDtypeStruct((bsz, n_lat, d), BF16),
            jax.ShapeDtypeStruct((n_exp, bsz * n_lat), F32),
        ],
        compiler_params=_cparams("parallel", "parallel"),
        name="ssd_finish",
    )(yf, yb, xbc, z, xview, mods, dsk, norm_g.reshape(1, d_inner), out_w.astype(BF16),
      norm2_g.reshape(1, d), router_wt, router_b.reshape(n_exp, 1))


def kernel(x, c, ctx, c_ctx, mod_w, mod_b, norm1_g, norm2_g, final_g, s5_lam_re, s5_lam_im, s5_log_dt, s5_b_re, s5_b_im, s5_c_re, s5_c_im, s5_d, s5_glu_w, s5_glu_b, ssd_in_w, ssd_conv_w, ssd_conv_b, ssd_dt_bias, ssd_a_log, ssd_d, ssd_norm_g, ssd_out_w, router_w, router_b, moe_w1, moe_w3, moe_w2):
    bsz, n_lat, d = x.shape
    n_ctx = ctx.shape[1]
    lt = n_lat + n_ctx
    n_exp = router_w.shape[1]
    assert n_lat % TOKEN_TILE == 0 and n_ctx % TOKEN_TILE == 0
    assert TOKEN_TILE == MOE_TILE and MOE_TILE % SSD_CHUNK == 0

    mods = _modulation(c, c_ctx, mod_w, mod_b)
    router_wt = router_w.T.astype(F32)
    w1 = moe_w1.astype(BF16)
    w3 = moe_w3.astype(BF16)
    w2 = moe_w2.astype(BF16)
    nlt = n_lat // TOKEN_TILE
    tpb = lt // TOKEN_TILE

    xall = jnp.concatenate([x, ctx], axis=1)
    hn = _prenorm(xall, norm1_g[0], mods[0], nlt)
    s5w = _s5_weights(s5_lam_re[0], s5_lam_im[0], s5_log_dt[0], s5_b_re[0], s5_b_im[0], s5_c_re[0], s5_c_im[0])
    y = _s5_scan(hn, n_lat, s5w)
    x1, hn2, gates_t = _glu_head(y, hn, xall, mods[0], s5_d[0], s5_glu_w[0], s5_glu_b[0], norm2_g[0],
                                 router_wt, router_b, nlt)
    x2 = _moe(hn2.reshape(bsz * lt, d), gates_t.T, x1.reshape(bsz * lt, d), mods[0], w1[0], w3[0], w2[0],
              tiles_per_batch=tpb, n_lat_tiles=nlt).reshape(bsz, lt, d)

    z, xbc_pre, dt, cs, cst = _ssd_inproj(x2, n_lat, norm1_g[1], mods[1], ssd_in_w[0], ssd_dt_bias[0], ssd_a_log[0])
    xbc = _ssd_conv(xbc_pre, n_lat, ssd_conv_w[0], ssd_conv_b[0])
    yf, yb = _ssd_scan(xbc, dt, cs, cst, n_lat)
    x3, hn3, gates3_t = _ssd_finish(yf, yb, xbc, z, x2, n_lat, mods[1], ssd_d[0], ssd_norm_g[0], ssd_out_w[0],
                                    norm2_g[1], router_wt, router_b)
    out = _moe(hn3.reshape(bsz * n_lat, d), gates3_t.T, x3.reshape(bsz * n_lat, d), mods[1], w1[1], w3[1], w2[1],
               tiles_per_batch=nlt, n_lat_tiles=nlt, final_g=final_g)
    return out.reshape(bsz, n_lat, d)
```

```python
import functools

import jax
import jax.numpy as jnp
from jax import lax
from jax.experimental import pallas as pl
from jax.experimental.pallas import tpu as pltpu

F32 = jnp.float32
BF16 = jnp.bfloat16
HIGHEST = lax.Precision.HIGHEST

GRID_W = 64
RMS_EPS = 1e-6

S5_GROUP = 16
S5_STATE = 64
S5_T = 16
S5_GB = 8

SSD_HEAD_DIM = 64
SSD_GROUPS = 8
SSD_STATE = 128
SSD_CONV = 5
SSD_CHUNK = 128

N_EXPERT_GROUPS = 4
TOP_K = 2

TOKEN_TILE = 256
MOE_BLOCK = 512
MOE_PIECE = 16
MOE_TILE = 256
VMEM_LIMIT_BYTES = 56 * 1024 * 1024


def _cparams(*sem):
    return pltpu.CompilerParams(dimension_semantics=sem, vmem_limit_bytes=VMEM_LIMIT_BYTES)


def _sigmoid(v):
    return 1.0 / (1.0 + jnp.exp(-v))


def _silu(v):
    return v * _sigmoid(v)


def _gelu_tanh(v):
    return 0.5 * v * (1.0 + jnp.tanh(0.7978845608028654 * (v + 0.044715 * (v * v * v))))


def _rms(v, g):
    return v * lax.rsqrt(jnp.mean(v * v, axis=-1, keepdims=True) + RMS_EPS) * g


def _mod_kernel(cc_ref, w_ref, b_ref, o_ref):
    a = _silu(cc_ref[...])
    o_ref[...] = jnp.dot(a, w_ref[...], preferred_element_type=F32, precision=HIGHEST) + b_ref[...]


def _modulation(c, c_ctx, mod_w, mod_b):
    depth, d, d6 = mod_w.shape
    bsz = c.shape[0]
    rows = 8
    cc = jnp.zeros((rows, d), F32).at[:bsz].set(c).at[bsz].set(c_ctx)
    tn = d6 // 4
    out = pl.pallas_call(
        _mod_kernel,
        grid=(depth, d6 // tn),
        in_specs=[
            pl.BlockSpec((rows, d), lambda i, j: (0, 0)),
            pl.BlockSpec((None, d, tn), lambda i, j: (i, 0, j)),
            pl.BlockSpec((None, 1, tn), lambda i, j: (i, 0, j)),
        ],
        out_specs=pl.BlockSpec((None, rows, tn), lambda i, j: (i, 0, j)),
        out_shape=jax.ShapeDtypeStruct((depth, rows, d6), F32),
        compiler_params=_cparams("parallel", "parallel"),
        name="modulation",
    )(cc, mod_w, mod_b.reshape(depth, 1, d6))
    lat = out[:, :bsz].reshape(depth, bsz, 1, 6, d)
    ctx = jnp.broadcast_to(out[:, bsz].reshape(depth, 1, 1, 6, d), (depth, bsz, 1, 6, d))
    return jnp.concatenate([lat, ctx], axis=2).reshape(depth, bsz * 2, 6, d)


def _prenorm_kernel(x_ref, g_ref, mod_ref, o_ref):
    m = mod_ref[...]
    hn = _rms(x_ref[...], g_ref[...]) * (1.0 + m[1:2]) + m[0:1]
    o_ref[...] = hn.astype(o_ref.dtype)


def _prenorm(xall, g, mods, n_lat_tiles):
    bsz, lt, d = xall.shape
    nt = lt // TOKEN_TILE
    return pl.pallas_call(
        _prenorm_kernel,
        grid=(bsz, nt),
        in_specs=[
            pl.BlockSpec((None, TOKEN_TILE, d), lambda b, i: (b, i, 0)),
            pl.BlockSpec((1, d), lambda b, i: (0, 0)),
            pl.BlockSpec((None, 6, d), lambda b, i: (b * 2 + (i >= n_lat_tiles).astype(jnp.int32), 0, 0)),
        ],
        out_specs=pl.BlockSpec((None, TOKEN_TILE, d), lambda b, i: (b, i, 0)),
        out_shape=jax.ShapeDtypeStruct((bsz, lt, d), F32),
        compiler_params=_cparams("parallel", "parallel"),
        name="prenorm",
    )(xall, g.reshape(1, d), mods)


def _s5_weights(lam_re, lam_im, log_dt, b_re, b_im, c_re, c_im):
    t = S5_T
    k16 = b_re.shape[-1]

    def cmul(ar, ai, br, bi):
        return ar * br - ai * bi, ar * bi + ai * br

    def direction(k):
        lr, li = lam_re[k], lam_im[k]
        step = jnp.exp(log_dt[k])[:, None]
        mag = jnp.exp(lr * step)
        abar_r = mag * jnp.cos(li * step)
        abar_i = mag * jnp.sin(li * step)
        den = lr * lr + li * li
        q_r = ((abar_r - 1.0) * lr + abar_i * li) / den
        q_i = (abar_i * lr - (abar_r - 1.0) * li) / den
        bb_r, bb_i = cmul(q_r[..., None], q_i[..., None], b_re, b_im)

        def power(tau):
            tau = jnp.asarray(tau, F32)[None, :, None]
            m = jnp.exp((lr * step)[:, None, :] * tau)
            return m * jnp.cos((li * step)[:, None, :] * tau), m * jnp.sin((li * step)[:, None, :] * tau)

        return bb_r.transpose(0, 2, 1), bb_i.transpose(0, 2, 1), power

    rows = lambda v: jnp.repeat(v, k16, axis=1)
    row_tile = lambda v: jnp.tile(v, (1, t, 1))
    cols = lambda v: jnp.repeat(v.transpose(0, 2, 1), k16, axis=2)
    col_tile = lambda v: jnp.tile(v, (1, 1, t))
    ct_r, ct_i = c_re.transpose(0, 2, 1), c_im.transpose(0, 2, 1)
    steps = jnp.arange(t)

    def left(bt_r, bt_i, power, tau):
        pr, pi = power(tau)
        return cmul(row_tile(bt_r), row_tile(bt_i), rows(pr), rows(pi))

    def right(power, tau):
        pr, pi = power(tau)
        return cmul(col_tile(ct_r), col_tile(ct_i), cols(pr), cols(pi))

    bf_r, bf_i, pow_f = direction(0)
    bb_r, bb_i, pow_b = direction(1)
    lf_r, lf_i = left(bf_r, bf_i, pow_f, -steps)
    rf_r, rf_i = right(pow_f, steps)
    lb_r, lb_i = left(bb_r, bb_i, pow_b, steps)
    rb_r, rb_i = right(pow_b, -steps)
    lf = jnp.concatenate([lf_r, lf_i], axis=-1)
    lb = jnp.concatenate([lb_r, lb_i], axis=-1)
    rf = jnp.concatenate([rf_r, -rf_i], axis=1)
    rb = jnp.concatenate([rb_r, -rb_i], axis=1)

    sf_r, sf_i = left(bf_r, bf_i, pow_f, t - 1 - steps)
    ws = jnp.concatenate([sf_r, lb_r, sf_i, lb_i], axis=-1)

    of_r, of_i = right(pow_f, steps + 1)
    ob_r, ob_i = right(pow_b, t - steps)
    zero = jnp.zeros_like(of_r)
    w2 = jnp.concatenate([of_r, zero, -of_i, zero, zero, ob_r, zero, -ob_i], axis=1)

    af_r, af_i = pow_f([t])
    ab_r, ab_i = pow_b([t])
    ar = jnp.concatenate([af_r[:, 0], ab_r[:, 0]], axis=-1)
    ai = jnp.concatenate([af_i[:, 0], ab_i[:, 0]], axis=-1)
    return ws.astype(BF16), lf, rf, lb, rb, w2.astype(BF16), ar, ai


def _s5_row_block(nc):
    return max(rb for rb in range(16, min(nc, 176) + 1, 16) if nc % rb == 0)


def _dot_split3(a, b):
    a_hi = a.astype(BF16)
    a_lo = (a - a_hi.astype(F32)).astype(BF16)
    b_hi = b.astype(BF16)
    b_lo = (b - b_hi.astype(F32)).astype(BF16)
    dot = functools.partial(jnp.dot, preferred_element_type=F32)
    return dot(a_hi, b_hi) + dot(a_hi, b_lo) + dot(a_lo, b_hi)


def _s5_kernel(hn_ref, ws_ref, lf_ref, rf_ref, lb_ref, rb_ref, w2_ref, ar_ref, ai_ref, yo_ref,
               u_ref, y_ref, wm_ref, sre, sim, hre_f, him_f, hre_b, him_b, *, n_chunks, n_ctx_chunks, pitch):
    nc, ncc = n_chunks, n_ctx_chunks
    ncl = nc - ncc
    p = S5_STATE
    t_len = S5_T
    gl = S5_GROUP
    per_half = 128 // gl
    rb = _s5_row_block(nc)
    lane_slot = lax.broadcasted_iota(jnp.int32, (rb, 128), 1) // gl

    @pl.when(pl.program_id(1) == 0)
    def _():
        tk = t_len * gl
        src_tok = lax.broadcasted_iota(jnp.int32, (tk, tk), 0) // gl
        dst_tok = lax.broadcasted_iota(jnp.int32, (tk, tk), 1) // gl
        for g in range(S5_GB):
            causal = _dot_split3(lf_ref[g], rf_ref[g])
            anti = _dot_split3(lb_ref[g], rb_ref[g])
            wm = jnp.where(dst_tok >= src_tok, causal, 0.0) + jnp.where(src_tok >= dst_tok, anti, 0.0)
            wm_ref[g] = wm.astype(wm_ref.dtype)

    def gather_u(blk, carry):
        r0 = pl.multiple_of(blk * rb, 16)
        a = [hn_ref[pl.ds(r0 * t_len + t, rb, stride=t_len), :] for t in range(t_len)]
        for i in range(S5_GB):
            halves = []
            for hb in range(t_len // per_half):
                acc = None
                for j in range(per_half):
                    src = a[per_half * hb + j]
                    shift = (gl * (j - i)) % 128
                    rolled = src if shift == 0 else pltpu.roll(src, shift, axis=1)
                    acc = rolled if acc is None else jnp.where(lane_slot == j, rolled, acc)
                halves.append(acc)
            u_ref[i, pl.ds(r0, rb), :] = jnp.concatenate(halves, axis=1).astype(u_ref.dtype)
        return carry

    lax.fori_loop(0, nc // rb, gather_u, 0)

    for g in range(S5_GB):
        s = jnp.dot(u_ref[g], ws_ref[g], preferred_element_type=F32)
        sre[pl.ds(g * pitch, nc), :] = s[:, : 2 * p]
        sim[pl.ds(g * pitch, nc), :] = s[:, 2 * p:]

    ar = ar_ref[...]
    ai = ai_ref[...]
    fwd_lane = lax.broadcasted_iota(jnp.int32, (S5_GB, 2 * p), 1) < p

    def step(k, carry):
        h_r, h_i = carry
        cf = jnp.where(k < ncc, ncl + k, k - ncc)
        cb = nc - 1 - k
        rows_f = pl.ds(cf, S5_GB, stride=pitch)
        rows_b = pl.ds(cb, S5_GB, stride=pitch)
        hre_f[rows_f, :] = h_r
        him_f[rows_f, :] = h_i
        hre_b[rows_b, :] = h_r
        him_b[rows_b, :] = h_i
        s_r = jnp.where(fwd_lane, sre[rows_f, :], sre[rows_b, :])
        s_i = jnp.where(fwd_lane, sim[rows_f, :], sim[rows_b, :])
        n_r = ar * h_r - ai * h_i + s_r
        n_i = ar * h_i + ai * h_r + s_i
        return n_r, n_i

    zero = jnp.zeros((S5_GB, 2 * p), F32)
    lax.fori_loop(0, nc, step, (zero, zero))

    for g in range(S5_GB):
        rows = pl.ds(g * pitch, nc)
        hin = jnp.concatenate([hre_f[rows, :], him_f[rows, :], hre_b[rows, :], him_b[rows, :]], axis=1)
        out = jnp.dot(u_ref[g], wm_ref[g], preferred_element_type=F32)
        out = out + jnp.dot(hin.astype(BF16), w2_ref[g], preferred_element_type=F32)
        y_ref[g] = out

    def scatter_y(blk, carry):
        r0 = pl.multiple_of(blk * rb, 16)
        yv = [y_ref[i, pl.ds(r0, rb), :] for i in range(S5_GB)]
        for t in range(t_len):
            hb, j = divmod(t, per_half)
            acc = None
            for i in range(S5_GB):
                src = yv[i][:, hb * 128:(hb + 1) * 128]
                shift = (gl * (i - j)) % 128
                rolled = src if shift == 0 else pltpu.roll(src, shift, axis=1)
                acc = rolled if acc is None else jnp.where(lane_slot == i, rolled, acc)
            yo_ref[pl.ds(r0 * t_len + t, rb, stride=t_len), :] = acc
        return carry

    lax.fori_loop(0, nc // rb, scatter_y, 0)


def _s5_scan(hn, n_lat, weights):
    bsz, lt, d = hn.shape
    ngrp = d // S5_GROUP
    t = S5_T
    tk = t * S5_GROUP
    nc = lt // t
    ncc = (lt - n_lat) // t
    gb = S5_GB
    assert gb * S5_GROUP == 128 and (128 // S5_GROUP) == gb and t % gb == 0
    ws, lf, rf, lb, rb, w2, ar, ai = weights
    p2 = 2 * S5_STATE
    pitch = nc + 8 if (nc // 8) % 2 == 0 else nc
    kern = functools.partial(_s5_kernel, n_chunks=nc, n_ctx_chunks=ncc, pitch=pitch)
    per_group = lambda *s: pl.BlockSpec((gb,) + s, lambda gi, b: (gi,) + tuple(0 for _ in s))
    return pl.pallas_call(
        kern,
        grid=(ngrp // gb, bsz),
        in_specs=[
            pl.BlockSpec((None, lt, 128), lambda gi, b: (b, 0, gi)),
            per_group(tk, 2 * p2),
            per_group(tk, p2), per_group(p2, tk), per_group(tk, p2), per_group(p2, tk),
            per_group(4 * p2, tk),
            per_group(p2), per_group(p2),
        ],
        out_specs=pl.BlockSpec((None, lt, 128), lambda gi, b: (b, 0, gi)),
        out_shape=jax.ShapeDtypeStruct((bsz, lt, d), F32),
        scratch_shapes=[pltpu.VMEM((gb, nc, tk), BF16), pltpu.VMEM((gb, nc, tk), F32), pltpu.VMEM((gb, tk, tk), BF16)]
        + [pltpu.VMEM((gb * pitch, p2), F32) for _ in range(6)],
        compiler_params=_cparams("parallel", "arbitrary"),
        name="s5_scan",
    )(hn, ws, lf, rf, lb, rb, w2, ar, ai)


def _router_gates(hn2, rw_ref, rb_ref):
    n_exp = rw_ref.shape[0]
    epg = n_exp // N_EXPERT_GROUPS
    logits = lax.dot_general(rw_ref[...], hn2, (((1,), (1,)), ((), ())),
                             preferred_element_type=F32, precision=HIGHEST)
    s = _sigmoid(logits)
    sel = s + rb_ref[...]
    row = [sel[e:e + 1] for e in range(n_exp)]
    gscore = []
    for gi in range(N_EXPERT_GROUPS):
        a, b, c, dd = row[gi * epg: gi * epg + epg]
        hi1, lo1 = jnp.maximum(a, b), jnp.minimum(a, b)
        hi2, lo2 = jnp.maximum(c, dd), jnp.minimum(c, dd)
        gscore.append(jnp.maximum(hi1, hi2) + jnp.maximum(jnp.minimum(hi1, hi2), jnp.maximum(lo1, lo2)))
    gmax = functools.reduce(jnp.maximum, gscore)
    gates = []
    taken = None
    for gi in range(N_EXPERT_GROUPS):
        is_max = gscore[gi] == gmax
        best = is_max if taken is None else jnp.logical_and(is_max, jnp.logical_not(taken))
        taken = is_max if taken is None else jnp.logical_or(taken, is_max)
        for e in range(gi * epg, gi * epg + epg):
            rank = jnp.zeros_like(row[e])
            for j in range(gi * epg, gi * epg + epg):
                if j == e:
                    continue
                ahead = (row[j] >= row[e]) if j < e else (row[j] > row[e])
                rank = rank + ahead.astype(F32)
            chosen = jnp.logical_and(best, rank < float(TOP_K))
            gates.append(jnp.where(chosen, s[e:e + 1], 0.0))
    g = jnp.concatenate(gates, axis=0)
    return g / jnp.sum(g, axis=0, keepdims=True)


def _glu_kernel(y_ref, u_ref, x_ref, mod_ref, d_ref, w_ref, b_ref, g2_ref, rw_ref, rb_ref,
                x1_ref, hn2_ref, gates_ref):
    d = x_ref.shape[-1]
    m = mod_ref[...]
    u = u_ref[...].astype(F32)
    a = _gelu_tanh(y_ref[...].astype(F32) + d_ref[...] * u)
    z = jnp.dot(a.astype(BF16), w_ref[...], preferred_element_type=F32) + b_ref[...]
    out = z[:, :d] * _sigmoid(z[:, d:])
    x1 = x_ref[...] + m[2:3] * out
    x1_ref[...] = x1
    hn2 = _rms(x1, g2_ref[...]) * (1.0 + m[4:5]) + m[3:4]
    hn2_ref[...] = hn2.astype(hn2_ref.dtype)
    gates_ref[...] = _router_gates(hn2, rw_ref, rb_ref)


def _glu_head(y, hn, xall, mods, d_skip, glu_w, glu_b, norm2_g, router_wt, router_b, n_lat_tiles):
    bsz, lt, d = xall.shape
    nt = lt // TOKEN_TILE
    n_exp = router_wt.shape[0]
    tok = pl.BlockSpec((None, TOKEN_TILE, d), lambda b, i: (b, i, 0))
    vec = lambda n: pl.BlockSpec((1, n), lambda b, i: (0, 0))
    return pl.pallas_call(
        _glu_kernel,
        grid=(bsz, nt),
        in_specs=[
            tok, tok, tok,
            pl.BlockSpec((None, 6, d), lambda b, i: (b * 2 + (i >= n_lat_tiles).astype(jnp.int32), 0, 0)),
            vec(d),
            pl.BlockSpec((d, 2 * d), lambda b, i: (0, 0)),
            vec(2 * d),
            vec(d),
            pl.BlockSpec((n_exp, d), lambda b, i: (0, 0)),
            pl.BlockSpec((n_exp, 1), lambda b, i: (0, 0)),
        ],
        out_specs=[
            tok, tok,
            pl.BlockSpec((n_exp, TOKEN_TILE), lambda b, i: (0, b * nt + i)),
        ],
        out_shape=[
            jax.ShapeDtypeStruct((bsz, lt, d), F32),
            jax.ShapeDtypeStruct((bsz, lt, d), BF16),
            jax.ShapeDtypeStruct((n_exp, bsz * lt), F32),
        ],
        compiler_params=_cparams("parallel", "parallel"),
        name="s5_glu_head",
    )(y, hn, xall, mods, d_skip.reshape(1, d), glu_w.astype(BF16), glu_b.reshape(1, 2 * d),
      norm2_g.reshape(1, d), router_wt, router_b.reshape(n_exp, 1))


def _moe_slots_padded(total):
    return jnp.floor((total + float(MOE_PIECE - 1)) * (1.0 / MOE_PIECE)) * float(MOE_PIECE)


def _moe_sort_kernel(gt_ref, t_ref, ts_ref, cnt_ref):
    n_exp, nb = gt_ref.shape
    nbpad = ts_ref.shape[0]
    sel = gt_ref[...] > 0.0
    sel_b = jnp.where(sel, 1.0, 0.0).astype(BF16)
    earlier = lax.broadcasted_iota(jnp.int32, (nb, nb), 0) < lax.broadcasted_iota(jnp.int32, (nb, nb), 1)
    rank = jnp.dot(sel_b, jnp.where(earlier, 1.0, 0.0).astype(BF16), preferred_element_type=F32)
    total = jnp.sum(jnp.where(sel, 1.0, 0.0), axis=1, keepdims=True)
    padded = jnp.broadcast_to(_moe_slots_padded(total), (n_exp, 128))
    below = lax.broadcasted_iota(jnp.int32, (n_exp, n_exp), 1) < lax.broadcasted_iota(jnp.int32, (n_exp, n_exp), 0)
    offs = jnp.dot(jnp.where(below, 1.0, 0.0).astype(BF16), padded.astype(BF16), preferred_element_type=F32)[:, 0:1]
    dest = offs + rank
    d_lo = jnp.min(jnp.where(sel, dest, float(nbpad)), axis=0, keepdims=True).astype(jnp.int32)
    d_hi = jnp.max(jnp.where(sel, dest, -1.0), axis=0, keepdims=True).astype(jnp.int32)
    slot = lax.broadcasted_iota(jnp.int32, (nbpad, nb), 0)
    perm = jnp.where(jnp.logical_or(slot == d_lo, slot == d_hi), 1.0, 0.0).astype(BF16)
    ts_ref[...] = jnp.dot(perm, t_ref[...], preferred_element_type=F32).astype(ts_ref.dtype)
    cnt_ref[...] = padded


def _moe_expert_kernel(tile_e, npieces, rows, ntiles, ts_hbm, w1_ref, w3_ref, w2_ref, ys_hbm,
                       tbuf, ybuf, sem_in, sem_out):
    t = pl.program_id(0)
    nt = ntiles[0]
    last = pl.num_programs(0) - 1
    ppt = MOE_TILE // MOE_PIECE
    slot = t % 2

    def piece_rows(tt, p):
        return pl.ds(pl.multiple_of(rows[tt * ppt + p], MOE_PIECE), MOE_PIECE)

    def for_pieces(tt, fn):
        for p in range(ppt):
            @pl.when(p < npieces[tt])
            def _(p=p):
                fn(p)

    def copy_in(tt, sl, p):
        return pltpu.make_async_copy(ts_hbm.at[piece_rows(tt, p), :],
                                     tbuf.at[sl, pl.ds(p * MOE_PIECE, MOE_PIECE), :], sem_in.at[sl])

    def copy_out(tt, sl, p):
        return pltpu.make_async_copy(ybuf.at[sl, pl.ds(p * MOE_PIECE, MOE_PIECE), :],
                                     ys_hbm.at[piece_rows(tt, p), :], sem_out.at[sl])

    @pl.when(t == 0)
    def _():
        tbuf[...] = jnp.zeros_like(tbuf)

        @pl.when(nt > 0)
        def _():
            for_pieces(0, lambda p: copy_in(0, 0, p).start())

    @pl.when(t + 1 < nt)
    def _():
        for_pieces(t + 1, lambda p: copy_in(t + 1, 1 - slot, p).start())

    @pl.when(jnp.logical_and(t >= 2, t - 2 < nt))
    def _():
        for_pieces(t - 2, lambda p: copy_out(t - 2, slot, p).wait())

    @pl.when(t < nt)
    def _():
        for_pieces(t, lambda p: copy_in(t, slot, p).wait())
        x = tbuf[slot]
        h = _silu(jnp.dot(x, w1_ref[...], preferred_element_type=F32)) * jnp.dot(x, w3_ref[...], preferred_element_type=F32)
        ybuf[slot] = jnp.dot(h.astype(BF16), w2_ref[...], preferred_element_type=F32).astype(ybuf.dtype)
        for_pieces(t, lambda p: copy_out(t, slot, p).start())

    @pl.when(t == last)
    def _():
        @pl.when(jnp.logical_and(last >= 1, last - 1 < nt))
        def _():
            for_pieces(last - 1, lambda p: copy_out(last - 1, 1 - slot, p).wait())

        @pl.when(last < nt)
        def _():
            for_pieces(last, lambda p: copy_out(last, slot, p).wait())


def _moe_unsort_kernel(ys_ref, g_ref, x_ref, g2_ref, *rest, final):
    if final:
        fg_ref, o_ref = rest
    else:
        (o_ref,) = rest
    nb, n_exp = g_ref.shape
    nbpad = ys_ref.shape[0]
    gates = g_ref[...]
    sel = gates > 0.0
    sel_b = jnp.where(sel, 1.0, 0.0).astype(BF16)
    earlier = lax.broadcasted_iota(jnp.int32, (nb, nb), 1) < lax.broadcasted_iota(jnp.int32, (nb, nb), 0)
    rank = jnp.dot(jnp.where(earlier, 1.0, 0.0).astype(BF16), sel_b, preferred_element_type=F32)
    total = jnp.sum(jnp.where(sel, 1.0, 0.0), axis=0, keepdims=True)
    padded = jnp.broadcast_to(_moe_slots_padded(total), (8, n_exp))
    below = lax.broadcasted_iota(jnp.int32, (n_exp, n_exp), 0) < lax.broadcasted_iota(jnp.int32, (n_exp, n_exp), 1)
    offs = jnp.dot(padded.astype(BF16), jnp.where(below, 1.0, 0.0).astype(BF16), preferred_element_type=F32)[0:1]
    dest = offs + rank
    d_lo = jnp.min(jnp.where(sel, dest, float(nbpad)), axis=1, keepdims=True)
    d_hi = jnp.max(jnp.where(sel, dest, -1.0), axis=1, keepdims=True)
    g_lo = jnp.sum(jnp.where(jnp.logical_and(sel, dest == d_lo), gates, 0.0), axis=1, keepdims=True)
    g_hi = jnp.sum(jnp.where(jnp.logical_and(sel, dest == d_hi), gates, 0.0), axis=1, keepdims=True)
    slot = lax.broadcasted_iota(jnp.int32, (nb, nbpad), 1)
    ys = ys_ref[...]

    def pick(dcol):
        onehot = jnp.where(slot == dcol.astype(jnp.int32), 1.0, 0.0).astype(BF16)
        return jnp.dot(onehot, ys, preferred_element_type=F32)

    moe = g_lo * pick(d_lo) + g_hi * pick(d_hi)
    half = nb // g2_ref.shape[0]
    d = x_ref.shape[1]
    for j in range(g2_ref.shape[0]):
        r = slice(j * half, (j + 1) * half)
        xn = x_ref[r, :] + g2_ref[j] * moe[r]
        if final:
            xn = _rms(xn, fg_ref[...])
            for c in range(half // SSD_CHUNK):
                col = (j * half) // SSD_CHUNK + c
                o_ref[:, col * d:(col + 1) * d] = xn[c * SSD_CHUNK:(c + 1) * SSD_CHUNK]
        else:
            o_ref[r, :] = xn


def _moe_schedule(counts, nbpad, n_tiles):
    nblk, n_exp = counts.shape
    ppt = MOE_TILE // MOE_PIECE
    pc = counts // MOE_PIECE
    loc = jnp.cumsum(pc, axis=1) - pc
    cum_b = jnp.cumsum(pc, axis=0)
    np_e = cum_b[-1]
    tiles_e = (np_e + ppt - 1) // ppt
    tile_end = jnp.cumsum(tiles_e)
    ntiles = tile_end[-1]
    t_idx = jnp.arange(n_tiles, dtype=jnp.int32)
    tile_e = jnp.minimum(jnp.sum((tile_end[None, :] <= t_idx[:, None]).astype(jnp.int32), axis=1), n_exp - 1)
    first = (tile_end - tiles_e)[tile_e]
    piece0 = (t_idx - first) * ppt
    npieces = jnp.where(t_idx < ntiles, jnp.clip(np_e[tile_e] - piece0, 0, ppt), 0)
    i = piece0[:, None] + jnp.arange(ppt, dtype=jnp.int32)[None, :]
    cum_t = cum_b.T[tile_e]
    blk = jnp.minimum(jnp.sum((cum_t[:, None, :] <= i[:, :, None]).astype(jnp.int32), axis=2), nblk - 1)
    before = jnp.take_along_axis(cum_t - pc.T[tile_e], blk, axis=1)
    within = i - before + jnp.take_along_axis(loc.T[tile_e], blk, axis=1)
    rows = blk * nbpad + within * MOE_PIECE
    rows = jnp.where(jnp.arange(ppt)[None, :] < npieces[:, None], rows, 0)
    return (tile_e.astype(jnp.int32), npieces.astype(jnp.int32), rows.reshape(-1).astype(jnp.int32),
            ntiles.reshape(1).astype(jnp.int32))


def _moe(t, gates_t, xres, g2rows, w1, w3, w2, *, final_g=None, blocks_per_batch=None):
    n, d = t.shape
    n_exp, _, f = w1.shape
    nb = MOE_BLOCK
    nblk = n // nb
    nbpad = TOP_K * nb + n_exp * MOE_PIECE
    final = final_g is not None

    ts, cnt = pl.pallas_call(
        _moe_sort_kernel,
        grid=(nblk,),
        in_specs=[pl.BlockSpec((n_exp, nb), lambda j: (0, j)), pl.BlockSpec((nb, d), lambda j: (j, 0))],
        out_specs=[pl.BlockSpec((nbpad, d), lambda j: (j, 0)), pl.BlockSpec((None, n_exp, 128), lambda j: (j, 0, 0))],
        out_shape=[jax.ShapeDtypeStruct((nblk * nbpad, d), BF16), jax.ShapeDtypeStruct((nblk, n_exp, 128), F32)],
        compiler_params=_cparams("parallel"),
        name="moe_sort",
    )(gates_t, t)

    n_tiles = nblk * nbpad // MOE_TILE + n_exp
    tile_e, npieces, rows, ntiles = _moe_schedule(cnt[:, :, 0].astype(jnp.int32), nbpad, n_tiles)
    wspec = lambda shape: pl.BlockSpec((None,) + shape, lambda i, te, npc, rw, nt: (te[i], 0, 0))
    ys = pl.pallas_call(
        _moe_expert_kernel,
        grid_spec=pltpu.PrefetchScalarGridSpec(
            num_scalar_prefetch=4,
            grid=(n_tiles,),
            in_specs=[pl.BlockSpec(memory_space=pl.ANY), wspec((d, f)), wspec((d, f)), wspec((f, d))],
            out_specs=pl.BlockSpec(memory_space=pl.ANY),
            scratch_shapes=[pltpu.VMEM((2, MOE_TILE, d), BF16), pltpu.VMEM((2, MOE_TILE, d), BF16),
                            pltpu.SemaphoreType.DMA((2,)), pltpu.SemaphoreType.DMA((2,))],
        ),
        out_shape=jax.ShapeDtypeStruct((nblk * nbpad, d), BF16),
        input_output_aliases={4: 0},
        compiler_params=_cparams("arbitrary"),
        name="moe_experts",
    )(tile_e, npieces, rows, ntiles, ts, w1, w3, w2)

    halves = nb // TOKEN_TILE
    in_specs = [
        pl.BlockSpec((nbpad, d), lambda j: (j, 0)),
        pl.BlockSpec((nb, n_exp), lambda j: (j, 0)),
        pl.BlockSpec((nb, d), lambda j: (j, 0)),
        pl.BlockSpec((halves, 1, d), lambda j: (j, 0, 0)),
    ]
    args = [ys, gates_t.T, xres, g2rows]
    if final:
        in_specs.append(pl.BlockSpec((1, d), lambda j: (0, 0)))
        args.append(final_g.reshape(1, d))
        cols = nb // SSD_CHUNK
        out_spec = pl.BlockSpec((None, SSD_CHUNK, cols * d), lambda j: (j // blocks_per_batch, 0, j % blocks_per_batch))
        out_shape = jax.ShapeDtypeStruct((nblk // blocks_per_batch, SSD_CHUNK, GRID_W * d), F32)
    else:
        out_spec = pl.BlockSpec((nb, d), lambda j: (j, 0))
        out_shape = jax.ShapeDtypeStruct((n, d), F32)
    return pl.pallas_call(
        functools.partial(_moe_unsort_kernel, final=final),
        grid=(nblk,),
        in_specs=in_specs,
        out_specs=out_spec,
        out_shape=out_shape,
        compiler_params=_cparams("parallel"),
        name="moe_unsort_final" if final else "moe_unsort",
    )(*args)


def _ssd_inproj_kernel(xl_ref, xc_ref, g_ref, mod_ref, wz_ref, wx_ref, wdt_ref, wdtt_ref, bias_ref, biast_ref,
                       a_ref, at_ref, z_ref, xbc_ref, dt_ref, cs_ref, cst_ref, xt, *, n_lat_chunks):
    c = pl.program_id(1)

    @pl.when(c < n_lat_chunks)
    def _():
        xt[...] = xl_ref[...]

    @pl.when(c >= n_lat_chunks)
    def _():
        xt[...] = xc_ref[...]

    m = mod_ref[...]
    hn = (_rms(xt[...], g_ref[...]) * (1.0 + m[1:2]) + m[0:1]).astype(BF16)
    z_ref[...] = jnp.dot(hn, wz_ref[...], preferred_element_type=F32).astype(z_ref.dtype)
    xbc_ref[...] = jnp.dot(hn, wx_ref[...], preferred_element_type=F32).astype(xbc_ref.dtype)

    def softplus(v):
        return jnp.maximum(v, 0.0) + jnp.log(1.0 + jnp.exp(-jnp.abs(v)))

    q = SSD_CHUNK
    r_i = lax.broadcasted_iota(jnp.int32, (q, q), 0)
    c_i = lax.broadcasted_iota(jnp.int32, (q, q), 1)
    nh = a_ref.shape[1] // 2
    dt = softplus(jnp.dot(hn, wdt_ref[...], preferred_element_type=F32) + bias_ref[...])
    da = dt * a_ref[...]
    lower = (r_i >= c_i).astype(F32)
    upper = (r_i <= c_i).astype(F32)
    cs_f = jnp.dot(lower, da[:, :nh], preferred_element_type=F32, precision=HIGHEST)
    cs_b = jnp.dot(upper, da[:, nh:], preferred_element_type=F32, precision=HIGHEST)
    cs = jnp.concatenate([cs_f, cs_b], axis=1)
    dtt = softplus(lax.dot_general(wdtt_ref[...], hn, (((1,), (1,)), ((), ())), preferred_element_type=F32)
                   + biast_ref[...])
    dat = dtt * at_ref[...]
    cst_f = jnp.dot(dat[:nh], upper, preferred_element_type=F32, precision=HIGHEST)
    cst_b = jnp.dot(dat[nh:], lower, preferred_element_type=F32, precision=HIGHEST)
    cst = jnp.concatenate([cst_f, cst_b], axis=0)
    r = 4
    for j in range(dt_ref.shape[0]):
        dt_ref[j] = dt[:, j * r:(j + 1) * r]
        cs_ref[j] = cs[:, j * r:(j + 1) * r]
        cst_ref[j] = cst[j * r:(j + 1) * r, :]


def _ssd_inproj(xall, n_lat, norm_g, mods, in_w, dt_bias, a_log):
    bsz, lt, d = xall.shape
    q = SSD_CHUNK
    ncl = n_lat // q
    nc = lt // q
    nh2 = dt_bias.size
    d_inner = (nh2 // 2) * SSD_HEAD_DIM
    conv_ch = in_w.shape[1] - d_inner - nh2
    wz = in_w[:, :d_inner].astype(BF16)
    wx = in_w[:, d_inner:d_inner + conv_ch].astype(BF16)
    wdt = in_w[:, d_inner + conv_ch:].astype(BF16)
    a = -jnp.exp(a_log.astype(F32)).reshape(1, nh2)
    bias = dt_bias.astype(F32).reshape(1, nh2)
    ngr = nh2 // 4
    assert n_lat // GRID_W == q, "one SSD chunk per latent grid column"
    xview = xall.reshape(bsz, lt // GRID_W, GRID_W * d)
    kern = functools.partial(_ssd_inproj_kernel, n_lat_chunks=ncl)
    full = lambda s: pl.BlockSpec(s, lambda b, c: tuple(0 for _ in s))
    return pl.pallas_call(
        kern,
        grid=(bsz, nc),
        in_specs=[
            pl.BlockSpec((None, q, d), lambda b, c: (b, 0, jnp.minimum(c, ncl - 1))),
            pl.BlockSpec((None, q, d), lambda b, c: (b, jnp.maximum(c, ncl), 0)),
            full((1, d)),
            pl.BlockSpec((None, 6, d), lambda b, c: (b * 2 + (c >= ncl).astype(jnp.int32), 0, 0)),
            full((d, d_inner)), full((d, conv_ch)), full((d, nh2)), full((nh2, d)),
            full((1, nh2)), full((nh2, 1)), full((1, nh2)), full((nh2, 1)),
        ],
        out_specs=[
            pl.BlockSpec((None, q, d_inner), lambda b, c: (b, c, 0)),
            pl.BlockSpec((None, q, conv_ch), lambda b, c: (b, c, 0)),
            pl.BlockSpec((None, ngr, q, 4), lambda b, c: (b, 0, c, 0)),
            pl.BlockSpec((None, ngr, q, 4), lambda b, c: (b, 0, c, 0)),
            pl.BlockSpec((None, None, ngr, 4, q), lambda b, c: (b, c, 0, 0, 0)),
        ],
        out_shape=[
            jax.ShapeDtypeStruct((bsz, lt, d_inner), BF16),
            jax.ShapeDtypeStruct((bsz, lt, conv_ch), BF16),
            jax.ShapeDtypeStruct((bsz, ngr, lt, 4), F32),
            jax.ShapeDtypeStruct((bsz, ngr, lt, 4), F32),
            jax.ShapeDtypeStruct((bsz, nc, ngr, 4, q), F32),
        ],
        scratch_shapes=[pltpu.VMEM((q, d), F32)],
        compiler_params=_cparams("parallel", "parallel"),
        name="ssd_inproj",
    )(xview, xall, norm_g.reshape(1, d), mods, wz, wx, wdt, wdt.T, bias, bias.reshape(nh2, 1),
      a, a.reshape(nh2, 1))


def _ssd_conv_kernel(cur_ref, prev_ref, next_ref, w_ref, b_ref, o_ref, ext, *, n_lat_chunks, n_chunks):
    c = pl.program_id(1)
    q = SSD_CHUNK
    halo = prev_ref.shape[0]
    first = jnp.logical_or(c == 0, c == n_lat_chunks)
    last = jnp.logical_or(c == n_lat_chunks - 1, c == n_chunks - 1)
    prev = prev_ref[...].astype(F32)
    nxt = next_ref[...].astype(F32)
    ext[0:8, :] = jnp.where(first, 0.0, prev[halo - 8:halo])
    ext[8:8 + q, :] = cur_ref[...].astype(F32)
    ext[8 + q:16 + q, :] = jnp.where(last, 0.0, nxt[0:8])
    acc = jnp.zeros(o_ref.shape, F32) + b_ref[...]
    w = w_ref[...]
    for k in range(SSD_CONV):
        acc = acc + w[k:k + 1] * ext[pl.ds(8 - SSD_CONV // 2 + k, q), :]
    o_ref[...] = _silu(acc).astype(o_ref.dtype)


def _ssd_conv(xbc, n_lat, conv_w, conv_b):
    bsz, lt, ch = xbc.shape
    q = SSD_CHUNK
    nc, ncl = lt // q, n_lat // q
    halo = 16
    hb = q // halo
    cw = ch
    kern = functools.partial(_ssd_conv_kernel, n_lat_chunks=ncl, n_chunks=nc)
    return pl.pallas_call(
        kern,
        grid=(bsz, nc, ch // cw),
        in_specs=[
            pl.BlockSpec((None, q, cw), lambda b, c, j: (b, c, j)),
            pl.BlockSpec((None, halo, cw), lambda b, c, j: (b, jnp.maximum(c * hb - 1, 0), j)),
            pl.BlockSpec((None, halo, cw), lambda b, c, j: (b, jnp.minimum((c + 1) * hb, nc * hb - 1), j)),
            pl.BlockSpec((SSD_CONV, cw), lambda b, c, j: (0, j)),
            pl.BlockSpec((1, cw), lambda b, c, j: (0, j)),
        ],
        out_specs=pl.BlockSpec((None, q, cw), lambda b, c, j: (b, c, j)),
        out_shape=jax.ShapeDtypeStruct((bsz, lt, ch), BF16),
        scratch_shapes=[pltpu.VMEM((q + 16, cw), F32)],
        compiler_params=_cparams("parallel", "parallel", "parallel"),
        name="ssd_conv",
    )(xbc, xbc, xbc, conv_w.astype(F32), conv_b.astype(F32).reshape(1, ch))


def _ssd_scan_dir(x_ref, b_ref, c_ref, dt_ref, cs_ref, cst_ref, y_ref, state, reverse):
    q = SSD_CHUNK
    hp = SSD_HEAD_DIM
    gw = x_ref.shape[1]
    r = gw // hp
    x = x_ref[...].astype(F32)
    bm = b_ref[...]
    cm = c_ref[...]
    dt = dt_ref[...]
    cs = cs_ref[...]
    cst = cst_ref[...]
    v = jnp.concatenate([cs, dt], axis=1)
    v_hi = v.astype(BF16)
    v_r1 = v - v_hi.astype(F32)
    v_mid = v_r1.astype(BF16)
    v_lo = (v_r1 - v_mid.astype(F32)).astype(BF16)
    lhs = jnp.concatenate([v_hi, v_mid, v_lo], axis=1)
    wide = 2 * gw + r * q
    col = lax.broadcasted_iota(jnp.int32, (6 * r, wide), 0) % (2 * r)
    ln = lax.broadcasted_iota(jnp.int32, (6 * r, wide), 1)
    cs_hit = jnp.logical_or(jnp.logical_and(ln < gw, ln // hp == col),
                            jnp.logical_and(ln >= 2 * gw, (ln - 2 * gw) // q == col))
    dt_hit = jnp.logical_and(jnp.logical_and(ln >= gw, ln < 2 * gw), (ln - gw) // hp == col - r)
    hit = jnp.logical_or(jnp.logical_and(col < r, cs_hit), jnp.logical_and(col >= r, dt_hit))
    spread = jnp.where(hit, 1.0, 0.0).astype(BF16)
    spread_out = jnp.dot(lhs, spread, preferred_element_type=F32)
    cs_x = spread_out[:, :gw]
    dt_x = spread_out[:, gw:2 * gw]
    cs_q = spread_out[:, 2 * gw:]
    end = 0 if reverse else q - 1
    cs_end = cs_x[end:end + 1, :]
    xdt = x * dt_x
    xw = (xdt * jnp.exp(cs_end - cs_x)).astype(BF16)
    xdt = xdt.astype(BF16)
    cb = lax.dot_general(cm, bm, (((1,), (1,)), ((), ())), preferred_element_type=F32)
    y_off = jnp.dot(cm, state[...].astype(BF16), preferred_element_type=F32) * jnp.exp(cs_x)
    r_i = lax.broadcasted_iota(jnp.int32, (q, q), 0)
    c_i = lax.broadcasted_iota(jnp.int32, (q, q), 1)
    mask = (r_i <= c_i) if reverse else (r_i >= c_i)
    lane = lax.broadcasted_iota(jnp.int32, (q, 2 * hp), 1)
    ys = []
    for pair in range(r // 2):
        xp = xdt[:, pair * 2 * hp:(pair + 1) * 2 * hp]
        yd = []
        for h in (2 * pair, 2 * pair + 1):
            seg = jnp.exp(jnp.minimum(cs_q[:, h * q:(h + 1) * q] - cst[h:h + 1, :], 0.0))
            gmat = (cb * jnp.where(mask, seg, 0.0)).astype(BF16)
            yd.append(jnp.dot(gmat, xp, preferred_element_type=F32))
        ys.append(jnp.where(lane < hp, yd[0], yd[1]))
    y_ref[...] = (jnp.concatenate(ys, axis=1) + y_off).astype(y_ref.dtype)
    upd = lax.dot_general(bm, xw, (((0,), (0,)), ((), ())), preferred_element_type=F32)
    state[...] = state[...] * jnp.exp(cs_end) + upd


def _ssd_scan_kernel(xf, bf, cf, dtf, csf, cstf, xb, bb, cb, dtb, csb, cstb, yf_ref, yb_ref, state_f, state_b):
    @pl.when(pl.program_id(2) == 0)
    def _():
        state_f[...] = jnp.zeros_like(state_f)
        state_b[...] = jnp.zeros_like(state_b)

    _ssd_scan_dir(xf, bf, cf, dtf, csf, cstf, yf_ref, state_f, False)
    _ssd_scan_dir(xb, bb, cb, dtb, csb, cstb, yb_ref, state_b, True)


def _ssd_scan(xbc, dt, cs, cst, n_lat):
    bsz, lt, ch = xbc.shape
    q = SSD_CHUNK
    nc, ncl = lt // q, n_lat // q
    ngr = SSD_GROUPS
    n = SSD_STATE
    d_inner = ch - 2 * ngr * n
    gw = d_inner // ngr
    bcol, ccol = d_inner // n, d_inner // n + ngr

    def specs(chunk, doff):
        return [
            pl.BlockSpec((None, q, gw), lambda b, g, k: (b, chunk(k), g)),
            pl.BlockSpec((None, q, n), lambda b, g, k: (b, chunk(k), bcol + g)),
            pl.BlockSpec((None, q, n), lambda b, g, k: (b, chunk(k), ccol + g)),
            pl.BlockSpec((None, None, q, 4), lambda b, g, k: (b, doff + g, chunk(k), 0)),
            pl.BlockSpec((None, None, q, 4), lambda b, g, k: (b, doff + g, chunk(k), 0)),
            pl.BlockSpec((None, None, None, 4, q), lambda b, g, k: (b, chunk(k), doff + g, 0, 0)),
        ]

    fwd = lambda k: (k + ncl) % nc
    bwd = lambda k: nc - 1 - k
    out = jax.ShapeDtypeStruct((bsz, lt, d_inner), BF16)
    return pl.pallas_call(
        _ssd_scan_kernel,
        grid=(bsz, ngr, nc),
        in_specs=specs(fwd, 0) + specs(bwd, ngr),
        out_specs=[pl.BlockSpec((None, q, gw), lambda b, g, k: (b, fwd(k), g)),
                   pl.BlockSpec((None, q, gw), lambda b, g, k: (b, bwd(k), g))],
        out_shape=[out, out],
        scratch_shapes=[pltpu.VMEM((n, gw), F32), pltpu.VMEM((n, gw), F32)],
        compiler_params=_cparams("parallel", "parallel", "arbitrary"),
        name="ssd_scan",
    )(xbc, xbc, xbc, dt, cs, cst, xbc, xbc, xbc, dt, cs, cst)


def _ssd_finish_kernel(yf_ref, yb_ref, xs_ref, z_ref, x_ref, mod_ref, dsk_ref, ng_ref, w_ref, g2_ref,
                       rw_ref, rb_ref, x3_ref, hn2_ref, gates_ref):
    m = mod_ref[...]
    y = yf_ref[...].astype(F32) + yb_ref[...].astype(F32) + dsk_ref[...] * xs_ref[...].astype(F32)
    gated = y * _silu(z_ref[...].astype(F32))
    nrm = _rms(gated, ng_ref[...])
    out = jnp.dot(nrm.astype(BF16), w_ref[...], preferred_element_type=F32)
    x3 = x_ref[...] + m[2:3] * out
    x3_ref[...] = x3
    hn2 = _rms(x3, g2_ref[...]) * (1.0 + m[4:5]) + m[3:4]
    hn2_ref[...] = hn2.astype(hn2_ref.dtype)
    gates_ref[...] = _router_gates(hn2, rw_ref, rb_ref)


def _ssd_finish(yf, yb, xbc, z, xall, n_lat, mods, d_skip, norm_g, out_w, norm2_g, router_wt, router_b):
    bsz, lt, d_inner = z.shape
    d = xall.shape[-1]
    q = SSD_CHUNK
    ncl = n_lat // q
    n_exp = router_wt.shape[0]
    xview = xall.reshape(bsz, xall.shape[1] // GRID_W, GRID_W * d)
    inner = pl.BlockSpec((None, q, d_inner), lambda b, c: (b, c, 0))
    tok = pl.BlockSpec((None, q, d), lambda b, c: (b, c, 0))
    full = lambda s: pl.BlockSpec(s, lambda b, c: tuple(0 for _ in s))
    dsk = jnp.repeat(d_skip.astype(F32), SSD_HEAD_DIM).reshape(1, d_inner)
    return pl.pallas_call(
        _ssd_finish_kernel,
        grid=(bsz, ncl),
        in_specs=[
            inner, inner, inner, inner,
            pl.BlockSpec((None, q, d), lambda b, c: (b, 0, c)),
            pl.BlockSpec((None, 6, d), lambda b, c: (b * 2, 0, 0)),
            full((1, d_inner)), full((1, d_inner)), full((d_inner, d)), full((1, d)),
            full((n_exp, d)), full((n_exp, 1)),
        ],
        out_specs=[tok, tok, pl.BlockSpec((n_exp, q), lambda b, c: (0, b * ncl + c))],
        out_shape=[
            jax.ShapeDtypeStruct((bsz, n_lat, d), F32),
            jax.ShapeDtypeStruct((bsz, n_lat, d), BF16),
            jax.ShapeDtypeStruct((n_exp, bsz * n_lat), F32),
        ],
        compiler_params=_cparams("parallel", "parallel"),
        name="ssd_finish",
    )(yf, yb, xbc, z, xview, mods, dsk, norm_g.reshape(1, d_inner), out_w.astype(BF16),
      norm2_g.reshape(1, d), router_wt, router_b.reshape(n_exp, 1))


def kernel(x, c, ctx, c_ctx, mod_w, mod_b, norm1_g, norm2_g, final_g, s5_lam_re, s5_lam_im, s5_log_dt, s5_b_re, s5_b_im, s5_c_re, s5_c_im, s5_d, s5_glu_w, s5_glu_b, ssd_in_w, ssd_conv_w, ssd_conv_b, ssd_dt_bias, ssd_a_log, ssd_d, ssd_norm_g, ssd_out_w, router_w, router_b, moe_w1, moe_w3, moe_w2):
    bsz, n_lat, d = x.shape
    n_ctx = ctx.shape[1]
    lt = n_lat + n_ctx
    n_exp = router_w.shape[1]
    assert n_lat % TOKEN_TILE == 0 and n_ctx % TOKEN_TILE == 0
    assert (bsz * lt) % MOE_BLOCK == 0 and n_lat % MOE_BLOCK == 0 and MOE_BLOCK % TOKEN_TILE == 0
    assert TOKEN_TILE % SSD_CHUNK == 0

    mods = _modulation(c, c_ctx, mod_w, mod_b)
    router_wt = router_w.T.astype(F32)
    w1 = moe_w1.astype(BF16)
    w3 = moe_w3.astype(BF16)
    w2 = moe_w2.astype(BF16)
    nlt = n_lat // TOKEN_TILE
    tpb = lt // TOKEN_TILE

    xall = jnp.concatenate([x, ctx], axis=1)
    hn = _prenorm(xall, norm1_g[0], mods[0], nlt)
    s5w = _s5_weights(s5_lam_re[0], s5_lam_im[0], s5_log_dt[0], s5_b_re[0], s5_b_im[0], s5_c_re[0], s5_c_im[0])
    y = _s5_scan(hn, n_lat, s5w)
    x1, hn2, gates_t = _glu_head(y, hn, xall, mods[0], s5_d[0], s5_glu_w[0], s5_glu_b[0], norm2_g[0],
                                 router_wt, router_b, nlt)
    g2_lat = jnp.broadcast_to(mods[0][0::2, None, 5], (bsz, nlt, d))
    g2_ctx = jnp.broadcast_to(mods[0][1::2, None, 5], (bsz, tpb - nlt, d))
    g2rows = jnp.concatenate([g2_lat, g2_ctx], axis=1).reshape(bsz * tpb, 1, d)
    x2 = _moe(hn2.reshape(bsz * lt, d), gates_t, x1.reshape(bsz * lt, d), g2rows, w1[0], w3[0], w2[0]).reshape(bsz, lt, d)

    z, xbc_pre, dt, cs, cst = _ssd_inproj(x2, n_lat, norm1_g[1], mods[1], ssd_in_w[0], ssd_dt_bias[0], ssd_a_log[0])
    xbc = _ssd_conv(xbc_pre, n_lat, ssd_conv_w[0], ssd_conv_b[0])
    yf, yb = _ssd_scan(xbc, dt, cs, cst, n_lat)
    x3, hn3, gates3_t = _ssd_finish(yf, yb, xbc, z, x2, n_lat, mods[1], ssd_d[0], ssd_norm_g[0], ssd_out_w[0],
                                    norm2_g[1], router_wt, router_b)
    g2rows = jnp.broadcast_to(mods[1][0::2, None, 5], (bsz, nlt, d)).reshape(bsz * nlt, 1, d)
    out = _moe(hn3.reshape(bsz * n_lat, d), gates3_t, x3.reshape(bsz * n_lat, d), g2rows, w1[1], w3[1], w2[1],
               final_g=final_g, blocks_per_batch=n_lat // MOE_BLOCK)
    return out.reshape(bsz, n_lat, d)
```

```python
import functools

import jax
import jax.numpy as jnp
from jax import lax
from jax.experimental import pallas as pl
from jax.experimental.pallas import tpu as pltpu

F32 = jnp.float32
BF16 = jnp.bfloat16
HIGHEST = lax.Precision.HIGHEST

GRID_W = 64
RMS_EPS = 1e-6

S5_GROUP = 16
S5_STATE = 64
S5_T = 16
S5_GB = 8

SSD_HEAD_DIM = 64
SSD_GROUPS = 8
SSD_STATE = 128
SSD_CONV = 5
SSD_CHUNK = 128

N_EXPERT_GROUPS = 4
TOP_K = 2

TOKEN_TILE = 256
MOE_BLOCK = 512
MOE_PIECE = 16
MOE_TILE = 256
VMEM_LIMIT_BYTES = 56 * 1024 * 1024


def _cparams(*sem):
    return pltpu.CompilerParams(dimension_semantics=sem, vmem_limit_bytes=VMEM_LIMIT_BYTES)


def _sigmoid(v):
    return 1.0 / (1.0 + jnp.exp(-v))


def _silu(v):
    return v * _sigmoid(v)


def _gelu_tanh(v):
    return 0.5 * v * (1.0 + jnp.tanh(0.7978845608028654 * (v + 0.044715 * (v * v * v))))


def _rms(v, g):
    return v * lax.rsqrt(jnp.mean(v * v, axis=-1, keepdims=True) + RMS_EPS) * g


def _mod_kernel(cc_ref, w_ref, b_ref, o_ref):
    a = _silu(cc_ref[...])
    o_ref[...] = jnp.dot(a, w_ref[...], preferred_element_type=F32, precision=HIGHEST) + b_ref[...]


def _modulation(c, c_ctx, mod_w, mod_b):
    depth, d, d6 = mod_w.shape
    bsz = c.shape[0]
    rows = 8
    cc = jnp.zeros((rows, d), F32).at[:bsz].set(c).at[bsz].set(c_ctx)
    tn = d6 // 4
    out = pl.pallas_call(
        _mod_kernel,
        grid=(depth, d6 // tn),
        in_specs=[
            pl.BlockSpec((rows, d), lambda i, j: (0, 0)),
            pl.BlockSpec((None, d, tn), lambda i, j: (i, 0, j)),
            pl.BlockSpec((None, 1, tn), lambda i, j: (i, 0, j)),
        ],
        out_specs=pl.BlockSpec((None, rows, tn), lambda i, j: (i, 0, j)),
        out_shape=jax.ShapeDtypeStruct((depth, rows, d6), F32),
        compiler_params=_cparams("parallel", "parallel"),
        name="modulation",
    )(cc, mod_w, mod_b.reshape(depth, 1, d6))
    lat = out[:, :bsz].reshape(depth, bsz, 1, 6, d)
    ctx = jnp.broadcast_to(out[:, bsz].reshape(depth, 1, 1, 6, d), (depth, bsz, 1, 6, d))
    return jnp.concatenate([lat, ctx], axis=2).reshape(depth, bsz * 2, 6, d)


def _prenorm_kernel(x_ref, g_ref, mod_ref, o_ref):
    m = mod_ref[...]
    hn = _rms(x_ref[...], g_ref[...]) * (1.0 + m[1:2]) + m[0:1]
    o_ref[...] = hn.astype(o_ref.dtype)


def _prenorm(xall, g, mods, n_lat_tiles):
    bsz, lt, d = xall.shape
    nt = lt // TOKEN_TILE
    return pl.pallas_call(
        _prenorm_kernel,
        grid=(bsz, nt),
        in_specs=[
            pl.BlockSpec((None, TOKEN_TILE, d), lambda b, i: (b, i, 0)),
            pl.BlockSpec((1, d), lambda b, i: (0, 0)),
            pl.BlockSpec((None, 6, d), lambda b, i: (b * 2 + (i >= n_lat_tiles).astype(jnp.int32), 0, 0)),
        ],
        out_specs=pl.BlockSpec((None, TOKEN_TILE, d), lambda b, i: (b, i, 0)),
        out_shape=jax.ShapeDtypeStruct((bsz, lt, d), F32),
        compiler_params=_cparams("parallel", "parallel"),
        name="prenorm",
    )(xall, g.reshape(1, d), mods)


def _s5_weights(lam_re, lam_im, log_dt, b_re, b_im, c_re, c_im):
    t = S5_T
    k16 = b_re.shape[-1]

    def cmul(ar, ai, br, bi):
        return ar * br - ai * bi, ar * bi + ai * br

    def direction(k):
        lr, li = lam_re[k], lam_im[k]
        step = jnp.exp(log_dt[k])[:, None]
        mag = jnp.exp(lr * step)
        abar_r = mag * jnp.cos(li * step)
        abar_i = mag * jnp.sin(li * step)
        den = lr * lr + li * li
        q_r = ((abar_r - 1.0) * lr + abar_i * li) / den
        q_i = (abar_i * lr - (abar_r - 1.0) * li) / den
        bb_r, bb_i = cmul(q_r[..., None], q_i[..., None], b_re, b_im)

        def power(tau):
            tau = jnp.asarray(tau, F32)[None, :, None]
            m = jnp.exp((lr * step)[:, None, :] * tau)
            return m * jnp.cos((li * step)[:, None, :] * tau), m * jnp.sin((li * step)[:, None, :] * tau)

        return bb_r.transpose(0, 2, 1), bb_i.transpose(0, 2, 1), power

    rows = lambda v: jnp.repeat(v, k16, axis=1)
    row_tile = lambda v: jnp.tile(v, (1, t, 1))
    cols = lambda v: jnp.repeat(v.transpose(0, 2, 1), k16, axis=2)
    col_tile = lambda v: jnp.tile(v, (1, 1, t))
    ct_r, ct_i = c_re.transpose(0, 2, 1), c_im.transpose(0, 2, 1)
    steps = jnp.arange(t)

    def left(bt_r, bt_i, power, tau):
        pr, pi = power(tau)
        return cmul(row_tile(bt_r), row_tile(bt_i), rows(pr), rows(pi))

    def right(power, tau):
        pr, pi = power(tau)
        return cmul(col_tile(ct_r), col_tile(ct_i), cols(pr), cols(pi))

    bf_r, bf_i, pow_f = direction(0)
    bb_r, bb_i, pow_b = direction(1)
    lf_r, lf_i = left(bf_r, bf_i, pow_f, -steps)
    rf_r, rf_i = right(pow_f, steps)
    lb_r, lb_i = left(bb_r, bb_i, pow_b, steps)
    rb_r, rb_i = right(pow_b, -steps)
    lf = jnp.concatenate([lf_r, lf_i], axis=-1)
    lb = jnp.concatenate([lb_r, lb_i], axis=-1)
    rf = jnp.concatenate([rf_r, -rf_i], axis=1)
    rb = jnp.concatenate([rb_r, -rb_i], axis=1)

    sf_r, sf_i = left(bf_r, bf_i, pow_f, t - 1 - steps)
    ws = jnp.concatenate([sf_r, lb_r, sf_i, lb_i], axis=-1)

    of_r, of_i = right(pow_f, steps + 1)
    ob_r, ob_i = right(pow_b, t - steps)
    zero = jnp.zeros_like(of_r)
    w2 = jnp.concatenate([of_r, zero, -of_i, zero, zero, ob_r, zero, -ob_i], axis=1)

    af_r, af_i = pow_f([t])
    ab_r, ab_i = pow_b([t])
    ar = jnp.concatenate([af_r[:, 0], ab_r[:, 0]], axis=-1)
    ai = jnp.concatenate([af_i[:, 0], ab_i[:, 0]], axis=-1)
    return ws.astype(BF16), lf, rf, lb, rb, w2.astype(BF16), ar, ai


def _s5_row_block(nc):
    return max(rb for rb in range(16, min(nc, 176) + 1, 16) if nc % rb == 0)


def _dot_split3(a, b):
    a_hi = a.astype(BF16)
    a_lo = (a - a_hi.astype(F32)).astype(BF16)
    b_hi = b.astype(BF16)
    b_lo = (b - b_hi.astype(F32)).astype(BF16)
    dot = functools.partial(jnp.dot, preferred_element_type=F32)
    return dot(a_hi, b_hi) + dot(a_hi, b_lo) + dot(a_lo, b_hi)


def _s5_kernel(hn_ref, ws_ref, lf_ref, rf_ref, lb_ref, rb_ref, w2_ref, ar_ref, ai_ref, yo_ref,
               u_ref, y_ref, wm_ref, sre, sim, hre_f, him_f, hre_b, him_b, *, n_chunks, n_ctx_chunks, pitch):
    nc, ncc = n_chunks, n_ctx_chunks
    ncl = nc - ncc
    p = S5_STATE
    t_len = S5_T
    gl = S5_GROUP
    per_half = 128 // gl
    rb = _s5_row_block(nc)
    lane_slot = lax.broadcasted_iota(jnp.int32, (rb, 128), 1) // gl

    @pl.when(pl.program_id(1) == 0)
    def _():
        tk = t_len * gl
        src_tok = lax.broadcasted_iota(jnp.int32, (tk, tk), 0) // gl
        dst_tok = lax.broadcasted_iota(jnp.int32, (tk, tk), 1) // gl
        for g in range(S5_GB):
            causal = _dot_split3(lf_ref[g], rf_ref[g])
            anti = _dot_split3(lb_ref[g], rb_ref[g])
            wm = jnp.where(dst_tok >= src_tok, causal, 0.0) + jnp.where(src_tok >= dst_tok, anti, 0.0)
            wm_ref[g] = wm.astype(wm_ref.dtype)

    def gather_u(blk, carry):
        r0 = pl.multiple_of(blk * rb, 16)
        a = [hn_ref[pl.ds(r0 * t_len + t, rb, stride=t_len), :] for t in range(t_len)]
        for i in range(S5_GB):
            halves = []
            for hb in range(t_len // per_half):
                acc = None
                for j in range(per_half):
                    src = a[per_half * hb + j]
                    shift = (gl * (j - i)) % 128
                    rolled = src if shift == 0 else pltpu.roll(src, shift, axis=1)
                    acc = rolled if acc is None else jnp.where(lane_slot == j, rolled, acc)
                halves.append(acc)
            u_ref[i, pl.ds(r0, rb), :] = jnp.concatenate(halves, axis=1).astype(u_ref.dtype)
        return carry

    lax.fori_loop(0, nc // rb, gather_u, 0)

    for g in range(S5_GB):
        s = jnp.dot(u_ref[g], ws_ref[g], preferred_element_type=F32)
        sre[pl.ds(g * pitch, nc), :] = s[:, : 2 * p]
        sim[pl.ds(g * pitch, nc), :] = s[:, 2 * p:]

    ar = ar_ref[...]
    ai = ai_ref[...]
    fwd_lane = lax.broadcasted_iota(jnp.int32, (S5_GB, 2 * p), 1) < p

    def step(k, carry):
        h_r, h_i = carry
        cf = jnp.where(k < ncc, ncl + k, k - ncc)
        cb = nc - 1 - k
        rows_f = pl.ds(cf, S5_GB, stride=pitch)
        rows_b = pl.ds(cb, S5_GB, stride=pitch)
        hre_f[rows_f, :] = h_r
        him_f[rows_f, :] = h_i
        hre_b[rows_b, :] = h_r
        him_b[rows_b, :] = h_i
        s_r = jnp.where(fwd_lane, sre[rows_f, :], sre[rows_b, :])
        s_i = jnp.where(fwd_lane, sim[rows_f, :], sim[rows_b, :])
        n_r = ar * h_r - ai * h_i + s_r
        n_i = ar * h_i + ai * h_r + s_i
        return n_r, n_i

    zero = jnp.zeros((S5_GB, 2 * p), F32)
    lax.fori_loop(0, nc, step, (zero, zero))

    for g in range(S5_GB):
        rows = pl.ds(g * pitch, nc)
        hin = jnp.concatenate([hre_f[rows, :], him_f[rows, :], hre_b[rows, :], him_b[rows, :]], axis=1)
        out = jnp.dot(u_ref[g], wm_ref[g], preferred_element_type=F32)
        out = out + jnp.dot(hin.astype(BF16), w2_ref[g], preferred_element_type=F32)
        y_ref[g] = out

    def scatter_y(blk, carry):
        r0 = pl.multiple_of(blk * rb, 16)
        yv = [y_ref[i, pl.ds(r0, rb), :] for i in range(S5_GB)]
        for t in range(t_len):
            hb, j = divmod(t, per_half)
            acc = None
            for i in range(S5_GB):
                src = yv[i][:, hb * 128:(hb + 1) * 128]
                shift = (gl * (i - j)) % 128
                rolled = src if shift == 0 else pltpu.roll(src, shift, axis=1)
                acc = rolled if acc is None else jnp.where(lane_slot == i, rolled, acc)
            yo_ref[pl.ds(r0 * t_len + t, rb, stride=t_len), :] = acc
        return carry

    lax.fori_loop(0, nc // rb, scatter_y, 0)


def _s5_scan(hn, n_lat, weights):
    bsz, lt, d = hn.shape
    ngrp = d // S5_GROUP
    t = S5_T
    tk = t * S5_GROUP
    nc = lt // t
    ncc = (lt - n_lat) // t
    gb = S5_GB
    assert gb * S5_GROUP == 128 and (128 // S5_GROUP) == gb and t % gb == 0
    ws, lf, rf, lb, rb, w2, ar, ai = weights
    p2 = 2 * S5_STATE
    pitch = nc + 8 if (nc // 8) % 2 == 0 else nc
    kern = functools.partial(_s5_kernel, n_chunks=nc, n_ctx_chunks=ncc, pitch=pitch)
    per_group = lambda *s: pl.BlockSpec((gb,) + s, lambda gi, b: (gi,) + tuple(0 for _ in s))
    return pl.pallas_call(
        kern,
        grid=(ngrp // gb, bsz),
        in_specs=[
            pl.BlockSpec((None, lt, 128), lambda gi, b: (b, 0, gi)),
            per_group(tk, 2 * p2),
            per_group(tk, p2), per_group(p2, tk), per_group(tk, p2), per_group(p2, tk),
            per_group(4 * p2, tk),
            per_group(p2), per_group(p2),
        ],
        out_specs=pl.BlockSpec((None, lt, 128), lambda gi, b: (b, 0, gi)),
        out_shape=jax.ShapeDtypeStruct((bsz, lt, d), F32),
        scratch_shapes=[pltpu.VMEM((gb, nc, tk), BF16), pltpu.VMEM((gb, nc, tk), F32), pltpu.VMEM((gb, tk, tk), BF16)]
        + [pltpu.VMEM((gb * pitch, p2), F32) for _ in range(6)],
        compiler_params=_cparams("parallel", "arbitrary"),
        name="s5_scan",
    )(hn, ws, lf, rf, lb, rb, w2, ar, ai)


def _router_gates(hn2, rw_ref, rb_ref):
    n_exp = rw_ref.shape[0]
    epg = n_exp // N_EXPERT_GROUPS
    logits = lax.dot_general(rw_ref[...], hn2, (((1,), (1,)), ((), ())),
                             preferred_element_type=F32, precision=HIGHEST)
    s = _sigmoid(logits)
    sel = s + rb_ref[...]
    row = [sel[e:e + 1] for e in range(n_exp)]
    gscore = []
    for gi in range(N_EXPERT_GROUPS):
        a, b, c, dd = row[gi * epg: gi * epg + epg]
        hi1, lo1 = jnp.maximum(a, b), jnp.minimum(a, b)
        hi2, lo2 = jnp.maximum(c, dd), jnp.minimum(c, dd)
        gscore.append(jnp.maximum(hi1, hi2) + jnp.maximum(jnp.minimum(hi1, hi2), jnp.maximum(lo1, lo2)))
    gmax = functools.reduce(jnp.maximum, gscore)
    gates = []
    taken = None
    for gi in range(N_EXPERT_GROUPS):
        is_max = gscore[gi] == gmax
        best = is_max if taken is None else jnp.logical_and(is_max, jnp.logical_not(taken))
        taken = is_max if taken is None else jnp.logical_or(taken, is_max)
        for e in range(gi * epg, gi * epg + epg):
            rank = jnp.zeros_like(row[e])
            for j in range(gi * epg, gi * epg + epg):
                if j == e:
                    continue
                ahead = (row[j] >= row[e]) if j < e else (row[j] > row[e])
                rank = rank + ahead.astype(F32)
            chosen = jnp.logical_and(best, rank < float(TOP_K))
            gates.append(jnp.where(chosen, s[e:e + 1], 0.0))
    g = jnp.concatenate(gates, axis=0)
    return g / jnp.sum(g, axis=0, keepdims=True)


def _glu_kernel(y_ref, u_ref, x_ref, mod_ref, d_ref, w_ref, b_ref, g2_ref, rw_ref, rb_ref,
                x1_ref, hn2_ref, gates_ref):
    d = x_ref.shape[-1]
    m = mod_ref[...]
    u = u_ref[...].astype(F32)
    a = _gelu_tanh(y_ref[...].astype(F32) + d_ref[...] * u)
    z = jnp.dot(a.astype(BF16), w_ref[...], preferred_element_type=F32) + b_ref[...]
    out = z[:, :d] * _sigmoid(z[:, d:])
    x1 = x_ref[...] + m[2:3] * out
    x1_ref[...] = x1
    hn2 = _rms(x1, g2_ref[...]) * (1.0 + m[4:5]) + m[3:4]
    hn2_ref[...] = hn2.astype(hn2_ref.dtype)
    gates_ref[...] = _router_gates(hn2, rw_ref, rb_ref)


def _glu_head(y, hn, xall, mods, d_skip, glu_w, glu_b, norm2_g, router_wt, router_b, n_lat_tiles):
    bsz, lt, d = xall.shape
    nt = lt // TOKEN_TILE
    n_exp = router_wt.shape[0]
    tok = pl.BlockSpec((None, TOKEN_TILE, d), lambda b, i: (b, i, 0))
    vec = lambda n: pl.BlockSpec((1, n), lambda b, i: (0, 0))
    return pl.pallas_call(
        _glu_kernel,
        grid=(bsz, nt),
        in_specs=[
            tok, tok, tok,
            pl.BlockSpec((None, 6, d), lambda b, i: (b * 2 + (i >= n_lat_tiles).astype(jnp.int32), 0, 0)),
            vec(d),
            pl.BlockSpec((d, 2 * d), lambda b, i: (0, 0)),
            vec(2 * d),
            vec(d),
            pl.BlockSpec((n_exp, d), lambda b, i: (0, 0)),
            pl.BlockSpec((n_exp, 1), lambda b, i: (0, 0)),
        ],
        out_specs=[
            tok, tok,
            pl.BlockSpec((n_exp, TOKEN_TILE), lambda b, i: (0, b * nt + i)),
        ],
        out_shape=[
            jax.ShapeDtypeStruct((bsz, lt, d), F32),
            jax.ShapeDtypeStruct((bsz, lt, d), BF16),
            jax.ShapeDtypeStruct((n_exp, bsz * lt), F32),
        ],
        compiler_params=_cparams("parallel", "parallel"),
        name="s5_glu_head",
    )(y, hn, xall, mods, d_skip.reshape(1, d), glu_w.astype(BF16), glu_b.reshape(1, 2 * d),
      norm2_g.reshape(1, d), router_wt, router_b.reshape(n_exp, 1))


def _moe_slots_padded(total):
    return jnp.floor((total + float(MOE_PIECE - 1)) * (1.0 / MOE_PIECE)) * float(MOE_PIECE)


def _moe_sort_kernel(gt_ref, t_ref, ts_ref, cnt_ref):
    n_exp, nb = gt_ref.shape
    nbpad = ts_ref.shape[0]
    sel = gt_ref[...] > 0.0
    sel_b = jnp.where(sel, 1.0, 0.0).astype(BF16)
    earlier = lax.broadcasted_iota(jnp.int32, (nb, nb), 0) < lax.broadcasted_iota(jnp.int32, (nb, nb), 1)
    rank = jnp.dot(sel_b, jnp.where(earlier, 1.0, 0.0).astype(BF16), preferred_element_type=F32)
    total = jnp.sum(jnp.where(sel, 1.0, 0.0), axis=1, keepdims=True)
    padded = jnp.broadcast_to(_moe_slots_padded(total), (n_exp, 128))
    below = lax.broadcasted_iota(jnp.int32, (n_exp, n_exp), 1) < lax.broadcasted_iota(jnp.int32, (n_exp, n_exp), 0)
    offs = jnp.dot(jnp.where(below, 1.0, 0.0).astype(BF16), padded.astype(BF16), preferred_element_type=F32)[:, 0:1]
    dest = offs + rank
    d_lo = jnp.min(jnp.where(sel, dest, float(nbpad)), axis=0, keepdims=True).astype(jnp.int32)
    d_hi = jnp.max(jnp.where(sel, dest, -1.0), axis=0, keepdims=True).astype(jnp.int32)
    slot = lax.broadcasted_iota(jnp.int32, (nbpad, nb), 0)
    perm = jnp.where(jnp.logical_or(slot == d_lo, slot == d_hi), 1.0, 0.0).astype(BF16)
    ts_ref[...] = jnp.dot(perm, t_ref[...], preferred_element_type=F32).astype(ts_ref.dtype)
    cnt_ref[...] = padded


def _moe_expert_kernel(tile_e, npieces, rows, ntiles, ts_hbm, w1_ref, w3_ref, w2_ref, ys_hbm,
                       tbuf, ybuf, sem_in, sem_out):
    t = pl.program_id(0)
    nt = ntiles[0]
    last = pl.num_programs(0) - 1
    ppt = MOE_TILE // MOE_PIECE
    slot = t % 2

    def piece_rows(tt, p):
        return pl.ds(pl.multiple_of(rows[tt * ppt + p], MOE_PIECE), MOE_PIECE)

    def for_pieces(tt, fn):
        for p in range(ppt):
            @pl.when(p < npieces[tt])
            def _(p=p):
                fn(p)

    def copy_in(tt, sl, p):
        return pltpu.make_async_copy(ts_hbm.at[piece_rows(tt, p), :],
                                     tbuf.at[sl, pl.ds(p * MOE_PIECE, MOE_PIECE), :], sem_in.at[sl])

    def copy_out(tt, sl, p):
        return pltpu.make_async_copy(ybuf.at[sl, pl.ds(p * MOE_PIECE, MOE_PIECE), :],
                                     ys_hbm.at[piece_rows(tt, p), :], sem_out.at[sl])

    @pl.when(t == 0)
    def _():
        tbuf[...] = jnp.zeros_like(tbuf)

        @pl.when(nt > 0)
        def _():
            for_pieces(0, lambda p: copy_in(0, 0, p).start())

    @pl.when(t + 1 < nt)
    def _():
        for_pieces(t + 1, lambda p: copy_in(t + 1, 1 - slot, p).start())

    @pl.when(jnp.logical_and(t >= 2, t - 2 < nt))
    def _():
        for_pieces(t - 2, lambda p: copy_out(t - 2, slot, p).wait())

    @pl.when(t < nt)
    def _():
        for_pieces(t, lambda p: copy_in(t, slot, p).wait())
        x = tbuf[slot]
        h = _silu(jnp.dot(x, w1_ref[...], preferred_element_type=F32)) * jnp.dot(x, w3_ref[...], preferred_element_type=F32)
        ybuf[slot] = jnp.dot(h.astype(BF16), w2_ref[...], preferred_element_type=F32).astype(ybuf.dtype)
        for_pieces(t, lambda p: copy_out(t, slot, p).start())

    @pl.when(t == last)
    def _():
        @pl.when(jnp.logical_and(last >= 1, last - 1 < nt))
        def _():
            for_pieces(last - 1, lambda p: copy_out(last - 1, 1 - slot, p).wait())

        @pl.when(last < nt)
        def _():
            for_pieces(last, lambda p: copy_out(last, slot, p).wait())


def _moe_unsort_kernel(ys_ref, g_ref, x_ref, g2_ref, *rest, final):
    if final:
        fg_ref, o_ref = rest
    else:
        (o_ref,) = rest
    nb, n_exp = g_ref.shape
    nbpad = ys_ref.shape[0]
    gates = g_ref[...]
    sel = gates > 0.0
    sel_b = jnp.where(sel, 1.0, 0.0).astype(BF16)
    earlier = lax.broadcasted_iota(jnp.int32, (nb, nb), 1) < lax.broadcasted_iota(jnp.int32, (nb, nb), 0)
    rank = jnp.dot(jnp.where(earlier, 1.0, 0.0).astype(BF16), sel_b, preferred_element_type=F32)
    total = jnp.sum(jnp.where(sel, 1.0, 0.0), axis=0, keepdims=True)
    padded = jnp.broadcast_to(_moe_slots_padded(total), (8, n_exp))
    below = lax.broadcasted_iota(jnp.int32, (n_exp, n_exp), 0) < lax.broadcasted_iota(jnp.int32, (n_exp, n_exp), 1)
    offs = jnp.dot(padded.astype(BF16), jnp.where(below, 1.0, 0.0).astype(BF16), preferred_element_type=F32)[0:1]
    dest = offs + rank
    d_lo = jnp.min(jnp.where(sel, dest, float(nbpad)), axis=1, keepdims=True)
    d_hi = jnp.max(jnp.where(sel, dest, -1.0), axis=1, keepdims=True)
    g_lo = jnp.sum(jnp.where(jnp.logical_and(sel, dest == d_lo), gates, 0.0), axis=1, keepdims=True)
    g_hi = jnp.sum(jnp.where(jnp.logical_and(sel, dest == d_hi), gates, 0.0), axis=1, keepdims=True)
    slot = lax.broadcasted_iota(jnp.int32, (nb, nbpad), 1)
    ys = ys_ref[...]

    def pick(dcol):
        onehot = jnp.where(slot == dcol.astype(jnp.int32), 1.0, 0.0).astype(BF16)
        return jnp.dot(onehot, ys, preferred_element_type=F32)

    moe = g_lo * pick(d_lo) + g_hi * pick(d_hi)
    half = nb // g2_ref.shape[0]
    d = x_ref.shape[1]
    for j in range(g2_ref.shape[0]):
        r = slice(j * half, (j + 1) * half)
        xn = x_ref[r, :] + g2_ref[j] * moe[r]
        if final:
            xn = _rms(xn, fg_ref[...])
            for c in range(half // SSD_CHUNK):
                col = (j * half) // SSD_CHUNK + c
                o_ref[:, col * d:(col + 1) * d] = xn[c * SSD_CHUNK:(c + 1) * SSD_CHUNK]
        else:
            o_ref[r, :] = xn


def _moe_schedule(counts, nbpad, n_tiles):
    nblk, n_exp = counts.shape
    ppt = MOE_TILE // MOE_PIECE
    pc = counts // MOE_PIECE
    loc = jnp.cumsum(pc, axis=1) - pc
    cum_b = jnp.cumsum(pc, axis=0)
    np_e = cum_b[-1]
    tiles_e = (np_e + ppt - 1) // ppt
    tile_end = jnp.cumsum(tiles_e)
    ntiles = tile_end[-1]
    t_idx = jnp.arange(n_tiles, dtype=jnp.int32)
    tile_e = jnp.minimum(jnp.sum((tile_end[None, :] <= t_idx[:, None]).astype(jnp.int32), axis=1), n_exp - 1)
    first = (tile_end - tiles_e)[tile_e]
    piece0 = (t_idx - first) * ppt
    npieces = jnp.where(t_idx < ntiles, jnp.clip(np_e[tile_e] - piece0, 0, ppt), 0)
    i = piece0[:, None] + jnp.arange(ppt, dtype=jnp.int32)[None, :]
    cum_t = cum_b.T[tile_e]
    blk = jnp.minimum(jnp.sum((cum_t[:, None, :] <= i[:, :, None]).astype(jnp.int32), axis=2), nblk - 1)
    before = jnp.take_along_axis(cum_t - pc.T[tile_e], blk, axis=1)
    within = i - before + jnp.take_along_axis(loc.T[tile_e], blk, axis=1)
    rows = blk * nbpad + within * MOE_PIECE
    rows = jnp.where(jnp.arange(ppt)[None, :] < npieces[:, None], rows, 0)
    return (tile_e.astype(jnp.int32), npieces.astype(jnp.int32), rows.reshape(-1).astype(jnp.int32),
            ntiles.reshape(1).astype(jnp.int32))


def _moe(t, gates_t, xres, g2rows, w1, w3, w2, *, final_g=None, blocks_per_batch=None):
    n, d = t.shape
    n_exp, _, f = w1.shape
    nb = MOE_BLOCK
    nblk = n // nb
    nbpad = TOP_K * nb + n_exp * MOE_PIECE
    final = final_g is not None

    ts, cnt = pl.pallas_call(
        _moe_sort_kernel,
        grid=(nblk,),
        in_specs=[pl.BlockSpec((n_exp, nb), lambda j: (0, j)), pl.BlockSpec((nb, d), lambda j: (j, 0))],
        out_specs=[pl.BlockSpec((nbpad, d), lambda j: (j, 0)), pl.BlockSpec((None, n_exp, 128), lambda j: (j, 0, 0))],
        out_shape=[jax.ShapeDtypeStruct((nblk * nbpad, d), BF16), jax.ShapeDtypeStruct((nblk, n_exp, 128), F32)],
        compiler_params=_cparams("parallel"),
        name="moe_sort",
    )(gates_t, t)

    n_tiles = nblk * nbpad // MOE_TILE + n_exp
    tile_e, npieces, rows, ntiles = _moe_schedule(cnt[:, :, 0].astype(jnp.int32), nbpad, n_tiles)
    wspec = lambda shape: pl.BlockSpec((None,) + shape, lambda i, te, npc, rw, nt: (te[i], 0, 0))
    ys = pl.pallas_call(
        _moe_expert_kernel,
        grid_spec=pltpu.PrefetchScalarGridSpec(
            num_scalar_prefetch=4,
            grid=(n_tiles,),
            in_specs=[pl.BlockSpec(memory_space=pl.ANY), wspec((d, f)), wspec((d, f)), wspec((f, d))],
            out_specs=pl.BlockSpec(memory_space=pl.ANY),
            scratch_shapes=[pltpu.VMEM((2, MOE_TILE, d), BF16), pltpu.VMEM((2, MOE_TILE, d), BF16),
                            pltpu.SemaphoreType.DMA((2,)), pltpu.SemaphoreType.DMA((2,))],
        ),
        out_shape=jax.ShapeDtypeStruct((nblk * nbpad, d), BF16),
        input_output_aliases={4: 0},
        compiler_params=_cparams("arbitrary"),
        name="moe_experts",
    )(tile_e, npieces, rows, ntiles, ts, w1, w3, w2)

    halves = nb // TOKEN_TILE
    in_specs = [
        pl.BlockSpec((nbpad, d), lambda j: (j, 0)),
        pl.BlockSpec((nb, n_exp), lambda j: (j, 0)),
        pl.BlockSpec((nb, d), lambda j: (j, 0)),
        pl.BlockSpec((halves, 1, d), lambda j: (j, 0, 0)),
    ]
    args = [ys, gates_t.T, xres, g2rows]
    if final:
        in_specs.append(pl.BlockSpec((1, d), lambda j: (0, 0)))
        args.append(final_g.reshape(1, d))
        cols = nb // SSD_CHUNK
        out_spec = pl.BlockSpec((None, SSD_CHUNK, cols * d), lambda j: (j // blocks_per_batch, 0, j % blocks_per_batch))
        out_shape = jax.ShapeDtypeStruct((nblk // blocks_per_batch, SSD_CHUNK, GRID_W * d), F32)
    else:
        out_spec = pl.BlockSpec((nb, d), lambda j: (j, 0))
        out_shape = jax.ShapeDtypeStruct((n, d), F32)
    return pl.pallas_call(
        functools.partial(_moe_unsort_kernel, final=final),
        grid=(nblk,),
        in_specs=in_specs,
        out_specs=out_spec,
        out_shape=out_shape,
        compiler_params=_cparams("parallel"),
        name="moe_unsort_final" if final else "moe_unsort",
    )(*args)


SSD_COLS_PER_STEP = 8


def _ssd_inproj_kernel(xl_ref, xc_ref, g_ref, mod_ref, wz_ref, wx_ref, wdt_ref, wdtt_ref, bias_ref, biast_ref,
                       a_ref, at_ref, cw_ref, cb_ref, z_ref, xbc_ref, csdt_ref, cst_ref, xp_ref, slab, xt, ext,
                       *, n_lat_steps, n_ctx_chunks):
    step = pl.program_id(1)
    i = pl.program_id(2)
    q = SSD_CHUNK
    ncols = SSD_COLS_PER_STEP
    d = xt.shape[1]
    is_lat = step < n_lat_steps
    c = step * ncols + i
    n_lat_chunks = n_lat_steps * ncols
    n_chunks = n_lat_chunks + n_ctx_chunks

    def emit_conv():
        acc = jnp.zeros(xbc_ref.shape, F32) + cb_ref[...]
        w = cw_ref[...]
        for k in range(SSD_CONV):
            acc = acc + w[k:k + 1] * ext[pl.ds(8 - SSD_CONV // 2 + k, q), :]
        xbc_ref[...] = _silu(acc).astype(xbc_ref.dtype)

    @pl.when(jnp.logical_and(i == 0, is_lat))
    def _():
        for s in range(d // 128):
            slab[...] = xl_ref[:, :, s * 128:(s + 1) * 128].reshape(q * ncols, 128)
            for w in range(ncols):
                xt[w * q:(w + 1) * q, s * 128:(s + 1) * 128] = slab[pl.ds(w, q, stride=ncols), :]

    @pl.when(jnp.logical_and(i == 0, jnp.logical_not(is_lat)))
    def _():
        xt[0:n_ctx_chunks * q, :] = xc_ref[...]

    @pl.when(jnp.logical_or(is_lat, i < n_ctx_chunks))
    def _():
        m = mod_ref[...]
        x = xt[pl.ds(pl.multiple_of(i * q, q), q), :]
        xp_ref[...] = x
        hn = (_rms(x, g_ref[...]) * (1.0 + m[1:2]) + m[0:1]).astype(BF16)
        z_ref[...] = jnp.dot(hn, wz_ref[...], preferred_element_type=F32).astype(z_ref.dtype)
        xbc_new = jnp.dot(hn, wx_ref[...], preferred_element_type=F32)
        starts = jnp.logical_or(c == 0, c == n_lat_chunks)
        ext[8 + q:16 + q, :] = jnp.where(starts, 0.0, xbc_new[0:8])

        @pl.when(c >= 1)
        def _():
            emit_conv()

        ext[0:8, :] = jnp.where(starts, 0.0, ext[q:q + 8, :])
        ext[8:8 + q, :] = xbc_new

        def softplus(v):
            return jnp.maximum(v, 0.0) + jnp.log(1.0 + jnp.exp(-jnp.abs(v)))

        r_i = lax.broadcasted_iota(jnp.int32, (q, q), 0)
        c_i = lax.broadcasted_iota(jnp.int32, (q, q), 1)
        lower = (r_i >= c_i).astype(F32)
        upper = (r_i <= c_i).astype(F32)
        dt = softplus(jnp.dot(hn, wdt_ref[...], preferred_element_type=F32) + bias_ref[...])
        da = dt * a_ref[...]
        half = da.shape[1] // 2
        cs = jnp.concatenate([jnp.dot(lower, da[:, :half], preferred_element_type=F32, precision=HIGHEST),
                              jnp.dot(upper, da[:, half:], preferred_element_type=F32, precision=HIGHEST)], axis=1)
        lane = lax.broadcasted_iota(jnp.int32, dt.shape, 1)
        csdt_ref[...] = jnp.where(lane % 8 < 4, cs, dt)
        nh = at_ref.shape[0] // 2
        dtt = softplus(lax.dot_general(wdtt_ref[...], hn, (((1,), (1,)), ((), ())), preferred_element_type=F32)
                       + biast_ref[...])
        dat = dtt * at_ref[...]
        cst_f = jnp.dot(dat[:nh], upper, preferred_element_type=F32, precision=HIGHEST)
        cst_b = jnp.dot(dat[nh:], lower, preferred_element_type=F32, precision=HIGHEST)
        cst = jnp.concatenate([cst_f, cst_b], axis=0)
        for j in range(cst_ref.shape[0]):
            cst_ref[j] = cst[j * 4:(j + 1) * 4, :]

    @pl.when(c == n_chunks)
    def _():
        ext[8 + q:16 + q, :] = jnp.zeros((8, ext.shape[1]), F32)
        emit_conv()


def _ssd_inproj(xall, n_lat, norm_g, mods, in_w, dt_bias, a_log, conv_w, conv_b):
    bsz, lt, d = xall.shape
    q = SSD_CHUNK
    ncols = SSD_COLS_PER_STEP
    ncl = n_lat // q
    nc = lt // q
    ncc = nc - ncl
    n_ctx = lt - n_lat
    nh2 = dt_bias.size
    d_inner = (nh2 // 2) * SSD_HEAD_DIM
    conv_ch = in_w.shape[1] - d_inner - nh2
    wz = in_w[:, :d_inner].astype(BF16)
    wx = in_w[:, d_inner:d_inner + conv_ch].astype(BF16)
    wdt = in_w[:, d_inner + conv_ch:].astype(BF16)
    a = -jnp.exp(a_log.astype(F32)).reshape(nh2)
    bias = dt_bias.astype(F32).reshape(nh2)
    ngr = nh2 // 4
    lanes = jnp.arange(2 * nh2)
    dup = (lanes // 8) * 4 + lanes % 4
    assert n_lat // GRID_W == q and GRID_W % ncols == 0 and lt % GRID_W == 0 and ncc < ncols and n_lat % n_ctx == 0
    nls = ncl // ncols
    xgrid = xall.reshape(bsz, lt // GRID_W, GRID_W, d)
    kern = functools.partial(_ssd_inproj_kernel, n_lat_steps=nls, n_ctx_chunks=ncc)
    full = lambda s: pl.BlockSpec(s, lambda b, st, i: tuple(0 for _ in s))
    chunk = lambda b, st, i: jnp.minimum(st * ncols + i, nc - 1)
    rows = lambda width: pl.BlockSpec((None, q, width), lambda b, st, i: (b, chunk(b, st, i), 0))
    return pl.pallas_call(
        kern,
        grid=(bsz, nls + 1, ncols),
        in_specs=[
            pl.BlockSpec((None, q, ncols, d), lambda b, st, i: (b, 0, jnp.minimum(st, nls - 1), 0)),
            pl.BlockSpec((None, n_ctx, d), lambda b, st, i: (b, n_lat // n_ctx, 0)),
            full((1, d)),
            pl.BlockSpec((None, 6, d), lambda b, st, i: (b * 2 + (st >= nls).astype(jnp.int32), 0, 0)),
            full((d, d_inner)), full((d, conv_ch)), full((d, 2 * nh2)), full((nh2, d)),
            full((1, 2 * nh2)), full((nh2, 1)), full((1, 2 * nh2)), full((nh2, 1)),
            full((SSD_CONV, conv_ch)), full((1, conv_ch)),
        ],
        out_specs=[
            rows(d_inner),
            pl.BlockSpec((None, q, conv_ch), lambda b, st, i: (b, jnp.clip(st * ncols + i - 1, 0, nc - 1), 0)),
            rows(2 * nh2),
            pl.BlockSpec((None, None, ngr, 4, q), lambda b, st, i: (b, chunk(b, st, i), 0, 0, 0)),
            rows(d),
        ],
        out_shape=[
            jax.ShapeDtypeStruct((bsz, lt, d_inner), BF16),
            jax.ShapeDtypeStruct((bsz, lt, conv_ch), BF16),
            jax.ShapeDtypeStruct((bsz, lt, 2 * nh2), F32),
            jax.ShapeDtypeStruct((bsz, nc, ngr, 4, q), F32),
            jax.ShapeDtypeStruct((bsz, lt, d), F32),
        ],
        scratch_shapes=[pltpu.VMEM((q * ncols, 128), F32), pltpu.VMEM((q * ncols, d), F32),
                        pltpu.VMEM((q + 16, conv_ch), F32)],
        compiler_params=_cparams("parallel", "arbitrary", "arbitrary"),
        name="ssd_inproj",
    )(xgrid, xall, norm_g.reshape(1, d), mods, wz, wx, wdt[:, dup], wdt.T, bias[dup].reshape(1, -1),
      bias.reshape(nh2, 1), a[dup].reshape(1, -1), a.reshape(nh2, 1), conv_w.astype(F32),
      conv_b.astype(F32).reshape(1, conv_ch))


def _ssd_scan_dir(x_ref, b_ref, c_ref, csdt_ref, cst_ref, y_ref, state, reverse, lane_group):
    q = SSD_CHUNK
    hp = SSD_HEAD_DIM
    gw = x_ref.shape[1]
    r = gw // hp
    x = x_ref[...].astype(F32)
    bm = b_ref[...]
    cm = c_ref[...]
    cst = cst_ref[...]
    v = pltpu.roll(csdt_ref[...], (128 - 2 * r * lane_group) % 128, axis=1)[:, :2 * r]
    v_hi = v.astype(BF16)
    v_r1 = v - v_hi.astype(F32)
    v_mid = v_r1.astype(BF16)
    v_lo = (v_r1 - v_mid.astype(F32)).astype(BF16)
    lhs = jnp.concatenate([v_hi, v_mid, v_lo], axis=1)
    wide = 2 * gw + r * q
    col = lax.broadcasted_iota(jnp.int32, (6 * r, wide), 0) % (2 * r)
    ln = lax.broadcasted_iota(jnp.int32, (6 * r, wide), 1)
    cs_hit = jnp.logical_or(jnp.logical_and(ln < gw, ln // hp == col),
                            jnp.logical_and(ln >= 2 * gw, (ln - 2 * gw) // q == col))
    dt_hit = jnp.logical_and(jnp.logical_and(ln >= gw, ln < 2 * gw), (ln - gw) // hp == col - r)
    hit = jnp.logical_or(jnp.logical_and(col < r, cs_hit), jnp.logical_and(col >= r, dt_hit))
    spread = jnp.where(hit, 1.0, 0.0).astype(BF16)
    spread_out = jnp.dot(lhs, spread, preferred_element_type=F32)
    cs_x = spread_out[:, :gw]
    dt_x = spread_out[:, gw:2 * gw]
    cs_q = spread_out[:, 2 * gw:]
    end = 0 if reverse else q - 1
    cs_end = cs_x[end:end + 1, :]
    xdt = x * dt_x
    xw = (xdt * jnp.exp(cs_end - cs_x)).astype(BF16)
    xdt = xdt.astype(BF16)
    cb = lax.dot_general(cm, bm, (((1,), (1,)), ((), ())), preferred_element_type=F32)
    y_off = jnp.dot(cm, state[...].astype(BF16), preferred_element_type=F32) * jnp.exp(cs_x)
    r_i = lax.broadcasted_iota(jnp.int32, (q, q), 0)
    c_i = lax.broadcasted_iota(jnp.int32, (q, q), 1)
    mask = (r_i <= c_i) if reverse else (r_i >= c_i)
    lane = lax.broadcasted_iota(jnp.int32, (q, 2 * hp), 1)
    ys = []
    for pair in range(r // 2):
        xp = xdt[:, pair * 2 * hp:(pair + 1) * 2 * hp]
        yd = []
        for h in (2 * pair, 2 * pair + 1):
            seg = jnp.exp(jnp.minimum(cs_q[:, h * q:(h + 1) * q] - cst[h:h + 1, :], 0.0))
            gmat = (cb * jnp.where(mask, seg, 0.0)).astype(BF16)
            yd.append(jnp.dot(gmat, xp, preferred_element_type=F32))
        ys.append(jnp.where(lane < hp, yd[0], yd[1]))
    y_ref[...] = (jnp.concatenate(ys, axis=1) + y_off).astype(y_ref.dtype)
    upd = lax.dot_general(bm, xw, (((0,), (0,)), ((), ())), preferred_element_type=F32)
    state[...] = state[...] * jnp.exp(cs_end) + upd


def _ssd_scan_kernel(xf, bf, cf, csdtf, cstf, xb, bb, cb, csdtb, cstb, yf_ref, yb_ref, state_f, state_b):
    @pl.when(pl.program_id(2) == 0)
    def _():
        state_f[...] = jnp.zeros_like(state_f)
        state_b[...] = jnp.zeros_like(state_b)

    g = pl.program_id(1)
    _ssd_scan_dir(xf, bf, cf, csdtf, cstf, yf_ref, state_f, False, g)
    _ssd_scan_dir(xb, bb, cb, csdtb, cstb, yb_ref, state_b, True, g + pl.num_programs(1))


def _ssd_scan(xbc, csdt, cst, n_lat):
    bsz, lt, ch = xbc.shape
    q = SSD_CHUNK
    nc, ncl = lt // q, n_lat // q
    ngr = SSD_GROUPS
    n = SSD_STATE
    d_inner = ch - 2 * ngr * n
    gw = d_inner // ngr
    bcol, ccol = d_inner // n, d_inner // n + ngr

    def specs(chunk, doff):
        return [
            pl.BlockSpec((None, q, gw), lambda b, g, k: (b, chunk(k), g)),
            pl.BlockSpec((None, q, n), lambda b, g, k: (b, chunk(k), bcol + g)),
            pl.BlockSpec((None, q, n), lambda b, g, k: (b, chunk(k), ccol + g)),
            pl.BlockSpec((None, q, csdt.shape[2]), lambda b, g, k: (b, chunk(k), 0)),
            pl.BlockSpec((None, None, None, 4, q), lambda b, g, k: (b, chunk(k), doff + g, 0, 0)),
        ]

    fwd = lambda k: (k + ncl) % nc
    bwd = lambda k: nc - 1 - k
    out = jax.ShapeDtypeStruct((bsz, lt, d_inner), BF16)
    return pl.pallas_call(
        _ssd_scan_kernel,
        grid=(bsz, ngr, nc),
        in_specs=specs(fwd, 0) + specs(bwd, ngr),
        out_specs=[pl.BlockSpec((None, q, gw), lambda b, g, k: (b, fwd(k), g)),
                   pl.BlockSpec((None, q, gw), lambda b, g, k: (b, bwd(k), g))],
        out_shape=[out, out],
        scratch_shapes=[pltpu.VMEM((n, gw), F32), pltpu.VMEM((n, gw), F32)],
        compiler_params=_cparams("parallel", "parallel", "arbitrary"),
        name="ssd_scan",
    )(xbc, xbc, xbc, csdt, cst, xbc, xbc, xbc, csdt, cst)


def _ssd_finish_kernel(yf_ref, yb_ref, xs_ref, z_ref, x_ref, mod_ref, dsk_ref, ng_ref, w_ref, g2_ref,
                       rw_ref, rb_ref, x3_ref, hn2_ref, gates_ref):
    m = mod_ref[...]
    y = yf_ref[...].astype(F32) + yb_ref[...].astype(F32) + dsk_ref[...] * xs_ref[...].astype(F32)
    gated = y * _silu(z_ref[...].astype(F32))
    nrm = _rms(gated, ng_ref[...])
    out = jnp.dot(nrm.astype(BF16), w_ref[...], preferred_element_type=F32)
    x3 = x_ref[...] + m[2:3] * out
    x3_ref[...] = x3
    hn2 = _rms(x3, g2_ref[...]) * (1.0 + m[4:5]) + m[3:4]
    hn2_ref[...] = hn2.astype(hn2_ref.dtype)
    gates_ref[...] = _router_gates(hn2, rw_ref, rb_ref)


def _ssd_finish(yf, yb, xbc, z, xall, n_lat, mods, d_skip, norm_g, out_w, norm2_g, router_wt, router_b):
    bsz, lt, d_inner = z.shape
    d = xall.shape[-1]
    q = SSD_CHUNK
    ncl = n_lat // q
    n_exp = router_wt.shape[0]
    inner = pl.BlockSpec((None, q, d_inner), lambda b, c: (b, c, 0))
    tok = pl.BlockSpec((None, q, d), lambda b, c: (b, c, 0))
    full = lambda s: pl.BlockSpec(s, lambda b, c: tuple(0 for _ in s))
    dsk = jnp.repeat(d_skip.astype(F32), SSD_HEAD_DIM).reshape(1, d_inner)
    return pl.pallas_call(
        _ssd_finish_kernel,
        grid=(bsz, ncl),
        in_specs=[
            inner, inner, inner, inner,
            tok,
            pl.BlockSpec((None, 6, d), lambda b, c: (b * 2, 0, 0)),
            full((1, d_inner)), full((1, d_inner)), full((d_inner, d)), full((1, d)),
            full((n_exp, d)), full((n_exp, 1)),
        ],
        out_specs=[tok, tok, pl.BlockSpec((n_exp, q), lambda b, c: (0, b * ncl + c))],
        out_shape=[
            jax.ShapeDtypeStruct((bsz, n_lat, d), F32),
            jax.ShapeDtypeStruct((bsz, n_lat, d), BF16),
            jax.ShapeDtypeStruct((n_exp, bsz * n_lat), F32),
        ],
        compiler_params=_cparams("parallel", "parallel"),
        name="ssd_finish",
    )(yf, yb, xbc, z, xall, mods, dsk, norm_g.reshape(1, d_inner), out_w.astype(BF16),
      norm2_g.reshape(1, d), router_wt, router_b.reshape(n_exp, 1))


def kernel(x, c, ctx, c_ctx, mod_w, mod_b, norm1_g, norm2_g, final_g, s5_lam_re, s5_lam_im, s5_log_dt, s5_b_re, s5_b_im, s5_c_re, s5_c_im, s5_d, s5_glu_w, s5_glu_b, ssd_in_w, ssd_conv_w, ssd_conv_b, ssd_dt_bias, ssd_a_log, ssd_d, ssd_norm_g, ssd_out_w, router_w, router_b, moe_w1, moe_w3, moe_w2):
    bsz, n_lat, d = x.shape
    n_ctx = ctx.shape[1]
    lt = n_lat + n_ctx
    n_exp = router_w.shape[1]
    assert n_lat % TOKEN_TILE == 0 and n_ctx % TOKEN_TILE == 0
    assert (bsz * lt) % MOE_BLOCK == 0 and n_lat % MOE_BLOCK == 0 and MOE_BLOCK % TOKEN_TILE == 0
    assert TOKEN_TILE % SSD_CHUNK == 0

    mods = _modulation(c, c_ctx, mod_w, mod_b)
    router_wt = router_w.T.astype(F32)
    w1 = moe_w1.astype(BF16)
    w3 = moe_w3.astype(BF16)
    w2 = moe_w2.astype(BF16)
    nlt = n_lat // TOKEN_TILE
    tpb = lt // TOKEN_TILE

    xall = jnp.concatenate([x, ctx], axis=1)
    hn = _prenorm(xall, norm1_g[0], mods[0], nlt)
    s5w = _s5_weights(s5_lam_re[0], s5_lam_im[0], s5_log_dt[0], s5_b_re[0], s5_b_im[0], s5_c_re[0], s5_c_im[0])
    y = _s5_scan(hn, n_lat, s5w)
    x1, hn2, gates_t = _glu_head(y, hn, xall, mods[0], s5_d[0], s5_glu_w[0], s5_glu_b[0], norm2_g[0],
                                 router_wt, router_b, nlt)
    g2_lat = jnp.broadcast_to(mods[0][0::2, None, 5], (bsz, nlt, d))
    g2_ctx = jnp.broadcast_to(mods[0][1::2, None, 5], (bsz, tpb - nlt, d))
    g2rows = jnp.concatenate([g2_lat, g2_ctx], axis=1).reshape(bsz * tpb, 1, d)
    x2 = _moe(hn2.reshape(bsz * lt, d), gates_t, x1.reshape(bsz * lt, d), g2rows, w1[0], w3[0], w2[0]).reshape(bsz, lt, d)

    z, xbc, csdt, cst, x2p = _ssd_inproj(x2, n_lat, norm1_g[1], mods[1], ssd_in_w[0], ssd_dt_bias[0], ssd_a_log[0],
                                             ssd_conv_w[0], ssd_conv_b[0])
    yf, yb = _ssd_scan(xbc, csdt, cst, n_lat)
    x3, hn3, gates3_t = _ssd_finish(yf, yb, xbc, z, x2p, n_lat, mods[1], ssd_d[0], ssd_norm_g[0], ssd_out_w[0],
                                    norm2_g[1], router_wt, router_b)
    g2rows = jnp.broadcast_to(mods[1][0::2, None, 5], (bsz, nlt, d)).reshape(bsz * nlt, 1, d)
    out = _moe(hn3.reshape(bsz * n_lat, d), gates3_t, x3.reshape(bsz * n_lat, d), g2rows, w1[1], w3[1], w2[1],
               final_g=final_g, blocks_per_batch=n_lat // MOE_BLOCK)
    return out.reshape(bsz, n_lat, d)
```

```python
import functools

import jax
import jax.numpy as jnp
from jax import lax
from jax.experimental import pallas as pl
from jax.experimental.pallas import tpu as pltpu

F32 = jnp.float32
BF16 = jnp.bfloat16
HIGHEST = lax.Precision.HIGHEST

GRID_W = 64
RMS_EPS = 1e-6

S5_GROUP = 16
S5_STATE = 64
S5_T = 16
S5_GB = 8

SSD_HEAD_DIM = 64
SSD_GROUPS = 8
SSD_STATE = 128
SSD_CONV = 5
SSD_CHUNK = 128

N_EXPERT_GROUPS = 4
TOP_K = 2

TOKEN_TILE = 256
MOE_BLOCK = 512
MOE_PIECE = 16
MOE_TILE = 256
VMEM_LIMIT_BYTES = 56 * 1024 * 1024


def _cparams(*sem):
    return pltpu.CompilerParams(dimension_semantics=sem, vmem_limit_bytes=VMEM_LIMIT_BYTES)


def _sigmoid(v):
    return 1.0 / (1.0 + jnp.exp(-v))


def _silu(v):
    return v * _sigmoid(v)


def _gelu_tanh(v):
    return 0.5 * v * (1.0 + jnp.tanh(0.7978845608028654 * (v + 0.044715 * (v * v * v))))


def _rms(v, g):
    return v * lax.rsqrt(jnp.mean(v * v, axis=-1, keepdims=True) + RMS_EPS) * g


def _mod_kernel(cc_ref, w_ref, b_ref, o_ref):
    a = _silu(cc_ref[...])
    o_ref[...] = jnp.dot(a, w_ref[...], preferred_element_type=F32, precision=HIGHEST) + b_ref[...]


def _modulation(c, c_ctx, mod_w, mod_b):
    depth, d, d6 = mod_w.shape
    bsz = c.shape[0]
    rows = 8
    cc = jnp.zeros((rows, d), F32).at[:bsz].set(c).at[bsz].set(c_ctx)
    tn = d6 // 4
    out = pl.pallas_call(
        _mod_kernel,
        grid=(depth, d6 // tn),
        in_specs=[
            pl.BlockSpec((rows, d), lambda i, j: (0, 0)),
            pl.BlockSpec((None, d, tn), lambda i, j: (i, 0, j)),
            pl.BlockSpec((None, 1, tn), lambda i, j: (i, 0, j)),
        ],
        out_specs=pl.BlockSpec((None, rows, tn), lambda i, j: (i, 0, j)),
        out_shape=jax.ShapeDtypeStruct((depth, rows, d6), F32),
        compiler_params=_cparams("parallel", "parallel"),
        name="modulation",
    )(cc, mod_w, mod_b.reshape(depth, 1, d6))
    lat = out[:, :bsz].reshape(depth, bsz, 1, 6, d)
    ctx = jnp.broadcast_to(out[:, bsz].reshape(depth, 1, 1, 6, d), (depth, bsz, 1, 6, d))
    return jnp.concatenate([lat, ctx], axis=2).reshape(depth, bsz * 2, 6, d)


def _prenorm_kernel(x_ref, g_ref, mod_ref, o_ref):
    m = mod_ref[...]
    hn = _rms(x_ref[...], g_ref[...]) * (1.0 + m[1:2]) + m[0:1]
    o_ref[...] = hn.astype(o_ref.dtype)


def _prenorm(xall, g, mods, n_lat_tiles):
    bsz, lt, d = xall.shape
    nt = lt // TOKEN_TILE
    return pl.pallas_call(
        _prenorm_kernel,
        grid=(bsz, nt),
        in_specs=[
            pl.BlockSpec((None, TOKEN_TILE, d), lambda b, i: (b, i, 0)),
            pl.BlockSpec((1, d), lambda b, i: (0, 0)),
            pl.BlockSpec((None, 6, d), lambda b, i: (b * 2 + (i >= n_lat_tiles).astype(jnp.int32), 0, 0)),
        ],
        out_specs=pl.BlockSpec((None, TOKEN_TILE, d), lambda b, i: (b, i, 0)),
        out_shape=jax.ShapeDtypeStruct((bsz, lt, d), F32),
        compiler_params=_cparams("parallel", "parallel"),
        name="prenorm",
    )(xall, g.reshape(1, d), mods)


def _s5_weights(lam_re, lam_im, log_dt, b_re, b_im, c_re, c_im):
    t = S5_T
    k16 = b_re.shape[-1]

    def cmul(ar, ai, br, bi):
        return ar * br - ai * bi, ar * bi + ai * br

    def direction(k):
        lr, li = lam_re[k], lam_im[k]
        step = jnp.exp(log_dt[k])[:, None]
        mag = jnp.exp(lr * step)
        abar_r = mag * jnp.cos(li * step)
        abar_i = mag * jnp.sin(li * step)
        den = lr * lr + li * li
        q_r = ((abar_r - 1.0) * lr + abar_i * li) / den
        q_i = (abar_i * lr - (abar_r - 1.0) * li) / den
        bb_r, bb_i = cmul(q_r[..., None], q_i[..., None], b_re, b_im)

        def power(tau):
            tau = jnp.asarray(tau, F32)[None, :, None]
            m = jnp.exp((lr * step)[:, None, :] * tau)
            return m * jnp.cos((li * step)[:, None, :] * tau), m * jnp.sin((li * step)[:, None, :] * tau)

        return bb_r.transpose(0, 2, 1), bb_i.transpose(0, 2, 1), power

    rows = lambda v: jnp.repeat(v, k16, axis=1)
    row_tile = lambda v: jnp.tile(v, (1, t, 1))
    cols = lambda v: jnp.repeat(v.transpose(0, 2, 1), k16, axis=2)
    col_tile = lambda v: jnp.tile(v, (1, 1, t))
    ct_r, ct_i = c_re.transpose(0, 2, 1), c_im.transpose(0, 2, 1)
    steps = jnp.arange(t)

    def left(bt_r, bt_i, power, tau):
        pr, pi = power(tau)
        return cmul(row_tile(bt_r), row_tile(bt_i), rows(pr), rows(pi))

    def right(power, tau):
        pr, pi = power(tau)
        return cmul(col_tile(ct_r), col_tile(ct_i), cols(pr), cols(pi))

    bf_r, bf_i, pow_f = direction(0)
    bb_r, bb_i, pow_b = direction(1)
    lf_r, lf_i = left(bf_r, bf_i, pow_f, -steps)
    rf_r, rf_i = right(pow_f, steps)
    lb_r, lb_i = left(bb_r, bb_i, pow_b, steps)
    rb_r, rb_i = right(pow_b, -steps)
    lf = jnp.concatenate([lf_r, lf_i], axis=-1)
    lb = jnp.concatenate([lb_r, lb_i], axis=-1)
    rf = jnp.concatenate([rf_r, -rf_i], axis=1)
    rb = jnp.concatenate([rb_r, -rb_i], axis=1)

    sf_r, sf_i = left(bf_r, bf_i, pow_f, t - 1 - steps)
    ws = jnp.concatenate([sf_r, lb_r, sf_i, lb_i], axis=-1)

    of_r, of_i = right(pow_f, steps + 1)
    ob_r, ob_i = right(pow_b, t - steps)
    zero = jnp.zeros_like(of_r)
    w2 = jnp.concatenate([of_r, zero, -of_i, zero, zero, ob_r, zero, -ob_i], axis=1)

    af_r, af_i = pow_f([t])
    ab_r, ab_i = pow_b([t])
    ar = jnp.concatenate([af_r[:, 0], ab_r[:, 0]], axis=-1)
    ai = jnp.concatenate([af_i[:, 0], ab_i[:, 0]], axis=-1)
    return ws.astype(BF16), lf, rf, lb, rb, w2.astype(BF16), ar, ai


def _s5_row_block(nc):
    return max(rb for rb in range(16, min(nc, 176) + 1, 16) if nc % rb == 0)


def _dot_split3(a, b):
    a_hi = a.astype(BF16)
    a_lo = (a - a_hi.astype(F32)).astype(BF16)
    b_hi = b.astype(BF16)
    b_lo = (b - b_hi.astype(F32)).astype(BF16)
    dot = functools.partial(jnp.dot, preferred_element_type=F32)
    return dot(a_hi, b_hi) + dot(a_hi, b_lo) + dot(a_lo, b_hi)


def _s5_kernel(hn_ref, ws_ref, lf_ref, rf_ref, lb_ref, rb_ref, w2_ref, ar_ref, ai_ref, yo_ref,
               u_ref, y_ref, wm_ref, sre, sim, hre_f, him_f, hre_b, him_b, *, n_chunks, n_ctx_chunks, pitch):
    nc, ncc = n_chunks, n_ctx_chunks
    ncl = nc - ncc
    p = S5_STATE
    t_len = S5_T
    gl = S5_GROUP
    per_half = 128 // gl
    rb = _s5_row_block(nc)
    lane_slot = lax.broadcasted_iota(jnp.int32, (rb, 128), 1) // gl

    @pl.when(pl.program_id(1) == 0)
    def _():
        tk = t_len * gl
        src_tok = lax.broadcasted_iota(jnp.int32, (tk, tk), 0) // gl
        dst_tok = lax.broadcasted_iota(jnp.int32, (tk, tk), 1) // gl
        for g in range(S5_GB):
            causal = _dot_split3(lf_ref[g], rf_ref[g])
            anti = _dot_split3(lb_ref[g], rb_ref[g])
            wm = jnp.where(dst_tok >= src_tok, causal, 0.0) + jnp.where(src_tok >= dst_tok, anti, 0.0)
            wm_ref[g] = wm.astype(wm_ref.dtype)

    def gather_u(blk, carry):
        r0 = pl.multiple_of(blk * rb, 16)
        a = [hn_ref[pl.ds(r0 * t_len + t, rb, stride=t_len), :] for t in range(t_len)]
        for i in range(S5_GB):
            halves = []
            for hb in range(t_len // per_half):
                acc = None
                for j in range(per_half):
                    src = a[per_half * hb + j]
                    shift = (gl * (j - i)) % 128
                    rolled = src if shift == 0 else pltpu.roll(src, shift, axis=1)
                    acc = rolled if acc is None else jnp.where(lane_slot == j, rolled, acc)
                halves.append(acc)
            u_ref[i, pl.ds(r0, rb), :] = jnp.concatenate(halves, axis=1).astype(u_ref.dtype)
        return carry

    lax.fori_loop(0, nc // rb, gather_u, 0)

    for g in range(S5_GB):
        s = jnp.dot(u_ref[g], ws_ref[g], preferred_element_type=F32)
        sre[pl.ds(g * pitch, nc), :] = s[:, : 2 * p]
        sim[pl.ds(g * pitch, nc), :] = s[:, 2 * p:]

    ar = ar_ref[...]
    ai = ai_ref[...]
    fwd_lane = lax.broadcasted_iota(jnp.int32, (S5_GB, 2 * p), 1) < p

    def step(k, carry):
        h_r, h_i = carry
        cf = jnp.where(k < ncc, ncl + k, k - ncc)
        cb = nc - 1 - k
        rows_f = pl.ds(cf, S5_GB, stride=pitch)
        rows_b = pl.ds(cb, S5_GB, stride=pitch)
        hre_f[rows_f, :] = h_r
        him_f[rows_f, :] = h_i
        hre_b[rows_b, :] = h_r
        him_b[rows_b, :] = h_i
        s_r = jnp.where(fwd_lane, sre[rows_f, :], sre[rows_b, :])
        s_i = jnp.where(fwd_lane, sim[rows_f, :], sim[rows_b, :])
        n_r = ar * h_r - ai * h_i + s_r
        n_i = ar * h_i + ai * h_r + s_i
        return n_r, n_i

    zero = jnp.zeros((S5_GB, 2 * p), F32)
    lax.fori_loop(0, nc, step, (zero, zero))

    for g in range(S5_GB):
        rows = pl.ds(g * pitch, nc)
        hin = jnp.concatenate([hre_f[rows, :], him_f[rows, :], hre_b[rows, :], him_b[rows, :]], axis=1)
        out = jnp.dot(u_ref[g], wm_ref[g], preferred_element_type=F32)
        out = out + jnp.dot(hin.astype(BF16), w2_ref[g], preferred_element_type=F32)
        y_ref[g] = out

    def scatter_y(blk, carry):
        r0 = pl.multiple_of(blk * rb, 16)
        yv = [y_ref[i, pl.ds(r0, rb), :] for i in range(S5_GB)]
        for t in range(t_len):
            hb, j = divmod(t, per_half)
            acc = None
            for i in range(S5_GB):
                src = yv[i][:, hb * 128:(hb + 1) * 128]
                shift = (gl * (i - j)) % 128
                rolled = src if shift == 0 else pltpu.roll(src, shift, axis=1)
                acc = rolled if acc is None else jnp.where(lane_slot == i, rolled, acc)
            yo_ref[pl.ds(r0 * t_len + t, rb, stride=t_len), :] = acc
        return carry

    lax.fori_loop(0, nc // rb, scatter_y, 0)


def _s5_scan(hn, n_lat, weights):
    bsz, lt, d = hn.shape
    ngrp = d // S5_GROUP
    t = S5_T
    tk = t * S5_GROUP
    nc = lt // t
    ncc = (lt - n_lat) // t
    gb = S5_GB
    assert gb * S5_GROUP == 128 and (128 // S5_GROUP) == gb and t % gb == 0
    ws, lf, rf, lb, rb, w2, ar, ai = weights
    p2 = 2 * S5_STATE
    pitch = nc + 8 if (nc // 8) % 2 == 0 else nc
    kern = functools.partial(_s5_kernel, n_chunks=nc, n_ctx_chunks=ncc, pitch=pitch)
    per_group = lambda *s: pl.BlockSpec((gb,) + s, lambda gi, b: (gi,) + tuple(0 for _ in s))
    return pl.pallas_call(
        kern,
        grid=(ngrp // gb, bsz),
        in_specs=[
            pl.BlockSpec((None, lt, 128), lambda gi, b: (b, 0, gi)),
            per_group(tk, 2 * p2),
            per_group(tk, p2), per_group(p2, tk), per_group(tk, p2), per_group(p2, tk),
            per_group(4 * p2, tk),
            per_group(p2), per_group(p2),
        ],
        out_specs=pl.BlockSpec((None, lt, 128), lambda gi, b: (b, 0, gi)),
        out_shape=jax.ShapeDtypeStruct((bsz, lt, d), F32),
        scratch_shapes=[pltpu.VMEM((gb, nc, tk), BF16), pltpu.VMEM((gb, nc, tk), F32), pltpu.VMEM((gb, tk, tk), BF16)]
        + [pltpu.VMEM((gb * pitch, p2), F32) for _ in range(6)],
        compiler_params=_cparams("parallel", "arbitrary"),
        name="s5_scan",
    )(hn, ws, lf, rf, lb, rb, w2, ar, ai)


def _router_gates(hn2, rw_ref, rb_ref):
    n_exp = rw_ref.shape[0]
    epg = n_exp // N_EXPERT_GROUPS
    logits = lax.dot_general(rw_ref[...], hn2, (((1,), (1,)), ((), ())),
                             preferred_element_type=F32, precision=HIGHEST)
    s = _sigmoid(logits)
    sel = s + rb_ref[...]
    row = [sel[e:e + 1] for e in range(n_exp)]
    gscore = []
    for gi in range(N_EXPERT_GROUPS):
        a, b, c, dd = row[gi * epg: gi * epg + epg]
        hi1, lo1 = jnp.maximum(a, b), jnp.minimum(a, b)
        hi2, lo2 = jnp.maximum(c, dd), jnp.minimum(c, dd)
        gscore.append(jnp.maximum(hi1, hi2) + jnp.maximum(jnp.minimum(hi1, hi2), jnp.maximum(lo1, lo2)))
    gmax = functools.reduce(jnp.maximum, gscore)
    gates = []
    taken = None
    for gi in range(N_EXPERT_GROUPS):
        is_max = gscore[gi] == gmax
        best = is_max if taken is None else jnp.logical_and(is_max, jnp.logical_not(taken))
        taken = is_max if taken is None else jnp.logical_or(taken, is_max)
        for e in range(gi * epg, gi * epg + epg):
            rank = jnp.zeros_like(row[e])
            for j in range(gi * epg, gi * epg + epg):
                if j == e:
                    continue
                ahead = (row[j] >= row[e]) if j < e else (row[j] > row[e])
                rank = rank + ahead.astype(F32)
            chosen = jnp.logical_and(best, rank < float(TOP_K))
            gates.append(jnp.where(chosen, s[e:e + 1], 0.0))
    g = jnp.concatenate(gates, axis=0)
    return g / jnp.sum(g, axis=0, keepdims=True)


def _glu_kernel(y_ref, u_ref, x_ref, mod_ref, d_ref, w_ref, b_ref, g2_ref, rw_ref, rb_ref,
                x1_ref, hn2_ref, gates_ref):
    d = x_ref.shape[-1]
    m = mod_ref[...]
    u = u_ref[...].astype(F32)
    a = _gelu_tanh(y_ref[...].astype(F32) + d_ref[...] * u)
    z = jnp.dot(a.astype(BF16), w_ref[...], preferred_element_type=F32) + b_ref[...]
    out = z[:, :d] * _sigmoid(z[:, d:])
    x1 = x_ref[...] + m[2:3] * out
    x1_ref[...] = x1
    hn2 = _rms(x1, g2_ref[...]) * (1.0 + m[4:5]) + m[3:4]
    hn2_ref[...] = hn2.astype(hn2_ref.dtype)
    gates_ref[...] = _router_gates(hn2, rw_ref, rb_ref)


def _glu_head(y, hn, xall, mods, d_skip, glu_w, glu_b, norm2_g, router_wt, router_b, n_lat_tiles):
    bsz, lt, d = xall.shape
    nt = lt // TOKEN_TILE
    n_exp = router_wt.shape[0]
    tok = pl.BlockSpec((None, TOKEN_TILE, d), lambda b, i: (b, i, 0))
    vec = lambda n: pl.BlockSpec((1, n), lambda b, i: (0, 0))
    return pl.pallas_call(
        _glu_kernel,
        grid=(bsz, nt),
        in_specs=[
            tok, tok, tok,
            pl.BlockSpec((None, 6, d), lambda b, i: (b * 2 + (i >= n_lat_tiles).astype(jnp.int32), 0, 0)),
            vec(d),
            pl.BlockSpec((d, 2 * d), lambda b, i: (0, 0)),
            vec(2 * d),
            vec(d),
            pl.BlockSpec((n_exp, d), lambda b, i: (0, 0)),
            pl.BlockSpec((n_exp, 1), lambda b, i: (0, 0)),
        ],
        out_specs=[
            tok, tok,
            pl.BlockSpec((n_exp, TOKEN_TILE), lambda b, i: (0, b * nt + i)),
        ],
        out_shape=[
            jax.ShapeDtypeStruct((bsz, lt, d), F32),
            jax.ShapeDtypeStruct((bsz, lt, d), BF16),
            jax.ShapeDtypeStruct((n_exp, bsz * lt), F32),
        ],
        compiler_params=_cparams("parallel", "parallel"),
        name="s5_glu_head",
    )(y, hn, xall, mods, d_skip.reshape(1, d), glu_w.astype(BF16), glu_b.reshape(1, 2 * d),
      norm2_g.reshape(1, d), router_wt, router_b.reshape(n_exp, 1))


def _moe_slots_padded(total):
    return jnp.floor((total + float(MOE_PIECE - 1)) * (1.0 / MOE_PIECE)) * float(MOE_PIECE)


def _moe_sort_kernel(gt_ref, t_ref, ts_ref, cnt_ref):
    n_exp, nb = gt_ref.shape
    nbpad = ts_ref.shape[0]
    sel = gt_ref[...] > 0.0
    sel_b = jnp.where(sel, 1.0, 0.0).astype(BF16)
    earlier = lax.broadcasted_iota(jnp.int32, (nb, nb), 0) < lax.broadcasted_iota(jnp.int32, (nb, nb), 1)
    rank = jnp.dot(sel_b, jnp.where(earlier, 1.0, 0.0).astype(BF16), preferred_element_type=F32)
    total = jnp.sum(jnp.where(sel, 1.0, 0.0), axis=1, keepdims=True)
    padded = jnp.broadcast_to(_moe_slots_padded(total), (n_exp, 128))
    below = lax.broadcasted_iota(jnp.int32, (n_exp, n_exp), 1) < lax.broadcasted_iota(jnp.int32, (n_exp, n_exp), 0)
    offs = jnp.dot(jnp.where(below, 1.0, 0.0).astype(BF16), padded.astype(BF16), preferred_element_type=F32)[:, 0:1]
    dest = offs + rank
    d_lo = jnp.min(jnp.where(sel, dest, float(nbpad)), axis=0, keepdims=True).astype(jnp.int32)
    d_hi = jnp.max(jnp.where(sel, dest, -1.0), axis=0, keepdims=True).astype(jnp.int32)
    slot = lax.broadcasted_iota(jnp.int32, (nbpad, nb), 0)
    perm = jnp.where(jnp.logical_or(slot == d_lo, slot == d_hi), 1.0, 0.0).astype(BF16)
    ts_ref[...] = jnp.dot(perm, t_ref[...], preferred_element_type=F32).astype(ts_ref.dtype)
    cnt_ref[...] = padded


def _moe_expert_kernel(tile_e, npieces, rows, ntiles, ts_hbm, w1_ref, w3_ref, w2_ref, ys_hbm,
                       tbuf, ybuf, sem_in, sem_out):
    t = pl.program_id(0)
    nt = ntiles[0]
    last = pl.num_programs(0) - 1
    ppt = MOE_TILE // MOE_PIECE
    slot = t % 2

    def piece_rows(tt, p):
        return pl.ds(pl.multiple_of(rows[tt * ppt + p], MOE_PIECE), MOE_PIECE)

    def for_pieces(tt, fn):
        for p in range(ppt):
            @pl.when(p < npieces[tt])
            def _(p=p):
                fn(p)

    def copy_in(tt, sl, p):
        return pltpu.make_async_copy(ts_hbm.at[piece_rows(tt, p), :],
                                     tbuf.at[sl, pl.ds(p * MOE_PIECE, MOE_PIECE), :], sem_in.at[sl])

    def copy_out(tt, sl, p):
        return pltpu.make_async_copy(ybuf.at[sl, pl.ds(p * MOE_PIECE, MOE_PIECE), :],
                                     ys_hbm.at[piece_rows(tt, p), :], sem_out.at[sl])

    @pl.when(t == 0)
    def _():
        tbuf[...] = jnp.zeros_like(tbuf)

        @pl.when(nt > 0)
        def _():
            for_pieces(0, lambda p: copy_in(0, 0, p).start())

    @pl.when(t + 1 < nt)
    def _():
        for_pieces(t + 1, lambda p: copy_in(t + 1, 1 - slot, p).start())

    @pl.when(jnp.logical_and(t >= 2, t - 2 < nt))
    def _():
        for_pieces(t - 2, lambda p: copy_out(t - 2, slot, p).wait())

    @pl.when(t < nt)
    def _():
        for_pieces(t, lambda p: copy_in(t, slot, p).wait())
        x = tbuf[slot]
        h = _silu(jnp.dot(x, w1_ref[...], preferred_element_type=F32)) * jnp.dot(x, w3_ref[...], preferred_element_type=F32)
        ybuf[slot] = jnp.dot(h.astype(BF16), w2_ref[...], preferred_element_type=F32).astype(ybuf.dtype)
        for_pieces(t, lambda p: copy_out(t, slot, p).start())

    @pl.when(t == last)
    def _():
        @pl.when(jnp.logical_and(last >= 1, last - 1 < nt))
        def _():
            for_pieces(last - 1, lambda p: copy_out(last - 1, 1 - slot, p).wait())

        @pl.when(last < nt)
        def _():
            for_pieces(last, lambda p: copy_out(last, slot, p).wait())


def _moe_unsort_kernel(ys_ref, g_ref, x_ref, g2_ref, *rest, final):
    if final:
        fg_ref, o_ref = rest
    else:
        (o_ref,) = rest
    nb, n_exp = g_ref.shape
    nbpad = ys_ref.shape[0]
    gates = g_ref[...]
    sel = gates > 0.0
    sel_b = jnp.where(sel, 1.0, 0.0).astype(BF16)
    earlier = lax.broadcasted_iota(jnp.int32, (nb, nb), 1) < lax.broadcasted_iota(jnp.int32, (nb, nb), 0)
    rank = jnp.dot(jnp.where(earlier, 1.0, 0.0).astype(BF16), sel_b, preferred_element_type=F32)
    total = jnp.sum(jnp.where(sel, 1.0, 0.0), axis=0, keepdims=True)
    padded = jnp.broadcast_to(_moe_slots_padded(total), (8, n_exp))
    below = lax.broadcasted_iota(jnp.int32, (n_exp, n_exp), 0) < lax.broadcasted_iota(jnp.int32, (n_exp, n_exp), 1)
    offs = jnp.dot(padded.astype(BF16), jnp.where(below, 1.0, 0.0).astype(BF16), preferred_element_type=F32)[0:1]
    dest = offs + rank
    d_lo = jnp.min(jnp.where(sel, dest, float(nbpad)), axis=1, keepdims=True)
    d_hi = jnp.max(jnp.where(sel, dest, -1.0), axis=1, keepdims=True)
    g_lo = jnp.sum(jnp.where(jnp.logical_and(sel, dest == d_lo), gates, 0.0), axis=1, keepdims=True)
    g_hi = jnp.sum(jnp.where(jnp.logical_and(sel, dest == d_hi), gates, 0.0), axis=1, keepdims=True)
    slot = lax.broadcasted_iota(jnp.int32, (nb, nbpad), 1)
    ys = ys_ref[...]

    def pick(dcol):
        onehot = jnp.where(slot == dcol.astype(jnp.int32), 1.0, 0.0).astype(BF16)
        return jnp.dot(onehot, ys, preferred_element_type=F32)

    moe = g_lo * pick(d_lo) + g_hi * pick(d_hi)
    half = nb // g2_ref.shape[0]
    d = x_ref.shape[1]
    for j in range(g2_ref.shape[0]):
        r = slice(j * half, (j + 1) * half)
        xn = x_ref[r, :] + g2_ref[j] * moe[r]
        if final:
            xn = _rms(xn, fg_ref[...])
            for c in range(half // SSD_CHUNK):
                col = (j * half) // SSD_CHUNK + c
                o_ref[:, col * d:(col + 1) * d] = xn[c * SSD_CHUNK:(c + 1) * SSD_CHUNK]
        else:
            o_ref[r, :] = xn


def _moe_schedule(counts, nbpad, n_tiles):
    nblk, n_exp = counts.shape
    ppt = MOE_TILE // MOE_PIECE
    pc = counts // MOE_PIECE
    loc = jnp.cumsum(pc, axis=1) - pc
    cum_b = jnp.cumsum(pc, axis=0)
    np_e = cum_b[-1]
    tiles_e = (np_e + ppt - 1) // ppt
    tile_end = jnp.cumsum(tiles_e)
    ntiles = tile_end[-1]
    t_idx = jnp.arange(n_tiles, dtype=jnp.int32)
    tile_e = jnp.minimum(jnp.sum((tile_end[None, :] <= t_idx[:, None]).astype(jnp.int32), axis=1), n_exp - 1)
    first = (tile_end - tiles_e)[tile_e]
    piece0 = (t_idx - first) * ppt
    npieces = jnp.where(t_idx < ntiles, jnp.clip(np_e[tile_e] - piece0, 0, ppt), 0)
    i = piece0[:, None] + jnp.arange(ppt, dtype=jnp.int32)[None, :]
    cum_t = cum_b.T[tile_e]
    blk = jnp.minimum(jnp.sum((cum_t[:, None, :] <= i[:, :, None]).astype(jnp.int32), axis=2), nblk - 1)
    before = jnp.take_along_axis(cum_t - pc.T[tile_e], blk, axis=1)
    within = i - before + jnp.take_along_axis(loc.T[tile_e], blk, axis=1)
    rows = blk * nbpad + within * MOE_PIECE
    rows = jnp.where(jnp.arange(ppt)[None, :] < npieces[:, None], rows, 0)
    return (tile_e.astype(jnp.int32), npieces.astype(jnp.int32), rows.reshape(-1).astype(jnp.int32),
            ntiles.reshape(1).astype(jnp.int32))


def _moe(t, gates_t, xres, g2rows, w1, w3, w2, *, final_g=None, blocks_per_batch=None):
    n, d = t.shape
    n_exp, _, f = w1.shape
    nb = MOE_BLOCK
    nblk = n // nb
    nbpad = TOP_K * nb + n_exp * MOE_PIECE
    final = final_g is not None

    ts, cnt = pl.pallas_call(
        _moe_sort_kernel,
        grid=(nblk,),
        in_specs=[pl.BlockSpec((n_exp, nb), lambda j: (0, j)), pl.BlockSpec((nb, d), lambda j: (j, 0))],
        out_specs=[pl.BlockSpec((nbpad, d), lambda j: (j, 0)), pl.BlockSpec((None, n_exp, 128), lambda j: (j, 0, 0))],
        out_shape=[jax.ShapeDtypeStruct((nblk * nbpad, d), BF16), jax.ShapeDtypeStruct((nblk, n_exp, 128), F32)],
        compiler_params=_cparams("parallel"),
        name="moe_sort",
    )(gates_t, t)

    n_tiles = nblk * nbpad // MOE_TILE + n_exp
    tile_e, npieces, rows, ntiles = _moe_schedule(cnt[:, :, 0].astype(jnp.int32), nbpad, n_tiles)
    wspec = lambda shape: pl.BlockSpec((None,) + shape, lambda i, te, npc, rw, nt: (te[i], 0, 0))
    ys = pl.pallas_call(
        _moe_expert_kernel,
        grid_spec=pltpu.PrefetchScalarGridSpec(
            num_scalar_prefetch=4,
            grid=(n_tiles,),
            in_specs=[pl.BlockSpec(memory_space=pl.ANY), wspec((d, f)), wspec((d, f)), wspec((f, d))],
            out_specs=pl.BlockSpec(memory_space=pl.ANY),
            scratch_shapes=[pltpu.VMEM((2, MOE_TILE, d), BF16), pltpu.VMEM((2, MOE_TILE, d), BF16),
                            pltpu.SemaphoreType.DMA((2,)), pltpu.SemaphoreType.DMA((2,))],
        ),
        out_shape=jax.ShapeDtypeStruct((nblk * nbpad, d), BF16),
        input_output_aliases={4: 0},
        compiler_params=_cparams("arbitrary"),
        name="moe_experts",
    )(tile_e, npieces, rows, ntiles, ts, w1, w3, w2)

    halves = nb // TOKEN_TILE
    in_specs = [
        pl.BlockSpec((nbpad, d), lambda j: (j, 0)),
        pl.BlockSpec((nb, n_exp), lambda j: (j, 0)),
        pl.BlockSpec((nb, d), lambda j: (j, 0)),
        pl.BlockSpec((halves, 1, d), lambda j: (j, 0, 0)),
    ]
    args = [ys, gates_t.T, xres, g2rows]
    if final:
        in_specs.append(pl.BlockSpec((1, d), lambda j: (0, 0)))
        args.append(final_g.reshape(1, d))
        cols = nb // SSD_CHUNK
        out_spec = pl.BlockSpec((None, SSD_CHUNK, cols * d), lambda j: (j // blocks_per_batch, 0, j % blocks_per_batch))
        out_shape = jax.ShapeDtypeStruct((nblk // blocks_per_batch, SSD_CHUNK, GRID_W * d), F32)
    else:
        out_spec = pl.BlockSpec((nb, d), lambda j: (j, 0))
        out_shape = jax.ShapeDtypeStruct((n, d), F32)
    return pl.pallas_call(
        functools.partial(_moe_unsort_kernel, final=final),
        grid=(nblk,),
        in_specs=in_specs,
        out_specs=out_spec,
        out_shape=out_shape,
        compiler_params=_cparams("parallel"),
        name="moe_unsort_final" if final else "moe_unsort",
    )(*args)


SSD_COLS_PER_STEP = 8


def _ssd_inproj_kernel(xl_ref, xc_ref, g_ref, mod_ref, wz_ref, wx_ref, wdt_ref, wdtt_ref, bias_ref, biast_ref,
                       a_ref, at_ref, cw_ref, cb_ref, z_ref, xbc_ref, csdt_ref, cst_ref, xp_ref, slab, xt, ext,
                       *, n_lat_steps, n_ctx_chunks):
    step = pl.program_id(1)
    i = pl.program_id(2)
    q = SSD_CHUNK
    ncols = SSD_COLS_PER_STEP
    d = xt.shape[1]
    is_lat = step < n_lat_steps
    c = step * ncols + i
    n_lat_chunks = n_lat_steps * ncols
    n_chunks = n_lat_chunks + n_ctx_chunks

    def emit_conv():
        acc = jnp.zeros(xbc_ref.shape, F32) + cb_ref[...]
        w = cw_ref[...]
        for k in range(SSD_CONV):
            acc = acc + w[k:k + 1] * ext[pl.ds(8 - SSD_CONV // 2 + k, q), :]
        xbc_ref[...] = _silu(acc).astype(xbc_ref.dtype)

    @pl.when(jnp.logical_and(i == 0, is_lat))
    def _():
        for s in range(d // 128):
            slab[...] = xl_ref[:, :, s * 128:(s + 1) * 128].reshape(q * ncols, 128)
            for w in range(ncols):
                xt[w * q:(w + 1) * q, s * 128:(s + 1) * 128] = slab[pl.ds(w, q, stride=ncols), :]

    @pl.when(jnp.logical_and(i == 0, jnp.logical_not(is_lat)))
    def _():
        xt[0:n_ctx_chunks * q, :] = xc_ref[...]

    @pl.when(c == 0)
    def _():
        ext[...] = jnp.zeros_like(ext)

    @pl.when(jnp.logical_or(is_lat, i < n_ctx_chunks))
    def _():
        m = mod_ref[...]
        x = xt[pl.ds(pl.multiple_of(i * q, q), q), :]
        xp_ref[...] = x
        hn = (_rms(x, g_ref[...]) * (1.0 + m[1:2]) + m[0:1]).astype(BF16)
        z_ref[...] = jnp.dot(hn, wz_ref[...], preferred_element_type=F32).astype(z_ref.dtype)
        xbc_new = jnp.dot(hn, wx_ref[...], preferred_element_type=F32)
        starts = jnp.logical_or(c == 0, c == n_lat_chunks)
        ext[8 + q:16 + q, :] = jnp.where(starts, 0.0, xbc_new[0:8])
        emit_conv()
        ext[0:8, :] = jnp.where(starts, 0.0, ext[q:q + 8, :])
        ext[8:8 + q, :] = xbc_new

        def softplus(v):
            return jnp.maximum(v, 0.0) + jnp.log(1.0 + jnp.exp(-jnp.abs(v)))

        r_i = lax.broadcasted_iota(jnp.int32, (q, q), 0)
        c_i = lax.broadcasted_iota(jnp.int32, (q, q), 1)
        lower = (r_i >= c_i).astype(F32)
        upper = (r_i <= c_i).astype(F32)
        dt = softplus(jnp.dot(hn, wdt_ref[...], preferred_element_type=F32) + bias_ref[...])
        da = dt * a_ref[...]
        half = da.shape[1] // 2
        cs = jnp.concatenate([jnp.dot(lower, da[:, :half], preferred_element_type=F32, precision=HIGHEST),
                              jnp.dot(upper, da[:, half:], preferred_element_type=F32, precision=HIGHEST)], axis=1)
        lane = lax.broadcasted_iota(jnp.int32, dt.shape, 1)
        csdt_ref[...] = jnp.where(lane % 8 < 4, cs, dt)
        nh = at_ref.shape[0] // 2
        dtt = softplus(lax.dot_general(wdtt_ref[...], hn, (((1,), (1,)), ((), ())), preferred_element_type=F32)
                       + biast_ref[...])
        dat = dtt * at_ref[...]
        cst_f = jnp.dot(dat[:nh], upper, preferred_element_type=F32, precision=HIGHEST)
        cst_b = jnp.dot(dat[nh:], lower, preferred_element_type=F32, precision=HIGHEST)
        cst = jnp.concatenate([cst_f, cst_b], axis=0)
        for j in range(cst_ref.shape[0]):
            cst_ref[j] = cst[j * 4:(j + 1) * 4, :]

    @pl.when(c == n_chunks)
    def _():
        ext[8 + q:16 + q, :] = jnp.zeros((8, ext.shape[1]), F32)
        emit_conv()


def _ssd_inproj(xall, n_lat, norm_g, mods, in_w, dt_bias, a_log, conv_w, conv_b):
    bsz, lt, d = xall.shape
    q = SSD_CHUNK
    ncols = SSD_COLS_PER_STEP
    ncl = n_lat // q
    nc = lt // q
    ncc = nc - ncl
    n_ctx = lt - n_lat
    nh2 = dt_bias.size
    d_inner = (nh2 // 2) * SSD_HEAD_DIM
    conv_ch = in_w.shape[1] - d_inner - nh2
    wz = in_w[:, :d_inner].astype(BF16)
    wx = in_w[:, d_inner:d_inner + conv_ch].astype(BF16)
    wdt = in_w[:, d_inner + conv_ch:].astype(BF16)
    a = -jnp.exp(a_log.astype(F32)).reshape(nh2)
    bias = dt_bias.astype(F32).reshape(nh2)
    ngr = nh2 // 4
    lanes = jnp.arange(2 * nh2)
    dup = (lanes // 8) * 4 + lanes % 4
    assert n_lat // GRID_W == q and GRID_W % ncols == 0 and lt % GRID_W == 0 and ncc < ncols and n_lat % n_ctx == 0
    nls = ncl // ncols
    xgrid = xall.reshape(bsz, lt // GRID_W, GRID_W, d)
    kern = functools.partial(_ssd_inproj_kernel, n_lat_steps=nls, n_ctx_chunks=ncc)
    full = lambda s: pl.BlockSpec(s, lambda b, st, i: tuple(0 for _ in s))
    chunk = lambda b, st, i: jnp.minimum(st * ncols + i, nc - 1)
    rows = lambda width: pl.BlockSpec((None, q, width), lambda b, st, i: (b, chunk(b, st, i), 0))
    return pl.pallas_call(
        kern,
        grid=(bsz, nls + 1, ncols),
        in_specs=[
            pl.BlockSpec((None, q, ncols, d), lambda b, st, i: (b, 0, jnp.minimum(st, nls - 1), 0)),
            pl.BlockSpec((None, n_ctx, d), lambda b, st, i: (b, n_lat // n_ctx, 0)),
            full((1, d)),
            pl.BlockSpec((None, 6, d), lambda b, st, i: (b * 2 + (st >= nls).astype(jnp.int32), 0, 0)),
            full((d, d_inner)), full((d, conv_ch)), full((d, 2 * nh2)), full((nh2, d)),
            full((1, 2 * nh2)), full((nh2, 1)), full((1, 2 * nh2)), full((nh2, 1)),
            full((SSD_CONV, conv_ch)), full((1, conv_ch)),
        ],
        out_specs=[
            rows(d_inner),
            pl.BlockSpec((None, q, conv_ch), lambda b, st, i: (b, jnp.clip(st * ncols + i - 1, 0, nc - 1), 0)),
            rows(2 * nh2),
            pl.BlockSpec((None, None, ngr, 4, q), lambda b, st, i: (b, chunk(b, st, i), 0, 0, 0)),
            rows(d),
        ],
        out_shape=[
            jax.ShapeDtypeStruct((bsz, lt, d_inner), BF16),
            jax.ShapeDtypeStruct((bsz, lt, conv_ch), BF16),
            jax.ShapeDtypeStruct((bsz, lt, 2 * nh2), F32),
            jax.ShapeDtypeStruct((bsz, nc, ngr, 4, q), F32),
            jax.ShapeDtypeStruct((bsz, lt, d), F32),
        ],
        scratch_shapes=[pltpu.VMEM((q * ncols, 128), F32), pltpu.VMEM((q * ncols, d), F32),
                        pltpu.VMEM((q + 16, conv_ch), F32)],
        compiler_params=_cparams("parallel", "arbitrary", "arbitrary"),
        name="ssd_inproj",
    )(xgrid, xall, norm_g.reshape(1, d), mods, wz, wx, wdt[:, dup], wdt.T, bias[dup].reshape(1, -1),
      bias.reshape(nh2, 1), a[dup].reshape(1, -1), a.reshape(nh2, 1), conv_w.astype(F32),
      conv_b.astype(F32).reshape(1, conv_ch))


def _ssd_scan_dir(x, bm, cm, v, cst, state, spread, reverse):
    q = SSD_CHUNK
    hp = SSD_HEAD_DIM
    gw = x.shape[1]
    r = gw // hp
    v_hi = v.astype(BF16)
    v_r1 = v - v_hi.astype(F32)
    v_mid = v_r1.astype(BF16)
    v_lo = (v_r1 - v_mid.astype(F32)).astype(BF16)
    lhs = jnp.concatenate([v_hi, v_mid, v_lo], axis=1)
    spread_out = jnp.dot(lhs, spread, preferred_element_type=F32)
    cs_x = spread_out[:, :gw]
    dt_x = spread_out[:, gw:2 * gw]
    cs_q = spread_out[:, 2 * gw:]
    end = 0 if reverse else q - 1
    cs_end = cs_x[end:end + 1, :]
    xdt = x * dt_x
    xw = (xdt * jnp.exp(cs_end - cs_x)).astype(BF16)
    xdt = xdt.astype(BF16)
    cb = lax.dot_general(cm, bm, (((1,), (1,)), ((), ())), preferred_element_type=F32)
    y_off = jnp.dot(cm, state.astype(BF16), preferred_element_type=F32) * jnp.exp(cs_x)
    r_i = lax.broadcasted_iota(jnp.int32, (q, q), 0)
    c_i = lax.broadcasted_iota(jnp.int32, (q, q), 1)
    mask = (r_i <= c_i) if reverse else (r_i >= c_i)
    lane = lax.broadcasted_iota(jnp.int32, (q, 2 * hp), 1)
    ys = []
    for pair in range(r // 2):
        xp = xdt[:, pair * 2 * hp:(pair + 1) * 2 * hp]
        yd = []
        for h in (2 * pair, 2 * pair + 1):
            seg = jnp.exp(jnp.minimum(cs_q[:, h * q:(h + 1) * q] - cst[h:h + 1, :], 0.0))
            gmat = (cb * jnp.where(mask, seg, 0.0)).astype(BF16)
            yd.append(jnp.dot(gmat, xp, preferred_element_type=F32))
        ys.append(jnp.where(lane < hp, yd[0], yd[1]))
    upd = lax.dot_general(bm, xw, (((0,), (0,)), ((), ())), preferred_element_type=F32)
    return jnp.concatenate(ys, axis=1) + y_off, state * jnp.exp(cs_end) + upd


def _ssd_spread_matrix(r, gw, hp, q):
    wide = 2 * gw + r * q
    col = lax.broadcasted_iota(jnp.int32, (6 * r, wide), 0) % (2 * r)
    ln = lax.broadcasted_iota(jnp.int32, (6 * r, wide), 1)
    cs_hit = jnp.logical_or(jnp.logical_and(ln < gw, ln // hp == col),
                            jnp.logical_and(ln >= 2 * gw, (ln - 2 * gw) // q == col))
    dt_hit = jnp.logical_and(jnp.logical_and(ln >= gw, ln < 2 * gw), (ln - gw) // hp == col - r)
    hit = jnp.logical_or(jnp.logical_and(col < r, cs_hit), jnp.logical_and(col >= r, dt_hit))
    return jnp.where(hit, 1.0, 0.0).astype(BF16)


def _ssd_scan_kernel(xf, bf, cf, csdtf, cstf, xb, bb, cb, csdtb, cstb, yf_ref, yb_ref, state_f, state_b):
    @pl.when(pl.program_id(1) == 0)
    def _():
        state_f[...] = jnp.zeros_like(state_f)
        state_b[...] = jnp.zeros_like(state_b)

    ngr, n, gw = state_f.shape
    r = gw // SSD_HEAD_DIM
    spread = _ssd_spread_matrix(r, gw, SSD_HEAD_DIM, SSD_CHUNK)
    dirs = ((xf, bf, cf, csdtf, cstf, yf_ref, state_f, False, 0), (xb, bb, cb, csdtb, cstb, yb_ref, state_b, True, ngr))
    for g in range(ngr):
        for x_ref, b_ref, c_ref, csdt_ref, cst_ref, y_ref, state, reverse, lane_group0 in dirs:
            j = lane_group0 + g
            y, new_state = _ssd_scan_dir(
                x_ref[:, g * gw:(g + 1) * gw].astype(F32), b_ref[:, g * n:(g + 1) * n], c_ref[:, g * n:(g + 1) * n],
                csdt_ref[:, 2 * r * j:2 * r * (j + 1)], cst_ref[g], state[g], spread, reverse)
            y_ref[:, g * gw:(g + 1) * gw] = y.astype(y_ref.dtype)
            state[g] = new_state


def _ssd_scan(xbc, csdt, cst, n_lat):
    bsz, lt, ch = xbc.shape
    q = SSD_CHUNK
    nc, ncl = lt // q, n_lat // q
    ngr = SSD_GROUPS
    n = SSD_STATE
    d_inner = ch - 2 * ngr * n
    gw = d_inner // ngr
    assert d_inner % (ngr * n) == 0

    def specs(chunk, direction):
        return [
            pl.BlockSpec((None, q, d_inner), lambda b, k: (b, chunk(k), 0)),
            pl.BlockSpec((None, q, ngr * n), lambda b, k: (b, chunk(k), d_inner // (ngr * n))),
            pl.BlockSpec((None, q, ngr * n), lambda b, k: (b, chunk(k), d_inner // (ngr * n) + 1)),
            pl.BlockSpec((None, q, csdt.shape[2]), lambda b, k: (b, chunk(k), 0)),
            pl.BlockSpec((None, None, ngr, 4, q), lambda b, k: (b, chunk(k), direction, 0, 0)),
        ]

    fwd = lambda k: (k + ncl) % nc
    bwd = lambda k: nc - 1 - k
    out = jax.ShapeDtypeStruct((bsz, lt, d_inner), BF16)
    return pl.pallas_call(
        _ssd_scan_kernel,
        grid=(bsz, nc),
        in_specs=specs(fwd, 0) + specs(bwd, 1),
        out_specs=[pl.BlockSpec((None, q, d_inner), lambda b, k: (b, fwd(k), 0)),
                   pl.BlockSpec((None, q, d_inner), lambda b, k: (b, bwd(k), 0))],
        out_shape=[out, out],
        scratch_shapes=[pltpu.VMEM((ngr, n, gw), F32), pltpu.VMEM((ngr, n, gw), F32)],
        compiler_params=_cparams("parallel", "arbitrary"),
        name="ssd_scan",
    )(xbc, xbc, xbc, csdt, cst, xbc, xbc, xbc, csdt, cst)


def _ssd_finish_kernel(yf_ref, yb_ref, xs_ref, z_ref, x_ref, mod_ref, dsk_ref, ng_ref, w_ref, g2_ref,
                       rw_ref, rb_ref, x3_ref, hn2_ref, gates_ref):
    m = mod_ref[...]
    y = yf_ref[...].astype(F32) + yb_ref[...].astype(F32) + dsk_ref[...] * xs_ref[...].astype(F32)
    gated = y * _silu(z_ref[...].astype(F32))
    nrm = _rms(gated, ng_ref[...])
    out = jnp.dot(nrm.astype(BF16), w_ref[...], preferred_element_type=F32)
    x3 = x_ref[...] + m[2:3] * out
    x3_ref[...] = x3
    hn2 = _rms(x3, g2_ref[...]) * (1.0 + m[4:5]) + m[3:4]
    hn2_ref[...] = hn2.astype(hn2_ref.dtype)
    gates_ref[...] = _router_gates(hn2, rw_ref, rb_ref)


def _ssd_finish(yf, yb, xbc, z, xall, n_lat, mods, d_skip, norm_g, out_w, norm2_g, router_wt, router_b):
    bsz, lt, d_inner = z.shape
    d = xall.shape[-1]
    q = TOKEN_TILE
    ncl = n_lat // q
    n_exp = router_wt.shape[0]
    inner = pl.BlockSpec((None, q, d_inner), lambda b, c: (b, c, 0))
    tok = pl.BlockSpec((None, q, d), lambda b, c: (b, c, 0))
    full = lambda s: pl.BlockSpec(s, lambda b, c: tuple(0 for _ in s))
    dsk = jnp.repeat(d_skip.astype(F32), SSD_HEAD_DIM).reshape(1, d_inner)
    return pl.pallas_call(
        _ssd_finish_kernel,
        grid=(bsz, ncl),
        in_specs=[
            inner, inner, inner, inner,
            tok,
            pl.BlockSpec((None, 6, d), lambda b, c: (b * 2, 0, 0)),
            full((1, d_inner)), full((1, d_inner)), full((d_inner, d)), full((1, d)),
            full((n_exp, d)), full((n_exp, 1)),
        ],
        out_specs=[tok, tok, pl.BlockSpec((n_exp, q), lambda b, c: (0, b * ncl + c))],
        out_shape=[
            jax.ShapeDtypeStruct((bsz, n_lat, d), F32),
            jax.ShapeDtypeStruct((bsz, n_lat, d), BF16),
            jax.ShapeDtypeStruct((n_exp, bsz * n_lat), F32),
        ],
        compiler_params=_cparams("parallel", "parallel"),
        name="ssd_finish",
    )(yf, yb, xbc, z, xall, mods, dsk, norm_g.reshape(1, d_inner), out_w.astype(BF16),
      norm2_g.reshape(1, d), router_wt, router_b.reshape(n_exp, 1))


def kernel(x, c, ctx, c_ctx, mod_w, mod_b, norm1_g, norm2_g, final_g, s5_lam_re, s5_lam_im, s5_log_dt, s5_b_re, s5_b_im, s5_c_re, s5_c_im, s5_d, s5_glu_w, s5_glu_b, ssd_in_w, ssd_conv_w, ssd_conv_b, ssd_dt_bias, ssd_a_log, ssd_d, ssd_norm_g, ssd_out_w, router_w, router_b, moe_w1, moe_w3, moe_w2):
    bsz, n_lat, d = x.shape
    n_ctx = ctx.shape[1]
    lt = n_lat + n_ctx
    n_exp = router_w.shape[1]
    assert n_lat % TOKEN_TILE == 0 and n_ctx % TOKEN_TILE == 0
    assert (bsz * lt) % MOE_BLOCK == 0 and n_lat % MOE_BLOCK == 0 and MOE_BLOCK % TOKEN_TILE == 0
    assert TOKEN_TILE % SSD_CHUNK == 0

    mods = _modulation(c, c_ctx, mod_w, mod_b)
    router_wt = router_w.T.astype(F32)
    w1 = moe_w1.astype(BF16)
    w3 = moe_w3.astype(BF16)
    w2 = moe_w2.astype(BF16)
    nlt = n_lat // TOKEN_TILE
    tpb = lt // TOKEN_TILE

    xall = jnp.concatenate([x, ctx], axis=1)
    hn = _prenorm(xall, norm1_g[0], mods[0], nlt)
    s5w = _s5_weights(s5_lam_re[0], s5_lam_im[0], s5_log_dt[0], s5_b_re[0], s5_b_im[0], s5_c_re[0], s5_c_im[0])
    y = _s5_scan(hn, n_lat, s5w)
    x1, hn2, gates_t = _glu_head(y, hn, xall, mods[0], s5_d[0], s5_glu_w[0], s5_glu_b[0], norm2_g[0],
                                 router_wt, router_b, nlt)
    g2_lat = jnp.broadcast_to(mods[0][0::2, None, 5], (bsz, nlt, d))
    g2_ctx = jnp.broadcast_to(mods[0][1::2, None, 5], (bsz, tpb - nlt, d))
    g2rows = jnp.concatenate([g2_lat, g2_ctx], axis=1).reshape(bsz * tpb, 1, d)
    x2 = _moe(hn2.reshape(bsz * lt, d), gates_t, x1.reshape(bsz * lt, d), g2rows, w1[0], w3[0], w2[0]).reshape(bsz, lt, d)

    z, xbc, csdt, cst, x2p = _ssd_inproj(x2, n_lat, norm1_g[1], mods[1], ssd_in_w[0], ssd_dt_bias[0], ssd_a_log[0],
                                             ssd_conv_w[0], ssd_conv_b[0])
    yf, yb = _ssd_scan(xbc, csdt, cst, n_lat)
    x3, hn3, gates3_t = _ssd_finish(yf, yb, xbc, z, x2p, n_lat, mods[1], ssd_d[0], ssd_norm_g[0], ssd_out_w[0],
                                    norm2_g[1], router_wt, router_b)
    g2rows = jnp.broadcast_to(mods[1][0::2, None, 5], (bsz, nlt, d)).reshape(bsz * nlt, 1, d)
    out = _moe(hn3.reshape(bsz * n_lat, d), gates3_t, x3.reshape(bsz * n_lat, d), g2rows, w1[1], w3[1], w2[1],
               final_g=final_g, blocks_per_batch=n_lat // MOE_BLOCK)
    return out.reshape(bsz, n_lat, d)
```

```python
import functools

import jax
import jax.numpy as jnp
from jax import lax
from jax.experimental import pallas as pl
from jax.experimental.pallas import tpu as pltpu

F32 = jnp.float32
BF16 = jnp.bfloat16
HIGHEST = lax.Precision.HIGHEST

GRID_W = 64
RMS_EPS = 1e-6

S5_GROUP = 16
S5_STATE = 64
S5_T = 16
S5_GB = 8

SSD_HEAD_DIM = 64
SSD_GROUPS = 8
SSD_STATE = 128
SSD_CONV = 5
SSD_CHUNK = 128

N_EXPERT_GROUPS = 4
TOP_K = 2

TOKEN_TILE = 256
MOE_BLOCK = 512
MOE_PIECE = 16
MOE_TILE = 256
VMEM_LIMIT_BYTES = 56 * 1024 * 1024


def _cparams(*sem):
    return pltpu.CompilerParams(dimension_semantics=sem, vmem_limit_bytes=VMEM_LIMIT_BYTES)


def _sigmoid(v):
    return 1.0 / (1.0 + jnp.exp(-v))


def _silu(v):
    return v * _sigmoid(v)


def _gelu_tanh(v):
    return 0.5 * v * (1.0 + jnp.tanh(0.7978845608028654 * (v + 0.044715 * (v * v * v))))


def _rms(v, g):
    return v * lax.rsqrt(jnp.mean(v * v, axis=-1, keepdims=True) + RMS_EPS) * g


def _mod_kernel(cc_ref, w_ref, b_ref, o_ref):
    a = _silu(cc_ref[...])
    o_ref[...] = jnp.dot(a, w_ref[...], preferred_element_type=F32, precision=HIGHEST) + b_ref[...]


def _modulation(c, c_ctx, mod_w, mod_b):
    depth, d, d6 = mod_w.shape
    bsz = c.shape[0]
    rows = 8
    cc = jnp.zeros((rows, d), F32).at[:bsz].set(c).at[bsz].set(c_ctx)
    tn = d6 // 4
    out = pl.pallas_call(
        _mod_kernel,
        grid=(depth, d6 // tn),
        in_specs=[
            pl.BlockSpec((rows, d), lambda i, j: (0, 0)),
            pl.BlockSpec((None, d, tn), lambda i, j: (i, 0, j)),
            pl.BlockSpec((None, 1, tn), lambda i, j: (i, 0, j)),
        ],
        out_specs=pl.BlockSpec((None, rows, tn), lambda i, j: (i, 0, j)),
        out_shape=jax.ShapeDtypeStruct((depth, rows, d6), F32),
        compiler_params=_cparams("parallel", "parallel"),
        name="modulation",
    )(cc, mod_w, mod_b.reshape(depth, 1, d6))
    lat = out[:, :bsz].reshape(depth, bsz, 1, 6, d)
    ctx = jnp.broadcast_to(out[:, bsz].reshape(depth, 1, 1, 6, d), (depth, bsz, 1, 6, d))
    return jnp.concatenate([lat, ctx], axis=2).reshape(depth, bsz * 2, 6, d)


def _prenorm_kernel(x_ref, g_ref, mod_ref, o_ref):
    m = mod_ref[...]
    hn = _rms(x_ref[...], g_ref[...]) * (1.0 + m[1:2]) + m[0:1]
    o_ref[...] = hn.astype(o_ref.dtype)


def _prenorm(xall, g, mods, n_lat_tiles):
    bsz, lt, d = xall.shape
    nt = lt // TOKEN_TILE
    return pl.pallas_call(
        _prenorm_kernel,
        grid=(bsz, nt),
        in_specs=[
            pl.BlockSpec((None, TOKEN_TILE, d), lambda b, i: (b, i, 0)),
            pl.BlockSpec((1, d), lambda b, i: (0, 0)),
            pl.BlockSpec((None, 6, d), lambda b, i: (b * 2 + (i >= n_lat_tiles).astype(jnp.int32), 0, 0)),
        ],
        out_specs=pl.BlockSpec((None, TOKEN_TILE, d), lambda b, i: (b, i, 0)),
        out_shape=jax.ShapeDtypeStruct((bsz, lt, d), F32),
        compiler_params=_cparams("parallel", "parallel"),
        name="prenorm",
    )(xall, g.reshape(1, d), mods)


def _s5_weights(lam_re, lam_im, log_dt, b_re, b_im, c_re, c_im):
    t = S5_T
    k16 = b_re.shape[-1]

    def cmul(ar, ai, br, bi):
        return ar * br - ai * bi, ar * bi + ai * br

    def direction(k):
        lr, li = lam_re[k], lam_im[k]
        step = jnp.exp(log_dt[k])[:, None]
        mag = jnp.exp(lr * step)
        abar_r = mag * jnp.cos(li * step)
        abar_i = mag * jnp.sin(li * step)
        den = lr * lr + li * li
        q_r = ((abar_r - 1.0) * lr + abar_i * li) / den
        q_i = (abar_i * lr - (abar_r - 1.0) * li) / den
        bb_r, bb_i = cmul(q_r[..., None], q_i[..., None], b_re, b_im)

        def power(tau):
            tau = jnp.asarray(tau, F32)[None, :, None]
            m = jnp.exp((lr * step)[:, None, :] * tau)
            return m * jnp.cos((li * step)[:, None, :] * tau), m * jnp.sin((li * step)[:, None, :] * tau)

        return bb_r.transpose(0, 2, 1), bb_i.transpose(0, 2, 1), power

    rows = lambda v: jnp.repeat(v, k16, axis=1)
    row_tile = lambda v: jnp.tile(v, (1, t, 1))
    cols = lambda v: jnp.repeat(v.transpose(0, 2, 1), k16, axis=2)
    col_tile = lambda v: jnp.tile(v, (1, 1, t))
    ct_r, ct_i = c_re.transpose(0, 2, 1), c_im.transpose(0, 2, 1)
    steps = jnp.arange(t)

    def left(bt_r, bt_i, power, tau):
        pr, pi = power(tau)
        return cmul(row_tile(bt_r), row_tile(bt_i), rows(pr), rows(pi))

    def right(power, tau):
        pr, pi = power(tau)
        return cmul(col_tile(ct_r), col_tile(ct_i), cols(pr), cols(pi))

    bf_r, bf_i, pow_f = direction(0)
    bb_r, bb_i, pow_b = direction(1)
    lf_r, lf_i = left(bf_r, bf_i, pow_f, -steps)
    rf_r, rf_i = right(pow_f, steps)
    lb_r, lb_i = left(bb_r, bb_i, pow_b, steps)
    rb_r, rb_i = right(pow_b, -steps)
    lf = jnp.concatenate([lf_r, lf_i], axis=-1)
    lb = jnp.concatenate([lb_r, lb_i], axis=-1)
    rf = jnp.concatenate([rf_r, -rf_i], axis=1)
    rb = jnp.concatenate([rb_r, -rb_i], axis=1)

    sf_r, sf_i = left(bf_r, bf_i, pow_f, t - 1 - steps)
    ws = jnp.concatenate([sf_r, lb_r, sf_i, lb_i], axis=-1)

    of_r, of_i = right(pow_f, steps + 1)
    ob_r, ob_i = right(pow_b, t - steps)
    zero = jnp.zeros_like(of_r)
    w2 = jnp.concatenate([of_r, zero, -of_i, zero, zero, ob_r, zero, -ob_i], axis=1)

    af_r, af_i = pow_f([t])
    ab_r, ab_i = pow_b([t])
    ar = jnp.concatenate([af_r[:, 0], ab_r[:, 0]], axis=-1)
    ai = jnp.concatenate([af_i[:, 0], ab_i[:, 0]], axis=-1)
    return ws.astype(BF16), lf, rf, lb, rb, w2.astype(BF16), ar, ai


def _s5_row_block(nc):
    return max(rb for rb in range(16, min(nc, 176) + 1, 16) if nc % rb == 0)


def _dot_split3(a, b):
    a_hi = a.astype(BF16)
    a_lo = (a - a_hi.astype(F32)).astype(BF16)
    b_hi = b.astype(BF16)
    b_lo = (b - b_hi.astype(F32)).astype(BF16)
    dot = functools.partial(jnp.dot, preferred_element_type=F32)
    return dot(a_hi, b_hi) + dot(a_hi, b_lo) + dot(a_lo, b_hi)


def _s5_kernel(hn_ref, ws_ref, lf_ref, rf_ref, lb_ref, rb_ref, w2_ref, ar_ref, ai_ref, yo_ref,
               u_ref, y_ref, wm_ref, sre, sim, hre_f, him_f, hre_b, him_b, *, n_chunks, n_ctx_chunks, pitch):
    nc, ncc = n_chunks, n_ctx_chunks
    ncl = nc - ncc
    p = S5_STATE
    t_len = S5_T
    gl = S5_GROUP
    per_half = 128 // gl
    rb = _s5_row_block(nc)
    lane_slot = lax.broadcasted_iota(jnp.int32, (rb, 128), 1) // gl

    @pl.when(pl.program_id(1) == 0)
    def _():
        tk = t_len * gl
        src_tok = lax.broadcasted_iota(jnp.int32, (tk, tk), 0) // gl
        dst_tok = lax.broadcasted_iota(jnp.int32, (tk, tk), 1) // gl
        for g in range(S5_GB):
            causal = _dot_split3(lf_ref[g], rf_ref[g])
            anti = _dot_split3(lb_ref[g], rb_ref[g])
            wm = jnp.where(dst_tok >= src_tok, causal, 0.0) + jnp.where(src_tok >= dst_tok, anti, 0.0)
            wm_ref[g] = wm.astype(wm_ref.dtype)

    def slot_transpose(xs):
        xs = list(xs)
        bit = per_half // 2
        while bit >= 1:
            upper = (lane_slot // bit) % 2 == 1
            nxt = list(xs)
            for p in range(per_half):
                if p & bit:
                    continue
                lo, hi = xs[p], xs[p + bit]
                nxt[p] = jnp.where(upper, pltpu.roll(hi, bit * gl, axis=1), lo)
                nxt[p + bit] = jnp.where(upper, hi, pltpu.roll(lo, 128 - bit * gl, axis=1))
            xs = nxt
            bit //= 2
        return xs

    def gather_u(blk, carry):
        r0 = pl.multiple_of(blk * rb, 16)
        halves = []
        for hb in range(t_len // per_half):
            a = [hn_ref[pl.ds(r0 * t_len + hb * per_half + j, rb, stride=t_len), :] for j in range(per_half)]
            halves.append(slot_transpose(a))
        for i in range(S5_GB):
            u_ref[i, pl.ds(r0, rb), :] = jnp.concatenate([h[i] for h in halves], axis=1).astype(u_ref.dtype)
        return carry

    lax.fori_loop(0, nc // rb, gather_u, 0)

    for g in range(S5_GB):
        s = jnp.dot(u_ref[g], ws_ref[g], preferred_element_type=F32)
        sre[pl.ds(g * pitch, nc), :] = s[:, : 2 * p]
        sim[pl.ds(g * pitch, nc), :] = s[:, 2 * p:]

    ar = ar_ref[...]
    ai = ai_ref[...]
    fwd_lane = lax.broadcasted_iota(jnp.int32, (S5_GB, 2 * p), 1) < p

    def step(k, carry):
        h_r, h_i = carry
        cf = jnp.where(k < ncc, ncl + k, k - ncc)
        cb = nc - 1 - k
        rows_f = pl.ds(cf, S5_GB, stride=pitch)
        rows_b = pl.ds(cb, S5_GB, stride=pitch)
        hre_f[rows_f, :] = h_r
        him_f[rows_f, :] = h_i
        hre_b[rows_b, :] = h_r
        him_b[rows_b, :] = h_i
        s_r = jnp.where(fwd_lane, sre[rows_f, :], sre[rows_b, :])
        s_i = jnp.where(fwd_lane, sim[rows_f, :], sim[rows_b, :])
        n_r = ar * h_r - ai * h_i + s_r
        n_i = ar * h_i + ai * h_r + s_i
        return n_r, n_i

    zero = jnp.zeros((S5_GB, 2 * p), F32)
    lax.fori_loop(0, nc, step, (zero, zero))

    for g in range(S5_GB):
        rows = pl.ds(g * pitch, nc)
        hin = jnp.concatenate([hre_f[rows, :], him_f[rows, :], hre_b[rows, :], him_b[rows, :]], axis=1)
        out = jnp.dot(u_ref[g], wm_ref[g], preferred_element_type=F32)
        out = out + jnp.dot(hin.astype(BF16), w2_ref[g], preferred_element_type=F32)
        y_ref[g] = out

    def scatter_y(blk, carry):
        r0 = pl.multiple_of(blk * rb, 16)
        for hb in range(t_len // per_half):
            yv = [y_ref[i, pl.ds(r0, rb), hb * 128:(hb + 1) * 128] for i in range(S5_GB)]
            for j, tok in enumerate(slot_transpose(yv)):
                yo_ref[pl.ds(r0 * t_len + hb * per_half + j, rb, stride=t_len), :] = tok
        return carry

    lax.fori_loop(0, nc // rb, scatter_y, 0)


def _s5_scan(hn, n_lat, weights):
    bsz, lt, d = hn.shape
    ngrp = d // S5_GROUP
    t = S5_T
    tk = t * S5_GROUP
    nc = lt // t
    ncc = (lt - n_lat) // t
    gb = S5_GB
    assert gb * S5_GROUP == 128 and (128 // S5_GROUP) == gb and t % gb == 0
    ws, lf, rf, lb, rb, w2, ar, ai = weights
    p2 = 2 * S5_STATE
    pitch = nc + 8 if (nc // 8) % 2 == 0 else nc
    kern = functools.partial(_s5_kernel, n_chunks=nc, n_ctx_chunks=ncc, pitch=pitch)
    per_group = lambda *s: pl.BlockSpec((gb,) + s, lambda gi, b: (gi,) + tuple(0 for _ in s))
    return pl.pallas_call(
        kern,
        grid=(ngrp // gb, bsz),
        in_specs=[
            pl.BlockSpec((None, lt, 128), lambda gi, b: (b, 0, gi)),
            per_group(tk, 2 * p2),
            per_group(tk, p2), per_group(p2, tk), per_group(tk, p2), per_group(p2, tk),
            per_group(4 * p2, tk),
            per_group(p2), per_group(p2),
        ],
        out_specs=pl.BlockSpec((None, lt, 128), lambda gi, b: (b, 0, gi)),
        out_shape=jax.ShapeDtypeStruct((bsz, lt, d), F32),
        scratch_shapes=[pltpu.VMEM((gb, nc, tk), BF16), pltpu.VMEM((gb, nc, tk), F32), pltpu.VMEM((gb, tk, tk), BF16)]
        + [pltpu.VMEM((gb * pitch, p2), F32) for _ in range(6)],
        compiler_params=_cparams("parallel", "arbitrary"),
        name="s5_scan",
    )(hn, ws, lf, rf, lb, rb, w2, ar, ai)


def _router_gates(hn2, rw_ref, rb_ref):
    n_exp = rw_ref.shape[0]
    epg = n_exp // N_EXPERT_GROUPS
    logits = lax.dot_general(rw_ref[...], hn2, (((1,), (1,)), ((), ())),
                             preferred_element_type=F32, precision=HIGHEST)
    s = _sigmoid(logits)
    sel = s + rb_ref[...]
    row = [sel[e:e + 1] for e in range(n_exp)]
    gscore = []
    for gi in range(N_EXPERT_GROUPS):
        a, b, c, dd = row[gi * epg: gi * epg + epg]
        hi1, lo1 = jnp.maximum(a, b), jnp.minimum(a, b)
        hi2, lo2 = jnp.maximum(c, dd), jnp.minimum(c, dd)
        gscore.append(jnp.maximum(hi1, hi2) + jnp.maximum(jnp.minimum(hi1, hi2), jnp.maximum(lo1, lo2)))
    gmax = functools.reduce(jnp.maximum, gscore)
    gates = []
    taken = None
    for gi in range(N_EXPERT_GROUPS):
        is_max = gscore[gi] == gmax
        best = is_max if taken is None else jnp.logical_and(is_max, jnp.logical_not(taken))
        taken = is_max if taken is None else jnp.logical_or(taken, is_max)
        for e in range(gi * epg, gi * epg + epg):
            rank = jnp.zeros_like(row[e])
            for j in range(gi * epg, gi * epg + epg):
                if j == e:
                    continue
                ahead = (row[j] >= row[e]) if j < e else (row[j] > row[e])
                rank = rank + ahead.astype(F32)
            chosen = jnp.logical_and(best, rank < float(TOP_K))
            gates.append(jnp.where(chosen, s[e:e + 1], 0.0))
    g = jnp.concatenate(gates, axis=0)
    return g / jnp.sum(g, axis=0, keepdims=True)


def _glu_kernel(y_ref, u_ref, x_ref, mod_ref, d_ref, w_ref, b_ref, g2_ref, rw_ref, rb_ref,
                x1_ref, hn2_ref, gates_ref):
    d = x_ref.shape[-1]
    m = mod_ref[...]
    u = u_ref[...].astype(F32)
    a = _gelu_tanh(y_ref[...].astype(F32) + d_ref[...] * u)
    z = jnp.dot(a.astype(BF16), w_ref[...], preferred_element_type=F32) + b_ref[...]
    out = z[:, :d] * _sigmoid(z[:, d:])
    x1 = x_ref[...] + m[2:3] * out
    x1_ref[...] = x1
    hn2 = _rms(x1, g2_ref[...]) * (1.0 + m[4:5]) + m[3:4]
    hn2_ref[...] = hn2.astype(hn2_ref.dtype)
    gates_ref[...] = _router_gates(hn2, rw_ref, rb_ref)


def _glu_head(y, hn, xall, mods, d_skip, glu_w, glu_b, norm2_g, router_wt, router_b, n_lat_tiles):
    bsz, lt, d = xall.shape
    nt = lt // TOKEN_TILE
    n_exp = router_wt.shape[0]
    tok = pl.BlockSpec((None, TOKEN_TILE, d), lambda b, i: (b, i, 0))
    vec = lambda n: pl.BlockSpec((1, n), lambda b, i: (0, 0))
    return pl.pallas_call(
        _glu_kernel,
        grid=(bsz, nt),
        in_specs=[
            tok, tok, tok,
            pl.BlockSpec((None, 6, d), lambda b, i: (b * 2 + (i >= n_lat_tiles).astype(jnp.int32), 0, 0)),
            vec(d),
            pl.BlockSpec((d, 2 * d), lambda b, i: (0, 0)),
            vec(2 * d),
            vec(d),
            pl.BlockSpec((n_exp, d), lambda b, i: (0, 0)),
            pl.BlockSpec((n_exp, 1), lambda b, i: (0, 0)),
        ],
        out_specs=[
            tok, tok,
            pl.BlockSpec((n_exp, TOKEN_TILE), lambda b, i: (0, b * nt + i)),
        ],
        out_shape=[
            jax.ShapeDtypeStruct((bsz, lt, d), F32),
            jax.ShapeDtypeStruct((bsz, lt, d), BF16),
            jax.ShapeDtypeStruct((n_exp, bsz * lt), F32),
        ],
        compiler_params=_cparams("parallel", "parallel"),
        name="s5_glu_head",
    )(y, hn, xall, mods, d_skip.reshape(1, d), glu_w.astype(BF16), glu_b.reshape(1, 2 * d),
      norm2_g.reshape(1, d), router_wt, router_b.reshape(n_exp, 1))


def _moe_slots_padded(total):
    return jnp.floor((total + float(MOE_PIECE - 1)) * (1.0 / MOE_PIECE)) * float(MOE_PIECE)


def _moe_sort_kernel(gt_ref, t_ref, ts_ref, cnt_ref):
    n_exp, nb = gt_ref.shape
    nbpad = ts_ref.shape[0]
    sel = gt_ref[...] > 0.0
    sel_b = jnp.where(sel, 1.0, 0.0).astype(BF16)
    earlier = lax.broadcasted_iota(jnp.int32, (nb, nb), 0) < lax.broadcasted_iota(jnp.int32, (nb, nb), 1)
    rank = jnp.dot(sel_b, jnp.where(earlier, 1.0, 0.0).astype(BF16), preferred_element_type=F32)
    total = jnp.sum(jnp.where(sel, 1.0, 0.0), axis=1, keepdims=True)
    padded = jnp.broadcast_to(_moe_slots_padded(total), (n_exp, 128))
    below = lax.broadcasted_iota(jnp.int32, (n_exp, n_exp), 1) < lax.broadcasted_iota(jnp.int32, (n_exp, n_exp), 0)
    offs = jnp.dot(jnp.where(below, 1.0, 0.0).astype(BF16), padded.astype(BF16), preferred_element_type=F32)[:, 0:1]
    dest = offs + rank
    d_lo = jnp.min(jnp.where(sel, dest, float(nbpad)), axis=0, keepdims=True).astype(jnp.int32)
    d_hi = jnp.max(jnp.where(sel, dest, -1.0), axis=0, keepdims=True).astype(jnp.int32)
    slot = lax.broadcasted_iota(jnp.int32, (nbpad, nb), 0)
    perm = jnp.where(jnp.logical_or(slot == d_lo, slot == d_hi), 1.0, 0.0).astype(BF16)
    ts_ref[...] = jnp.dot(perm, t_ref[...], preferred_element_type=F32).astype(ts_ref.dtype)
    cnt_ref[...] = padded


def _moe_expert_kernel(tile_e, npieces, rows, ntiles, ts_hbm, w1_ref, w3_ref, w2_ref, ys_hbm,
                       tbuf, ybuf, w1b, w3b, w2b, sem_in, sem_out):
    t = pl.program_id(0)
    nt = ntiles[0]
    last = pl.num_programs(0) - 1
    ppt = MOE_TILE // MOE_PIECE
    slot = t % 2

    def piece_rows(tt, p):
        return pl.ds(pl.multiple_of(rows[tt * ppt + p], MOE_PIECE), MOE_PIECE)

    def for_pieces(tt, fn):
        for p in range(ppt):
            @pl.when(p < npieces[tt])
            def _(p=p):
                fn(p)

    def copy_in(tt, sl, p):
        return pltpu.make_async_copy(ts_hbm.at[piece_rows(tt, p), :],
                                     tbuf.at[sl, pl.ds(p * MOE_PIECE, MOE_PIECE), :], sem_in.at[sl])

    def copy_out(tt, sl, p):
        return pltpu.make_async_copy(ybuf.at[sl, pl.ds(p * MOE_PIECE, MOE_PIECE), :],
                                     ys_hbm.at[piece_rows(tt, p), :], sem_out.at[sl])

    @pl.when(t == 0)
    def _():
        tbuf[...] = jnp.zeros_like(tbuf)

        @pl.when(nt > 0)
        def _():
            for_pieces(0, lambda p: copy_in(0, 0, p).start())

    @pl.when(t + 1 < nt)
    def _():
        for_pieces(t + 1, lambda p: copy_in(t + 1, 1 - slot, p).start())

    @pl.when(jnp.logical_and(t >= 2, t - 2 < nt))
    def _():
        for_pieces(t - 2, lambda p: copy_out(t - 2, slot, p).wait())

    @pl.when(t < nt)
    def _():
        @pl.when(jnp.logical_or(t == 0, tile_e[t] != tile_e[jnp.maximum(t - 1, 0)]))
        def _():
            w1b[...] = w1_ref[...].astype(BF16)
            w3b[...] = w3_ref[...].astype(BF16)
            w2b[...] = w2_ref[...].astype(BF16)

        for_pieces(t, lambda p: copy_in(t, slot, p).wait())
        x = tbuf[slot]
        h = _silu(jnp.dot(x, w1b[...], preferred_element_type=F32)) * jnp.dot(x, w3b[...], preferred_element_type=F32)
        ybuf[slot] = jnp.dot(h.astype(BF16), w2b[...], preferred_element_type=F32).astype(ybuf.dtype)
        for_pieces(t, lambda p: copy_out(t, slot, p).start())

    @pl.when(t == last)
    def _():
        @pl.when(jnp.logical_and(last >= 1, last - 1 < nt))
        def _():
            for_pieces(last - 1, lambda p: copy_out(last - 1, 1 - slot, p).wait())

        @pl.when(last < nt)
        def _():
            for_pieces(last, lambda p: copy_out(last, slot, p).wait())


def _moe_unsort_kernel(ys_ref, g_ref, x_ref, g2_ref, *rest, final):
    if final:
        fg_ref, o_ref = rest
    else:
        (o_ref,) = rest
    nb, n_exp = g_ref.shape
    nbpad = ys_ref.shape[0]
    gates = g_ref[...]
    sel = gates > 0.0
    sel_b = jnp.where(sel, 1.0, 0.0).astype(BF16)
    earlier = lax.broadcasted_iota(jnp.int32, (nb, nb), 1) < lax.broadcasted_iota(jnp.int32, (nb, nb), 0)
    rank = jnp.dot(jnp.where(earlier, 1.0, 0.0).astype(BF16), sel_b, preferred_element_type=F32)
    total = jnp.sum(jnp.where(sel, 1.0, 0.0), axis=0, keepdims=True)
    padded = jnp.broadcast_to(_moe_slots_padded(total), (8, n_exp))
    below = lax.broadcasted_iota(jnp.int32, (n_exp, n_exp), 0) < lax.broadcasted_iota(jnp.int32, (n_exp, n_exp), 1)
    offs = jnp.dot(padded.astype(BF16), jnp.where(below, 1.0, 0.0).astype(BF16), preferred_element_type=F32)[0:1]
    dest = offs + rank
    d_lo = jnp.min(jnp.where(sel, dest, float(nbpad)), axis=1, keepdims=True)
    d_hi = jnp.max(jnp.where(sel, dest, -1.0), axis=1, keepdims=True)
    g_lo = jnp.sum(jnp.where(jnp.logical_and(sel, dest == d_lo), gates, 0.0), axis=1, keepdims=True)
    g_hi = jnp.sum(jnp.where(jnp.logical_and(sel, dest == d_hi), gates, 0.0), axis=1, keepdims=True)
    slot = lax.broadcasted_iota(jnp.int32, (nb, nbpad), 1)
    ys = ys_ref[...]

    def pick(dcol):
        onehot = jnp.where(slot == dcol.astype(jnp.int32), 1.0, 0.0).astype(BF16)
        return jnp.dot(onehot, ys, preferred_element_type=F32)

    moe = g_lo * pick(d_lo) + g_hi * pick(d_hi)
    half = nb // g2_ref.shape[0]
    d = x_ref.shape[1]
    for j in range(g2_ref.shape[0]):
        r = slice(j * half, (j + 1) * half)
        xn = x_ref[r, :] + g2_ref[j] * moe[r]
        if final:
            xn = _rms(xn, fg_ref[...])
            for c in range(half // SSD_CHUNK):
                col = (j * half) // SSD_CHUNK + c
                o_ref[:, col * d:(col + 1) * d] = xn[c * SSD_CHUNK:(c + 1) * SSD_CHUNK]
        else:
            o_ref[r, :] = xn


def _moe_schedule(counts, nbpad, n_tiles):
    nblk, n_exp = counts.shape
    ppt = MOE_TILE // MOE_PIECE
    pc = counts // MOE_PIECE
    loc = jnp.cumsum(pc, axis=1) - pc
    cum_b = jnp.cumsum(pc, axis=0)
    np_e = cum_b[-1]
    tiles_e = (np_e + ppt - 1) // ppt
    tile_end = jnp.cumsum(tiles_e)
    ntiles = tile_end[-1]
    t_idx = jnp.arange(n_tiles, dtype=jnp.int32)
    tile_e = jnp.minimum(jnp.sum((tile_end[None, :] <= t_idx[:, None]).astype(jnp.int32), axis=1), n_exp - 1)
    first = (tile_end - tiles_e)[tile_e]
    piece0 = (t_idx - first) * ppt
    npieces = jnp.where(t_idx < ntiles, jnp.clip(np_e[tile_e] - piece0, 0, ppt), 0)
    i = piece0[:, None] + jnp.arange(ppt, dtype=jnp.int32)[None, :]
    cum_t = cum_b.T[tile_e]
    blk = jnp.minimum(jnp.sum((cum_t[:, None, :] <= i[:, :, None]).astype(jnp.int32), axis=2), nblk - 1)
    before = jnp.take_along_axis(cum_t - pc.T[tile_e], blk, axis=1)
    within = i - before + jnp.take_along_axis(loc.T[tile_e], blk, axis=1)
    rows = blk * nbpad + within * MOE_PIECE
    rows = jnp.where(jnp.arange(ppt)[None, :] < npieces[:, None], rows, 0)
    return (tile_e.astype(jnp.int32), npieces.astype(jnp.int32), rows.reshape(-1).astype(jnp.int32),
            ntiles.reshape(1).astype(jnp.int32))


def _moe(t, gates_t, xres, g2rows, w1, w3, w2, *, final_g=None, blocks_per_batch=None):
    n, d = t.shape
    n_exp, _, f = w1.shape
    nb = MOE_BLOCK
    nblk = n // nb
    nbpad = TOP_K * nb + n_exp * MOE_PIECE
    final = final_g is not None

    ts, cnt = pl.pallas_call(
        _moe_sort_kernel,
        grid=(nblk,),
        in_specs=[pl.BlockSpec((n_exp, nb), lambda j: (0, j)), pl.BlockSpec((nb, d), lambda j: (j, 0))],
        out_specs=[pl.BlockSpec((nbpad, d), lambda j: (j, 0)), pl.BlockSpec((None, n_exp, 128), lambda j: (j, 0, 0))],
        out_shape=[jax.ShapeDtypeStruct((nblk * nbpad, d), BF16), jax.ShapeDtypeStruct((nblk, n_exp, 128), F32)],
        compiler_params=_cparams("parallel"),
        name="moe_sort",
    )(gates_t, t)

    n_tiles = nblk * nbpad // MOE_TILE + n_exp
    tile_e, npieces, rows, ntiles = _moe_schedule(cnt[:, :, 0].astype(jnp.int32), nbpad, n_tiles)
    wspec = lambda shape: pl.BlockSpec((None,) + shape, lambda i, te, npc, rw, nt: (te[i], 0, 0))
    ys = pl.pallas_call(
        _moe_expert_kernel,
        grid_spec=pltpu.PrefetchScalarGridSpec(
            num_scalar_prefetch=4,
            grid=(n_tiles,),
            in_specs=[pl.BlockSpec(memory_space=pl.ANY), wspec((d, f)), wspec((d, f)), wspec((f, d))],
            out_specs=pl.BlockSpec(memory_space=pl.ANY),
            scratch_shapes=[pltpu.VMEM((2, MOE_TILE, d), BF16), pltpu.VMEM((2, MOE_TILE, d), BF16),
                            pltpu.VMEM((d, f), BF16), pltpu.VMEM((d, f), BF16), pltpu.VMEM((f, d), BF16),
                            pltpu.SemaphoreType.DMA((2,)), pltpu.SemaphoreType.DMA((2,))],
        ),
        out_shape=jax.ShapeDtypeStruct((nblk * nbpad, d), BF16),
        input_output_aliases={4: 0},
        compiler_params=_cparams("arbitrary"),
        name="moe_experts",
    )(tile_e, npieces, rows, ntiles, ts, w1, w3, w2)

    halves = nb // TOKEN_TILE
    in_specs = [
        pl.BlockSpec((nbpad, d), lambda j: (j, 0)),
        pl.BlockSpec((nb, n_exp), lambda j: (j, 0)),
        pl.BlockSpec((nb, d), lambda j: (j, 0)),
        pl.BlockSpec((halves, 1, d), lambda j: (j, 0, 0)),
    ]
    args = [ys, gates_t.T, xres, g2rows]
    if final:
        in_specs.append(pl.BlockSpec((1, d), lambda j: (0, 0)))
        args.append(final_g.reshape(1, d))
        cols = nb // SSD_CHUNK
        out_spec = pl.BlockSpec((None, SSD_CHUNK, cols * d), lambda j: (j // blocks_per_batch, 0, j % blocks_per_batch))
        out_shape = jax.ShapeDtypeStruct((nblk // blocks_per_batch, SSD_CHUNK, GRID_W * d), F32)
    else:
        out_spec = pl.BlockSpec((nb, d), lambda j: (j, 0))
        out_shape = jax.ShapeDtypeStruct((n, d), F32)
    return pl.pallas_call(
        functools.partial(_moe_unsort_kernel, final=final),
        grid=(nblk,),
        in_specs=in_specs,
        out_specs=out_spec,
        out_shape=out_shape,
        compiler_params=_cparams("parallel"),
        name="moe_unsort_final" if final else "moe_unsort",
    )(*args)


SSD_COLS_PER_STEP = 8


def _ssd_inproj_kernel(xl_ref, xc_ref, g_ref, mod_ref, wz_ref, wx_ref, wdt_ref, wdtt_ref, bias_ref, biast_ref,
                       a_ref, at_ref, cw_ref, cb_ref, z_ref, xbc_ref, csdt_ref, cst_ref, xp_ref, slab, xt, ext,
                       *, n_lat_steps, n_ctx_chunks):
    step = pl.program_id(1)
    i = pl.program_id(2)
    q = SSD_CHUNK
    ncols = SSD_COLS_PER_STEP
    d = xt.shape[1]
    is_lat = step < n_lat_steps
    c = step * ncols + i
    n_lat_chunks = n_lat_steps * ncols
    n_chunks = n_lat_chunks + n_ctx_chunks

    slot = c % 2
    other = 1 - slot

    def emit_conv():
        acc = jnp.zeros(xbc_ref.shape, F32) + cb_ref[...]
        w = cw_ref[...]
        for k in range(SSD_CONV):
            acc = acc + w[k:k + 1] * ext[slot, pl.ds(8 - SSD_CONV // 2 + k, q), :]
        xbc_ref[...] = _silu(acc).astype(xbc_ref.dtype)

    @pl.when(jnp.logical_and(i == 0, is_lat))
    def _():
        for s in range(d // 128):
            slab[...] = xl_ref[:, :, s * 128:(s + 1) * 128].reshape(q * ncols, 128)
            for w in range(ncols):
                xt[w * q:(w + 1) * q, s * 128:(s + 1) * 128] = slab[pl.ds(w, q, stride=ncols), :]

    @pl.when(jnp.logical_and(i == 0, jnp.logical_not(is_lat)))
    def _():
        xt[0:n_ctx_chunks * q, :] = xc_ref[...]

    @pl.when(c == 0)
    def _():
        ext[...] = jnp.zeros_like(ext)

    @pl.when(jnp.logical_or(is_lat, i < n_ctx_chunks))
    def _():
        emit_conv()
        m = mod_ref[...]
        x = xt[pl.ds(pl.multiple_of(i * q, q), q), :]
        xp_ref[...] = x
        hn = (_rms(x, g_ref[...]) * (1.0 + m[1:2]) + m[0:1]).astype(BF16)
        z_ref[...] = jnp.dot(hn, wz_ref[...], preferred_element_type=F32).astype(z_ref.dtype)
        xbc_new = jnp.dot(hn, wx_ref[...], preferred_element_type=F32)
        starts = jnp.logical_or(c == 0, c == n_lat_chunks)
        ext[other, 8 + q:16 + q, :] = jnp.where(starts, 0.0, xbc_new[0:8])
        ext[slot, 0:8, :] = jnp.where(starts, 0.0, ext[other, q:q + 8, :])
        ext[slot, 8:8 + q, :] = xbc_new

        def softplus(v):
            return jnp.maximum(v, 0.0) + jnp.log(1.0 + jnp.exp(-jnp.abs(v)))

        r_i = lax.broadcasted_iota(jnp.int32, (q, q), 0)
        c_i = lax.broadcasted_iota(jnp.int32, (q, q), 1)
        lower = (r_i >= c_i).astype(F32)
        upper = (r_i <= c_i).astype(F32)
        dt = softplus(jnp.dot(hn, wdt_ref[...], preferred_element_type=F32) + bias_ref[...])
        da = dt * a_ref[...]
        half = da.shape[1] // 2
        cs = jnp.concatenate([jnp.dot(lower, da[:, :half], preferred_element_type=F32, precision=HIGHEST),
                              jnp.dot(upper, da[:, half:], preferred_element_type=F32, precision=HIGHEST)], axis=1)
        lane = lax.broadcasted_iota(jnp.int32, dt.shape, 1)
        csdt_ref[...] = jnp.where(lane % 8 < 4, cs, dt)
        nh = at_ref.shape[0] // 2
        dtt = softplus(lax.dot_general(wdtt_ref[...], hn, (((1,), (1,)), ((), ())), preferred_element_type=F32)
                       + biast_ref[...])
        dat = dtt * at_ref[...]
        cst_f = jnp.dot(dat[:nh], upper, preferred_element_type=F32, precision=HIGHEST)
        cst_b = jnp.dot(dat[nh:], lower, preferred_element_type=F32, precision=HIGHEST)
        cst = jnp.concatenate([cst_f, cst_b], axis=0)
        for j in range(cst_ref.shape[0]):
            cst_ref[j] = cst[j * 4:(j + 1) * 4, :]

    @pl.when(jnp.logical_and(c >= n_chunks, c < n_chunks + 2))
    def _():
        @pl.when(c == n_chunks)
        def _():
            ext[other, 8 + q:16 + q, :] = jnp.zeros((8, ext.shape[2]), F32)

        emit_conv()


def _ssd_inproj(xall, n_lat, norm_g, mods, in_w, dt_bias, a_log, conv_w, conv_b):
    bsz, lt, d = xall.shape
    q = SSD_CHUNK
    ncols = SSD_COLS_PER_STEP
    ncl = n_lat // q
    nc = lt // q
    ncc = nc - ncl
    n_ctx = lt - n_lat
    nh2 = dt_bias.size
    d_inner = (nh2 // 2) * SSD_HEAD_DIM
    conv_ch = in_w.shape[1] - d_inner - nh2
    wz = in_w[:, :d_inner].astype(BF16)
    wx = in_w[:, d_inner:d_inner + conv_ch].astype(BF16)
    wdt = in_w[:, d_inner + conv_ch:].astype(BF16)
    a = -jnp.exp(a_log.astype(F32)).reshape(nh2)
    bias = dt_bias.astype(F32).reshape(nh2)
    ngr = nh2 // 4
    lanes = jnp.arange(2 * nh2)
    dup = (lanes // 8) * 4 + lanes % 4
    assert n_lat // GRID_W == q and GRID_W % ncols == 0 and lt % GRID_W == 0 and ncc + 2 <= ncols and n_lat % n_ctx == 0
    nls = ncl // ncols
    xgrid = xall.reshape(bsz, lt // GRID_W, GRID_W, d)
    kern = functools.partial(_ssd_inproj_kernel, n_lat_steps=nls, n_ctx_chunks=ncc)
    full = lambda s: pl.BlockSpec(s, lambda b, st, i: tuple(0 for _ in s))
    chunk = lambda b, st, i: jnp.minimum(st * ncols + i, nc - 1)
    rows = lambda width: pl.BlockSpec((None, q, width), lambda b, st, i: (b, chunk(b, st, i), 0))
    return pl.pallas_call(
        kern,
        grid=(bsz, nls + 1, ncols),
        in_specs=[
            pl.BlockSpec((None, q, ncols, d), lambda b, st, i: (b, 0, jnp.minimum(st, nls - 1), 0)),
            pl.BlockSpec((None, n_ctx, d), lambda b, st, i: (b, n_lat // n_ctx, 0)),
            full((1, d)),
            pl.BlockSpec((None, 6, d), lambda b, st, i: (b * 2 + (st >= nls).astype(jnp.int32), 0, 0)),
            full((d, d_inner)), full((d, conv_ch)), full((d, 2 * nh2)), full((nh2, d)),
            full((1, 2 * nh2)), full((nh2, 1)), full((1, 2 * nh2)), full((nh2, 1)),
            full((SSD_CONV, conv_ch)), full((1, conv_ch)),
        ],
        out_specs=[
            rows(d_inner),
            pl.BlockSpec((None, q, conv_ch), lambda b, st, i: (b, jnp.clip(st * ncols + i - 2, 0, nc - 1), 0)),
            rows(2 * nh2),
            pl.BlockSpec((None, None, ngr, 4, q), lambda b, st, i: (b, chunk(b, st, i), 0, 0, 0)),
            rows(d),
        ],
        out_shape=[
            jax.ShapeDtypeStruct((bsz, lt, d_inner), BF16),
            jax.ShapeDtypeStruct((bsz, lt, conv_ch), BF16),
            jax.ShapeDtypeStruct((bsz, lt, 2 * nh2), F32),
            jax.ShapeDtypeStruct((bsz, nc, ngr, 4, q), F32),
            jax.ShapeDtypeStruct((bsz, lt, d), F32),
        ],
        scratch_shapes=[pltpu.VMEM((q * ncols, 128), F32), pltpu.VMEM((q * ncols, d), F32),
                        pltpu.VMEM((2, q + 16, conv_ch), F32)],
        compiler_params=_cparams("parallel", "arbitrary", "arbitrary"),
        name="ssd_inproj",
    )(xgrid, xall, norm_g.reshape(1, d), mods, wz, wx, wdt[:, dup], wdt.T, bias[dup].reshape(1, -1),
      bias.reshape(nh2, 1), a[dup].reshape(1, -1), a.reshape(nh2, 1), conv_w.astype(F32),
      conv_b.astype(F32).reshape(1, conv_ch))


def _ssd_scan_dir(x, bm, cm, v, cst, state, spread, reverse):
    q = SSD_CHUNK
    hp = SSD_HEAD_DIM
    gw = x.shape[1]
    r = gw // hp
    v_hi = v.astype(BF16)
    v_r1 = v - v_hi.astype(F32)
    v_mid = v_r1.astype(BF16)
    v_lo = (v_r1 - v_mid.astype(F32)).astype(BF16)
    lhs = jnp.concatenate([v_hi, v_mid, v_lo], axis=1)
    spread_out = jnp.dot(lhs, spread, preferred_element_type=F32)
    cs_x = spread_out[:, :gw]
    dt_x = spread_out[:, gw:2 * gw]
    cs_q = spread_out[:, 2 * gw:]
    end = 0 if reverse else q - 1
    cs_end = cs_x[end:end + 1, :]
    xdt = x * dt_x
    xw = (xdt * jnp.exp(cs_end - cs_x)).astype(BF16)
    xdt = xdt.astype(BF16)
    cb = lax.dot_general(cm, bm, (((1,), (1,)), ((), ())), preferred_element_type=F32)
    y_off = jnp.dot(cm, state.astype(BF16), preferred_element_type=F32) * jnp.exp(cs_x)
    r_i = lax.broadcasted_iota(jnp.int32, (q, q), 0)
    c_i = lax.broadcasted_iota(jnp.int32, (q, q), 1)
    mask = (r_i <= c_i) if reverse else (r_i >= c_i)
    lane = lax.broadcasted_iota(jnp.int32, (q, 2 * hp), 1)
    ys = []
    for pair in range(r // 2):
        xp = xdt[:, pair * 2 * hp:(pair + 1) * 2 * hp]
        yd = []
        for h in (2 * pair, 2 * pair + 1):
            seg = jnp.exp(jnp.minimum(cs_q[:, h * q:(h + 1) * q] - cst[h:h + 1, :], 0.0))
            gmat = (cb * jnp.where(mask, seg, 0.0)).astype(BF16)
            yd.append(jnp.dot(gmat, xp, preferred_element_type=F32))
        ys.append(jnp.where(lane < hp, yd[0], yd[1]))
    upd = lax.dot_general(bm, xw, (((0,), (0,)), ((), ())), preferred_element_type=F32)
    return jnp.concatenate(ys, axis=1) + y_off, state * jnp.exp(cs_end) + upd


def _ssd_spread_matrix(r, gw, hp, q):
    wide = 2 * gw + r * q
    col = lax.broadcasted_iota(jnp.int32, (6 * r, wide), 0) % (2 * r)
    ln = lax.broadcasted_iota(jnp.int32, (6 * r, wide), 1)
    cs_hit = jnp.logical_or(jnp.logical_and(ln < gw, ln // hp == col),
                            jnp.logical_and(ln >= 2 * gw, (ln - 2 * gw) // q == col))
    dt_hit = jnp.logical_and(jnp.logical_and(ln >= gw, ln < 2 * gw), (ln - gw) // hp == col - r)
    hit = jnp.logical_or(jnp.logical_and(col < r, cs_hit), jnp.logical_and(col >= r, dt_hit))
    return jnp.where(hit, 1.0, 0.0).astype(BF16)


def _ssd_scan_kernel(xf, bf, cf, csdtf, cstf, xb, bb, cb, csdtb, cstb, yf_ref, yb_ref, state_f, state_b):
    @pl.when(pl.program_id(1) == 0)
    def _():
        state_f[...] = jnp.zeros_like(state_f)
        state_b[...] = jnp.zeros_like(state_b)

    ngr, n, gw = state_f.shape
    r = gw // SSD_HEAD_DIM
    spread = _ssd_spread_matrix(r, gw, SSD_HEAD_DIM, SSD_CHUNK)
    dirs = ((xf, bf, cf, csdtf, cstf, yf_ref, state_f, False, 0), (xb, bb, cb, csdtb, cstb, yb_ref, state_b, True, ngr))
    for g in range(ngr):
        for x_ref, b_ref, c_ref, csdt_ref, cst_ref, y_ref, state, reverse, lane_group0 in dirs:
            j = lane_group0 + g
            y, new_state = _ssd_scan_dir(
                x_ref[:, g * gw:(g + 1) * gw].astype(F32), b_ref[:, g * n:(g + 1) * n], c_ref[:, g * n:(g + 1) * n],
                csdt_ref[:, 2 * r * j:2 * r * (j + 1)], cst_ref[g], state[g], spread, reverse)
            y_ref[:, g * gw:(g + 1) * gw] = y.astype(y_ref.dtype)
            state[g] = new_state


def _ssd_scan(xbc, csdt, cst, n_lat):
    bsz, lt, ch = xbc.shape
    q = SSD_CHUNK
    nc, ncl = lt // q, n_lat // q
    ngr = SSD_GROUPS
    n = SSD_STATE
    d_inner = ch - 2 * ngr * n
    gw = d_inner // ngr
    assert d_inner % (ngr * n) == 0

    def specs(chunk, direction):
        return [
            pl.BlockSpec((None, q, d_inner), lambda b, k: (b, chunk(k), 0)),
            pl.BlockSpec((None, q, ngr * n), lambda b, k: (b, chunk(k), d_inner // (ngr * n))),
            pl.BlockSpec((None, q, ngr * n), lambda b, k: (b, chunk(k), d_inner // (ngr * n) + 1)),
            pl.BlockSpec((None, q, csdt.shape[2]), lambda b, k: (b, chunk(k), 0)),
            pl.BlockSpec((None, None, ngr, 4, q), lambda b, k: (b, chunk(k), direction, 0, 0)),
        ]

    fwd = lambda k: (k + ncl) % nc
    bwd = lambda k: nc - 1 - k
    out = jax.ShapeDtypeStruct((bsz, lt, d_inner), BF16)
    return pl.pallas_call(
        _ssd_scan_kernel,
        grid=(bsz, nc),
        in_specs=specs(fwd, 0) + specs(bwd, 1),
        out_specs=[pl.BlockSpec((None, q, d_inner), lambda b, k: (b, fwd(k), 0)),
                   pl.BlockSpec((None, q, d_inner), lambda b, k: (b, bwd(k), 0))],
        out_shape=[out, out],
        scratch_shapes=[pltpu.VMEM((ngr, n, gw), F32), pltpu.VMEM((ngr, n, gw), F32)],
        compiler_params=_cparams("parallel", "arbitrary"),
        name="ssd_scan",
    )(xbc, xbc, xbc, csdt, cst, xbc, xbc, xbc, csdt, cst)


def _ssd_finish_kernel(yf_ref, yb_ref, xs_ref, z_ref, x_ref, mod_ref, dsk_ref, ng_ref, w_ref, g2_ref,
                       rw_ref, rb_ref, x3_ref, hn2_ref, gates_ref):
    m = mod_ref[...]
    y = yf_ref[...].astype(F32) + yb_ref[...].astype(F32) + dsk_ref[...] * xs_ref[...].astype(F32)
    gated = y * _silu(z_ref[...].astype(F32))
    nrm = _rms(gated, ng_ref[...])
    out = jnp.dot(nrm.astype(BF16), w_ref[...], preferred_element_type=F32)
    x3 = x_ref[...] + m[2:3] * out
    x3_ref[...] = x3
    hn2 = _rms(x3, g2_ref[...]) * (1.0 + m[4:5]) + m[3:4]
    hn2_ref[...] = hn2.astype(hn2_ref.dtype)
    gates_ref[...] = _router_gates(hn2, rw_ref, rb_ref)


def _ssd_finish(yf, yb, xbc, z, xall, n_lat, mods, d_skip, norm_g, out_w, norm2_g, router_wt, router_b):
    bsz, lt, d_inner = z.shape
    d = xall.shape[-1]
    q = TOKEN_TILE
    ncl = n_lat // q
    n_exp = router_wt.shape[0]
    inner = pl.BlockSpec((None, q, d_inner), lambda b, c: (b, c, 0))
    tok = pl.BlockSpec((None, q, d), lambda b, c: (b, c, 0))
    full = lambda s: pl.BlockSpec(s, lambda b, c: tuple(0 for _ in s))
    dsk = jnp.repeat(d_skip.astype(F32), SSD_HEAD_DIM).reshape(1, d_inner)
    return pl.pallas_call(
        _ssd_finish_kernel,
        grid=(bsz, ncl),
        in_specs=[
            inner, inner, inner, inner,
            tok,
            pl.BlockSpec((None, 6, d), lambda b, c: (b * 2, 0, 0)),
            full((1, d_inner)), full((1, d_inner)), full((d_inner, d)), full((1, d)),
            full((n_exp, d)), full((n_exp, 1)),
        ],
        out_specs=[tok, tok, pl.BlockSpec((n_exp, q), lambda b, c: (0, b * ncl + c))],
        out_shape=[
            jax.ShapeDtypeStruct((bsz, n_lat, d), F32),
            jax.ShapeDtypeStruct((bsz, n_lat, d), BF16),
            jax.ShapeDtypeStruct((n_exp, bsz * n_lat), F32),
        ],
        compiler_params=_cparams("parallel", "parallel"),
        name="ssd_finish",
    )(yf, yb, xbc, z, xall, mods, dsk, norm_g.reshape(1, d_inner), out_w.astype(BF16),
      norm2_g.reshape(1, d), router_wt, router_b.reshape(n_exp, 1))


def kernel(x, c, ctx, c_ctx, mod_w, mod_b, norm1_g, norm2_g, final_g, s5_lam_re, s5_lam_im, s5_log_dt, s5_b_re, s5_b_im, s5_c_re, s5_c_im, s5_d, s5_glu_w, s5_glu_b, ssd_in_w, ssd_conv_w, ssd_conv_b, ssd_dt_bias, ssd_a_log, ssd_d, ssd_norm_g, ssd_out_w, router_w, router_b, moe_w1, moe_w3, moe_w2):
    bsz, n_lat, d = x.shape
    n_ctx = ctx.shape[1]
    lt = n_lat + n_ctx
    n_exp = router_w.shape[1]
    assert n_lat % TOKEN_TILE == 0 and n_ctx % TOKEN_TILE == 0
    assert (bsz * lt) % MOE_BLOCK == 0 and n_lat % MOE_BLOCK == 0 and MOE_BLOCK % TOKEN_TILE == 0
    assert TOKEN_TILE % SSD_CHUNK == 0

    mods = _modulation(c, c_ctx, mod_w, mod_b)
    router_wt = router_w.T.astype(F32)
    w1, w3, w2 = moe_w1, moe_w3, moe_w2
    nlt = n_lat // TOKEN_TILE
    tpb = lt // TOKEN_TILE

    xall = jnp.concatenate([x, ctx], axis=1)
    hn = _prenorm(xall, norm1_g[0], mods[0], nlt)
    s5w = _s5_weights(s5_lam_re[0], s5_lam_im[0], s5_log_dt[0], s5_b_re[0], s5_b_im[0], s5_c_re[0], s5_c_im[0])
    y = _s5_scan(hn, n_lat, s5w)
    x1, hn2, gates_t = _glu_head(y, hn, xall, mods[0], s5_d[0], s5_glu_w[0], s5_glu_b[0], norm2_g[0],
                                 router_wt, router_b, nlt)
    g2_lat = jnp.broadcast_to(mods[0][0::2, None, 5], (bsz, nlt, d))
    g2_ctx = jnp.broadcast_to(mods[0][1::2, None, 5], (bsz, tpb - nlt, d))
    g2rows = jnp.concatenate([g2_lat, g2_ctx], axis=1).reshape(bsz * tpb, 1, d)
    x2 = _moe(hn2.reshape(bsz * lt, d), gates_t, x1.reshape(bsz * lt, d), g2rows, w1[0], w3[0], w2[0]).reshape(bsz, lt, d)

    z, xbc, csdt, cst, x2p = _ssd_inproj(x2, n_lat, norm1_g[1], mods[1], ssd_in_w[0], ssd_dt_bias[0], ssd_a_log[0],
                                             ssd_conv_w[0], ssd_conv_b[0])
    yf, yb = _ssd_scan(xbc, csdt, cst, n_lat)
    x3, hn3, gates3_t = _ssd_finish(yf, yb, xbc, z, x2p, n_lat, mods[1], ssd_d[0], ssd_norm_g[0], ssd_out_w[0],
                                    norm2_g[1], router_wt, router_b)
    g2rows = jnp.broadcast_to(mods[1][0::2, None, 5], (bsz, nlt, d)).reshape(bsz * nlt, 1, d)
    out = _moe(hn3.reshape(bsz * n_lat, d), gates3_t, x3.reshape(bsz * n_lat, d), g2rows, w1[1], w3[1], w2[1],
               final_g=final_g, blocks_per_batch=n_lat // MOE_BLOCK)
    return out.reshape(bsz, n_lat, d)
```

```python
import functools

import jax
import jax.numpy as jnp
from jax import lax
from jax.experimental import pallas as pl
from jax.experimental.pallas import tpu as pltpu

F32 = jnp.float32
BF16 = jnp.bfloat16
HIGHEST = lax.Precision.HIGHEST

GRID_W = 64
RMS_EPS = 1e-6

S5_GROUP = 16
S5_STATE = 64
S5_T = 16
S5_GB = 8

SSD_HEAD_DIM = 64
SSD_GROUPS = 8
SSD_STATE = 128
SSD_CONV = 5
SSD_CHUNK = 128

N_EXPERT_GROUPS = 4
TOP_K = 2

TOKEN_TILE = 256
ROW_SUBTILES = 2
MOE_BLOCK = 512
MOE_PIECE = 16
MOE_TILE = 256
VMEM_LIMIT_BYTES = 56 * 1024 * 1024


def _cparams(*sem):
    return pltpu.CompilerParams(dimension_semantics=sem, vmem_limit_bytes=VMEM_LIMIT_BYTES)


def _sigmoid(v):
    return 1.0 / (1.0 + jnp.exp(-v))


def _silu(v):
    return v * _sigmoid(v)


def _gelu_tanh(v):
    return 0.5 * v * (1.0 + jnp.tanh(0.7978845608028654 * (v + 0.044715 * (v * v * v))))


def _rms(v, g):
    return v * lax.rsqrt(jnp.mean(v * v, axis=-1, keepdims=True) + RMS_EPS) * g


def _mod_kernel(cc_ref, w_ref, b_ref, o_ref):
    a = _silu(cc_ref[...])
    o_ref[...] = jnp.dot(a, w_ref[...], preferred_element_type=F32, precision=HIGHEST) + b_ref[...]


def _modulation(c, c_ctx, mod_w, mod_b):
    depth, d, d6 = mod_w.shape
    bsz = c.shape[0]
    rows = 8
    cc = jnp.zeros((rows, d), F32).at[:bsz].set(c).at[bsz].set(c_ctx)
    tn = d6 // 4
    out = pl.pallas_call(
        _mod_kernel,
        grid=(depth, d6 // tn),
        in_specs=[
            pl.BlockSpec((rows, d), lambda i, j: (0, 0)),
            pl.BlockSpec((None, d, tn), lambda i, j: (i, 0, j)),
            pl.BlockSpec((None, 1, tn), lambda i, j: (i, 0, j)),
        ],
        out_specs=pl.BlockSpec((None, rows, tn), lambda i, j: (i, 0, j)),
        out_shape=jax.ShapeDtypeStruct((depth, rows, d6), F32),
        compiler_params=_cparams("parallel", "parallel"),
        name="modulation",
    )(cc, mod_w, mod_b.reshape(depth, 1, d6))
    lat = out[:, :bsz].reshape(depth, bsz, 1, 6, d)
    ctx = jnp.broadcast_to(out[:, bsz].reshape(depth, 1, 1, 6, d), (depth, bsz, 1, 6, d))
    return jnp.concatenate([lat, ctx], axis=2).reshape(depth, bsz * 2, 6, d)


def _prenorm_kernel(x_ref, g_ref, mod_ref, o_ref):
    m = mod_ref[...]
    hn = _rms(x_ref[...], g_ref[...]) * (1.0 + m[1:2]) + m[0:1]
    o_ref[...] = hn.astype(o_ref.dtype)


def _prenorm(xall, g, mods, n_lat_tiles):
    bsz, lt, d = xall.shape
    nt = lt // TOKEN_TILE
    return pl.pallas_call(
        _prenorm_kernel,
        grid=(bsz, nt),
        in_specs=[
            pl.BlockSpec((None, TOKEN_TILE, d), lambda b, i: (b, i, 0)),
            pl.BlockSpec((1, d), lambda b, i: (0, 0)),
            pl.BlockSpec((None, 6, d), lambda b, i: (b * 2 + (i >= n_lat_tiles).astype(jnp.int32), 0, 0)),
        ],
        out_specs=pl.BlockSpec((None, TOKEN_TILE, d), lambda b, i: (b, i, 0)),
        out_shape=jax.ShapeDtypeStruct((bsz, lt, d), F32),
        compiler_params=_cparams("parallel", "parallel"),
        name="prenorm",
    )(xall, g.reshape(1, d), mods)


def _s5_weights(lam_re, lam_im, log_dt, b_re, b_im, c_re, c_im):
    t = S5_T
    k16 = b_re.shape[-1]

    def cmul(ar, ai, br, bi):
        return ar * br - ai * bi, ar * bi + ai * br

    def direction(k):
        lr, li = lam_re[k], lam_im[k]
        step = jnp.exp(log_dt[k])[:, None]
        mag = jnp.exp(lr * step)
        abar_r = mag * jnp.cos(li * step)
        abar_i = mag * jnp.sin(li * step)
        den = lr * lr + li * li
        q_r = ((abar_r - 1.0) * lr + abar_i * li) / den
        q_i = (abar_i * lr - (abar_r - 1.0) * li) / den
        bb_r, bb_i = cmul(q_r[..., None], q_i[..., None], b_re, b_im)

        def power(tau):
            tau = jnp.asarray(tau, F32)[None, :, None]
            m = jnp.exp((lr * step)[:, None, :] * tau)
            return m * jnp.cos((li * step)[:, None, :] * tau), m * jnp.sin((li * step)[:, None, :] * tau)

        return bb_r.transpose(0, 2, 1), bb_i.transpose(0, 2, 1), power

    rows = lambda v: jnp.repeat(v, k16, axis=1)
    row_tile = lambda v: jnp.tile(v, (1, t, 1))
    cols = lambda v: jnp.repeat(v.transpose(0, 2, 1), k16, axis=2)
    col_tile = lambda v: jnp.tile(v, (1, 1, t))
    ct_r, ct_i = c_re.transpose(0, 2, 1), c_im.transpose(0, 2, 1)
    steps = jnp.arange(t)

    def left(bt_r, bt_i, power, tau):
        pr, pi = power(tau)
        return cmul(row_tile(bt_r), row_tile(bt_i), rows(pr), rows(pi))

    def right(power, tau):
        pr, pi = power(tau)
        return cmul(col_tile(ct_r), col_tile(ct_i), cols(pr), cols(pi))

    bf_r, bf_i, pow_f = direction(0)
    bb_r, bb_i, pow_b = direction(1)
    lf_r, lf_i = left(bf_r, bf_i, pow_f, -steps)
    rf_r, rf_i = right(pow_f, steps)
    lb_r, lb_i = left(bb_r, bb_i, pow_b, steps)
    rb_r, rb_i = right(pow_b, -steps)
    lf = jnp.concatenate([lf_r, lf_i], axis=-1)
    lb = jnp.concatenate([lb_r, lb_i], axis=-1)
    rf = jnp.concatenate([rf_r, -rf_i], axis=1)
    rb = jnp.concatenate([rb_r, -rb_i], axis=1)

    sf_r, sf_i = left(bf_r, bf_i, pow_f, t - 1 - steps)
    ws = jnp.concatenate([sf_r, lb_r, sf_i, lb_i], axis=-1)

    of_r, of_i = right(pow_f, steps + 1)
    ob_r, ob_i = right(pow_b, t - steps)
    zero = jnp.zeros_like(of_r)
    w2 = jnp.concatenate([of_r, zero, -of_i, zero, zero, ob_r, zero, -ob_i], axis=1)

    af_r, af_i = pow_f([t])
    ab_r, ab_i = pow_b([t])
    ar = jnp.concatenate([af_r[:, 0], ab_r[:, 0]], axis=-1)
    ai = jnp.concatenate([af_i[:, 0], ab_i[:, 0]], axis=-1)
    return ws.astype(BF16), lf, rf, lb, rb, w2.astype(BF16), ar, ai


def _s5_row_block(nc):
    return max(rb for rb in range(16, min(nc, 176) + 1, 16) if nc % rb == 0)


def _dot_split3(a, b):
    a_hi = a.astype(BF16)
    a_lo = (a - a_hi.astype(F32)).astype(BF16)
    b_hi = b.astype(BF16)
    b_lo = (b - b_hi.astype(F32)).astype(BF16)
    dot = functools.partial(jnp.dot, preferred_element_type=F32)
    return dot(a_hi, b_hi) + dot(a_hi, b_lo) + dot(a_lo, b_hi)


def _s5_kernel(hn_ref, ws_ref, lf_ref, rf_ref, lb_ref, rb_ref, w2_ref, ar_ref, ai_ref, yo_ref,
               u_ref, y_ref, wm_ref, sre, sim, hre_f, him_f, hre_b, him_b, *, n_chunks, n_ctx_chunks, pitch):
    nc, ncc = n_chunks, n_ctx_chunks
    ncl = nc - ncc
    p = S5_STATE
    t_len = S5_T
    gl = S5_GROUP
    per_half = 128 // gl
    rb = _s5_row_block(nc)
    lane_slot = lax.broadcasted_iota(jnp.int32, (rb, 128), 1) // gl

    @pl.when(pl.program_id(1) == 0)
    def _():
        tk = t_len * gl
        src_tok = lax.broadcasted_iota(jnp.int32, (tk, tk), 0) // gl
        dst_tok = lax.broadcasted_iota(jnp.int32, (tk, tk), 1) // gl
        for g in range(S5_GB):
            causal = _dot_split3(lf_ref[g], rf_ref[g])
            anti = _dot_split3(lb_ref[g], rb_ref[g])
            wm = jnp.where(dst_tok >= src_tok, causal, 0.0) + jnp.where(src_tok >= dst_tok, anti, 0.0)
            wm_ref[g] = wm.astype(wm_ref.dtype)

    def slot_transpose(xs):
        xs = list(xs)
        bit = per_half // 2
        while bit >= 1:
            upper = (lane_slot // bit) % 2 == 1
            nxt = list(xs)
            for p in range(per_half):
                if p & bit:
                    continue
                lo, hi = xs[p], xs[p + bit]
                nxt[p] = jnp.where(upper, pltpu.roll(hi, bit * gl, axis=1), lo)
                nxt[p + bit] = jnp.where(upper, hi, pltpu.roll(lo, 128 - bit * gl, axis=1))
            xs = nxt
            bit //= 2
        return xs

    def gather_u(blk, carry):
        r0 = pl.multiple_of(blk * rb, 16)
        halves = []
        for hb in range(t_len // per_half):
            a = [hn_ref[pl.ds(r0 * t_len + hb * per_half + j, rb, stride=t_len), :] for j in range(per_half)]
            halves.append(slot_transpose(a))
        for i in range(S5_GB):
            u_ref[i, pl.ds(r0, rb), :] = jnp.concatenate([h[i] for h in halves], axis=1).astype(u_ref.dtype)
        return carry

    lax.fori_loop(0, nc // rb, gather_u, 0)

    for g in range(S5_GB):
        s = jnp.dot(u_ref[g], ws_ref[g], preferred_element_type=F32)
        sre[pl.ds(g * pitch, nc), :] = s[:, : 2 * p]
        sim[pl.ds(g * pitch, nc), :] = s[:, 2 * p:]

    ar = ar_ref[...]
    ai = ai_ref[...]
    fwd_lane = lax.broadcasted_iota(jnp.int32, (S5_GB, 2 * p), 1) < p

    def step(k, carry):
        h_r, h_i = carry
        cf = jnp.where(k < ncc, ncl + k, k - ncc)
        cb = nc - 1 - k
        rows_f = pl.ds(cf, S5_GB, stride=pitch)
        rows_b = pl.ds(cb, S5_GB, stride=pitch)
        hre_f[rows_f, :] = h_r
        him_f[rows_f, :] = h_i
        hre_b[rows_b, :] = h_r
        him_b[rows_b, :] = h_i
        s_r = jnp.where(fwd_lane, sre[rows_f, :], sre[rows_b, :])
        s_i = jnp.where(fwd_lane, sim[rows_f, :], sim[rows_b, :])
        n_r = ar * h_r - ai * h_i + s_r
        n_i = ar * h_i + ai * h_r + s_i
        return n_r, n_i

    zero = jnp.zeros((S5_GB, 2 * p), F32)
    lax.fori_loop(0, nc, step, (zero, zero))

    for g in range(S5_GB):
        rows = pl.ds(g * pitch, nc)
        hin = jnp.concatenate([hre_f[rows, :], him_f[rows, :], hre_b[rows, :], him_b[rows, :]], axis=1)
        out = jnp.dot(u_ref[g], wm_ref[g], preferred_element_type=F32)
        out = out + jnp.dot(hin.astype(BF16), w2_ref[g], preferred_element_type=F32)
        y_ref[g] = out

    def scatter_y(blk, carry):
        r0 = pl.multiple_of(blk * rb, 16)
        for hb in range(t_len // per_half):
            yv = [y_ref[i, pl.ds(r0, rb), hb * 128:(hb + 1) * 128] for i in range(S5_GB)]
            for j, tok in enumerate(slot_transpose(yv)):
                yo_ref[pl.ds(r0 * t_len + hb * per_half + j, rb, stride=t_len), :] = tok
        return carry

    lax.fori_loop(0, nc // rb, scatter_y, 0)


def _s5_scan(hn, n_lat, weights):
    bsz, lt, d = hn.shape
    ngrp = d // S5_GROUP
    t = S5_T
    tk = t * S5_GROUP
    nc = lt // t
    ncc = (lt - n_lat) // t
    gb = S5_GB
    assert gb * S5_GROUP == 128 and (128 // S5_GROUP) == gb and t % gb == 0
    ws, lf, rf, lb, rb, w2, ar, ai = weights
    p2 = 2 * S5_STATE
    pitch = nc + 8 if (nc // 8) % 2 == 0 else nc
    kern = functools.partial(_s5_kernel, n_chunks=nc, n_ctx_chunks=ncc, pitch=pitch)
    per_group = lambda *s: pl.BlockSpec((gb,) + s, lambda gi, b: (gi,) + tuple(0 for _ in s))
    return pl.pallas_call(
        kern,
        grid=(ngrp // gb, bsz),
        in_specs=[
            pl.BlockSpec((None, lt, 128), lambda gi, b: (b, 0, gi)),
            per_group(tk, 2 * p2),
            per_group(tk, p2), per_group(p2, tk), per_group(tk, p2), per_group(p2, tk),
            per_group(4 * p2, tk),
            per_group(p2), per_group(p2),
        ],
        out_specs=pl.BlockSpec((None, lt, 128), lambda gi, b: (b, 0, gi)),
        out_shape=jax.ShapeDtypeStruct((bsz, lt, d), F32),
        scratch_shapes=[pltpu.VMEM((gb, nc, tk), BF16), pltpu.VMEM((gb, nc, tk), F32), pltpu.VMEM((gb, tk, tk), BF16)]
        + [pltpu.VMEM((gb * pitch, p2), F32) for _ in range(6)],
        compiler_params=_cparams("parallel", "arbitrary"),
        name="s5_scan",
    )(hn, ws, lf, rf, lb, rb, w2, ar, ai)


def _router_gates(hn2, rw_ref, rb_ref):
    n_exp = rw_ref.shape[0]
    epg = n_exp // N_EXPERT_GROUPS
    logits = lax.dot_general(rw_ref[...], hn2, (((1,), (1,)), ((), ())),
                             preferred_element_type=F32, precision=HIGHEST)
    s = _sigmoid(logits)
    sel = s + rb_ref[...]
    row = [sel[e:e + 1] for e in range(n_exp)]
    gscore = []
    for gi in range(N_EXPERT_GROUPS):
        a, b, c, dd = row[gi * epg: gi * epg + epg]
        hi1, lo1 = jnp.maximum(a, b), jnp.minimum(a, b)
        hi2, lo2 = jnp.maximum(c, dd), jnp.minimum(c, dd)
        gscore.append(jnp.maximum(hi1, hi2) + jnp.maximum(jnp.minimum(hi1, hi2), jnp.maximum(lo1, lo2)))
    gmax = functools.reduce(jnp.maximum, gscore)
    gates = []
    taken = None
    for gi in range(N_EXPERT_GROUPS):
        is_max = gscore[gi] == gmax
        best = is_max if taken is None else jnp.logical_and(is_max, jnp.logical_not(taken))
        taken = is_max if taken is None else jnp.logical_or(taken, is_max)
        for e in range(gi * epg, gi * epg + epg):
            rank = jnp.zeros_like(row[e])
            for j in range(gi * epg, gi * epg + epg):
                if j == e:
                    continue
                ahead = (row[j] >= row[e]) if j < e else (row[j] > row[e])
                rank = rank + ahead.astype(F32)
            chosen = jnp.logical_and(best, rank < float(TOP_K))
            gates.append(jnp.where(chosen, s[e:e + 1], 0.0))
    g = jnp.concatenate(gates, axis=0)
    return g / jnp.sum(g, axis=0, keepdims=True)


def _glu_kernel(y_ref, u_ref, x_ref, mod_ref, d_ref, w_ref, b_ref, g2_ref, rw_ref, rb_ref,
                x1_ref, hn2_ref, gates_ref):
    d = x_ref.shape[-1]
    m = mod_ref[...]
    nsub = ROW_SUBTILES
    sub = x_ref.shape[0] // nsub
    for s in range(nsub):
        r = slice(s * sub, (s + 1) * sub)
        u = u_ref[r, :].astype(F32)
        a = _gelu_tanh(y_ref[r, :].astype(F32) + d_ref[...] * u)
        z = jnp.dot(a.astype(BF16), w_ref[...], preferred_element_type=F32) + b_ref[...]
        out = z[:, :d] * _sigmoid(z[:, d:])
        x1 = x_ref[r, :] + m[2:3] * out
        x1_ref[r, :] = x1
        hn2 = _rms(x1, g2_ref[...]) * (1.0 + m[4:5]) + m[3:4]
        hn2_ref[r, :] = hn2.astype(hn2_ref.dtype)
        gates_ref[:, r] = _router_gates(hn2, rw_ref, rb_ref)


def _glu_head(y, hn, xall, mods, d_skip, glu_w, glu_b, norm2_g, router_wt, router_b, n_lat_tiles):
    bsz, lt, d = xall.shape
    nt = lt // TOKEN_TILE
    n_exp = router_wt.shape[0]
    tok = pl.BlockSpec((None, TOKEN_TILE, d), lambda b, i: (b, i, 0))
    vec = lambda n: pl.BlockSpec((1, n), lambda b, i: (0, 0))
    return pl.pallas_call(
        _glu_kernel,
        grid=(bsz, nt),
        in_specs=[
            tok, tok, tok,
            pl.BlockSpec((None, 6, d), lambda b, i: (b * 2 + (i >= n_lat_tiles).astype(jnp.int32), 0, 0)),
            vec(d),
            pl.BlockSpec((d, 2 * d), lambda b, i: (0, 0)),
            vec(2 * d),
            vec(d),
            pl.BlockSpec((n_exp, d), lambda b, i: (0, 0)),
            pl.BlockSpec((n_exp, 1), lambda b, i: (0, 0)),
        ],
        out_specs=[
            tok, tok,
            pl.BlockSpec((n_exp, TOKEN_TILE), lambda b, i: (0, b * nt + i)),
        ],
        out_shape=[
            jax.ShapeDtypeStruct((bsz, lt, d), F32),
            jax.ShapeDtypeStruct((bsz, lt, d), BF16),
            jax.ShapeDtypeStruct((n_exp, bsz * lt), F32),
        ],
        compiler_params=_cparams("parallel", "parallel"),
        name="s5_glu_head",
    )(y, hn, xall, mods, d_skip.reshape(1, d), glu_w.astype(BF16), glu_b.reshape(1, 2 * d),
      norm2_g.reshape(1, d), router_wt, router_b.reshape(n_exp, 1))


def _moe_slots_padded(total):
    return jnp.floor((total + float(MOE_PIECE - 1)) * (1.0 / MOE_PIECE)) * float(MOE_PIECE)


def _moe_sort_kernel(gt_ref, t_ref, ts_ref, cnt_ref):
    n_exp, nb = gt_ref.shape
    nbpad = ts_ref.shape[0]
    sel = jnp.logical_and(gt_ref[...] > 0.0, pl.program_id(0) < pl.num_programs(0) - 1)
    sel_b = jnp.where(sel, 1.0, 0.0).astype(BF16)
    earlier = lax.broadcasted_iota(jnp.int32, (nb, nb), 0) < lax.broadcasted_iota(jnp.int32, (nb, nb), 1)
    rank = jnp.dot(sel_b, jnp.where(earlier, 1.0, 0.0).astype(BF16), preferred_element_type=F32)
    total = jnp.sum(jnp.where(sel, 1.0, 0.0), axis=1, keepdims=True)
    padded = jnp.broadcast_to(_moe_slots_padded(total), (n_exp, 128))
    below = lax.broadcasted_iota(jnp.int32, (n_exp, n_exp), 1) < lax.broadcasted_iota(jnp.int32, (n_exp, n_exp), 0)
    offs = jnp.dot(jnp.where(below, 1.0, 0.0).astype(BF16), padded.astype(BF16), preferred_element_type=F32)[:, 0:1]
    dest = offs + rank
    d_lo = jnp.min(jnp.where(sel, dest, float(nbpad)), axis=0, keepdims=True).astype(jnp.int32)
    d_hi = jnp.max(jnp.where(sel, dest, -1.0), axis=0, keepdims=True).astype(jnp.int32)
    slot = lax.broadcasted_iota(jnp.int32, (nbpad, nb), 0)
    perm = jnp.where(jnp.logical_or(slot == d_lo, slot == d_hi), 1.0, 0.0).astype(BF16)
    ts_ref[...] = jnp.dot(perm, t_ref[...], preferred_element_type=F32).astype(ts_ref.dtype)
    cnt_ref[...] = padded


def _moe_expert_kernel(tile_e, src_rows, dst_rows, ntiles, ts_hbm, w1_ref, w3_ref, w2_ref, ys_hbm,
                       tbuf, ybuf, w1b, w3b, w2b, sem_in, sem_out):
    t = pl.program_id(0)
    nt = ntiles[0]
    last = pl.num_programs(0) - 1
    ppt = MOE_TILE // MOE_PIECE
    slot = t % 2

    def rows_at(table, tt, p):
        return pl.ds(pl.multiple_of(table[tt * ppt + p], MOE_PIECE), MOE_PIECE)

    def copy_in(tt, sl, p):
        return pltpu.make_async_copy(ts_hbm.at[rows_at(src_rows, tt, p), :],
                                     tbuf.at[sl, pl.ds(p * MOE_PIECE, MOE_PIECE), :], sem_in.at[sl])

    def copy_out(tt, sl, p):
        return pltpu.make_async_copy(ybuf.at[sl, pl.ds(p * MOE_PIECE, MOE_PIECE), :],
                                     ys_hbm.at[rows_at(dst_rows, tt, p), :], sem_out.at[sl])

    def start_in(tt, sl):
        for p in range(ppt):
            copy_in(tt, sl, p).start()

    def wait_out(tt, sl):
        for p in range(ppt):
            copy_out(tt, sl, p).wait()

    @pl.when(jnp.logical_and(t == 0, nt > 0))
    def _():
        start_in(0, 0)

    @pl.when(t + 1 < nt)
    def _():
        start_in(t + 1, 1 - slot)

    @pl.when(jnp.logical_and(t >= 2, t - 2 < nt))
    def _():
        wait_out(t - 2, slot)

    @pl.when(t < nt)
    def _():
        @pl.when(jnp.logical_or(t == 0, tile_e[t] != tile_e[jnp.maximum(t - 1, 0)]))
        def _():
            w1b[...] = w1_ref[...].astype(BF16)
            w3b[...] = w3_ref[...].astype(BF16)
            w2b[...] = w2_ref[...].astype(BF16)

        for p in range(ppt):
            copy_in(t, slot, p).wait()
        x = tbuf[slot]
        h = _silu(jnp.dot(x, w1b[...], preferred_element_type=F32)) * jnp.dot(x, w3b[...], preferred_element_type=F32)
        ybuf[slot] = jnp.dot(h.astype(BF16), w2b[...], preferred_element_type=F32).astype(ybuf.dtype)
        for p in range(ppt):
            copy_out(t, slot, p).start()

    @pl.when(t == last)
    def _():
        @pl.when(jnp.logical_and(last >= 1, last - 1 < nt))
        def _():
            wait_out(last - 1, 1 - slot)

        @pl.when(last < nt)
        def _():
            wait_out(last, slot)


def _moe_unsort_kernel(ys_ref, g_ref, x_ref, g2_ref, *rest, final):
    if final:
        fg_ref, o_ref = rest
    else:
        (o_ref,) = rest
    nb, n_exp = g_ref.shape
    nbpad = ys_ref.shape[0]
    gates = g_ref[...]
    sel = gates > 0.0
    sel_b = jnp.where(sel, 1.0, 0.0).astype(BF16)
    earlier = lax.broadcasted_iota(jnp.int32, (nb, nb), 1) < lax.broadcasted_iota(jnp.int32, (nb, nb), 0)
    rank = jnp.dot(jnp.where(earlier, 1.0, 0.0).astype(BF16), sel_b, preferred_element_type=F32)
    total = jnp.sum(jnp.where(sel, 1.0, 0.0), axis=0, keepdims=True)
    padded = jnp.broadcast_to(_moe_slots_padded(total), (8, n_exp))
    below = lax.broadcasted_iota(jnp.int32, (n_exp, n_exp), 0) < lax.broadcasted_iota(jnp.int32, (n_exp, n_exp), 1)
    offs = jnp.dot(padded.astype(BF16), jnp.where(below, 1.0, 0.0).astype(BF16), preferred_element_type=F32)[0:1]
    dest = offs + rank
    d_lo = jnp.min(jnp.where(sel, dest, float(nbpad)), axis=1, keepdims=True)
    d_hi = jnp.max(jnp.where(sel, dest, -1.0), axis=1, keepdims=True)
    g_lo = jnp.sum(jnp.where(jnp.logical_and(sel, dest == d_lo), gates, 0.0), axis=1, keepdims=True)
    g_hi = jnp.sum(jnp.where(jnp.logical_and(sel, dest == d_hi), gates, 0.0), axis=1, keepdims=True)
    slot = lax.broadcasted_iota(jnp.int32, (nb, nbpad), 1)
    ys = ys_ref[...]

    def pick(dcol):
        onehot = jnp.where(slot == dcol.astype(jnp.int32), 1.0, 0.0).astype(BF16)
        return jnp.dot(onehot, ys, preferred_element_type=F32)

    moe = g_lo * pick(d_lo) + g_hi * pick(d_hi)
    half = nb // g2_ref.shape[0]
    d = x_ref.shape[1]
    for j in range(g2_ref.shape[0]):
        r = slice(j * half, (j + 1) * half)
        xn = x_ref[r, :] + g2_ref[j] * moe[r]
        if final:
            xn = _rms(xn, fg_ref[...])
            for c in range(half // SSD_CHUNK):
                col = (j * half) // SSD_CHUNK + c
                o_ref[:, col * d:(col + 1) * d] = xn[c * SSD_CHUNK:(c + 1) * SSD_CHUNK]
        else:
            o_ref[r, :] = xn


def _moe_schedule(counts, nbpad, n_tiles):
    nblk, n_exp = counts.shape
    ppt = MOE_TILE // MOE_PIECE
    pc = counts // MOE_PIECE
    loc = jnp.cumsum(pc, axis=1) - pc
    cum_b = jnp.cumsum(pc, axis=0)
    np_e = cum_b[-1]
    tiles_e = (np_e + ppt - 1) // ppt
    tile_end = jnp.cumsum(tiles_e)
    ntiles = tile_end[-1]
    t_idx = jnp.arange(n_tiles, dtype=jnp.int32)
    tile_e = jnp.minimum(jnp.sum((tile_end[None, :] <= t_idx[:, None]).astype(jnp.int32), axis=1), n_exp - 1)
    first = (tile_end - tiles_e)[tile_e]
    piece0 = (t_idx - first) * ppt
    npieces = jnp.where(t_idx < ntiles, jnp.clip(np_e[tile_e] - piece0, 0, ppt), 0)
    i = piece0[:, None] + jnp.arange(ppt, dtype=jnp.int32)[None, :]
    cum_t = cum_b.T[tile_e]
    blk = jnp.minimum(jnp.sum((cum_t[:, None, :] <= i[:, :, None]).astype(jnp.int32), axis=2), nblk - 1)
    before = jnp.take_along_axis(cum_t - pc.T[tile_e], blk, axis=1)
    within = i - before + jnp.take_along_axis(loc.T[tile_e], blk, axis=1)
    rows = blk * nbpad + within * MOE_PIECE
    real = jnp.arange(ppt)[None, :] < npieces[:, None]
    spare = nblk * nbpad
    piece = jnp.arange(ppt, dtype=jnp.int32)[None, :]
    src = jnp.where(real, rows, spare + piece * MOE_PIECE)
    dst = jnp.where(real, rows, spare + (ppt + (t_idx[:, None] % 2) * ppt + piece) * MOE_PIECE)
    return (tile_e.astype(jnp.int32), src.reshape(-1).astype(jnp.int32), dst.reshape(-1).astype(jnp.int32),
            ntiles.reshape(1).astype(jnp.int32))


def _moe(t, gates_t, xres, g2rows, w1, w3, w2, layer, *, final_g=None, blocks_per_batch=None):
    n, d = t.shape
    _, n_exp, _, f = w1.shape
    nb = MOE_BLOCK
    nblk = n // nb
    nbpad = TOP_K * nb + n_exp * MOE_PIECE
    final = final_g is not None

    ppt = MOE_TILE // MOE_PIECE
    assert 3 * ppt * MOE_PIECE <= nbpad
    last_blk = lambda j: jnp.minimum(j, nblk - 1)
    ts, cnt = pl.pallas_call(
        _moe_sort_kernel,
        grid=(nblk + 1,),
        in_specs=[pl.BlockSpec((n_exp, nb), lambda j: (0, last_blk(j))), pl.BlockSpec((nb, d), lambda j: (last_blk(j), 0))],
        out_specs=[pl.BlockSpec((nbpad, d), lambda j: (j, 0)), pl.BlockSpec((None, n_exp, 128), lambda j: (j, 0, 0))],
        out_shape=[jax.ShapeDtypeStruct(((nblk + 1) * nbpad, d), BF16), jax.ShapeDtypeStruct((nblk + 1, n_exp, 128), F32)],
        compiler_params=_cparams("parallel"),
        name="moe_sort",
    )(gates_t, t)

    n_tiles = nblk * nbpad // MOE_TILE + n_exp
    tile_e, src_rows, dst_rows, ntiles = _moe_schedule(cnt[:nblk, :, 0].astype(jnp.int32), nbpad, n_tiles)
    wspec = lambda shape: pl.BlockSpec((None, None) + shape, lambda i, te, sr, dr, nt: (layer, te[i], 0, 0))
    ys = pl.pallas_call(
        _moe_expert_kernel,
        grid_spec=pltpu.PrefetchScalarGridSpec(
            num_scalar_prefetch=4,
            grid=(n_tiles,),
            in_specs=[pl.BlockSpec(memory_space=pl.ANY), wspec((d, f)), wspec((d, f)), wspec((f, d))],
            out_specs=pl.BlockSpec(memory_space=pl.ANY),
            scratch_shapes=[pltpu.VMEM((2, MOE_TILE, d), BF16), pltpu.VMEM((2, MOE_TILE, d), BF16),
                            pltpu.VMEM((d, f), BF16), pltpu.VMEM((d, f), BF16), pltpu.VMEM((f, d), BF16),
                            pltpu.SemaphoreType.DMA((2,)), pltpu.SemaphoreType.DMA((2,))],
        ),
        out_shape=jax.ShapeDtypeStruct(((nblk + 1) * nbpad, d), BF16),
        input_output_aliases={4: 0},
        compiler_params=_cparams("arbitrary"),
        name="moe_experts",
    )(tile_e, src_rows, dst_rows, ntiles, ts, w1, w3, w2)

    halves = nb // TOKEN_TILE
    in_specs = [
        pl.BlockSpec((nbpad, d), lambda j: (j, 0)),
        pl.BlockSpec((nb, n_exp), lambda j: (j, 0)),
        pl.BlockSpec((nb, d), lambda j: (j, 0)),
        pl.BlockSpec((halves, 1, d), lambda j: (j, 0, 0)),
    ]
    args = [ys, gates_t.T, xres, g2rows]
    if final:
        in_specs.append(pl.BlockSpec((1, d), lambda j: (0, 0)))
        args.append(final_g.reshape(1, d))
        cols = nb // SSD_CHUNK
        out_spec = pl.BlockSpec((None, SSD_CHUNK, cols * d), lambda j: (j // blocks_per_batch, 0, j % blocks_per_batch))
        out_shape = jax.ShapeDtypeStruct((nblk // blocks_per_batch, SSD_CHUNK, GRID_W * d), F32)
    else:
        out_spec = pl.BlockSpec((nb, d), lambda j: (j, 0))
        out_shape = jax.ShapeDtypeStruct((n, d), F32)
    return pl.pallas_call(
        functools.partial(_moe_unsort_kernel, final=final),
        grid=(nblk,),
        in_specs=in_specs,
        out_specs=out_spec,
        out_shape=out_shape,
        compiler_params=_cparams("parallel"),
        name="moe_unsort_final" if final else "moe_unsort",
    )(*args)


SSD_COLS_PER_STEP = 8


def _ssd_inproj_kernel(xl_ref, xc_ref, g_ref, mod_ref, wz_ref, wx_ref, wdt_ref, wdtt_ref, bias_ref, biast_ref,
                       a_ref, at_ref, cw_ref, cb_ref, z_ref, xbc_ref, csdt_ref, cst_ref, xp_ref, slab, xt, ext,
                       *, n_lat_steps, n_ctx_chunks):
    step = pl.program_id(1)
    i = pl.program_id(2)
    q = SSD_CHUNK
    ncols = SSD_COLS_PER_STEP
    d = xt.shape[1]
    is_lat = step < n_lat_steps
    c = step * ncols + i
    n_lat_chunks = n_lat_steps * ncols
    n_chunks = n_lat_chunks + n_ctx_chunks

    slot = c % 2
    other = 1 - slot

    def emit_conv():
        acc = jnp.zeros(xbc_ref.shape, F32) + cb_ref[...]
        w = cw_ref[...]
        for k in range(SSD_CONV):
            acc = acc + w[k:k + 1] * ext[slot, pl.ds(8 - SSD_CONV // 2 + k, q), :]
        xbc_ref[...] = _silu(acc).astype(xbc_ref.dtype)

    @pl.when(jnp.logical_and(i == 0, is_lat))
    def _():
        for s in range(d // 128):
            slab[...] = xl_ref[:, :, s * 128:(s + 1) * 128].reshape(q * ncols, 128)
            for w in range(ncols):
                xt[w * q:(w + 1) * q, s * 128:(s + 1) * 128] = slab[pl.ds(w, q, stride=ncols), :]

    @pl.when(jnp.logical_and(i == 0, jnp.logical_not(is_lat)))
    def _():
        xt[0:n_ctx_chunks * q, :] = xc_ref[...]

    @pl.when(c == 0)
    def _():
        ext[...] = jnp.zeros_like(ext)

    @pl.when(jnp.logical_or(is_lat, i < n_ctx_chunks))
    def _():
        emit_conv()
        m = mod_ref[...]
        x = xt[pl.ds(pl.multiple_of(i * q, q), q), :]
        xp_ref[...] = x
        hn = (_rms(x, g_ref[...]) * (1.0 + m[1:2]) + m[0:1]).astype(BF16)
        z_ref[...] = jnp.dot(hn, wz_ref[...], preferred_element_type=F32).astype(z_ref.dtype)
        xbc_new = jnp.dot(hn, wx_ref[...], preferred_element_type=F32)
        starts = jnp.logical_or(c == 0, c == n_lat_chunks)
        ext[other, 8 + q:16 + q, :] = jnp.where(starts, 0.0, xbc_new[0:8])
        ext[slot, 0:8, :] = jnp.where(starts, 0.0, ext[other, q:q + 8, :])
        ext[slot, 8:8 + q, :] = xbc_new

        def softplus(v):
            return jnp.maximum(v, 0.0) + jnp.log(1.0 + jnp.exp(-jnp.abs(v)))

        r_i = lax.broadcasted_iota(jnp.int32, (q, q), 0)
        c_i = lax.broadcasted_iota(jnp.int32, (q, q), 1)
        lower = (r_i >= c_i).astype(F32)
        upper = (r_i <= c_i).astype(F32)
        dt = softplus(jnp.dot(hn, wdt_ref[...], preferred_element_type=F32) + bias_ref[...])
        da = dt * a_ref[...]
        half = da.shape[1] // 2
        cs = jnp.concatenate([jnp.dot(lower, da[:, :half], preferred_element_type=F32, precision=HIGHEST),
                              jnp.dot(upper, da[:, half:], preferred_element_type=F32, precision=HIGHEST)], axis=1)
        lane = lax.broadcasted_iota(jnp.int32, dt.shape, 1)
        csdt_ref[...] = jnp.where(lane % 8 < 4, cs, dt)
        nh = at_ref.shape[0] // 2
        dtt = softplus(lax.dot_general(wdtt_ref[...], hn, (((1,), (1,)), ((), ())), preferred_element_type=F32)
                       + biast_ref[...])
        dat = dtt * at_ref[...]
        cst_f = jnp.dot(dat[:nh], upper, preferred_element_type=F32, precision=HIGHEST)
        cst_b = jnp.dot(dat[nh:], lower, preferred_element_type=F32, precision=HIGHEST)
        cst = jnp.concatenate([cst_f, cst_b], axis=0)
        for j in range(cst_ref.shape[0]):
            cst_ref[j] = cst[j * 4:(j + 1) * 4, :]

    @pl.when(jnp.logical_and(c >= n_chunks, c < n_chunks + 2))
    def _():
        @pl.when(c == n_chunks)
        def _():
            ext[other, 8 + q:16 + q, :] = jnp.zeros((8, ext.shape[2]), F32)

        emit_conv()


def _ssd_inproj(xall, n_lat, norm_g, mods, in_w, dt_bias, a_log, conv_w, conv_b):
    bsz, lt, d = xall.shape
    q = SSD_CHUNK
    ncols = SSD_COLS_PER_STEP
    ncl = n_lat // q
    nc = lt // q
    ncc = nc - ncl
    n_ctx = lt - n_lat
    nh2 = dt_bias.size
    d_inner = (nh2 // 2) * SSD_HEAD_DIM
    conv_ch = in_w.shape[1] - d_inner - nh2
    wz = in_w[:, :d_inner].astype(BF16)
    wx = in_w[:, d_inner:d_inner + conv_ch].astype(BF16)
    wdt = in_w[:, d_inner + conv_ch:].astype(BF16)
    a = -jnp.exp(a_log.astype(F32)).reshape(nh2)
    bias = dt_bias.astype(F32).reshape(nh2)
    ngr = nh2 // 4
    lanes = jnp.arange(2 * nh2)
    dup = (lanes // 8) * 4 + lanes % 4
    assert n_lat // GRID_W == q and GRID_W % ncols == 0 and lt % GRID_W == 0 and ncc + 2 <= ncols and n_lat % n_ctx == 0
    nls = ncl // ncols
    xgrid = xall.reshape(bsz, lt // GRID_W, GRID_W, d)
    kern = functools.partial(_ssd_inproj_kernel, n_lat_steps=nls, n_ctx_chunks=ncc)
    full = lambda s: pl.BlockSpec(s, lambda b, st, i: tuple(0 for _ in s))
    chunk = lambda b, st, i: jnp.minimum(st * ncols + i, nc - 1)
    rows = lambda width: pl.BlockSpec((None, q, width), lambda b, st, i: (b, chunk(b, st, i), 0))
    return pl.pallas_call(
        kern,
        grid=(bsz, nls + 1, ncols),
        in_specs=[
            pl.BlockSpec((None, q, ncols, d), lambda b, st, i: (b, 0, jnp.minimum(st, nls - 1), 0)),
            pl.BlockSpec((None, n_ctx, d), lambda b, st, i: (b, n_lat // n_ctx, 0)),
            full((1, d)),
            pl.BlockSpec((None, 6, d), lambda b, st, i: (b * 2 + (st >= nls).astype(jnp.int32), 0, 0)),
            full((d, d_inner)), full((d, conv_ch)), full((d, 2 * nh2)), full((nh2, d)),
            full((1, 2 * nh2)), full((nh2, 1)), full((1, 2 * nh2)), full((nh2, 1)),
            full((SSD_CONV, conv_ch)), full((1, conv_ch)),
        ],
        out_specs=[
            rows(d_inner),
            pl.BlockSpec((None, q, conv_ch), lambda b, st, i: (b, jnp.clip(st * ncols + i - 2, 0, nc - 1), 0)),
            rows(2 * nh2),
            pl.BlockSpec((None, None, ngr, 4, q), lambda b, st, i: (b, chunk(b, st, i), 0, 0, 0)),
            rows(d),
        ],
        out_shape=[
            jax.ShapeDtypeStruct((bsz, lt, d_inner), BF16),
            jax.ShapeDtypeStruct((bsz, lt, conv_ch), BF16),
            jax.ShapeDtypeStruct((bsz, lt, 2 * nh2), F32),
            jax.ShapeDtypeStruct((bsz, nc, ngr, 4, q), F32),
            jax.ShapeDtypeStruct((bsz, lt, d), F32),
        ],
        scratch_shapes=[pltpu.VMEM((q * ncols, 128), F32), pltpu.VMEM((q * ncols, d), F32),
                        pltpu.VMEM((2, q + 16, conv_ch), F32)],
        compiler_params=_cparams("parallel", "arbitrary", "arbitrary"),
        name="ssd_inproj",
    )(xgrid, xall, norm_g.reshape(1, d), mods, wz, wx, wdt[:, dup], wdt.T, bias[dup].reshape(1, -1),
      bias.reshape(nh2, 1), a[dup].reshape(1, -1), a.reshape(nh2, 1), conv_w.astype(F32),
      conv_b.astype(F32).reshape(1, conv_ch))


def _ssd_scan_dir(x, bm, cm, v, cst, state, spread, reverse):
    q = SSD_CHUNK
    hp = SSD_HEAD_DIM
    gw = x.shape[1]
    r = gw // hp
    v_hi = v.astype(BF16)
    v_r1 = v - v_hi.astype(F32)
    v_mid = v_r1.astype(BF16)
    v_lo = (v_r1 - v_mid.astype(F32)).astype(BF16)
    lhs = jnp.concatenate([v_hi, v_mid, v_lo], axis=1)
    spread_out = jnp.dot(lhs, spread, preferred_element_type=F32)
    cs_x = spread_out[:, :gw]
    dt_x = spread_out[:, gw:]
    end = 0 if reverse else q - 1
    cs_end = cs_x[end:end + 1, :]
    xdt = x * dt_x
    xw = (xdt * jnp.exp(cs_end - cs_x)).astype(BF16)
    xdt = xdt.astype(BF16)
    cb = lax.dot_general(cm, bm, (((1,), (1,)), ((), ())), preferred_element_type=F32)
    y_off = jnp.dot(cm, state.astype(BF16), preferred_element_type=F32) * jnp.exp(cs_x)
    r_i = lax.broadcasted_iota(jnp.int32, (q, q), 0)
    c_i = lax.broadcasted_iota(jnp.int32, (q, q), 1)
    mask = (r_i <= c_i) if reverse else (r_i >= c_i)
    lane = lax.broadcasted_iota(jnp.int32, (q, 2 * hp), 1)
    ys = []
    for pair in range(r // 2):
        xp = xdt[:, pair * 2 * hp:(pair + 1) * 2 * hp]
        yd = []
        for h in (2 * pair, 2 * pair + 1):
            seg = jnp.exp(jnp.minimum(v[:, h:h + 1] - cst[h:h + 1, :], 0.0))
            gmat = (cb * jnp.where(mask, seg, 0.0)).astype(BF16)
            yd.append(jnp.dot(gmat, xp, preferred_element_type=F32))
        ys.append(jnp.where(lane < hp, yd[0], yd[1]))
    upd = lax.dot_general(bm, xw, (((0,), (0,)), ((), ())), preferred_element_type=F32)
    return jnp.concatenate(ys, axis=1) + y_off, state * jnp.exp(cs_end) + upd


def _ssd_spread_matrix(r, gw, hp):
    wide = 2 * gw
    col = lax.broadcasted_iota(jnp.int32, (6 * r, wide), 0) % (2 * r)
    ln = lax.broadcasted_iota(jnp.int32, (6 * r, wide), 1)
    cs_hit = jnp.logical_and(ln < gw, ln // hp == col)
    dt_hit = jnp.logical_and(ln >= gw, (ln - gw) // hp == col - r)
    hit = jnp.logical_or(jnp.logical_and(col < r, cs_hit), jnp.logical_and(col >= r, dt_hit))
    return jnp.where(hit, 1.0, 0.0).astype(BF16)


def _ssd_scan_kernel(xf, bf, cf, csdtf, cstf, xb, bb, cb, csdtb, cstb, yf_ref, yb_ref, state_f, state_b):
    @pl.when(pl.program_id(1) == 0)
    def _():
        state_f[...] = jnp.zeros_like(state_f)
        state_b[...] = jnp.zeros_like(state_b)

    ngr, n, gw = state_f.shape
    r = gw // SSD_HEAD_DIM
    spread = _ssd_spread_matrix(r, gw, SSD_HEAD_DIM)
    dirs = ((xf, bf, cf, csdtf, cstf, yf_ref, state_f, False, 0), (xb, bb, cb, csdtb, cstb, yb_ref, state_b, True, ngr))
    for g in range(ngr):
        for x_ref, b_ref, c_ref, csdt_ref, cst_ref, y_ref, state, reverse, lane_group0 in dirs:
            j = lane_group0 + g
            y, new_state = _ssd_scan_dir(
                x_ref[:, g * gw:(g + 1) * gw].astype(F32), b_ref[:, g * n:(g + 1) * n], c_ref[:, g * n:(g + 1) * n],
                csdt_ref[:, 2 * r * j:2 * r * (j + 1)], cst_ref[g], state[g], spread, reverse)
            y_ref[:, g * gw:(g + 1) * gw] = y.astype(y_ref.dtype)
            state[g] = new_state


def _ssd_scan(xbc, csdt, cst, n_lat):
    bsz, lt, ch = xbc.shape
    q = SSD_CHUNK
    nc, ncl = lt // q, n_lat // q
    ngr = SSD_GROUPS
    n = SSD_STATE
    d_inner = ch - 2 * ngr * n
    gw = d_inner // ngr
    assert d_inner % (ngr * n) == 0

    def specs(chunk, direction):
        return [
            pl.BlockSpec((None, q, d_inner), lambda b, k: (b, chunk(k), 0)),
            pl.BlockSpec((None, q, ngr * n), lambda b, k: (b, chunk(k), d_inner // (ngr * n))),
            pl.BlockSpec((None, q, ngr * n), lambda b, k: (b, chunk(k), d_inner // (ngr * n) + 1)),
            pl.BlockSpec((None, q, csdt.shape[2]), lambda b, k: (b, chunk(k), 0)),
            pl.BlockSpec((None, None, ngr, 4, q), lambda b, k: (b, chunk(k), direction, 0, 0)),
        ]

    fwd = lambda k: (k + ncl) % nc
    bwd = lambda k: nc - 1 - k
    out = jax.ShapeDtypeStruct((bsz, lt, d_inner), BF16)
    return pl.pallas_call(
        _ssd_scan_kernel,
        grid=(bsz, nc),
        in_specs=specs(fwd, 0) + specs(bwd, 1),
        out_specs=[pl.BlockSpec((None, q, d_inner), lambda b, k: (b, fwd(k), 0)),
                   pl.BlockSpec((None, q, d_inner), lambda b, k: (b, bwd(k), 0))],
        out_shape=[out, out],
        scratch_shapes=[pltpu.VMEM((ngr, n, gw), F32), pltpu.VMEM((ngr, n, gw), F32)],
        compiler_params=_cparams("parallel", "arbitrary"),
        name="ssd_scan",
    )(xbc, xbc, xbc, csdt, cst, xbc, xbc, xbc, csdt, cst)


def _ssd_finish_kernel(yf_ref, yb_ref, xs_ref, z_ref, x_ref, mod_ref, dsk_ref, ng_ref, w_ref, g2_ref,
                       rw_ref, rb_ref, x3_ref, hn2_ref, gates_ref):
    m = mod_ref[...]
    nsub = ROW_SUBTILES
    sub = x_ref.shape[0] // nsub
    for s in range(nsub):
        r = slice(s * sub, (s + 1) * sub)
        y = yf_ref[r, :].astype(F32) + yb_ref[r, :].astype(F32) + dsk_ref[...] * xs_ref[r, :].astype(F32)
        gated = y * _silu(z_ref[r, :].astype(F32))
        nrm = _rms(gated, ng_ref[...])
        out = jnp.dot(nrm.astype(BF16), w_ref[...], preferred_element_type=F32)
        x3 = x_ref[r, :] + m[2:3] * out
        x3_ref[r, :] = x3
        hn2 = _rms(x3, g2_ref[...]) * (1.0 + m[4:5]) + m[3:4]
        hn2_ref[r, :] = hn2.astype(hn2_ref.dtype)
        gates_ref[:, r] = _router_gates(hn2, rw_ref, rb_ref)


def _ssd_finish(yf, yb, xbc, z, xall, n_lat, mods, d_skip, norm_g, out_w, norm2_g, router_wt, router_b):
    bsz, lt, d_inner = z.shape
    d = xall.shape[-1]
    q = TOKEN_TILE
    ncl = n_lat // q
    n_exp = router_wt.shape[0]
    inner = pl.BlockSpec((None, q, d_inner), lambda b, c: (b, c, 0))
    tok = pl.BlockSpec((None, q, d), lambda b, c: (b, c, 0))
    full = lambda s: pl.BlockSpec(s, lambda b, c: tuple(0 for _ in s))
    dsk = jnp.repeat(d_skip.astype(F32), SSD_HEAD_DIM).reshape(1, d_inner)
    return pl.pallas_call(
        _ssd_finish_kernel,
        grid=(bsz, ncl),
        in_specs=[
            inner, inner, inner, inner,
            tok,
            pl.BlockSpec((None, 6, d), lambda b, c: (b * 2, 0, 0)),
            full((1, d_inner)), full((1, d_inner)), full((d_inner, d)), full((1, d)),
            full((n_exp, d)), full((n_exp, 1)),
        ],
        out_specs=[tok, tok, pl.BlockSpec((n_exp, q), lambda b, c: (0, b * ncl + c))],
        out_shape=[
            jax.ShapeDtypeStruct((bsz, n_lat, d), F32),
            jax.ShapeDtypeStruct((bsz, n_lat, d), BF16),
            jax.ShapeDtypeStruct((n_exp, bsz * n_lat), F32),
        ],
        compiler_params=_cparams("parallel", "parallel"),
        name="ssd_finish",
    )(yf, yb, xbc, z, xall, mods, dsk, norm_g.reshape(1, d_inner), out_w.astype(BF16),
      norm2_g.reshape(1, d), router_wt, router_b.reshape(n_exp, 1))


def kernel(x, c, ctx, c_ctx, mod_w, mod_b, norm1_g, norm2_g, final_g, s5_lam_re, s5_lam_im, s5_log_dt, s5_b_re, s5_b_im, s5_c_re, s5_c_im, s5_d, s5_glu_w, s5_glu_b, ssd_in_w, ssd_conv_w, ssd_conv_b, ssd_dt_bias, ssd_a_log, ssd_d, ssd_norm_g, ssd_out_w, router_w, router_b, moe_w1, moe_w3, moe_w2):
    bsz, n_lat, d = x.shape
    n_ctx = ctx.shape[1]
    lt = n_lat + n_ctx
    n_exp = router_w.shape[1]
    assert n_lat % TOKEN_TILE == 0 and n_ctx % TOKEN_TILE == 0
    assert (bsz * lt) % MOE_BLOCK == 0 and n_lat % MOE_BLOCK == 0 and MOE_BLOCK % TOKEN_TILE == 0
    assert TOKEN_TILE % SSD_CHUNK == 0

    mods = _modulation(c, c_ctx, mod_w, mod_b)
    router_wt = router_w.T.astype(F32)
    w1, w3, w2 = moe_w1, moe_w3, moe_w2
    nlt = n_lat // TOKEN_TILE
    tpb = lt // TOKEN_TILE

    xall = jnp.concatenate([x, ctx], axis=1)
    hn = _prenorm(xall, norm1_g[0], mods[0], nlt)
    s5w = _s5_weights(s5_lam_re[0], s5_lam_im[0], s5_log_dt[0], s5_b_re[0], s5_b_im[0], s5_c_re[0], s5_c_im[0])
    y = _s5_scan(hn, n_lat, s5w)
    x1, hn2, gates_t = _glu_head(y, hn, xall, mods[0], s5_d[0], s5_glu_w[0], s5_glu_b[0], norm2_g[0],
                                 router_wt, router_b, nlt)
    g2_lat = jnp.broadcast_to(mods[0][0::2, None, 5], (bsz, nlt, d))
    g2_ctx = jnp.broadcast_to(mods[0][1::2, None, 5], (bsz, tpb - nlt, d))
    g2rows = jnp.concatenate([g2_lat, g2_ctx], axis=1).reshape(bsz * tpb, 1, d)
    x2 = _moe(hn2.reshape(bsz * lt, d), gates_t, x1.reshape(bsz * lt, d), g2rows, w1, w3, w2, 0).reshape(bsz, lt, d)

    z, xbc, csdt, cst, x2p = _ssd_inproj(x2, n_lat, norm1_g[1], mods[1], ssd_in_w[0], ssd_dt_bias[0], ssd_a_log[0],
                                             ssd_conv_w[0], ssd_conv_b[0])
    yf, yb = _ssd_scan(xbc, csdt, cst, n_lat)
    x3, hn3, gates3_t = _ssd_finish(yf, yb, xbc, z, x2p, n_lat, mods[1], ssd_d[0], ssd_norm_g[0], ssd_out_w[0],
                                    norm2_g[1], router_wt, router_b)
    g2rows = jnp.broadcast_to(mods[1][0::2, None, 5], (bsz, nlt, d)).reshape(bsz * nlt, 1, d)
    out = _moe(hn3.reshape(bsz * n_lat, d), gates3_t, x3.reshape(bsz * n_lat, d), g2rows, w1, w3, w2, 1,
               final_g=final_g, blocks_per_batch=n_lat // MOE_BLOCK)
    return out.reshape(bsz, n_lat, d)
```

```python
import functools

import jax
import jax.numpy as jnp
from jax import lax
from jax.experimental import pallas as pl
from jax.experimental.pallas import tpu as pltpu

F32 = jnp.float32
BF16 = jnp.bfloat16
HIGHEST = lax.Precision.HIGHEST

GRID_W = 64
RMS_EPS = 1e-6

S5_GROUP = 16
S5_STATE = 64
S5_T = 16
S5_GB = 8

SSD_HEAD_DIM = 64
SSD_GROUPS = 8
SSD_STATE = 128
SSD_CONV = 5
SSD_CHUNK = 128

N_EXPERT_GROUPS = 4
TOP_K = 2

TOKEN_TILE = 256
ROW_SUBTILES = 2
MOE_BLOCK = 512
MOE_PIECE = 16
MOE_TILE = 256
VMEM_LIMIT_BYTES = 56 * 1024 * 1024


def _cparams(*sem):
    return pltpu.CompilerParams(dimension_semantics=sem, vmem_limit_bytes=VMEM_LIMIT_BYTES)


def _sigmoid(v):
    return 1.0 / (1.0 + jnp.exp(-v))


def _silu(v):
    return v * _sigmoid(v)


def _gelu_tanh(v):
    return 0.5 * v * (1.0 + jnp.tanh(0.7978845608028654 * (v + 0.044715 * (v * v * v))))


def _rms(v, g):
    return v * lax.rsqrt(jnp.mean(v * v, axis=-1, keepdims=True) + RMS_EPS) * g


def _mod_kernel(cc_ref, w_ref, b_ref, o_ref):
    a = _silu(cc_ref[...])
    o_ref[...] = jnp.dot(a, w_ref[...], preferred_element_type=F32, precision=HIGHEST) + b_ref[...]


def _modulation(c, c_ctx, mod_w, mod_b):
    depth, d, d6 = mod_w.shape
    bsz = c.shape[0]
    rows = 8
    cc = jnp.zeros((rows, d), F32).at[:bsz].set(c).at[bsz].set(c_ctx)
    tn = d6 // 4
    out = pl.pallas_call(
        _mod_kernel,
        grid=(depth, d6 // tn),
        in_specs=[
            pl.BlockSpec((rows, d), lambda i, j: (0, 0)),
            pl.BlockSpec((None, d, tn), lambda i, j: (i, 0, j)),
            pl.BlockSpec((None, 1, tn), lambda i, j: (i, 0, j)),
        ],
        out_specs=pl.BlockSpec((None, rows, tn), lambda i, j: (i, 0, j)),
        out_shape=jax.ShapeDtypeStruct((depth, rows, d6), F32),
        compiler_params=_cparams("parallel", "parallel"),
        name="modulation",
    )(cc, mod_w, mod_b.reshape(depth, 1, d6))
    lat = out[:, :bsz].reshape(depth, bsz, 1, 6, d)
    ctx = jnp.broadcast_to(out[:, bsz].reshape(depth, 1, 1, 6, d), (depth, bsz, 1, 6, d))
    return jnp.concatenate([lat, ctx], axis=2).reshape(depth, bsz * 2, 6, d)


def _prenorm_kernel(x_ref, c_ref, g_ref, mod_ref, o_ref, *, n_lat_tiles):
    m = mod_ref[...]
    x = jnp.where(pl.program_id(1) < n_lat_tiles, x_ref[...], c_ref[...])
    hn = _rms(x, g_ref[...]) * (1.0 + m[1:2]) + m[0:1]
    o_ref[...] = hn.astype(o_ref.dtype)


def _lat_ctx_specs(n_lat_tiles, d):
    return [pl.BlockSpec((None, TOKEN_TILE, d), lambda b, i: (b, jnp.minimum(i, n_lat_tiles - 1), 0)),
            pl.BlockSpec((None, TOKEN_TILE, d), lambda b, i: (b, jnp.maximum(i - n_lat_tiles, 0), 0))]


def _prenorm(x, ctx, g, mods, n_lat_tiles):
    bsz, n_lat, d = x.shape
    lt = n_lat + ctx.shape[1]
    nt = lt // TOKEN_TILE
    return pl.pallas_call(
        functools.partial(_prenorm_kernel, n_lat_tiles=n_lat_tiles),
        grid=(bsz, nt),
        in_specs=_lat_ctx_specs(n_lat_tiles, d) + [
            pl.BlockSpec((1, d), lambda b, i: (0, 0)),
            pl.BlockSpec((None, 6, d), lambda b, i: (b * 2 + (i >= n_lat_tiles).astype(jnp.int32), 0, 0)),
        ],
        out_specs=pl.BlockSpec((None, TOKEN_TILE, d), lambda b, i: (b, i, 0)),
        out_shape=jax.ShapeDtypeStruct((bsz, lt, d), F32),
        compiler_params=_cparams("parallel", "parallel"),
        name="prenorm",
    )(x, ctx, g.reshape(1, d), mods)


def _s5_weights(lam_re, lam_im, log_dt, b_re, b_im, c_re, c_im):
    t = S5_T
    k16 = b_re.shape[-1]

    def cmul(ar, ai, br, bi):
        return ar * br - ai * bi, ar * bi + ai * br

    def direction(k):
        lr, li = lam_re[k], lam_im[k]
        step = jnp.exp(log_dt[k])[:, None]
        mag = jnp.exp(lr * step)
        abar_r = mag * jnp.cos(li * step)
        abar_i = mag * jnp.sin(li * step)
        den = lr * lr + li * li
        q_r = ((abar_r - 1.0) * lr + abar_i * li) / den
        q_i = (abar_i * lr - (abar_r - 1.0) * li) / den
        bb_r, bb_i = cmul(q_r[..., None], q_i[..., None], b_re, b_im)

        def power(tau):
            tau = jnp.asarray(tau, F32)[None, :, None]
            m = jnp.exp((lr * step)[:, None, :] * tau)
            return m * jnp.cos((li * step)[:, None, :] * tau), m * jnp.sin((li * step)[:, None, :] * tau)

        return bb_r.transpose(0, 2, 1), bb_i.transpose(0, 2, 1), power

    rows = lambda v: jnp.repeat(v, k16, axis=1)
    row_tile = lambda v: jnp.tile(v, (1, t, 1))
    cols = lambda v: jnp.repeat(v.transpose(0, 2, 1), k16, axis=2)
    col_tile = lambda v: jnp.tile(v, (1, 1, t))
    ct_r, ct_i = c_re.transpose(0, 2, 1), c_im.transpose(0, 2, 1)
    steps = jnp.arange(t)

    def left(bt_r, bt_i, power, tau):
        pr, pi = power(tau)
        return cmul(row_tile(bt_r), row_tile(bt_i), rows(pr), rows(pi))

    def right(power, tau):
        pr, pi = power(tau)
        return cmul(col_tile(ct_r), col_tile(ct_i), cols(pr), cols(pi))

    bf_r, bf_i, pow_f = direction(0)
    bb_r, bb_i, pow_b = direction(1)
    lf_r, lf_i = left(bf_r, bf_i, pow_f, -steps)
    rf_r, rf_i = right(pow_f, steps)
    lb_r, lb_i = left(bb_r, bb_i, pow_b, steps)
    rb_r, rb_i = right(pow_b, -steps)
    lf = jnp.concatenate([lf_r, lf_i], axis=-1)
    lb = jnp.concatenate([lb_r, lb_i], axis=-1)
    rf = jnp.concatenate([rf_r, -rf_i], axis=1)
    rb = jnp.concatenate([rb_r, -rb_i], axis=1)

    sf_r, sf_i = left(bf_r, bf_i, pow_f, t - 1 - steps)
    ws = jnp.concatenate([sf_r, lb_r, sf_i, lb_i], axis=-1)

    of_r, of_i = right(pow_f, steps + 1)
    ob_r, ob_i = right(pow_b, t - steps)
    zero = jnp.zeros_like(of_r)
    w2 = jnp.concatenate([of_r, zero, -of_i, zero, zero, ob_r, zero, -ob_i], axis=1)

    af_r, af_i = pow_f([t])
    ab_r, ab_i = pow_b([t])
    ar = jnp.concatenate([af_r[:, 0], ab_r[:, 0]], axis=-1)
    ai = jnp.concatenate([af_i[:, 0], ab_i[:, 0]], axis=-1)
    return ws.astype(BF16), lf, rf, lb, rb, w2.astype(BF16), ar, ai


def _s5_row_block(nc):
    return max(rb for rb in range(16, min(nc, 176) + 1, 16) if nc % rb == 0)


def _dot_split3(a, b):
    a_hi = a.astype(BF16)
    a_lo = (a - a_hi.astype(F32)).astype(BF16)
    b_hi = b.astype(BF16)
    b_lo = (b - b_hi.astype(F32)).astype(BF16)
    dot = functools.partial(jnp.dot, preferred_element_type=F32)
    return dot(a_hi, b_hi) + dot(a_hi, b_lo) + dot(a_lo, b_hi)


def _s5_kernel(hn_ref, ws_ref, lf_ref, rf_ref, lb_ref, rb_ref, w2_ref, ar_ref, ai_ref, yo_ref,
               u_ref, y_ref, wm_ref, sre, sim, hre_f, him_f, hre_b, him_b, *, n_chunks, n_ctx_chunks, pitch):
    nc, ncc = n_chunks, n_ctx_chunks
    ncl = nc - ncc
    p = S5_STATE
    t_len = S5_T
    gl = S5_GROUP
    per_half = 128 // gl
    rb = _s5_row_block(nc)
    lane_slot = lax.broadcasted_iota(jnp.int32, (rb, 128), 1) // gl

    @pl.when(pl.program_id(1) == 0)
    def _():
        tk = t_len * gl
        src_tok = lax.broadcasted_iota(jnp.int32, (tk, tk), 0) // gl
        dst_tok = lax.broadcasted_iota(jnp.int32, (tk, tk), 1) // gl
        for g in range(S5_GB):
            causal = _dot_split3(lf_ref[g], rf_ref[g])
            anti = _dot_split3(lb_ref[g], rb_ref[g])
            wm = jnp.where(dst_tok >= src_tok, causal, 0.0) + jnp.where(src_tok >= dst_tok, anti, 0.0)
            wm_ref[g] = wm.astype(wm_ref.dtype)

    def slot_transpose(xs):
        xs = list(xs)
        bit = per_half // 2
        while bit >= 1:
            upper = (lane_slot // bit) % 2 == 1
            nxt = list(xs)
            for p in range(per_half):
                if p & bit:
                    continue
                lo, hi = xs[p], xs[p + bit]
                nxt[p] = jnp.where(upper, pltpu.roll(hi, bit * gl, axis=1), lo)
                nxt[p + bit] = jnp.where(upper, hi, pltpu.roll(lo, 128 - bit * gl, axis=1))
            xs = nxt
            bit //= 2
        return xs

    def gather_u(blk, carry):
        r0 = pl.multiple_of(blk * rb, 16)
        halves = []
        for hb in range(t_len // per_half):
            a = [hn_ref[pl.ds(r0 * t_len + hb * per_half + j, rb, stride=t_len), :] for j in range(per_half)]
            halves.append(slot_transpose(a))
        for i in range(S5_GB):
            u_ref[i, pl.ds(r0, rb), :] = jnp.concatenate([h[i] for h in halves], axis=1).astype(u_ref.dtype)
        return carry

    lax.fori_loop(0, nc // rb, gather_u, 0)

    for g in range(S5_GB):
        s = jnp.dot(u_ref[g], ws_ref[g], preferred_element_type=F32)
        sre[pl.ds(g * pitch, nc), :] = s[:, : 2 * p]
        sim[pl.ds(g * pitch, nc), :] = s[:, 2 * p:]

    ar = ar_ref[...]
    ai = ai_ref[...]
    fwd_lane = lax.broadcasted_iota(jnp.int32, (S5_GB, 2 * p), 1) < p

    def step(k, carry):
        h_r, h_i = carry
        cf = jnp.where(k < ncc, ncl + k, k - ncc)
        cb = nc - 1 - k
        rows_f = pl.ds(cf, S5_GB, stride=pitch)
        rows_b = pl.ds(cb, S5_GB, stride=pitch)
        hre_f[rows_f, :] = h_r
        him_f[rows_f, :] = h_i
        hre_b[rows_b, :] = h_r
        him_b[rows_b, :] = h_i
        s_r = jnp.where(fwd_lane, sre[rows_f, :], sre[rows_b, :])
        s_i = jnp.where(fwd_lane, sim[rows_f, :], sim[rows_b, :])
        n_r = ar * h_r - ai * h_i + s_r
        n_i = ar * h_i + ai * h_r + s_i
        return n_r, n_i

    zero = jnp.zeros((S5_GB, 2 * p), F32)
    lax.fori_loop(0, nc, step, (zero, zero))

    for g in range(S5_GB):
        rows = pl.ds(g * pitch, nc)
        hin = jnp.concatenate([hre_f[rows, :], him_f[rows, :], hre_b[rows, :], him_b[rows, :]], axis=1)
        out = jnp.dot(u_ref[g], wm_ref[g], preferred_element_type=F32)
        out = out + jnp.dot(hin.astype(BF16), w2_ref[g], preferred_element_type=F32)
        y_ref[g] = out

    def scatter_y(blk, carry):
        r0 = pl.multiple_of(blk * rb, 16)
        for hb in range(t_len // per_half):
            yv = [y_ref[i, pl.ds(r0, rb), hb * 128:(hb + 1) * 128] for i in range(S5_GB)]
            for j, tok in enumerate(slot_transpose(yv)):
                yo_ref[pl.ds(r0 * t_len + hb * per_half + j, rb, stride=t_len), :] = tok
        return carry

    lax.fori_loop(0, nc // rb, scatter_y, 0)


def _s5_scan(hn, n_lat, weights):
    bsz, lt, d = hn.shape
    ngrp = d // S5_GROUP
    t = S5_T
    tk = t * S5_GROUP
    nc = lt // t
    ncc = (lt - n_lat) // t
    gb = S5_GB
    assert gb * S5_GROUP == 128 and (128 // S5_GROUP) == gb and t % gb == 0
    ws, lf, rf, lb, rb, w2, ar, ai = weights
    p2 = 2 * S5_STATE
    pitch = nc + 8 if (nc // 8) % 2 == 0 else nc
    kern = functools.partial(_s5_kernel, n_chunks=nc, n_ctx_chunks=ncc, pitch=pitch)
    per_group = lambda *s: pl.BlockSpec((gb,) + s, lambda gi, b: (gi,) + tuple(0 for _ in s))
    return pl.pallas_call(
        kern,
        grid=(ngrp // gb, bsz),
        in_specs=[
            pl.BlockSpec((None, lt, 128), lambda gi, b: (b, 0, gi)),
            per_group(tk, 2 * p2),
            per_group(tk, p2), per_group(p2, tk), per_group(tk, p2), per_group(p2, tk),
            per_group(4 * p2, tk),
            per_group(p2), per_group(p2),
        ],
        out_specs=pl.BlockSpec((None, lt, 128), lambda gi, b: (b, 0, gi)),
        out_shape=jax.ShapeDtypeStruct((bsz, lt, d), F32),
        scratch_shapes=[pltpu.VMEM((gb, nc, tk), BF16), pltpu.VMEM((gb, nc, tk), F32), pltpu.VMEM((gb, tk, tk), BF16)]
        + [pltpu.VMEM((gb * pitch, p2), F32) for _ in range(6)],
        compiler_params=_cparams("parallel", "arbitrary"),
        name="s5_scan",
    )(hn, ws, lf, rf, lb, rb, w2, ar, ai)


def _router_gates(hn2, rw_ref, rb_ref):
    n_exp = rw_ref.shape[0]
    epg = n_exp // N_EXPERT_GROUPS
    logits = lax.dot_general(rw_ref[...], hn2, (((1,), (1,)), ((), ())),
                             preferred_element_type=F32, precision=HIGHEST)
    s = _sigmoid(logits)
    sel = s + rb_ref[...]
    row = [sel[e:e + 1] for e in range(n_exp)]
    gscore = []
    for gi in range(N_EXPERT_GROUPS):
        a, b, c, dd = row[gi * epg: gi * epg + epg]
        hi1, lo1 = jnp.maximum(a, b), jnp.minimum(a, b)
        hi2, lo2 = jnp.maximum(c, dd), jnp.minimum(c, dd)
        gscore.append(jnp.maximum(hi1, hi2) + jnp.maximum(jnp.minimum(hi1, hi2), jnp.maximum(lo1, lo2)))
    gmax = functools.reduce(jnp.maximum, gscore)
    gates = []
    taken = None
    for gi in range(N_EXPERT_GROUPS):
        is_max = gscore[gi] == gmax
        best = is_max if taken is None else jnp.logical_and(is_max, jnp.logical_not(taken))
        taken = is_max if taken is None else jnp.logical_or(taken, is_max)
        for e in range(gi * epg, gi * epg + epg):
            rank = jnp.zeros_like(row[e])
            for j in range(gi * epg, gi * epg + epg):
                if j == e:
                    continue
                ahead = (row[j] >= row[e]) if j < e else (row[j] > row[e])
                rank = rank + ahead.astype(F32)
            chosen = jnp.logical_and(best, rank < float(TOP_K))
            gates.append(jnp.where(chosen, s[e:e + 1], 0.0))
    g = jnp.concatenate(gates, axis=0)
    return g / jnp.sum(g, axis=0, keepdims=True)


def _glu_kernel(y_ref, u_ref, x_ref, c_ref, mod_ref, d_ref, w_ref, b_ref, g2_ref, rw_ref, rb_ref,
                x1_ref, hn2_ref, gates_ref, *, n_lat_tiles):
    d = x_ref.shape[-1]
    is_lat = pl.program_id(1) < n_lat_tiles
    m = mod_ref[...]
    nsub = ROW_SUBTILES
    sub = x_ref.shape[0] // nsub
    for s in range(nsub):
        r = slice(s * sub, (s + 1) * sub)
        u = u_ref[r, :].astype(F32)
        a = _gelu_tanh(y_ref[r, :].astype(F32) + d_ref[...] * u)
        z = jnp.dot(a.astype(BF16), w_ref[...], preferred_element_type=F32) + b_ref[...]
        out = z[:, :d] * _sigmoid(z[:, d:])
        x1 = jnp.where(is_lat, x_ref[r, :], c_ref[r, :]) + m[2:3] * out
        x1_ref[r, :] = x1
        hn2 = _rms(x1, g2_ref[...]) * (1.0 + m[4:5]) + m[3:4]
        hn2_ref[r, :] = hn2.astype(hn2_ref.dtype)
        gates_ref[:, r] = _router_gates(hn2, rw_ref, rb_ref)


def _glu_head(y, hn, x, ctx, mods, d_skip, glu_w, glu_b, norm2_g, router_wt, router_b, n_lat_tiles):
    bsz, lt, d = hn.shape
    nt = lt // TOKEN_TILE
    n_exp = router_wt.shape[0]
    tok = pl.BlockSpec((None, TOKEN_TILE, d), lambda b, i: (b, i, 0))
    vec = lambda n: pl.BlockSpec((1, n), lambda b, i: (0, 0))
    return pl.pallas_call(
        functools.partial(_glu_kernel, n_lat_tiles=n_lat_tiles),
        grid=(bsz, nt),
        in_specs=[tok, tok] + _lat_ctx_specs(n_lat_tiles, d) + [
            pl.BlockSpec((None, 6, d), lambda b, i: (b * 2 + (i >= n_lat_tiles).astype(jnp.int32), 0, 0)),
            vec(d),
            pl.BlockSpec((d, 2 * d), lambda b, i: (0, 0)),
            vec(2 * d),
            vec(d),
            pl.BlockSpec((n_exp, d), lambda b, i: (0, 0)),
            pl.BlockSpec((n_exp, 1), lambda b, i: (0, 0)),
        ],
        out_specs=[
            tok, tok,
            pl.BlockSpec((n_exp, TOKEN_TILE), lambda b, i: (0, b * nt + i)),
        ],
        out_shape=[
            jax.ShapeDtypeStruct((bsz, lt, d), F32),
            jax.ShapeDtypeStruct((bsz, lt, d), BF16),
            jax.ShapeDtypeStruct((n_exp, bsz * lt), F32),
        ],
        compiler_params=_cparams("parallel", "parallel"),
        name="s5_glu_head",
    )(y, hn, x, ctx, mods, d_skip.reshape(1, d), glu_w.astype(BF16), glu_b.reshape(1, 2 * d),
      norm2_g.reshape(1, d), router_wt, router_b.reshape(n_exp, 1))


def _moe_slots_padded(total):
    return jnp.floor((total + float(MOE_PIECE - 1)) * (1.0 / MOE_PIECE)) * float(MOE_PIECE)


def _moe_sort_kernel(gt_ref, t_ref, ts_ref, cnt_ref):
    n_exp, nb = gt_ref.shape
    nbpad = ts_ref.shape[0]
    sel = jnp.logical_and(gt_ref[...] > 0.0, pl.program_id(0) < pl.num_programs(0) - 1)
    sel_b = jnp.where(sel, 1.0, 0.0).astype(BF16)
    earlier = lax.broadcasted_iota(jnp.int32, (nb, nb), 0) < lax.broadcasted_iota(jnp.int32, (nb, nb), 1)
    rank = jnp.dot(sel_b, jnp.where(earlier, 1.0, 0.0).astype(BF16), preferred_element_type=F32)
    total = jnp.sum(jnp.where(sel, 1.0, 0.0), axis=1, keepdims=True)
    padded = jnp.broadcast_to(_moe_slots_padded(total), (n_exp, 128))
    below = lax.broadcasted_iota(jnp.int32, (n_exp, n_exp), 1) < lax.broadcasted_iota(jnp.int32, (n_exp, n_exp), 0)
    offs = jnp.dot(jnp.where(below, 1.0, 0.0).astype(BF16), padded.astype(BF16), preferred_element_type=F32)[:, 0:1]
    dest = offs + rank
    d_lo = jnp.min(jnp.where(sel, dest, float(nbpad)), axis=0, keepdims=True).astype(jnp.int32)
    d_hi = jnp.max(jnp.where(sel, dest, -1.0), axis=0, keepdims=True).astype(jnp.int32)
    slot = lax.broadcasted_iota(jnp.int32, (nbpad, nb), 0)
    perm = jnp.where(jnp.logical_or(slot == d_lo, slot == d_hi), 1.0, 0.0).astype(BF16)
    ts_ref[...] = jnp.dot(perm, t_ref[...], preferred_element_type=F32).astype(ts_ref.dtype)
    cnt_ref[...] = padded


def _moe_expert_kernel(tile_e, src_rows, dst_rows, ntiles, ts_hbm, w1_ref, w3_ref, w2_ref, ys_hbm,
                       tbuf, ybuf, w1b, w3b, w2b, sem_in, sem_out):
    t = pl.program_id(0)
    nt = ntiles[0]
    last = pl.num_programs(0) - 1
    ppt = MOE_TILE // MOE_PIECE
    slot = t % 2

    def rows_at(table, tt, p):
        return pl.ds(pl.multiple_of(table[tt * ppt + p], MOE_PIECE), MOE_PIECE)

    def copy_in(tt, sl, p):
        return pltpu.make_async_copy(ts_hbm.at[rows_at(src_rows, tt, p), :],
                                     tbuf.at[sl, pl.ds(p * MOE_PIECE, MOE_PIECE), :], sem_in.at[sl])

    def copy_out(tt, sl, p):
        return pltpu.make_async_copy(ybuf.at[sl, pl.ds(p * MOE_PIECE, MOE_PIECE), :],
                                     ys_hbm.at[rows_at(dst_rows, tt, p), :], sem_out.at[sl])

    def start_in(tt, sl):
        for p in range(ppt):
            copy_in(tt, sl, p).start()

    def wait_out(tt, sl):
        for p in range(ppt):
            copy_out(tt, sl, p).wait()

    @pl.when(jnp.logical_and(t == 0, nt > 0))
    def _():
        start_in(0, 0)

    @pl.when(t + 1 < nt)
    def _():
        start_in(t + 1, 1 - slot)

    @pl.when(jnp.logical_and(t >= 2, t - 2 < nt))
    def _():
        wait_out(t - 2, slot)

    @pl.when(t < nt)
    def _():
        @pl.when(jnp.logical_or(t == 0, tile_e[t] != tile_e[jnp.maximum(t - 1, 0)]))
        def _():
            w1b[...] = w1_ref[...].astype(BF16)
            w3b[...] = w3_ref[...].astype(BF16)
            w2b[...] = w2_ref[...].astype(BF16)

        for p in range(ppt):
            copy_in(t, slot, p).wait()
        x = tbuf[slot]
        h = _silu(jnp.dot(x, w1b[...], preferred_element_type=F32)) * jnp.dot(x, w3b[...], preferred_element_type=F32)
        ybuf[slot] = jnp.dot(h.astype(BF16), w2b[...], preferred_element_type=F32).astype(ybuf.dtype)
        for p in range(ppt):
            copy_out(t, slot, p).start()

    @pl.when(t == last)
    def _():
        @pl.when(jnp.logical_and(last >= 1, last - 1 < nt))
        def _():
            wait_out(last - 1, 1 - slot)

        @pl.when(last < nt)
        def _():
            wait_out(last, slot)


def _moe_unsort_kernel(ys_ref, g_ref, x_ref, g2_ref, *rest, final):
    if final:
        fg_ref, o_ref = rest
    else:
        (o_ref,) = rest
    nb, n_exp = g_ref.shape
    nbpad = ys_ref.shape[0]
    gates = g_ref[...]
    sel = gates > 0.0
    sel_b = jnp.where(sel, 1.0, 0.0).astype(BF16)
    earlier = lax.broadcasted_iota(jnp.int32, (nb, nb), 1) < lax.broadcasted_iota(jnp.int32, (nb, nb), 0)
    rank = jnp.dot(jnp.where(earlier, 1.0, 0.0).astype(BF16), sel_b, preferred_element_type=F32)
    total = jnp.sum(jnp.where(sel, 1.0, 0.0), axis=0, keepdims=True)
    padded = jnp.broadcast_to(_moe_slots_padded(total), (8, n_exp))
    below = lax.broadcasted_iota(jnp.int32, (n_exp, n_exp), 0) < lax.broadcasted_iota(jnp.int32, (n_exp, n_exp), 1)
    offs = jnp.dot(padded.astype(BF16), jnp.where(below, 1.0, 0.0).astype(BF16), preferred_element_type=F32)[0:1]
    dest = offs + rank
    d_lo = jnp.min(jnp.where(sel, dest, float(nbpad)), axis=1, keepdims=True)
    d_hi = jnp.max(jnp.where(sel, dest, -1.0), axis=1, keepdims=True)
    g_lo = jnp.sum(jnp.where(jnp.logical_and(sel, dest == d_lo), gates, 0.0), axis=1, keepdims=True)
    g_hi = jnp.sum(jnp.where(jnp.logical_and(sel, dest == d_hi), gates, 0.0), axis=1, keepdims=True)
    slot = lax.broadcasted_iota(jnp.int32, (nb, nbpad), 1)
    ys = ys_ref[...]

    def pick(dcol):
        onehot = jnp.where(slot == dcol.astype(jnp.int32), 1.0, 0.0).astype(BF16)
        return jnp.dot(onehot, ys, preferred_element_type=F32)

    moe = g_lo * pick(d_lo) + g_hi * pick(d_hi)
    half = nb // g2_ref.shape[0]
    d = x_ref.shape[1]
    for j in range(g2_ref.shape[0]):
        r = slice(j * half, (j + 1) * half)
        xn = x_ref[r, :] + g2_ref[j] * moe[r]
        if final:
            xn = _rms(xn, fg_ref[...])
            for c in range(half // SSD_CHUNK):
                col = (j * half) // SSD_CHUNK + c
                o_ref[:, col * d:(col + 1) * d] = xn[c * SSD_CHUNK:(c + 1) * SSD_CHUNK]
        else:
            o_ref[r, :] = xn


def _moe_schedule(counts, nbpad, n_tiles):
    nblk, n_exp = counts.shape
    ppt = MOE_TILE // MOE_PIECE
    pc = counts // MOE_PIECE
    loc = jnp.cumsum(pc, axis=1) - pc
    cum_b = jnp.cumsum(pc, axis=0)
    np_e = cum_b[-1]
    tiles_e = (np_e + ppt - 1) // ppt
    tile_end = jnp.cumsum(tiles_e)
    ntiles = tile_end[-1]
    t_idx = jnp.arange(n_tiles, dtype=jnp.int32)
    tile_e = jnp.minimum(jnp.sum((tile_end[None, :] <= t_idx[:, None]).astype(jnp.int32), axis=1), n_exp - 1)
    first = (tile_end - tiles_e)[tile_e]
    piece0 = (t_idx - first) * ppt
    npieces = jnp.where(t_idx < ntiles, jnp.clip(np_e[tile_e] - piece0, 0, ppt), 0)
    i = piece0[:, None] + jnp.arange(ppt, dtype=jnp.int32)[None, :]
    cum_t = cum_b.T[tile_e]
    blk = jnp.minimum(jnp.sum((cum_t[:, None, :] <= i[:, :, None]).astype(jnp.int32), axis=2), nblk - 1)
    before = jnp.take_along_axis(cum_t - pc.T[tile_e], blk, axis=1)
    within = i - before + jnp.take_along_axis(loc.T[tile_e], blk, axis=1)
    rows = blk * nbpad + within * MOE_PIECE
    real = jnp.arange(ppt)[None, :] < npieces[:, None]
    spare = nblk * nbpad
    piece = jnp.arange(ppt, dtype=jnp.int32)[None, :]
    src = jnp.where(real, rows, spare + piece * MOE_PIECE)
    dst = jnp.where(real, rows, spare + (ppt + (t_idx[:, None] % 2) * ppt + piece) * MOE_PIECE)
    return (tile_e.astype(jnp.int32), src.reshape(-1).astype(jnp.int32), dst.reshape(-1).astype(jnp.int32),
            ntiles.reshape(1).astype(jnp.int32))


def _moe(t, gates_t, xres, g2rows, w1, w3, w2, layer, *, final_g=None, blocks_per_batch=None):
    n, d = t.shape
    _, n_exp, _, f = w1.shape
    nb = MOE_BLOCK
    nblk = n // nb
    nbpad = TOP_K * nb + n_exp * MOE_PIECE
    final = final_g is not None

    ppt = MOE_TILE // MOE_PIECE
    assert 3 * ppt * MOE_PIECE <= nbpad
    last_blk = lambda j: jnp.minimum(j, nblk - 1)
    ts, cnt = pl.pallas_call(
        _moe_sort_kernel,
        grid=(nblk + 1,),
        in_specs=[pl.BlockSpec((n_exp, nb), lambda j: (0, last_blk(j))), pl.BlockSpec((nb, d), lambda j: (last_blk(j), 0))],
        out_specs=[pl.BlockSpec((nbpad, d), lambda j: (j, 0)), pl.BlockSpec((None, n_exp, 128), lambda j: (j, 0, 0))],
        out_shape=[jax.ShapeDtypeStruct(((nblk + 1) * nbpad, d), BF16), jax.ShapeDtypeStruct((nblk + 1, n_exp, 128), F32)],
        compiler_params=_cparams("parallel"),
        name="moe_sort",
    )(gates_t, t)

    n_tiles = nblk * nbpad // MOE_TILE + n_exp
    tile_e, src_rows, dst_rows, ntiles = _moe_schedule(cnt[:nblk, :, 0].astype(jnp.int32), nbpad, n_tiles)
    wspec = lambda shape: pl.BlockSpec((None, None) + shape, lambda i, te, sr, dr, nt: (layer, te[i], 0, 0))
    ys = pl.pallas_call(
        _moe_expert_kernel,
        grid_spec=pltpu.PrefetchScalarGridSpec(
            num_scalar_prefetch=4,
            grid=(n_tiles,),
            in_specs=[pl.BlockSpec(memory_space=pl.ANY), wspec((d, f)), wspec((d, f)), wspec((f, d))],
            out_specs=pl.BlockSpec(memory_space=pl.ANY),
            scratch_shapes=[pltpu.VMEM((2, MOE_TILE, d), BF16), pltpu.VMEM((2, MOE_TILE, d), BF16),
                            pltpu.VMEM((d, f), BF16), pltpu.VMEM((d, f), BF16), pltpu.VMEM((f, d), BF16),
                            pltpu.SemaphoreType.DMA((2,)), pltpu.SemaphoreType.DMA((2,))],
        ),
        out_shape=jax.ShapeDtypeStruct(((nblk + 1) * nbpad, d), BF16),
        input_output_aliases={4: 0},
        compiler_params=_cparams("arbitrary"),
        name="moe_experts",
    )(tile_e, src_rows, dst_rows, ntiles, ts, w1, w3, w2)

    halves = nb // TOKEN_TILE
    in_specs = [
        pl.BlockSpec((nbpad, d), lambda j: (j, 0)),
        pl.BlockSpec((nb, n_exp), lambda j: (j, 0)),
        pl.BlockSpec((nb, d), lambda j: (j, 0)),
        pl.BlockSpec((halves, 1, d), lambda j: (j, 0, 0)),
    ]
    args = [ys, gates_t.T, xres, g2rows]
    if final:
        in_specs.append(pl.BlockSpec((1, d), lambda j: (0, 0)))
        args.append(final_g.reshape(1, d))
        cols = nb // SSD_CHUNK
        out_spec = pl.BlockSpec((None, SSD_CHUNK, cols * d), lambda j: (j // blocks_per_batch, 0, j % blocks_per_batch))
        out_shape = jax.ShapeDtypeStruct((nblk // blocks_per_batch, SSD_CHUNK, GRID_W * d), F32)
    else:
        out_spec = pl.BlockSpec((nb, d), lambda j: (j, 0))
        out_shape = jax.ShapeDtypeStruct((n, d), F32)
    return pl.pallas_call(
        functools.partial(_moe_unsort_kernel, final=final),
        grid=(nblk,),
        in_specs=in_specs,
        out_specs=out_spec,
        out_shape=out_shape,
        compiler_params=_cparams("parallel"),
        name="moe_unsort_final" if final else "moe_unsort",
    )(*args)


SSD_COLS_PER_STEP = 8


def _ssd_inproj_kernel(xl_ref, xc_ref, g_ref, mod_ref, wz_ref, wx_ref, wdt_ref, wdtt_ref, bias_ref, biast_ref,
                       a_ref, at_ref, cw_ref, cb_ref, z_ref, xbc_ref, csdt_ref, cst_ref, xp_ref, slab, xt, ext, slabs, slabs_in,
                       *, n_lat_steps, n_ctx_chunks):
    step = pl.program_id(1)
    i = pl.program_id(2)
    q = SSD_CHUNK
    ncols = SSD_COLS_PER_STEP
    d = xt.shape[1]
    is_lat = step < n_lat_steps
    c = step * ncols + i
    n_lat_chunks = n_lat_steps * ncols
    n_chunks = n_lat_chunks + n_ctx_chunks

    slot = c % 2
    other = 1 - slot
    halo = SSD_CONV // 2
    nsl = ext.shape[2]
    pitch = slabs.shape[0] // nsl

    def emit_conv():
        w = [cw_ref[k] for k in range(SSD_CONV)]
        bias = cb_ref[...]

        def token(t, carry):
            acc = bias
            for k in range(SSD_CONV):
                acc = acc + w[k] * ext[slot, t + k]
            y = _silu(acc)
            for a in range(nsl // 8):
                slabs[pl.ds(a * 8 * pitch + t, 8, stride=pitch), :] = y[a * 8:(a + 1) * 8]
            return carry

        for t in range(q):
            token(t, 0)
        for s in range(nsl):
            xbc_ref[:, s * 128:(s + 1) * 128] = slabs[s * pitch:s * pitch + q, :].astype(xbc_ref.dtype)

    @pl.when(jnp.logical_and(i == 0, is_lat))
    def _():
        for s in range(d // 128):
            slab[...] = xl_ref[:, :, s * 128:(s + 1) * 128].reshape(q * ncols, 128)
            for w in range(ncols):
                xt[w * q:(w + 1) * q, s * 128:(s + 1) * 128] = slab[pl.ds(w, q, stride=ncols), :]

    @pl.when(jnp.logical_and(i == 0, jnp.logical_not(is_lat)))
    def _():
        xt[0:n_ctx_chunks * q, :] = xc_ref[...]

    @pl.when(c == 0)
    def _():
        ext[...] = jnp.zeros_like(ext)

    @pl.when(jnp.logical_or(is_lat, i < n_ctx_chunks))
    def _():
        emit_conv()
        m = mod_ref[...]
        x = xt[pl.ds(pl.multiple_of(i * q, q), q), :]
        xp_ref[...] = x
        hn = (_rms(x, g_ref[...]) * (1.0 + m[1:2]) + m[0:1]).astype(BF16)
        z_ref[...] = jnp.dot(hn, wz_ref[...], preferred_element_type=F32).astype(z_ref.dtype)
        xbc_new = jnp.dot(hn, wx_ref[...], preferred_element_type=F32)
        starts = jnp.logical_or(c == 0, c == n_lat_chunks)
        for s in range(nsl):
            slabs_in[s * pitch:s * pitch + q, :] = xbc_new[:, s * 128:(s + 1) * 128]

        def fill(t, carry):
            for a in range(nsl // 8):
                ext[slot, halo + t, a * 8:(a + 1) * 8, :] = slabs_in[pl.ds(a * 8 * pitch + t, 8, stride=pitch), :]
            return carry

        for t in range(q):
            fill(t, 0)
        ext[other, halo + q:2 * halo + q] = jnp.where(starts, 0.0, ext[slot, halo:2 * halo])
        ext[slot, 0:halo] = jnp.where(starts, 0.0, ext[other, q:q + halo])

        def softplus(v):
            return jnp.maximum(v, 0.0) + jnp.log(1.0 + jnp.exp(-jnp.abs(v)))

        r_i = lax.broadcasted_iota(jnp.int32, (q, q), 0)
        c_i = lax.broadcasted_iota(jnp.int32, (q, q), 1)
        lower = (r_i >= c_i).astype(F32)
        upper = (r_i <= c_i).astype(F32)
        dt = softplus(jnp.dot(hn, wdt_ref[...], preferred_element_type=F32) + bias_ref[...])
        da = dt * a_ref[...]
        half = da.shape[1] // 2
        cs = jnp.concatenate([jnp.dot(lower, da[:, :half], preferred_element_type=F32, precision=HIGHEST),
                              jnp.dot(upper, da[:, half:], preferred_element_type=F32, precision=HIGHEST)], axis=1)
        lane = lax.broadcasted_iota(jnp.int32, dt.shape, 1)
        csdt_ref[...] = jnp.where(lane % 8 < 4, cs, dt)
        nh = at_ref.shape[0] // 2
        dtt = softplus(lax.dot_general(wdtt_ref[...], hn, (((1,), (1,)), ((), ())), preferred_element_type=F32)
                       + biast_ref[...])
        dat = dtt * at_ref[...]
        cst_f = jnp.dot(dat[:nh], upper, preferred_element_type=F32, precision=HIGHEST)
        cst_b = jnp.dot(dat[nh:], lower, preferred_element_type=F32, precision=HIGHEST)
        cst = jnp.concatenate([cst_f, cst_b], axis=0)
        for j in range(cst_ref.shape[0]):
            cst_ref[j] = cst[j * 4:(j + 1) * 4, :]

    @pl.when(jnp.logical_and(c >= n_chunks, c < n_chunks + 2))
    def _():
        @pl.when(c == n_chunks)
        def _():
            ext[other, halo + q:2 * halo + q] = jnp.zeros((halo,) + ext.shape[2:], F32)

        emit_conv()


def _ssd_inproj(xall, n_lat, norm_g, mods, in_w, dt_bias, a_log, conv_w, conv_b):
    bsz, lt, d = xall.shape
    q = SSD_CHUNK
    ncols = SSD_COLS_PER_STEP
    ncl = n_lat // q
    nc = lt // q
    ncc = nc - ncl
    n_ctx = lt - n_lat
    nh2 = dt_bias.size
    d_inner = (nh2 // 2) * SSD_HEAD_DIM
    conv_ch = in_w.shape[1] - d_inner - nh2
    wz = in_w[:, :d_inner].astype(BF16)
    wx = in_w[:, d_inner:d_inner + conv_ch].astype(BF16)
    wdt = in_w[:, d_inner + conv_ch:].astype(BF16)
    a = -jnp.exp(a_log.astype(F32)).reshape(nh2)
    bias = dt_bias.astype(F32).reshape(nh2)
    ngr = nh2 // 4
    lanes = jnp.arange(2 * nh2)
    dup = (lanes // 8) * 4 + lanes % 4
    assert n_lat // GRID_W == q and GRID_W % ncols == 0 and lt % GRID_W == 0 and ncc + 2 <= ncols and n_lat % n_ctx == 0
    nls = ncl // ncols
    xgrid = xall.reshape(bsz, lt // GRID_W, GRID_W, d)
    kern = functools.partial(_ssd_inproj_kernel, n_lat_steps=nls, n_ctx_chunks=ncc)
    full = lambda s: pl.BlockSpec(s, lambda b, st, i: tuple(0 for _ in s))
    chunk = lambda b, st, i: jnp.minimum(st * ncols + i, nc - 1)
    rows = lambda width: pl.BlockSpec((None, q, width), lambda b, st, i: (b, chunk(b, st, i), 0))
    return pl.pallas_call(
        kern,
        grid=(bsz, nls + 1, ncols),
        in_specs=[
            pl.BlockSpec((None, q, ncols, d), lambda b, st, i: (b, 0, jnp.minimum(st, nls - 1), 0)),
            pl.BlockSpec((None, n_ctx, d), lambda b, st, i: (b, n_lat // n_ctx, 0)),
            full((1, d)),
            pl.BlockSpec((None, 6, d), lambda b, st, i: (b * 2 + (st >= nls).astype(jnp.int32), 0, 0)),
            full((d, d_inner)), full((d, conv_ch)), full((d, 2 * nh2)), full((nh2, d)),
            full((1, 2 * nh2)), full((nh2, 1)), full((1, 2 * nh2)), full((nh2, 1)),
            full((SSD_CONV, conv_ch // 128, 128)), full((conv_ch // 128, 128)),
        ],
        out_specs=[
            rows(d_inner),
            pl.BlockSpec((None, q, conv_ch), lambda b, st, i: (b, jnp.clip(st * ncols + i - 2, 0, nc - 1), 0)),
            rows(2 * nh2),
            pl.BlockSpec((None, None, ngr, 4, q), lambda b, st, i: (b, chunk(b, st, i), 0, 0, 0)),
            rows(d),
        ],
        out_shape=[
            jax.ShapeDtypeStruct((bsz, lt, d_inner), BF16),
            jax.ShapeDtypeStruct((bsz, lt, conv_ch), BF16),
            jax.ShapeDtypeStruct((bsz, lt, 2 * nh2), F32),
            jax.ShapeDtypeStruct((bsz, nc, ngr, 4, q), F32),
            jax.ShapeDtypeStruct((bsz, lt, d), F32),
        ],
        scratch_shapes=[pltpu.VMEM((q * ncols, 128), F32), pltpu.VMEM((q * ncols, d), F32),
                        pltpu.VMEM((2, q + 2 * (SSD_CONV // 2), conv_ch // 128, 128), F32),
                        pltpu.VMEM((conv_ch // 128 * (q + 8), 128), F32),
                        pltpu.VMEM((conv_ch // 128 * (q + 8), 128), F32)],
        compiler_params=_cparams("parallel", "arbitrary", "arbitrary"),
        name="ssd_inproj",
    )(xgrid, xall, norm_g.reshape(1, d), mods, wz, wx, wdt[:, dup], wdt.T, bias[dup].reshape(1, -1),
      bias.reshape(nh2, 1), a[dup].reshape(1, -1), a.reshape(nh2, 1), conv_w.astype(F32).reshape(SSD_CONV, conv_ch // 128, 128),
      conv_b.astype(F32).reshape(conv_ch // 128, 128))


def _ssd_scan_dir(x, bm, cm, v, cst, state, reverse):
    q = SSD_CHUNK
    hp = SSD_HEAD_DIM
    gw = x.shape[1]
    r = gw // hp
    lane = lax.broadcasted_iota(jnp.int32, (q, 2 * hp), 1)
    first = lane < hp
    bc = [jnp.broadcast_to(v[:, j:j + 1], (q, 2 * hp)) for j in range(2 * r)]

    def head_lanes(cols):
        return jnp.concatenate([jnp.where(first, cols[2 * p], cols[2 * p + 1]) for p in range(r // 2)], axis=1)

    cs_x = head_lanes(bc[:r])
    dt_x = head_lanes(bc[r:])
    end = 0 if reverse else q - 1
    cs_end = cs_x[end:end + 1, :]
    xdt = x * dt_x
    xw = (xdt * jnp.exp(cs_end - cs_x)).astype(BF16)
    cb = lax.dot_general(cm, bm, (((1,), (1,)), ((), ())), preferred_element_type=F32)
    y_off = jnp.dot(cm, state.astype(BF16), preferred_element_type=F32) * jnp.exp(cs_x)
    r_i = lax.broadcasted_iota(jnp.int32, (q, q), 0)
    c_i = lax.broadcasted_iota(jnp.int32, (q, q), 1)
    mask = (r_i <= c_i) if reverse else (r_i >= c_i)
    ys = []
    for pair in range(r // 2):
        xp = xdt[:, pair * 2 * hp:(pair + 1) * 2 * hp]
        rhs = jnp.concatenate([jnp.where(first, xp, 0.0), jnp.where(first, 0.0, xp)], axis=0).astype(BF16)
        gmats = []
        for h in (2 * pair, 2 * pair + 1):
            seg = jnp.exp(jnp.minimum(bc[h] - cst[h:h + 1, :], 0.0))
            gmats.append((cb * jnp.where(mask, seg, 0.0)).astype(BF16))
        ys.append(jnp.dot(jnp.concatenate(gmats, axis=1), rhs, preferred_element_type=F32))
    upd = lax.dot_general(bm, xw, (((0,), (0,)), ((), ())), preferred_element_type=F32)
    return jnp.concatenate(ys, axis=1) + y_off, state * jnp.exp(cs_end) + upd


def _ssd_scan_kernel(xf, bf, cf, csdtf, cstf, xb, bb, cb, csdtb, cstb, yf_ref, yb_ref, state_f, state_b):
    @pl.when(pl.program_id(1) == 0)
    def _():
        state_f[...] = jnp.zeros_like(state_f)
        state_b[...] = jnp.zeros_like(state_b)

    ngr, n, gw = state_f.shape
    r = gw // SSD_HEAD_DIM
    dirs = ((xf, bf, cf, csdtf, cstf, yf_ref, state_f, False, 0), (xb, bb, cb, csdtb, cstb, yb_ref, state_b, True, ngr))
    for g in range(ngr):
        for x_ref, b_ref, c_ref, csdt_ref, cst_ref, y_ref, state, reverse, lane_group0 in dirs:
            j = lane_group0 + g
            y, new_state = _ssd_scan_dir(
                x_ref[:, g * gw:(g + 1) * gw].astype(F32), b_ref[:, g * n:(g + 1) * n], c_ref[:, g * n:(g + 1) * n],
                csdt_ref[:, 2 * r * j:2 * r * (j + 1)], cst_ref[g], state[g], reverse)
            y_ref[:, g * gw:(g + 1) * gw] = y.astype(y_ref.dtype)
            state[g] = new_state


def _ssd_scan(xbc, csdt, cst, n_lat):
    bsz, lt, ch = xbc.shape
    q = SSD_CHUNK
    nc, ncl = lt // q, n_lat // q
    ngr = SSD_GROUPS
    n = SSD_STATE
    d_inner = ch - 2 * ngr * n
    gw = d_inner // ngr
    assert d_inner % (ngr * n) == 0 and q == 2 * SSD_HEAD_DIM

    def specs(chunk, direction):
        return [
            pl.BlockSpec((None, q, d_inner), lambda b, k: (b, chunk(k), 0)),
            pl.BlockSpec((None, q, ngr * n), lambda b, k: (b, chunk(k), d_inner // (ngr * n))),
            pl.BlockSpec((None, q, ngr * n), lambda b, k: (b, chunk(k), d_inner // (ngr * n) + 1)),
            pl.BlockSpec((None, q, csdt.shape[2]), lambda b, k: (b, chunk(k), 0)),
            pl.BlockSpec((None, None, ngr, 4, q), lambda b, k: (b, chunk(k), direction, 0, 0)),
        ]

    fwd = lambda k: (k + ncl) % nc
    bwd = lambda k: nc - 1 - k
    out = jax.ShapeDtypeStruct((bsz, lt, d_inner), BF16)
    return pl.pallas_call(
        _ssd_scan_kernel,
        grid=(bsz, nc),
        in_specs=specs(fwd, 0) + specs(bwd, 1),
        out_specs=[pl.BlockSpec((None, q, d_inner), lambda b, k: (b, fwd(k), 0)),
                   pl.BlockSpec((None, q, d_inner), lambda b, k: (b, bwd(k), 0))],
        out_shape=[out, out],
        scratch_shapes=[pltpu.VMEM((ngr, n, gw), F32), pltpu.VMEM((ngr, n, gw), F32)],
        compiler_params=_cparams("parallel", "arbitrary"),
        name="ssd_scan",
    )(xbc, xbc, xbc, csdt, cst, xbc, xbc, xbc, csdt, cst)


def _ssd_finish_kernel(yf_ref, yb_ref, xs_ref, z_ref, x_ref, mod_ref, dsk_ref, ng_ref, w_ref, g2_ref,
                       rw_ref, rb_ref, x3_ref, hn2_ref, gates_ref):
    m = mod_ref[...]
    nsub = ROW_SUBTILES
    sub = x_ref.shape[0] // nsub
    for s in range(nsub):
        r = slice(s * sub, (s + 1) * sub)
        y = yf_ref[r, :].astype(F32) + yb_ref[r, :].astype(F32) + dsk_ref[...] * xs_ref[r, :].astype(F32)
        gated = y * _silu(z_ref[r, :].astype(F32))
        nrm = _rms(gated, ng_ref[...])
        out = jnp.dot(nrm.astype(BF16), w_ref[...], preferred_element_type=F32)
        x3 = x_ref[r, :] + m[2:3] * out
        x3_ref[r, :] = x3
        hn2 = _rms(x3, g2_ref[...]) * (1.0 + m[4:5]) + m[3:4]
        hn2_ref[r, :] = hn2.astype(hn2_ref.dtype)
        gates_ref[:, r] = _router_gates(hn2, rw_ref, rb_ref)


def _ssd_finish(yf, yb, xbc, z, xall, n_lat, mods, d_skip, norm_g, out_w, norm2_g, router_wt, router_b):
    bsz, lt, d_inner = z.shape
    d = xall.shape[-1]
    q = TOKEN_TILE
    ncl = n_lat // q
    n_exp = router_wt.shape[0]
    inner = pl.BlockSpec((None, q, d_inner), lambda b, c: (b, c, 0))
    tok = pl.BlockSpec((None, q, d), lambda b, c: (b, c, 0))
    full = lambda s: pl.BlockSpec(s, lambda b, c: tuple(0 for _ in s))
    dsk = jnp.repeat(d_skip.astype(F32), SSD_HEAD_DIM).reshape(1, d_inner)
    return pl.pallas_call(
        _ssd_finish_kernel,
        grid=(bsz, ncl),
        in_specs=[
            inner, inner, inner, inner,
            tok,
            pl.BlockSpec((None, 6, d), lambda b, c: (b * 2, 0, 0)),
            full((1, d_inner)), full((1, d_inner)), full((d_inner, d)), full((1, d)),
            full((n_exp, d)), full((n_exp, 1)),
        ],
        out_specs=[tok, tok, pl.BlockSpec((n_exp, q), lambda b, c: (0, b * ncl + c))],
        out_shape=[
            jax.ShapeDtypeStruct((bsz, n_lat, d), F32),
            jax.ShapeDtypeStruct((bsz, n_lat, d), BF16),
            jax.ShapeDtypeStruct((n_exp, bsz * n_lat), F32),
        ],
        compiler_params=_cparams("parallel", "parallel"),
        name="ssd_finish",
    )(yf, yb, xbc, z, xall, mods, dsk, norm_g.reshape(1, d_inner), out_w.astype(BF16),
      norm2_g.reshape(1, d), router_wt, router_b.reshape(n_exp, 1))


def kernel(x, c, ctx, c_ctx, mod_w, mod_b, norm1_g, norm2_g, final_g, s5_lam_re, s5_lam_im, s5_log_dt, s5_b_re, s5_b_im, s5_c_re, s5_c_im, s5_d, s5_glu_w, s5_glu_b, ssd_in_w, ssd_conv_w, ssd_conv_b, ssd_dt_bias, ssd_a_log, ssd_d, ssd_norm_g, ssd_out_w, router_w, router_b, moe_w1, moe_w3, moe_w2):
    bsz, n_lat, d = x.shape
    n_ctx = ctx.shape[1]
    lt = n_lat + n_ctx
    n_exp = router_w.shape[1]
    assert n_lat % TOKEN_TILE == 0 and n_ctx % TOKEN_TILE == 0
    assert (bsz * lt) % MOE_BLOCK == 0 and n_lat % MOE_BLOCK == 0 and MOE_BLOCK % TOKEN_TILE == 0
    assert TOKEN_TILE % SSD_CHUNK == 0

    mods = _modulation(c, c_ctx, mod_w, mod_b)
    router_wt = router_w.T.astype(F32)
    w1, w3, w2 = moe_w1, moe_w3, moe_w2
    nlt = n_lat // TOKEN_TILE
    tpb = lt // TOKEN_TILE

    hn = _prenorm(x, ctx, norm1_g[0], mods[0], nlt)
    s5w = _s5_weights(s5_lam_re[0], s5_lam_im[0], s5_log_dt[0], s5_b_re[0], s5_b_im[0], s5_c_re[0], s5_c_im[0])
    y = _s5_scan(hn, n_lat, s5w)
    x1, hn2, gates_t = _glu_head(y, hn, x, ctx, mods[0], s5_d[0], s5_glu_w[0], s5_glu_b[0], norm2_g[0],
                                 router_wt, router_b, nlt)
    g2_lat = jnp.broadcast_to(mods[0][0::2, None, 5], (bsz, nlt, d))
    g2_ctx = jnp.broadcast_to(mods[0][1::2, None, 5], (bsz, tpb - nlt, d))
    g2rows = jnp.concatenate([g2_lat, g2_ctx], axis=1).reshape(bsz * tpb, 1, d)
    x2 = _moe(hn2.reshape(bsz * lt, d), gates_t, x1.reshape(bsz * lt, d), g2rows, w1, w3, w2, 0).reshape(bsz, lt, d)

    z, xbc, csdt, cst, x2p = _ssd_inproj(x2, n_lat, norm1_g[1], mods[1], ssd_in_w[0], ssd_dt_bias[0], ssd_a_log[0],
                                             ssd_conv_w[0], ssd_conv_b[0])
    yf, yb = _ssd_scan(xbc, csdt, cst, n_lat)
    x3, hn3, gates3_t = _ssd_finish(yf, yb, xbc, z, x2p, n_lat, mods[1], ssd_d[0], ssd_norm_g[0], ssd_out_w[0],
                                    norm2_g[1], router_wt, router_b)
    g2rows = jnp.broadcast_to(mods[1][0::2, None, 5], (bsz, nlt, d)).reshape(bsz * nlt, 1, d)
    out = _moe(hn3.reshape(bsz * n_lat, d), gates3_t, x3.reshape(bsz * n_lat, d), g2rows, w1, w3, w2, 1,
               final_g=final_g, blocks_per_batch=n_lat // MOE_BLOCK)
    return out.reshape(bsz, n_lat, d)
```

```python
import functools

import jax
import jax.numpy as jnp
from jax import lax
from jax.experimental import pallas as pl
from jax.experimental.pallas import tpu as pltpu

F32 = jnp.float32
BF16 = jnp.bfloat16
HIGHEST = lax.Precision.HIGHEST

GRID_W = 64
RMS_EPS = 1e-6

S5_GROUP = 16
S5_STATE = 64
S5_T = 16
S5_GB = 8

SSD_HEAD_DIM = 64
SSD_GROUPS = 8
SSD_STATE = 128
SSD_CONV = 5
SSD_CHUNK = 128

N_EXPERT_GROUPS = 4
TOP_K = 2

TOKEN_TILE = 256
ROW_SUBTILES = 2
MOE_BLOCK = 512
MOE_PIECE = 16
MOE_TILE = 256
VMEM_LIMIT_BYTES = 56 * 1024 * 1024


def _cparams(*sem):
    return pltpu.CompilerParams(dimension_semantics=sem, vmem_limit_bytes=VMEM_LIMIT_BYTES)


def _sigmoid(v):
    return 1.0 / (1.0 + jnp.exp(-v))


def _silu(v):
    return v * _sigmoid(v)


def _gelu_tanh(v):
    return 0.5 * v * (1.0 + jnp.tanh(0.7978845608028654 * (v + 0.044715 * (v * v * v))))


def _rms(v, g):
    return v * lax.rsqrt(jnp.mean(v * v, axis=-1, keepdims=True) + RMS_EPS) * g


def _mod_kernel(cc_ref, w_ref, b_ref, o_ref):
    a = _silu(cc_ref[...])
    o_ref[...] = jnp.dot(a, w_ref[...], preferred_element_type=F32, precision=HIGHEST) + b_ref[...]


def _modulation(c, c_ctx, mod_w, mod_b):
    depth, d, d6 = mod_w.shape
    bsz = c.shape[0]
    rows = 8
    cc = jnp.zeros((rows, d), F32).at[:bsz].set(c).at[bsz].set(c_ctx)
    tn = d6 // 4
    out = pl.pallas_call(
        _mod_kernel,
        grid=(depth, d6 // tn),
        in_specs=[
            pl.BlockSpec((rows, d), lambda i, j: (0, 0)),
            pl.BlockSpec((None, d, tn), lambda i, j: (i, 0, j)),
            pl.BlockSpec((None, 1, tn), lambda i, j: (i, 0, j)),
        ],
        out_specs=pl.BlockSpec((None, rows, tn), lambda i, j: (i, 0, j)),
        out_shape=jax.ShapeDtypeStruct((depth, rows, d6), F32),
        compiler_params=_cparams("parallel", "parallel"),
        name="modulation",
    )(cc, mod_w, mod_b.reshape(depth, 1, d6))
    lat = out[:, :bsz].reshape(depth, bsz, 1, 6, d)
    ctx = jnp.broadcast_to(out[:, bsz].reshape(depth, 1, 1, 6, d), (depth, bsz, 1, 6, d))
    return jnp.concatenate([lat, ctx], axis=2).reshape(depth, bsz * 2, 6, d)


def _prenorm_kernel(x_ref, c_ref, g_ref, mod_ref, o_ref, *, n_lat_tiles):
    m = mod_ref[...]
    x = jnp.where(pl.program_id(1) < n_lat_tiles, x_ref[...], c_ref[...])
    hn = _rms(x, g_ref[...]) * (1.0 + m[1:2]) + m[0:1]
    o_ref[...] = hn.astype(o_ref.dtype)


def _lat_ctx_specs(n_lat_tiles, d):
    return [pl.BlockSpec((None, TOKEN_TILE, d), lambda b, i: (b, jnp.minimum(i, n_lat_tiles - 1), 0)),
            pl.BlockSpec((None, TOKEN_TILE, d), lambda b, i: (b, jnp.maximum(i - n_lat_tiles, 0), 0))]


def _prenorm(x, ctx, g, mods, n_lat_tiles):
    bsz, n_lat, d = x.shape
    lt = n_lat + ctx.shape[1]
    nt = lt // TOKEN_TILE
    return pl.pallas_call(
        functools.partial(_prenorm_kernel, n_lat_tiles=n_lat_tiles),
        grid=(bsz, nt),
        in_specs=_lat_ctx_specs(n_lat_tiles, d) + [
            pl.BlockSpec((1, d), lambda b, i: (0, 0)),
            pl.BlockSpec((None, 6, d), lambda b, i: (b * 2 + (i >= n_lat_tiles).astype(jnp.int32), 0, 0)),
        ],
        out_specs=pl.BlockSpec((None, TOKEN_TILE, d), lambda b, i: (b, i, 0)),
        out_shape=jax.ShapeDtypeStruct((bsz, lt, d), F32),
        compiler_params=_cparams("parallel", "parallel"),
        name="prenorm",
    )(x, ctx, g.reshape(1, d), mods)


def _s5_weights(lam_re, lam_im, log_dt, b_re, b_im, c_re, c_im):
    t = S5_T
    k16 = b_re.shape[-1]

    def cmul(ar, ai, br, bi):
        return ar * br - ai * bi, ar * bi + ai * br

    def direction(k):
        lr, li = lam_re[k], lam_im[k]
        step = jnp.exp(log_dt[k])[:, None]
        mag = jnp.exp(lr * step)
        abar_r = mag * jnp.cos(li * step)
        abar_i = mag * jnp.sin(li * step)
        den = lr * lr + li * li
        q_r = ((abar_r - 1.0) * lr + abar_i * li) / den
        q_i = (abar_i * lr - (abar_r - 1.0) * li) / den
        bb_r, bb_i = cmul(q_r[..., None], q_i[..., None], b_re, b_im)

        def power(tau):
            tau = jnp.asarray(tau, F32)[None, :, None]
            m = jnp.exp((lr * step)[:, None, :] * tau)
            return m * jnp.cos((li * step)[:, None, :] * tau), m * jnp.sin((li * step)[:, None, :] * tau)

        return bb_r.transpose(0, 2, 1), bb_i.transpose(0, 2, 1), power

    rows = lambda v: jnp.repeat(v, k16, axis=1)
    row_tile = lambda v: jnp.tile(v, (1, t, 1))
    cols = lambda v: jnp.repeat(v.transpose(0, 2, 1), k16, axis=2)
    col_tile = lambda v: jnp.tile(v, (1, 1, t))
    ct_r, ct_i = c_re.transpose(0, 2, 1), c_im.transpose(0, 2, 1)
    steps = jnp.arange(t)

    def left(bt_r, bt_i, power, tau):
        pr, pi = power(tau)
        return cmul(row_tile(bt_r), row_tile(bt_i), rows(pr), rows(pi))

    def right(power, tau):
        pr, pi = power(tau)
        return cmul(col_tile(ct_r), col_tile(ct_i), cols(pr), cols(pi))

    bf_r, bf_i, pow_f = direction(0)
    bb_r, bb_i, pow_b = direction(1)
    lf_r, lf_i = left(bf_r, bf_i, pow_f, -steps)
    rf_r, rf_i = right(pow_f, steps)
    lb_r, lb_i = left(bb_r, bb_i, pow_b, steps)
    rb_r, rb_i = right(pow_b, -steps)
    lf = jnp.concatenate([lf_r, lf_i], axis=-1)
    lb = jnp.concatenate([lb_r, lb_i], axis=-1)
    rf = jnp.concatenate([rf_r, -rf_i], axis=1)
    rb = jnp.concatenate([rb_r, -rb_i], axis=1)

    sf_r, sf_i = left(bf_r, bf_i, pow_f, t - 1 - steps)
    ws = jnp.concatenate([sf_r, lb_r, sf_i, lb_i], axis=-1)

    of_r, of_i = right(pow_f, steps + 1)
    ob_r, ob_i = right(pow_b, t - steps)
    zero = jnp.zeros_like(of_r)
    w2 = jnp.concatenate([of_r, zero, -of_i, zero, zero, ob_r, zero, -ob_i], axis=1)

    af_r, af_i = pow_f([t])
    ab_r, ab_i = pow_b([t])
    ar = jnp.concatenate([af_r[:, 0], ab_r[:, 0]], axis=-1)
    ai = jnp.concatenate([af_i[:, 0], ab_i[:, 0]], axis=-1)
    return ws.astype(BF16), lf, rf, lb, rb, w2.astype(BF16), ar, ai


def _s5_row_block(nc):
    return max(rb for rb in range(16, min(nc, 176) + 1, 16) if nc % rb == 0)


def _dot_split3(a, b):
    a_hi = a.astype(BF16)
    a_lo = (a - a_hi.astype(F32)).astype(BF16)
    b_hi = b.astype(BF16)
    b_lo = (b - b_hi.astype(F32)).astype(BF16)
    dot = functools.partial(jnp.dot, preferred_element_type=F32)
    return dot(a_hi, b_hi) + dot(a_hi, b_lo) + dot(a_lo, b_hi)


def _s5_kernel(hn_ref, ws_ref, lf_ref, rf_ref, lb_ref, rb_ref, w2_ref, ar_ref, ai_ref, yo_ref,
               u_ref, y_ref, wm_ref, sre, sim, hre_f, him_f, hre_b, him_b, *, n_chunks, n_ctx_chunks, pitch):
    nc, ncc = n_chunks, n_ctx_chunks
    ncl = nc - ncc
    p = S5_STATE
    t_len = S5_T
    gl = S5_GROUP
    per_half = 128 // gl
    rb = _s5_row_block(nc)
    lane_slot = lax.broadcasted_iota(jnp.int32, (rb, 128), 1) // gl

    @pl.when(pl.program_id(1) == 0)
    def _():
        tk = t_len * gl
        src_tok = lax.broadcasted_iota(jnp.int32, (tk, tk), 0) // gl
        dst_tok = lax.broadcasted_iota(jnp.int32, (tk, tk), 1) // gl
        for g in range(S5_GB):
            causal = _dot_split3(lf_ref[g], rf_ref[g])
            anti = _dot_split3(lb_ref[g], rb_ref[g])
            wm = jnp.where(dst_tok >= src_tok, causal, 0.0) + jnp.where(src_tok >= dst_tok, anti, 0.0)
            wm_ref[g] = wm.astype(wm_ref.dtype)

    def slot_transpose(xs):
        xs = list(xs)
        bit = per_half // 2
        while bit >= 1:
            upper = (lane_slot // bit) % 2 == 1
            nxt = list(xs)
            for p in range(per_half):
                if p & bit:
                    continue
                lo, hi = xs[p], xs[p + bit]
                nxt[p] = jnp.where(upper, pltpu.roll(hi, bit * gl, axis=1), lo)
                nxt[p + bit] = jnp.where(upper, hi, pltpu.roll(lo, 128 - bit * gl, axis=1))
            xs = nxt
            bit //= 2
        return xs

    def gather_u(blk, carry):
        r0 = pl.multiple_of(blk * rb, 16)
        halves = []
        for hb in range(t_len // per_half):
            a = [hn_ref[pl.ds(r0 * t_len + hb * per_half + j, rb, stride=t_len), :] for j in range(per_half)]
            halves.append(slot_transpose(a))
        for i in range(S5_GB):
            u_ref[i, pl.ds(r0, rb), :] = jnp.concatenate([h[i] for h in halves], axis=1).astype(u_ref.dtype)
        return carry

    lax.fori_loop(0, nc // rb, gather_u, 0)

    for g in range(S5_GB):
        s = jnp.dot(u_ref[g], ws_ref[g], preferred_element_type=F32)
        sre[pl.ds(g * pitch, nc), :] = s[:, : 2 * p]
        sim[pl.ds(g * pitch, nc), :] = s[:, 2 * p:]

    ar = ar_ref[...]
    ai = ai_ref[...]
    fwd_lane = lax.broadcasted_iota(jnp.int32, (S5_GB, 2 * p), 1) < p

    def step(k, carry):
        h_r, h_i = carry
        cf = jnp.where(k < ncc, ncl + k, k - ncc)
        cb = nc - 1 - k
        rows_f = pl.ds(cf, S5_GB, stride=pitch)
        rows_b = pl.ds(cb, S5_GB, stride=pitch)
        hre_f[rows_f, :] = h_r
        him_f[rows_f, :] = h_i
        hre_b[rows_b, :] = h_r
        him_b[rows_b, :] = h_i
        s_r = jnp.where(fwd_lane, sre[rows_f, :], sre[rows_b, :])
        s_i = jnp.where(fwd_lane, sim[rows_f, :], sim[rows_b, :])
        n_r = ar * h_r - ai * h_i + s_r
        n_i = ar * h_i + ai * h_r + s_i
        return n_r, n_i

    zero = jnp.zeros((S5_GB, 2 * p), F32)
    lax.fori_loop(0, nc, step, (zero, zero))

    for g in range(S5_GB):
        rows = pl.ds(g * pitch, nc)
        hin = jnp.concatenate([hre_f[rows, :], him_f[rows, :], hre_b[rows, :], him_b[rows, :]], axis=1)
        out = jnp.dot(u_ref[g], wm_ref[g], preferred_element_type=F32)
        out = out + jnp.dot(hin.astype(BF16), w2_ref[g], preferred_element_type=F32)
        y_ref[g] = out

    def scatter_y(blk, carry):
        r0 = pl.multiple_of(blk * rb, 16)
        for hb in range(t_len // per_half):
            yv = [y_ref[i, pl.ds(r0, rb), hb * 128:(hb + 1) * 128] for i in range(S5_GB)]
            for j, tok in enumerate(slot_transpose(yv)):
                yo_ref[pl.ds(r0 * t_len + hb * per_half + j, rb, stride=t_len), :] = tok
        return carry

    lax.fori_loop(0, nc // rb, scatter_y, 0)


def _s5_scan(hn, n_lat, weights):
    bsz, lt, d = hn.shape
    ngrp = d // S5_GROUP
    t = S5_T
    tk = t * S5_GROUP
    nc = lt // t
    ncc = (lt - n_lat) // t
    gb = S5_GB
    assert gb * S5_GROUP == 128 and (128 // S5_GROUP) == gb and t % gb == 0
    ws, lf, rf, lb, rb, w2, ar, ai = weights
    p2 = 2 * S5_STATE
    pitch = nc + 8 if (nc // 8) % 2 == 0 else nc
    kern = functools.partial(_s5_kernel, n_chunks=nc, n_ctx_chunks=ncc, pitch=pitch)
    per_group = lambda *s: pl.BlockSpec((gb,) + s, lambda gi, b: (gi,) + tuple(0 for _ in s))
    return pl.pallas_call(
        kern,
        grid=(ngrp // gb, bsz),
        in_specs=[
            pl.BlockSpec((None, lt, 128), lambda gi, b: (b, 0, gi)),
            per_group(tk, 2 * p2),
            per_group(tk, p2), per_group(p2, tk), per_group(tk, p2), per_group(p2, tk),
            per_group(4 * p2, tk),
            per_group(p2), per_group(p2),
        ],
        out_specs=pl.BlockSpec((None, lt, 128), lambda gi, b: (b, 0, gi)),
        out_shape=jax.ShapeDtypeStruct((bsz, lt, d), F32),
        scratch_shapes=[pltpu.VMEM((gb, nc, tk), BF16), pltpu.VMEM((gb, nc, tk), F32), pltpu.VMEM((gb, tk, tk), BF16)]
        + [pltpu.VMEM((gb * pitch, p2), F32) for _ in range(6)],
        compiler_params=_cparams("parallel", "arbitrary"),
        name="s5_scan",
    )(hn, ws, lf, rf, lb, rb, w2, ar, ai)


def _router_gates(hn2, rw_ref, rb_ref):
    n_exp = rw_ref.shape[0]
    epg = n_exp // N_EXPERT_GROUPS
    logits = lax.dot_general(rw_ref[...], hn2, (((1,), (1,)), ((), ())),
                             preferred_element_type=F32, precision=HIGHEST)
    s = _sigmoid(logits)
    sel = s + rb_ref[...]
    row = [sel[e:e + 1] for e in range(n_exp)]
    gscore = []
    for gi in range(N_EXPERT_GROUPS):
        a, b, c, dd = row[gi * epg: gi * epg + epg]
        hi1, lo1 = jnp.maximum(a, b), jnp.minimum(a, b)
        hi2, lo2 = jnp.maximum(c, dd), jnp.minimum(c, dd)
        gscore.append(jnp.maximum(hi1, hi2) + jnp.maximum(jnp.minimum(hi1, hi2), jnp.maximum(lo1, lo2)))
    gmax = functools.reduce(jnp.maximum, gscore)
    gates = []
    taken = None
    for gi in range(N_EXPERT_GROUPS):
        is_max = gscore[gi] == gmax
        best = is_max if taken is None else jnp.logical_and(is_max, jnp.logical_not(taken))
        taken = is_max if taken is None else jnp.logical_or(taken, is_max)
        for e in range(gi * epg, gi * epg + epg):
            rank = jnp.zeros_like(row[e])
            for j in range(gi * epg, gi * epg + epg):
                if j == e:
                    continue
                ahead = (row[j] >= row[e]) if j < e else (row[j] > row[e])
                rank = rank + ahead.astype(F32)
            chosen = jnp.logical_and(best, rank < float(TOP_K))
            gates.append(jnp.where(chosen, s[e:e + 1], 0.0))
    g = jnp.concatenate(gates, axis=0)
    return g / jnp.sum(g, axis=0, keepdims=True)


def _glu_kernel(y_ref, u_ref, x_ref, c_ref, mod_ref, d_ref, w_ref, b_ref, g2_ref, rw_ref, rb_ref,
                x1_ref, hn2_ref, gates_ref, *, n_lat_tiles):
    d = x_ref.shape[-1]
    is_lat = pl.program_id(1) < n_lat_tiles
    m = mod_ref[...]
    nsub = ROW_SUBTILES
    sub = x_ref.shape[0] // nsub
    for s in range(nsub):
        r = slice(s * sub, (s + 1) * sub)
        u = u_ref[r, :].astype(F32)
        a = _gelu_tanh(y_ref[r, :].astype(F32) + d_ref[...] * u)
        z = jnp.dot(a.astype(BF16), w_ref[...], preferred_element_type=F32) + b_ref[...]
        out = z[:, :d] * _sigmoid(z[:, d:])
        x1 = jnp.where(is_lat, x_ref[r, :], c_ref[r, :]) + m[2:3] * out
        x1_ref[r, :] = x1
        hn2 = _rms(x1, g2_ref[...]) * (1.0 + m[4:5]) + m[3:4]
        hn2_ref[r, :] = hn2.astype(hn2_ref.dtype)
        gates_ref[:, r] = _router_gates(hn2, rw_ref, rb_ref)


def _glu_head(y, hn, x, ctx, mods, d_skip, glu_w, glu_b, norm2_g, router_wt, router_b, n_lat_tiles):
    bsz, lt, d = hn.shape
    nt = lt // TOKEN_TILE
    n_exp = router_wt.shape[0]
    tok = pl.BlockSpec((None, TOKEN_TILE, d), lambda b, i: (b, i, 0))
    vec = lambda n: pl.BlockSpec((1, n), lambda b, i: (0, 0))
    return pl.pallas_call(
        functools.partial(_glu_kernel, n_lat_tiles=n_lat_tiles),
        grid=(bsz, nt),
        in_specs=[tok, tok] + _lat_ctx_specs(n_lat_tiles, d) + [
            pl.BlockSpec((None, 6, d), lambda b, i: (b * 2 + (i >= n_lat_tiles).astype(jnp.int32), 0, 0)),
            vec(d),
            pl.BlockSpec((d, 2 * d), lambda b, i: (0, 0)),
            vec(2 * d),
            vec(d),
            pl.BlockSpec((n_exp, d), lambda b, i: (0, 0)),
            pl.BlockSpec((n_exp, 1), lambda b, i: (0, 0)),
        ],
        out_specs=[
            tok, tok,
            pl.BlockSpec((n_exp, TOKEN_TILE), lambda b, i: (0, b * nt + i)),
        ],
        out_shape=[
            jax.ShapeDtypeStruct((bsz, lt, d), F32),
            jax.ShapeDtypeStruct((bsz, lt, d), BF16),
            jax.ShapeDtypeStruct((n_exp, bsz * lt), F32),
        ],
        compiler_params=_cparams("parallel", "parallel"),
        name="s5_glu_head",
    )(y, hn, x, ctx, mods, d_skip.reshape(1, d), glu_w.astype(BF16), glu_b.reshape(1, 2 * d),
      norm2_g.reshape(1, d), router_wt, router_b.reshape(n_exp, 1))


def _moe_slots_padded(total):
    return jnp.floor((total + float(MOE_PIECE - 1)) * (1.0 / MOE_PIECE)) * float(MOE_PIECE)


def _moe_sort_kernel(gt_ref, t_ref, ts_ref, cnt_ref):
    n_exp, nb = gt_ref.shape
    nbpad = ts_ref.shape[0]
    sel = jnp.logical_and(gt_ref[...] > 0.0, pl.program_id(0) < pl.num_programs(0) - 1)
    sel_b = jnp.where(sel, 1.0, 0.0).astype(BF16)
    earlier = lax.broadcasted_iota(jnp.int32, (nb, nb), 0) < lax.broadcasted_iota(jnp.int32, (nb, nb), 1)
    rank = jnp.dot(sel_b, jnp.where(earlier, 1.0, 0.0).astype(BF16), preferred_element_type=F32)
    total = jnp.sum(jnp.where(sel, 1.0, 0.0), axis=1, keepdims=True)
    padded = jnp.broadcast_to(_moe_slots_padded(total), (n_exp, 128))
    below = lax.broadcasted_iota(jnp.int32, (n_exp, n_exp), 1) < lax.broadcasted_iota(jnp.int32, (n_exp, n_exp), 0)
    offs = jnp.dot(jnp.where(below, 1.0, 0.0).astype(BF16), padded.astype(BF16), preferred_element_type=F32)[:, 0:1]
    dest = offs + rank
    d_lo = jnp.min(jnp.where(sel, dest, float(nbpad)), axis=0, keepdims=True).astype(jnp.int32)
    d_hi = jnp.max(jnp.where(sel, dest, -1.0), axis=0, keepdims=True).astype(jnp.int32)
    slot = lax.broadcasted_iota(jnp.int32, (nbpad, nb), 0)
    perm = jnp.where(jnp.logical_or(slot == d_lo, slot == d_hi), 1.0, 0.0).astype(BF16)
    ts_ref[...] = jnp.dot(perm, t_ref[...], preferred_element_type=F32).astype(ts_ref.dtype)
    cnt_ref[...] = padded


def _moe_expert_kernel(tile_e, src_rows, dst_rows, ntiles, ts_hbm, w1_ref, w3_ref, w2_ref, ys_hbm,
                       tbuf, ybuf, w1b, w3b, w2b, sem_in, sem_out):
    t = pl.program_id(0)
    nt = ntiles[0]
    last = pl.num_programs(0) - 1
    ppt = MOE_TILE // MOE_PIECE
    slot = t % 2

    def rows_at(table, tt, p):
        return pl.ds(pl.multiple_of(table[tt * ppt + p], MOE_PIECE), MOE_PIECE)

    def copy_in(tt, sl, p):
        return pltpu.make_async_copy(ts_hbm.at[rows_at(src_rows, tt, p), :],
                                     tbuf.at[sl, pl.ds(p * MOE_PIECE, MOE_PIECE), :], sem_in.at[sl])

    def copy_out(tt, sl, p):
        return pltpu.make_async_copy(ybuf.at[sl, pl.ds(p * MOE_PIECE, MOE_PIECE), :],
                                     ys_hbm.at[rows_at(dst_rows, tt, p), :], sem_out.at[sl])

    def start_in(tt, sl):
        for p in range(ppt):
            copy_in(tt, sl, p).start()

    def wait_out(tt, sl):
        for p in range(ppt):
            copy_out(tt, sl, p).wait()

    @pl.when(jnp.logical_and(t == 0, nt > 0))
    def _():
        start_in(0, 0)

    @pl.when(t + 1 < nt)
    def _():
        start_in(t + 1, 1 - slot)

    @pl.when(jnp.logical_and(t >= 2, t - 2 < nt))
    def _():
        wait_out(t - 2, slot)

    @pl.when(t < nt)
    def _():
        @pl.when(jnp.logical_or(t == 0, tile_e[t] != tile_e[jnp.maximum(t - 1, 0)]))
        def _():
            w1b[...] = w1_ref[...].astype(BF16)
            w3b[...] = w3_ref[...].astype(BF16)
            w2b[...] = w2_ref[...].astype(BF16)

        for p in range(ppt):
            copy_in(t, slot, p).wait()
        x = tbuf[slot]
        h = _silu(jnp.dot(x, w1b[...], preferred_element_type=F32)) * jnp.dot(x, w3b[...], preferred_element_type=F32)
        ybuf[slot] = jnp.dot(h.astype(BF16), w2b[...], preferred_element_type=F32).astype(ybuf.dtype)
        for p in range(ppt):
            copy_out(t, slot, p).start()

    @pl.when(t == last)
    def _():
        @pl.when(jnp.logical_and(last >= 1, last - 1 < nt))
        def _():
            wait_out(last - 1, 1 - slot)

        @pl.when(last < nt)
        def _():
            wait_out(last, slot)


def _moe_unsort_kernel(ys_ref, g_ref, x_ref, g2_ref, *rest, final):
    if final:
        fg_ref, o_ref = rest
    else:
        (o_ref,) = rest
    nb, n_exp = g_ref.shape
    nbpad = ys_ref.shape[0]
    gates = g_ref[...]
    sel = gates > 0.0
    sel_b = jnp.where(sel, 1.0, 0.0).astype(BF16)
    earlier = lax.broadcasted_iota(jnp.int32, (nb, nb), 1) < lax.broadcasted_iota(jnp.int32, (nb, nb), 0)
    rank = jnp.dot(jnp.where(earlier, 1.0, 0.0).astype(BF16), sel_b, preferred_element_type=F32)
    total = jnp.sum(jnp.where(sel, 1.0, 0.0), axis=0, keepdims=True)
    padded = jnp.broadcast_to(_moe_slots_padded(total), (8, n_exp))
    below = lax.broadcasted_iota(jnp.int32, (n_exp, n_exp), 0) < lax.broadcasted_iota(jnp.int32, (n_exp, n_exp), 1)
    offs = jnp.dot(padded.astype(BF16), jnp.where(below, 1.0, 0.0).astype(BF16), preferred_element_type=F32)[0:1]
    dest = offs + rank
    d_lo = jnp.min(jnp.where(sel, dest, float(nbpad)), axis=1, keepdims=True)
    d_hi = jnp.max(jnp.where(sel, dest, -1.0), axis=1, keepdims=True)
    g_lo = jnp.sum(jnp.where(jnp.logical_and(sel, dest == d_lo), gates, 0.0), axis=1, keepdims=True)
    g_hi = jnp.sum(jnp.where(jnp.logical_and(sel, dest == d_hi), gates, 0.0), axis=1, keepdims=True)
    slot = lax.broadcasted_iota(jnp.int32, (nb, nbpad), 1)
    ys = ys_ref[...]

    def pick(dcol):
        onehot = jnp.where(slot == dcol.astype(jnp.int32), 1.0, 0.0).astype(BF16)
        return jnp.dot(onehot, ys, preferred_element_type=F32)

    moe = g_lo * pick(d_lo) + g_hi * pick(d_hi)
    half = nb // g2_ref.shape[0]
    d = x_ref.shape[1]
    for j in range(g2_ref.shape[0]):
        r = slice(j * half, (j + 1) * half)
        xn = x_ref[r, :] + g2_ref[j] * moe[r]
        if final:
            xn = _rms(xn, fg_ref[...])
            for c in range(half // SSD_CHUNK):
                col = (j * half) // SSD_CHUNK + c
                o_ref[:, col * d:(col + 1) * d] = xn[c * SSD_CHUNK:(c + 1) * SSD_CHUNK]
        else:
            o_ref[r, :] = xn


def _moe_schedule(counts, nbpad, n_tiles):
    nblk, n_exp = counts.shape
    ppt = MOE_TILE // MOE_PIECE
    pc = counts // MOE_PIECE
    loc = jnp.cumsum(pc, axis=1) - pc
    cum_b = jnp.cumsum(pc, axis=0)
    np_e = cum_b[-1]
    tiles_e = (np_e + ppt - 1) // ppt
    tile_end = jnp.cumsum(tiles_e)
    ntiles = tile_end[-1]
    t_idx = jnp.arange(n_tiles, dtype=jnp.int32)
    tile_e = jnp.minimum(jnp.sum((tile_end[None, :] <= t_idx[:, None]).astype(jnp.int32), axis=1), n_exp - 1)
    first = (tile_end - tiles_e)[tile_e]
    piece0 = (t_idx - first) * ppt
    npieces = jnp.where(t_idx < ntiles, jnp.clip(np_e[tile_e] - piece0, 0, ppt), 0)
    i = piece0[:, None] + jnp.arange(ppt, dtype=jnp.int32)[None, :]
    cum_t = cum_b.T[tile_e]
    blk = jnp.minimum(jnp.sum((cum_t[:, None, :] <= i[:, :, None]).astype(jnp.int32), axis=2), nblk - 1)
    before = jnp.take_along_axis(cum_t - pc.T[tile_e], blk, axis=1)
    within = i - before + jnp.take_along_axis(loc.T[tile_e], blk, axis=1)
    rows = blk * nbpad + within * MOE_PIECE
    real = jnp.arange(ppt)[None, :] < npieces[:, None]
    spare = nblk * nbpad
    piece = jnp.arange(ppt, dtype=jnp.int32)[None, :]
    src = jnp.where(real, rows, spare + piece * MOE_PIECE)
    dst = jnp.where(real, rows, spare + (ppt + (t_idx[:, None] % 2) * ppt + piece) * MOE_PIECE)
    return (tile_e.astype(jnp.int32), src.reshape(-1).astype(jnp.int32), dst.reshape(-1).astype(jnp.int32),
            ntiles.reshape(1).astype(jnp.int32))


def _moe(t, gates_t, xres, g2rows, w1, w3, w2, layer, *, final_g=None, blocks_per_batch=None):
    n, d = t.shape
    _, n_exp, _, f = w1.shape
    nb = MOE_BLOCK
    nblk = n // nb
    nbpad = TOP_K * nb + n_exp * MOE_PIECE
    final = final_g is not None

    ppt = MOE_TILE // MOE_PIECE
    assert 3 * ppt * MOE_PIECE <= nbpad
    last_blk = lambda j: jnp.minimum(j, nblk - 1)
    ts, cnt = pl.pallas_call(
        _moe_sort_kernel,
        grid=(nblk + 1,),
        in_specs=[pl.BlockSpec((n_exp, nb), lambda j: (0, last_blk(j))), pl.BlockSpec((nb, d), lambda j: (last_blk(j), 0))],
        out_specs=[pl.BlockSpec((nbpad, d), lambda j: (j, 0)), pl.BlockSpec((None, n_exp, 128), lambda j: (j, 0, 0))],
        out_shape=[jax.ShapeDtypeStruct(((nblk + 1) * nbpad, d), BF16), jax.ShapeDtypeStruct((nblk + 1, n_exp, 128), F32)],
        compiler_params=_cparams("parallel"),
        name="moe_sort",
    )(gates_t, t)

    n_tiles = nblk * nbpad // MOE_TILE + n_exp
    tile_e, src_rows, dst_rows, ntiles = _moe_schedule(cnt[:nblk, :, 0].astype(jnp.int32), nbpad, n_tiles)
    wspec = lambda shape: pl.BlockSpec((None, None) + shape, lambda i, te, sr, dr, nt: (layer, te[i], 0, 0))
    ys = pl.pallas_call(
        _moe_expert_kernel,
        grid_spec=pltpu.PrefetchScalarGridSpec(
            num_scalar_prefetch=4,
            grid=(n_tiles,),
            in_specs=[pl.BlockSpec(memory_space=pl.ANY), wspec((d, f)), wspec((d, f)), wspec((f, d))],
            out_specs=pl.BlockSpec(memory_space=pl.ANY),
            scratch_shapes=[pltpu.VMEM((2, MOE_TILE, d), BF16), pltpu.VMEM((2, MOE_TILE, d), BF16),
                            pltpu.VMEM((d, f), BF16), pltpu.VMEM((d, f), BF16), pltpu.VMEM((f, d), BF16),
                            pltpu.SemaphoreType.DMA((2,)), pltpu.SemaphoreType.DMA((2,))],
        ),
        out_shape=jax.ShapeDtypeStruct(((nblk + 1) * nbpad, d), BF16),
        input_output_aliases={4: 0},
        compiler_params=_cparams("arbitrary"),
        name="moe_experts",
    )(tile_e, src_rows, dst_rows, ntiles, ts, w1, w3, w2)

    halves = nb // TOKEN_TILE
    in_specs = [
        pl.BlockSpec((nbpad, d), lambda j: (j, 0)),
        pl.BlockSpec((nb, n_exp), lambda j: (j, 0)),
        pl.BlockSpec((nb, d), lambda j: (j, 0)),
        pl.BlockSpec((halves, 1, d), lambda j: (j, 0, 0)),
    ]
    args = [ys, gates_t.T, xres, g2rows]
    if final:
        in_specs.append(pl.BlockSpec((1, d), lambda j: (0, 0)))
        args.append(final_g.reshape(1, d))
        cols = nb // SSD_CHUNK
        out_spec = pl.BlockSpec((None, SSD_CHUNK, cols * d), lambda j: (j // blocks_per_batch, 0, j % blocks_per_batch))
        out_shape = jax.ShapeDtypeStruct((nblk // blocks_per_batch, SSD_CHUNK, GRID_W * d), F32)
    else:
        out_spec = pl.BlockSpec((nb, d), lambda j: (j, 0))
        out_shape = jax.ShapeDtypeStruct((n, d), F32)
    return pl.pallas_call(
        functools.partial(_moe_unsort_kernel, final=final),
        grid=(nblk,),
        in_specs=in_specs,
        out_specs=out_spec,
        out_shape=out_shape,
        compiler_params=_cparams("parallel"),
        name="moe_unsort_final" if final else "moe_unsort",
    )(*args)


SSD_COLS_PER_STEP = 8


def _ssd_inproj_kernel(xl_ref, xc_ref, g_ref, mod_ref, wz_ref, wx_ref, wdt_ref, wdtt_ref, bias_ref, biast_ref,
                       a_ref, at_ref, cw_ref, cb_ref, z_ref, xbc_ref, csdt_ref, cst_ref, xp_ref, slab, xt, ext, slabs, slabs_in,
                       *, n_lat_steps, n_ctx_chunks):
    step = pl.program_id(1)
    ip = pl.program_id(2)
    q = SSD_CHUNK
    ncols = SSD_COLS_PER_STEP
    d = xt.shape[1]
    is_lat = step < n_lat_steps
    c0 = step * ncols + 2 * ip
    n_lat_chunks = n_lat_steps * ncols
    n_chunks = n_lat_chunks + n_ctx_chunks
    halo = SSD_CONV // 2
    nsl = ext.shape[2]
    pitch = slabs.shape[1] // nsl

    def emit_conv(sub):
        w = [cw_ref[k] for k in range(SSD_CONV)]
        bias = cb_ref[...]
        for t in range(q):
            acc = bias
            for k in range(SSD_CONV):
                acc = acc + w[k] * ext[sub, t + k]
            y = _silu(acc)
            for a in range(nsl // 8):
                slabs[sub, pl.ds(a * 8 * pitch + t, 8, stride=pitch), :] = y[a * 8:(a + 1) * 8]
        for s in range(nsl):
            xbc_ref[sub * q:(sub + 1) * q, s * 128:(s + 1) * 128] = (
                slabs[sub, s * pitch:s * pitch + q, :].astype(xbc_ref.dtype))

    def project(sub):
        other = 1 - sub
        rows = slice(sub * q, (sub + 1) * q)
        m = mod_ref[...]
        x = xt[pl.ds(pl.multiple_of((2 * ip + sub) * q, q), q), :]
        xp_ref[rows, :] = x
        hn = (_rms(x, g_ref[...]) * (1.0 + m[1:2]) + m[0:1]).astype(BF16)
        z_ref[rows, :] = jnp.dot(hn, wz_ref[...], preferred_element_type=F32).astype(z_ref.dtype)
        xbc_new = jnp.dot(hn, wx_ref[...], preferred_element_type=F32)
        for s in range(nsl):
            slabs_in[sub, s * pitch:s * pitch + q, :] = xbc_new[:, s * 128:(s + 1) * 128]
        for t in range(q):
            for a in range(nsl // 8):
                ext[sub, halo + t, a * 8:(a + 1) * 8, :] = slabs_in[sub, pl.ds(a * 8 * pitch + t, 8, stride=pitch), :]
        if sub == 0:
            starts = jnp.logical_or(c0 == 0, c0 == n_lat_chunks)
            ext[other, halo + q:2 * halo + q] = jnp.where(starts, 0.0, ext[sub, halo:2 * halo])
            ext[sub, 0:halo] = jnp.where(starts, 0.0, ext[other, q:q + halo])
        else:
            ext[other, halo + q:2 * halo + q] = ext[sub, halo:2 * halo]
            ext[sub, 0:halo] = ext[other, q:q + halo]

        def softplus(v):
            return jnp.maximum(v, 0.0) + jnp.log(1.0 + jnp.exp(-jnp.abs(v)))

        r_i = lax.broadcasted_iota(jnp.int32, (q, q), 0)
        c_i = lax.broadcasted_iota(jnp.int32, (q, q), 1)
        lower = (r_i >= c_i).astype(F32)
        upper = (r_i <= c_i).astype(F32)
        dt = softplus(jnp.dot(hn, wdt_ref[...], preferred_element_type=F32) + bias_ref[...])
        da = dt * a_ref[...]
        half = da.shape[1] // 2
        cs = jnp.concatenate([jnp.dot(lower, da[:, :half], preferred_element_type=F32, precision=HIGHEST),
                              jnp.dot(upper, da[:, half:], preferred_element_type=F32, precision=HIGHEST)], axis=1)
        lane = lax.broadcasted_iota(jnp.int32, dt.shape, 1)
        csdt_ref[rows, :] = jnp.where(lane % 8 < 4, cs, dt)
        nh = at_ref.shape[0] // 2
        dtt = softplus(lax.dot_general(wdtt_ref[...], hn, (((1,), (1,)), ((), ())), preferred_element_type=F32)
                       + biast_ref[...])
        dat = dtt * at_ref[...]
        cst_f = jnp.dot(dat[:nh], upper, preferred_element_type=F32, precision=HIGHEST)
        cst_b = jnp.dot(dat[nh:], lower, preferred_element_type=F32, precision=HIGHEST)
        cst = jnp.concatenate([cst_f, cst_b], axis=0)
        for j in range(cst_ref.shape[1]):
            cst_ref[sub, j] = cst[j * 4:(j + 1) * 4, :]

    @pl.when(jnp.logical_and(ip == 0, is_lat))
    def _():
        for s in range(d // 128):
            slab[...] = xl_ref[:, :, s * 128:(s + 1) * 128].reshape(q * ncols, 128)
            for w in range(ncols):
                xt[w * q:(w + 1) * q, s * 128:(s + 1) * 128] = slab[pl.ds(w, q, stride=ncols), :]

    @pl.when(jnp.logical_and(ip == 0, jnp.logical_not(is_lat)))
    def _():
        xt[0:n_ctx_chunks * q, :] = xc_ref[...]

    @pl.when(c0 == 0)
    def _():
        ext[...] = jnp.zeros_like(ext)

    @pl.when(jnp.logical_or(is_lat, 2 * ip < n_ctx_chunks))
    def _():
        for sub in range(2):
            emit_conv(sub)
            project(sub)

    @pl.when(c0 == n_chunks)
    def _():
        ext[1, halo + q:2 * halo + q] = jnp.zeros((halo,) + ext.shape[2:], F32)
        for sub in range(2):
            emit_conv(sub)


def _ssd_inproj(xall, n_lat, norm_g, mods, in_w, dt_bias, a_log, conv_w, conv_b):
    bsz, lt, d = xall.shape
    q = SSD_CHUNK
    ncols = SSD_COLS_PER_STEP
    ncl = n_lat // q
    nc = lt // q
    ncc = nc - ncl
    n_ctx = lt - n_lat
    nh2 = dt_bias.size
    d_inner = (nh2 // 2) * SSD_HEAD_DIM
    conv_ch = in_w.shape[1] - d_inner - nh2
    wz = in_w[:, :d_inner].astype(BF16)
    wx = in_w[:, d_inner:d_inner + conv_ch].astype(BF16)
    wdt = in_w[:, d_inner + conv_ch:].astype(BF16)
    a = -jnp.exp(a_log.astype(F32)).reshape(nh2)
    bias = dt_bias.astype(F32).reshape(nh2)
    ngr = nh2 // 4
    lanes = jnp.arange(2 * nh2)
    dup = (lanes // 8) * 4 + lanes % 4
    assert n_lat // GRID_W == q and GRID_W % ncols == 0 and lt % GRID_W == 0 and ncc + 2 <= ncols and ncc % 2 == 0 and n_lat % n_ctx == 0
    nls = ncl // ncols
    xgrid = xall.reshape(bsz, lt // GRID_W, GRID_W, d)
    kern = functools.partial(_ssd_inproj_kernel, n_lat_steps=nls, n_ctx_chunks=ncc)
    full = lambda s: pl.BlockSpec(s, lambda b, st, i: tuple(0 for _ in s), pipeline_mode=pl.Buffered(1))
    pps = ncols // 2
    npairs = nc // 2
    pair = lambda b, st, i: jnp.minimum(st * pps + i, npairs - 1)
    rows = lambda width: pl.BlockSpec((None, 2 * q, width), lambda b, st, i: (b, pair(b, st, i), 0))
    return pl.pallas_call(
        kern,
        grid=(bsz, nls + 1, pps),
        in_specs=[
            pl.BlockSpec((None, q, ncols, d), lambda b, st, i: (b, 0, jnp.minimum(st, nls - 1), 0)),
            pl.BlockSpec((None, n_ctx, d), lambda b, st, i: (b, n_lat // n_ctx, 0)),
            full((1, d)),
            pl.BlockSpec((None, 6, d), lambda b, st, i: (b * 2 + (st >= nls).astype(jnp.int32), 0, 0)),
            full((d, d_inner)), full((d, conv_ch)), full((d, 2 * nh2)), full((nh2, d)),
            full((1, 2 * nh2)), full((nh2, 1)), full((1, 2 * nh2)), full((nh2, 1)),
            full((SSD_CONV, conv_ch // 128, 128)), full((conv_ch // 128, 128)),
        ],
        out_specs=[
            rows(d_inner),
            pl.BlockSpec((None, 2 * q, conv_ch), lambda b, st, i: (b, jnp.clip(st * pps + i - 1, 0, npairs - 1), 0)),
            rows(2 * nh2),
            pl.BlockSpec((None, 2, ngr, 4, q), lambda b, st, i: (b, pair(b, st, i), 0, 0, 0)),
            rows(d),
        ],
        out_shape=[
            jax.ShapeDtypeStruct((bsz, lt, d_inner), BF16),
            jax.ShapeDtypeStruct((bsz, lt, conv_ch), BF16),
            jax.ShapeDtypeStruct((bsz, lt, 2 * nh2), F32),
            jax.ShapeDtypeStruct((bsz, nc, ngr, 4, q), F32),
            jax.ShapeDtypeStruct((bsz, lt, d), F32),
        ],
        scratch_shapes=[pltpu.VMEM((q * ncols, 128), F32), pltpu.VMEM((q * ncols, d), F32),
                        pltpu.VMEM((2, q + 2 * (SSD_CONV // 2), conv_ch // 128, 128), F32),
                        pltpu.VMEM((2, conv_ch // 128 * (q + 8), 128), F32),
                        pltpu.VMEM((2, conv_ch // 128 * (q + 8), 128), F32)],
        compiler_params=_cparams("parallel", "arbitrary", "arbitrary"),
        name="ssd_inproj",
    )(xgrid, xall, norm_g.reshape(1, d), mods, wz, wx, wdt[:, dup], wdt.T, bias[dup].reshape(1, -1),
      bias.reshape(nh2, 1), a[dup].reshape(1, -1), a.reshape(nh2, 1), conv_w.astype(F32).reshape(SSD_CONV, conv_ch // 128, 128),
      conv_b.astype(F32).reshape(conv_ch // 128, 128))


def _ssd_scan_dir(x, bm, cm, v, cst, state, reverse):
    q = SSD_CHUNK
    hp = SSD_HEAD_DIM
    gw = x.shape[1]
    r = gw // hp
    lane = lax.broadcasted_iota(jnp.int32, (q, 2 * hp), 1)
    first = lane < hp
    bc = [jnp.broadcast_to(v[:, j:j + 1], (q, 2 * hp)) for j in range(2 * r)]

    def head_lanes(cols):
        return jnp.concatenate([jnp.where(first, cols[2 * p], cols[2 * p + 1]) for p in range(r // 2)], axis=1)

    cs_x = head_lanes(bc[:r])
    dt_x = head_lanes(bc[r:])
    end = 0 if reverse else q - 1
    cs_end = cs_x[end:end + 1, :]
    xdt = x * dt_x
    xw = (xdt * jnp.exp(cs_end - cs_x)).astype(BF16)
    cb = lax.dot_general(cm, bm, (((1,), (1,)), ((), ())), preferred_element_type=F32)
    y_off = jnp.dot(cm, state.astype(BF16), preferred_element_type=F32) * jnp.exp(cs_x)
    r_i = lax.broadcasted_iota(jnp.int32, (q, q), 0)
    c_i = lax.broadcasted_iota(jnp.int32, (q, q), 1)
    mask = (r_i <= c_i) if reverse else (r_i >= c_i)
    ys = []
    for pair in range(r // 2):
        xp = xdt[:, pair * 2 * hp:(pair + 1) * 2 * hp]
        rhs = jnp.concatenate([jnp.where(first, xp, 0.0), jnp.where(first, 0.0, xp)], axis=0).astype(BF16)
        gmats = []
        for h in (2 * pair, 2 * pair + 1):
            seg = jnp.exp(jnp.where(mask, bc[h] - cst[h:h + 1, :], -1e30))
            gmats.append((cb * seg).astype(BF16))
        ys.append(jnp.dot(jnp.concatenate(gmats, axis=1), rhs, preferred_element_type=F32))
    upd = lax.dot_general(bm, xw, (((0,), (0,)), ((), ())), preferred_element_type=F32)
    return jnp.concatenate(ys, axis=1) + y_off, state * jnp.exp(cs_end) + upd


def _ssd_scan_kernel(xf, bf, cf, csdtf, cstf, xb, bb, cb, csdtb, cstb, yf_ref, yb_ref, state_f, state_b):
    @pl.when(pl.program_id(1) == 0)
    def _():
        state_f[...] = jnp.zeros_like(state_f)
        state_b[...] = jnp.zeros_like(state_b)

    ngr, n, gw = state_f.shape
    r = gw // SSD_HEAD_DIM
    dirs = ((xf, bf, cf, csdtf, cstf, yf_ref, state_f, False, 0), (xb, bb, cb, csdtb, cstb, yb_ref, state_b, True, ngr))
    for g in range(ngr):
        for x_ref, b_ref, c_ref, csdt_ref, cst_ref, y_ref, state, reverse, lane_group0 in dirs:
            j = lane_group0 + g
            y, new_state = _ssd_scan_dir(
                x_ref[:, g * gw:(g + 1) * gw].astype(F32), b_ref[:, g * n:(g + 1) * n], c_ref[:, g * n:(g + 1) * n],
                csdt_ref[:, 2 * r * j:2 * r * (j + 1)], cst_ref[g], state[g], reverse)
            y_ref[:, g * gw:(g + 1) * gw] = y.astype(y_ref.dtype)
            state[g] = new_state


def _ssd_scan(xbc, csdt, cst, n_lat):
    bsz, lt, ch = xbc.shape
    q = SSD_CHUNK
    nc, ncl = lt // q, n_lat // q
    ngr = SSD_GROUPS
    n = SSD_STATE
    d_inner = ch - 2 * ngr * n
    gw = d_inner // ngr
    assert d_inner % (ngr * n) == 0 and q == 2 * SSD_HEAD_DIM

    def specs(chunk, direction):
        return [
            pl.BlockSpec((None, q, d_inner), lambda b, k: (b, chunk(k), 0)),
            pl.BlockSpec((None, q, ngr * n), lambda b, k: (b, chunk(k), d_inner // (ngr * n))),
            pl.BlockSpec((None, q, ngr * n), lambda b, k: (b, chunk(k), d_inner // (ngr * n) + 1)),
            pl.BlockSpec((None, q, csdt.shape[2]), lambda b, k: (b, chunk(k), 0)),
            pl.BlockSpec((None, None, ngr, 4, q), lambda b, k: (b, chunk(k), direction, 0, 0)),
        ]

    fwd = lambda k: (k + ncl) % nc
    bwd = lambda k: nc - 1 - k
    out = jax.ShapeDtypeStruct((bsz, lt, d_inner), BF16)
    return pl.pallas_call(
        _ssd_scan_kernel,
        grid=(bsz, nc),
        in_specs=specs(fwd, 0) + specs(bwd, 1),
        out_specs=[pl.BlockSpec((None, q, d_inner), lambda b, k: (b, fwd(k), 0)),
                   pl.BlockSpec((None, q, d_inner), lambda b, k: (b, bwd(k), 0))],
        out_shape=[out, out],
        scratch_shapes=[pltpu.VMEM((ngr, n, gw), F32), pltpu.VMEM((ngr, n, gw), F32)],
        compiler_params=_cparams("parallel", "arbitrary"),
        name="ssd_scan",
    )(xbc, xbc, xbc, csdt, cst, xbc, xbc, xbc, csdt, cst)


def _ssd_finish_kernel(yf_ref, yb_ref, xs_ref, z_ref, x_ref, mod_ref, dsk_ref, ng_ref, w_ref, g2_ref,
                       rw_ref, rb_ref, x3_ref, hn2_ref, gates_ref):
    m = mod_ref[...]
    nsub = ROW_SUBTILES
    sub = x_ref.shape[0] // nsub
    for s in range(nsub):
        r = slice(s * sub, (s + 1) * sub)
        y = yf_ref[r, :].astype(F32) + yb_ref[r, :].astype(F32) + dsk_ref[...] * xs_ref[r, :].astype(F32)
        gated = y * _silu(z_ref[r, :].astype(F32))
        nrm = _rms(gated, ng_ref[...])
        out = jnp.dot(nrm.astype(BF16), w_ref[...], preferred_element_type=F32)
        x3 = x_ref[r, :] + m[2:3] * out
        x3_ref[r, :] = x3
        hn2 = _rms(x3, g2_ref[...]) * (1.0 + m[4:5]) + m[3:4]
        hn2_ref[r, :] = hn2.astype(hn2_ref.dtype)
        gates_ref[:, r] = _router_gates(hn2, rw_ref, rb_ref)


def _ssd_finish(yf, yb, xbc, z, xall, n_lat, mods, d_skip, norm_g, out_w, norm2_g, router_wt, router_b):
    bsz, lt, d_inner = z.shape
    d = xall.shape[-1]
    q = TOKEN_TILE
    ncl = n_lat // q
    n_exp = router_wt.shape[0]
    inner = pl.BlockSpec((None, q, d_inner), lambda b, c: (b, c, 0))
    tok = pl.BlockSpec((None, q, d), lambda b, c: (b, c, 0))
    full = lambda s: pl.BlockSpec(s, lambda b, c: tuple(0 for _ in s))
    dsk = jnp.repeat(d_skip.astype(F32), SSD_HEAD_DIM).reshape(1, d_inner)
    return pl.pallas_call(
        _ssd_finish_kernel,
        grid=(bsz, ncl),
        in_specs=[
            inner, inner, inner, inner,
            tok,
            pl.BlockSpec((None, 6, d), lambda b, c: (b * 2, 0, 0)),
            full((1, d_inner)), full((1, d_inner)), full((d_inner, d)), full((1, d)),
            full((n_exp, d)), full((n_exp, 1)),
        ],
        out_specs=[tok, tok, pl.BlockSpec((n_exp, q), lambda b, c: (0, b * ncl + c))],
        out_shape=[
            jax.ShapeDtypeStruct((bsz, n_lat, d), F32),
            jax.ShapeDtypeStruct((bsz, n_lat, d), BF16),
            jax.ShapeDtypeStruct((n_exp, bsz * n_lat), F32),
        ],
        compiler_params=_cparams("parallel", "parallel"),
        name="ssd_finish",
    )(yf, yb, xbc, z, xall, mods, dsk, norm_g.reshape(1, d_inner), out_w.astype(BF16),
      norm2_g.reshape(1, d), router_wt, router_b.reshape(n_exp, 1))


def kernel(x, c, ctx, c_ctx, mod_w, mod_b, norm1_g, norm2_g, final_g, s5_lam_re, s5_lam_im, s5_log_dt, s5_b_re, s5_b_im, s5_c_re, s5_c_im, s5_d, s5_glu_w, s5_glu_b, ssd_in_w, ssd_conv_w, ssd_conv_b, ssd_dt_bias, ssd_a_log, ssd_d, ssd_norm_g, ssd_out_w, router_w, router_b, moe_w1, moe_w3, moe_w2):
    bsz, n_lat, d = x.shape
    n_ctx = ctx.shape[1]
    lt = n_lat + n_ctx
    n_exp = router_w.shape[1]
    assert n_lat % TOKEN_TILE == 0 and n_ctx % TOKEN_TILE == 0
    assert (bsz * lt) % MOE_BLOCK == 0 and n_lat % MOE_BLOCK == 0 and MOE_BLOCK % TOKEN_TILE == 0
    assert TOKEN_TILE % SSD_CHUNK == 0

    mods = _modulation(c, c_ctx, mod_w, mod_b)
    router_wt = router_w.T.astype(F32)
    w1, w3, w2 = moe_w1, moe_w3, moe_w2
    nlt = n_lat // TOKEN_TILE
    tpb = lt // TOKEN_TILE

    hn = _prenorm(x, ctx, norm1_g[0], mods[0], nlt)
    s5w = _s5_weights(s5_lam_re[0], s5_lam_im[0], s5_log_dt[0], s5_b_re[0], s5_b_im[0], s5_c_re[0], s5_c_im[0])
    y = _s5_scan(hn, n_lat, s5w)
    x1, hn2, gates_t = _glu_head(y, hn, x, ctx, mods[0], s5_d[0], s5_glu_w[0], s5_glu_b[0], norm2_g[0],
                                 router_wt, router_b, nlt)
    g2_lat = jnp.broadcast_to(mods[0][0::2, None, 5], (bsz, nlt, d))
    g2_ctx = jnp.broadcast_to(mods[0][1::2, None, 5], (bsz, tpb - nlt, d))
    g2rows = jnp.concatenate([g2_lat, g2_ctx], axis=1).reshape(bsz * tpb, 1, d)
    x2 = _moe(hn2.reshape(bsz * lt, d), gates_t, x1.reshape(bsz * lt, d), g2rows, w1, w3, w2, 0).reshape(bsz, lt, d)

    z, xbc, csdt, cst, x2p = _ssd_inproj(x2, n_lat, norm1_g[1], mods[1], ssd_in_w[0], ssd_dt_bias[0], ssd_a_log[0],
                                             ssd_conv_w[0], ssd_conv_b[0])
    yf, yb = _ssd_scan(xbc, csdt, cst, n_lat)
    x3, hn3, gates3_t = _ssd_finish(yf, yb, xbc, z, x2p, n_lat, mods[1], ssd_d[0], ssd_norm_g[0], ssd_out_w[0],
                                    norm2_g[1], router_wt, router_b)
    g2rows = jnp.broadcast_to(mods[1][0::2, None, 5], (bsz, nlt, d)).reshape(bsz * nlt, 1, d)
    out = _moe(hn3.reshape(bsz * n_lat, d), gates3_t, x3.reshape(bsz * n_lat, d), g2rows, w1, w3, w2, 1,
               final_g=final_g, blocks_per_batch=n_lat // MOE_BLOCK)
    return out.reshape(bsz, n_lat, d)
```

```python
import functools

import jax
import jax.numpy as jnp
from jax import lax
from jax.experimental import pallas as pl
from jax.experimental.pallas import tpu as pltpu

F32 = jnp.float32
BF16 = jnp.bfloat16
HIGHEST = lax.Precision.HIGHEST

GRID_W = 64
RMS_EPS = 1e-6
LOG2_E = 1.4426950408889634

S5_GROUP = 16
S5_STATE = 64
S5_T = 16
S5_GB = 8

SSD_HEAD_DIM = 64
SSD_GROUPS = 8
SSD_STATE = 128
SSD_CONV = 5
SSD_CHUNK = 128

N_EXPERT_GROUPS = 4
TOP_K = 2

TOKEN_TILE = 256
ROW_SUBTILES = 2
MOE_BLOCK = 512
MOE_PIECE = 16
MOE_TILE = 256
VMEM_LIMIT_BYTES = 56 * 1024 * 1024


def _cparams(*sem):
    return pltpu.CompilerParams(dimension_semantics=sem, vmem_limit_bytes=VMEM_LIMIT_BYTES)


def _sigmoid(v):
    return 1.0 / (1.0 + jnp.exp(-v))


def _silu(v):
    return v * _sigmoid(v)


def _gelu_tanh(v):
    return 0.5 * v * (1.0 + jnp.tanh(0.7978845608028654 * (v + 0.044715 * (v * v * v))))


def _rms(v, g):
    return v * lax.rsqrt(jnp.mean(v * v, axis=-1, keepdims=True) + RMS_EPS) * g


def _mod_kernel(cc_ref, w_ref, b_ref, o_ref):
    a = _silu(cc_ref[...])
    o_ref[...] = jnp.dot(a, w_ref[...], preferred_element_type=F32, precision=HIGHEST) + b_ref[...]


def _modulation(c, c_ctx, mod_w, mod_b):
    depth, d, d6 = mod_w.shape
    bsz = c.shape[0]
    rows = 8
    cc = jnp.zeros((rows, d), F32).at[:bsz].set(c).at[bsz].set(c_ctx)
    tn = d6 // 4
    out = pl.pallas_call(
        _mod_kernel,
        grid=(depth, d6 // tn),
        in_specs=[
            pl.BlockSpec((rows, d), lambda i, j: (0, 0)),
            pl.BlockSpec((None, d, tn), lambda i, j: (i, 0, j)),
            pl.BlockSpec((None, 1, tn), lambda i, j: (i, 0, j)),
        ],
        out_specs=pl.BlockSpec((None, rows, tn), lambda i, j: (i, 0, j)),
        out_shape=jax.ShapeDtypeStruct((depth, rows, d6), F32),
        compiler_params=_cparams("parallel", "parallel"),
        name="modulation",
    )(cc, mod_w, mod_b.reshape(depth, 1, d6))
    lat = out[:, :bsz].reshape(depth, bsz, 1, 6, d)
    ctx = jnp.broadcast_to(out[:, bsz].reshape(depth, 1, 1, 6, d), (depth, bsz, 1, 6, d))
    return jnp.concatenate([lat, ctx], axis=2).reshape(depth, bsz * 2, 6, d)


def _prenorm_kernel(x_ref, c_ref, g_ref, mod_ref, o_ref, *, n_lat_tiles):
    m = mod_ref[...]
    x = jnp.where(pl.program_id(1) < n_lat_tiles, x_ref[...], c_ref[...])
    hn = _rms(x, g_ref[...]) * (1.0 + m[1:2]) + m[0:1]
    o_ref[...] = hn.astype(o_ref.dtype)


def _lat_ctx_specs(n_lat_tiles, d):
    return [pl.BlockSpec((None, TOKEN_TILE, d), lambda b, i: (b, jnp.minimum(i, n_lat_tiles - 1), 0)),
            pl.BlockSpec((None, TOKEN_TILE, d), lambda b, i: (b, jnp.maximum(i - n_lat_tiles, 0), 0))]


def _prenorm(x, ctx, g, mods, n_lat_tiles):
    bsz, n_lat, d = x.shape
    lt = n_lat + ctx.shape[1]
    nt = lt // TOKEN_TILE
    return pl.pallas_call(
        functools.partial(_prenorm_kernel, n_lat_tiles=n_lat_tiles),
        grid=(bsz, nt),
        in_specs=_lat_ctx_specs(n_lat_tiles, d) + [
            pl.BlockSpec((1, d), lambda b, i: (0, 0)),
            pl.BlockSpec((None, 6, d), lambda b, i: (b * 2 + (i >= n_lat_tiles).astype(jnp.int32), 0, 0)),
        ],
        out_specs=pl.BlockSpec((None, TOKEN_TILE, d), lambda b, i: (b, i, 0)),
        out_shape=jax.ShapeDtypeStruct((bsz, lt, d), F32),
        compiler_params=_cparams("parallel", "parallel"),
        name="prenorm",
    )(x, ctx, g.reshape(1, d), mods)


def _s5_weights(lam_re, lam_im, log_dt, b_re, b_im, c_re, c_im):
    t = S5_T
    k16 = b_re.shape[-1]

    def cmul(ar, ai, br, bi):
        return ar * br - ai * bi, ar * bi + ai * br

    def direction(k):
        lr, li = lam_re[k], lam_im[k]
        step = jnp.exp(log_dt[k])[:, None]
        mag = jnp.exp(lr * step)
        abar_r = mag * jnp.cos(li * step)
        abar_i = mag * jnp.sin(li * step)
        den = lr * lr + li * li
        q_r = ((abar_r - 1.0) * lr + abar_i * li) / den
        q_i = (abar_i * lr - (abar_r - 1.0) * li) / den
        bb_r, bb_i = cmul(q_r[..., None], q_i[..., None], b_re, b_im)

        def power(tau):
            tau = jnp.asarray(tau, F32)[None, :, None]
            m = jnp.exp((lr * step)[:, None, :] * tau)
            return m * jnp.cos((li * step)[:, None, :] * tau), m * jnp.sin((li * step)[:, None, :] * tau)

        return bb_r.transpose(0, 2, 1), bb_i.transpose(0, 2, 1), power

    rows = lambda v: jnp.repeat(v, k16, axis=1)
    row_tile = lambda v: jnp.tile(v, (1, t, 1))
    cols = lambda v: jnp.repeat(v.transpose(0, 2, 1), k16, axis=2)
    col_tile = lambda v: jnp.tile(v, (1, 1, t))
    ct_r, ct_i = c_re.transpose(0, 2, 1), c_im.transpose(0, 2, 1)
    steps = jnp.arange(t)

    def left(bt_r, bt_i, power, tau):
        pr, pi = power(tau)
        return cmul(row_tile(bt_r), row_tile(bt_i), rows(pr), rows(pi))

    def right(power, tau):
        pr, pi = power(tau)
        return cmul(col_tile(ct_r), col_tile(ct_i), cols(pr), cols(pi))

    bf_r, bf_i, pow_f = direction(0)
    bb_r, bb_i, pow_b = direction(1)
    lf_r, lf_i = left(bf_r, bf_i, pow_f, -steps)
    rf_r, rf_i = right(pow_f, steps)
    lb_r, lb_i = left(bb_r, bb_i, pow_b, steps)
    rb_r, rb_i = right(pow_b, -steps)
    lf = jnp.concatenate([lf_r, lf_i], axis=-1)
    lb = jnp.concatenate([lb_r, lb_i], axis=-1)
    rf = jnp.concatenate([rf_r, -rf_i], axis=1)
    rb = jnp.concatenate([rb_r, -rb_i], axis=1)

    sf_r, sf_i = left(bf_r, bf_i, pow_f, t - 1 - steps)
    ws = jnp.concatenate([sf_r, lb_r, sf_i, lb_i], axis=-1)

    of_r, of_i = right(pow_f, steps + 1)
    ob_r, ob_i = right(pow_b, t - steps)
    zero = jnp.zeros_like(of_r)
    w2 = jnp.concatenate([of_r, zero, -of_i, zero, zero, ob_r, zero, -ob_i], axis=1)

    af_r, af_i = pow_f([t])
    ab_r, ab_i = pow_b([t])
    ar = jnp.concatenate([af_r[:, 0], ab_r[:, 0]], axis=-1)
    ai = jnp.concatenate([af_i[:, 0], ab_i[:, 0]], axis=-1)
    return ws.astype(BF16), lf, rf, lb, rb, w2.astype(BF16), ar, ai


def _s5_row_block(nc):
    return max(rb for rb in range(16, min(nc, 176) + 1, 16) if nc % rb == 0)


def _dot_split3(a, b):
    a_hi = a.astype(BF16)
    a_lo = (a - a_hi.astype(F32)).astype(BF16)
    b_hi = b.astype(BF16)
    b_lo = (b - b_hi.astype(F32)).astype(BF16)
    dot = functools.partial(jnp.dot, preferred_element_type=F32)
    return dot(a_hi, b_hi) + dot(a_hi, b_lo) + dot(a_lo, b_hi)


def _s5_kernel(hn_ref, ws_ref, lf_ref, rf_ref, lb_ref, rb_ref, w2_ref, ar_ref, ai_ref, yo_ref,
               u_ref, y_ref, wm_ref, sre, sim, hre_f, him_f, hre_b, him_b, *, n_chunks, n_ctx_chunks, pitch):
    nc, ncc = n_chunks, n_ctx_chunks
    ncl = nc - ncc
    p = S5_STATE
    t_len = S5_T
    gl = S5_GROUP
    per_half = 128 // gl
    rb = _s5_row_block(nc)
    lane_slot = lax.broadcasted_iota(jnp.int32, (rb, 128), 1) // gl

    @pl.when(pl.program_id(1) == 0)
    def _():
        tk = t_len * gl
        src_tok = lax.broadcasted_iota(jnp.int32, (tk, tk), 0) // gl
        dst_tok = lax.broadcasted_iota(jnp.int32, (tk, tk), 1) // gl
        for g in range(S5_GB):
            causal = _dot_split3(lf_ref[g], rf_ref[g])
            anti = _dot_split3(lb_ref[g], rb_ref[g])
            wm = jnp.where(dst_tok >= src_tok, causal, 0.0) + jnp.where(src_tok >= dst_tok, anti, 0.0)
            wm_ref[g] = wm.astype(wm_ref.dtype)

    def slot_transpose(xs):
        xs = list(xs)
        bit = per_half // 2
        while bit >= 1:
            upper = (lane_slot // bit) % 2 == 1
            nxt = list(xs)
            for p in range(per_half):
                if p & bit:
                    continue
                lo, hi = xs[p], xs[p + bit]
                nxt[p] = jnp.where(upper, pltpu.roll(hi, bit * gl, axis=1), lo)
                nxt[p + bit] = jnp.where(upper, hi, pltpu.roll(lo, 128 - bit * gl, axis=1))
            xs = nxt
            bit //= 2
        return xs

    def gather_u(blk, carry):
        r0 = pl.multiple_of(blk * rb, 16)
        halves = []
        for hb in range(t_len // per_half):
            a = [hn_ref[pl.ds(r0 * t_len + hb * per_half + j, rb, stride=t_len), :] for j in range(per_half)]
            halves.append(slot_transpose(a))
        for i in range(S5_GB):
            u_ref[i, pl.ds(r0, rb), :] = jnp.concatenate([h[i] for h in halves], axis=1).astype(u_ref.dtype)
        return carry

    lax.fori_loop(0, nc // rb, gather_u, 0)

    for g in range(S5_GB):
        s = jnp.dot(u_ref[g], ws_ref[g], preferred_element_type=F32)
        sre[pl.ds(g * pitch, nc), :] = s[:, : 2 * p]
        sim[pl.ds(g * pitch, nc), :] = s[:, 2 * p:]

    ar = ar_ref[...]
    ai = ai_ref[...]
    fwd_lane = lax.broadcasted_iota(jnp.int32, (S5_GB, 2 * p), 1) < p

    def step(k, carry):
        h_r, h_i = carry
        cf = jnp.where(k < ncc, ncl + k, k - ncc)
        cb = nc - 1 - k
        rows_f = pl.ds(cf, S5_GB, stride=pitch)
        rows_b = pl.ds(cb, S5_GB, stride=pitch)
        hre_f[rows_f, :] = h_r
        him_f[rows_f, :] = h_i
        hre_b[rows_b, :] = h_r
        him_b[rows_b, :] = h_i
        s_r = jnp.where(fwd_lane, sre[rows_f, :], sre[rows_b, :])
        s_i = jnp.where(fwd_lane, sim[rows_f, :], sim[rows_b, :])
        n_r = ar * h_r - ai * h_i + s_r
        n_i = ar * h_i + ai * h_r + s_i
        return n_r, n_i

    zero = jnp.zeros((S5_GB, 2 * p), F32)
    lax.fori_loop(0, nc, step, (zero, zero), unroll=2)

    for g in range(S5_GB):
        rows = pl.ds(g * pitch, nc)
        hin = jnp.concatenate([hre_f[rows, :], him_f[rows, :], hre_b[rows, :], him_b[rows, :]], axis=1)
        out = jnp.dot(u_ref[g], wm_ref[g], preferred_element_type=F32)
        out = out + jnp.dot(hin.astype(BF16), w2_ref[g], preferred_element_type=F32)
        y_ref[g] = out

    def scatter_y(blk, carry):
        r0 = pl.multiple_of(blk * rb, 16)
        for hb in range(t_len // per_half):
            yv = [y_ref[i, pl.ds(r0, rb), hb * 128:(hb + 1) * 128] for i in range(S5_GB)]
            for j, tok in enumerate(slot_transpose(yv)):
                yo_ref[pl.ds(r0 * t_len + hb * per_half + j, rb, stride=t_len), :] = tok
        return carry

    lax.fori_loop(0, nc // rb, scatter_y, 0)


def _s5_scan(hn, n_lat, weights):
    bsz, lt, d = hn.shape
    ngrp = d // S5_GROUP
    t = S5_T
    tk = t * S5_GROUP
    nc = lt // t
    ncc = (lt - n_lat) // t
    gb = S5_GB
    assert gb * S5_GROUP == 128 and (128 // S5_GROUP) == gb and t % gb == 0
    ws, lf, rf, lb, rb, w2, ar, ai = weights
    p2 = 2 * S5_STATE
    pitch = nc + 8 if (nc // 8) % 2 == 0 else nc
    kern = functools.partial(_s5_kernel, n_chunks=nc, n_ctx_chunks=ncc, pitch=pitch)
    per_group = lambda *s: pl.BlockSpec((gb,) + s, lambda gi, b: (gi,) + tuple(0 for _ in s))
    return pl.pallas_call(
        kern,
        grid=(ngrp // gb, bsz),
        in_specs=[
            pl.BlockSpec((None, lt, 128), lambda gi, b: (b, 0, gi)),
            per_group(tk, 2 * p2),
            per_group(tk, p2), per_group(p2, tk), per_group(tk, p2), per_group(p2, tk),
            per_group(4 * p2, tk),
            per_group(p2), per_group(p2),
        ],
        out_specs=pl.BlockSpec((None, lt, 128), lambda gi, b: (b, 0, gi)),
        out_shape=jax.ShapeDtypeStruct((bsz, lt, d), F32),
        scratch_shapes=[pltpu.VMEM((gb, nc, tk), BF16), pltpu.VMEM((gb, nc, tk), F32), pltpu.VMEM((gb, tk, tk), BF16)]
        + [pltpu.VMEM((gb * pitch, p2), F32) for _ in range(6)],
        compiler_params=_cparams("parallel", "arbitrary"),
        name="s5_scan",
    )(hn, ws, lf, rf, lb, rb, w2, ar, ai)


def _router_gates(hn2, rw_ref, rb_ref):
    n_exp = rw_ref.shape[0]
    epg = n_exp // N_EXPERT_GROUPS
    logits = lax.dot_general(rw_ref[...], hn2, (((1,), (1,)), ((), ())),
                             preferred_element_type=F32, precision=HIGHEST)
    s = _sigmoid(logits)
    sel = s + rb_ref[...]
    row = [sel[e:e + 1] for e in range(n_exp)]
    gscore = []
    for gi in range(N_EXPERT_GROUPS):
        a, b, c, dd = row[gi * epg: gi * epg + epg]
        hi1, lo1 = jnp.maximum(a, b), jnp.minimum(a, b)
        hi2, lo2 = jnp.maximum(c, dd), jnp.minimum(c, dd)
        gscore.append(jnp.maximum(hi1, hi2) + jnp.maximum(jnp.minimum(hi1, hi2), jnp.maximum(lo1, lo2)))
    gmax = functools.reduce(jnp.maximum, gscore)
    gates = []
    taken = None
    for gi in range(N_EXPERT_GROUPS):
        is_max = gscore[gi] == gmax
        best = is_max if taken is None else jnp.logical_and(is_max, jnp.logical_not(taken))
        taken = is_max if taken is None else jnp.logical_or(taken, is_max)
        for e in range(gi * epg, gi * epg + epg):
            rank = jnp.zeros_like(row[e])
            for j in range(gi * epg, gi * epg + epg):
                if j == e:
                    continue
                ahead = (row[j] >= row[e]) if j < e else (row[j] > row[e])
                rank = rank + ahead.astype(F32)
            chosen = jnp.logical_and(best, rank < float(TOP_K))
            gates.append(jnp.where(chosen, s[e:e + 1], 0.0))
    g = jnp.concatenate(gates, axis=0)
    return g / jnp.sum(g, axis=0, keepdims=True)


def _glu_kernel(y_ref, u_ref, x_ref, c_ref, mod_ref, d_ref, w_ref, b_ref, g2_ref, rw_ref, rb_ref,
                x1_ref, hn2_ref, gates_ref, *, n_lat_tiles):
    d = x_ref.shape[-1]
    is_lat = pl.program_id(1) < n_lat_tiles
    m = mod_ref[...]
    nsub = ROW_SUBTILES
    sub = x_ref.shape[0] // nsub
    for s in range(nsub):
        r = slice(s * sub, (s + 1) * sub)
        u = u_ref[r, :].astype(F32)
        a = _gelu_tanh(y_ref[r, :].astype(F32) + d_ref[...] * u)
        z = jnp.dot(a.astype(BF16), w_ref[...], preferred_element_type=F32) + b_ref[...]
        out = z[:, :d] * _sigmoid(z[:, d:])
        x1 = jnp.where(is_lat, x_ref[r, :], c_ref[r, :]) + m[2:3] * out
        x1_ref[r, :] = x1
        hn2 = _rms(x1, g2_ref[...]) * (1.0 + m[4:5]) + m[3:4]
        hn2_ref[r, :] = hn2.astype(hn2_ref.dtype)
        gates_ref[:, r] = _router_gates(hn2, rw_ref, rb_ref)


def _glu_head(y, hn, x, ctx, mods, d_skip, glu_w, glu_b, norm2_g, router_wt, router_b, n_lat_tiles):
    bsz, lt, d = hn.shape
    nt = lt // TOKEN_TILE
    n_exp = router_wt.shape[0]
    tok = pl.BlockSpec((None, TOKEN_TILE, d), lambda b, i: (b, i, 0))
    vec = lambda n: pl.BlockSpec((1, n), lambda b, i: (0, 0))
    return pl.pallas_call(
        functools.partial(_glu_kernel, n_lat_tiles=n_lat_tiles),
        grid=(bsz, nt),
        in_specs=[tok, tok] + _lat_ctx_specs(n_lat_tiles, d) + [
            pl.BlockSpec((None, 6, d), lambda b, i: (b * 2 + (i >= n_lat_tiles).astype(jnp.int32), 0, 0)),
            vec(d),
            pl.BlockSpec((d, 2 * d), lambda b, i: (0, 0)),
            vec(2 * d),
            vec(d),
            pl.BlockSpec((n_exp, d), lambda b, i: (0, 0)),
            pl.BlockSpec((n_exp, 1), lambda b, i: (0, 0)),
        ],
        out_specs=[
            tok, tok,
            pl.BlockSpec((n_exp, TOKEN_TILE), lambda b, i: (0, b * nt + i)),
        ],
        out_shape=[
            jax.ShapeDtypeStruct((bsz, lt, d), F32),
            jax.ShapeDtypeStruct((bsz, lt, d), BF16),
            jax.ShapeDtypeStruct((n_exp, bsz * lt), F32),
        ],
        compiler_params=_cparams("parallel", "parallel"),
        name="s5_glu_head",
    )(y, hn, x, ctx, mods, d_skip.reshape(1, d), glu_w.astype(BF16), glu_b.reshape(1, 2 * d),
      norm2_g.reshape(1, d), router_wt, router_b.reshape(n_exp, 1))


def _moe_slots_padded(total):
    return jnp.floor((total + float(MOE_PIECE - 1)) * (1.0 / MOE_PIECE)) * float(MOE_PIECE)


def _moe_sort_kernel(gt_ref, t_ref, ts_ref, cnt_ref):
    n_exp, nb = gt_ref.shape
    nbpad = ts_ref.shape[0]
    sel = jnp.logical_and(gt_ref[...] > 0.0, pl.program_id(0) < pl.num_programs(0) - 1)
    sel_b = jnp.where(sel, 1.0, 0.0).astype(BF16)
    earlier = lax.broadcasted_iota(jnp.int32, (nb, nb), 0) < lax.broadcasted_iota(jnp.int32, (nb, nb), 1)
    rank = jnp.dot(sel_b, jnp.where(earlier, 1.0, 0.0).astype(BF16), preferred_element_type=F32)
    total = jnp.sum(jnp.where(sel, 1.0, 0.0), axis=1, keepdims=True)
    padded = jnp.broadcast_to(_moe_slots_padded(total), (n_exp, 128))
    below = lax.broadcasted_iota(jnp.int32, (n_exp, n_exp), 1) < lax.broadcasted_iota(jnp.int32, (n_exp, n_exp), 0)
    offs = jnp.dot(jnp.where(below, 1.0, 0.0).astype(BF16), padded.astype(BF16), preferred_element_type=F32)[:, 0:1]
    dest = offs + rank
    d_lo_f = jnp.min(jnp.where(sel, dest, float(nbpad)), axis=0, keepdims=True)
    d_hi_f = jnp.max(jnp.where(sel, dest, -1.0), axis=0, keepdims=True)
    slot = lax.broadcasted_iota(jnp.int32, (nbpad, nb), 0)
    at_lo = slot == d_lo_f.astype(jnp.int32)
    at_hi = slot == d_hi_f.astype(jnp.int32)
    d = t_ref.shape[1]
    perm = jnp.where(jnp.logical_or(at_lo, at_hi), 1.0, 0.0).astype(BF16)
    ts_ref[:, :d] = jnp.dot(perm, t_ref[...], preferred_element_type=F32).astype(ts_ref.dtype)
    gts = gt_ref[...]

    def gate_rows(d_f):
        g = jnp.sum(jnp.where(jnp.logical_and(sel, dest == d_f), gts, 0.0), axis=0, keepdims=True)
        g_hi = g.astype(BF16).astype(F32)
        g_mid = (g - g_hi).astype(BF16).astype(F32)
        g_lo = g - g_hi - g_mid
        row = lax.broadcasted_iota(jnp.int32, (128, nb), 0)
        terms = jnp.where(row == 0, g_hi, jnp.where(row == 1, g_mid, jnp.where(row == 2, g_lo, 0.0)))
        return terms.astype(BF16)

    nt_dims = (((1,), (1,)), ((), ()))
    gate_cols = (lax.dot_general(jnp.where(at_lo, 1.0, 0.0).astype(BF16), gate_rows(d_lo_f), nt_dims,
                                 preferred_element_type=F32)
                 + lax.dot_general(jnp.where(at_hi, 1.0, 0.0).astype(BF16), gate_rows(d_hi_f), nt_dims,
                                   preferred_element_type=F32))
    ts_ref[:, d:] = gate_cols.astype(ts_ref.dtype)
    cnt_ref[...] = padded


def _moe_expert_kernel(tile_e, src_rows, dst_rows, ntiles, ts_hbm, w1_ref, w3_ref, w2_ref, ys_hbm,
                       tbuf, ybuf, w1b, w3b, w2b, sem_in, sem_out):
    t = pl.program_id(0)
    nt = ntiles[0]
    last = pl.num_programs(0) - 1
    ppt = MOE_TILE // MOE_PIECE
    slot = t % 2

    def rows_at(table, tt, p):
        return pl.ds(pl.multiple_of(table[tt * ppt + p], MOE_PIECE), MOE_PIECE)

    def copy_in(tt, sl, p):
        return pltpu.make_async_copy(ts_hbm.at[rows_at(src_rows, tt, p), :],
                                     tbuf.at[sl, pl.ds(p * MOE_PIECE, MOE_PIECE), :], sem_in.at[sl])

    def copy_out(tt, sl, p):
        return pltpu.make_async_copy(ybuf.at[sl, pl.ds(p * MOE_PIECE, MOE_PIECE), :],
                                     ys_hbm.at[rows_at(dst_rows, tt, p), pl.ds(0, ybuf.shape[2])], sem_out.at[sl])

    def start_in(tt, sl):
        for p in range(ppt):
            copy_in(tt, sl, p).start()

    def wait_out(tt, sl):
        for p in range(ppt):
            copy_out(tt, sl, p).wait()

    @pl.when(jnp.logical_and(t == 0, nt > 0))
    def _():
        start_in(0, 0)

    @pl.when(t + 1 < nt)
    def _():
        start_in(t + 1, 1 - slot)

    @pl.when(jnp.logical_and(t >= 2, t - 2 < nt))
    def _():
        wait_out(t - 2, slot)

    @pl.when(t < nt)
    def _():
        @pl.when(jnp.logical_or(t == 0, tile_e[t] != tile_e[jnp.maximum(t - 1, 0)]))
        def _():
            w1b[...] = w1_ref[...].astype(BF16)
            w3b[...] = w3_ref[...].astype(BF16)
            w2b[...] = w2_ref[...].astype(BF16)

        for p in range(ppt):
            copy_in(t, slot, p).wait()
        d = ybuf.shape[2]
        x = tbuf[slot, :, :d]
        gate = jnp.sum(tbuf[slot, :, d:].astype(F32), axis=1, keepdims=True)
        h = _silu(jnp.dot(x, w1b[...], preferred_element_type=F32)) * jnp.dot(x, w3b[...], preferred_element_type=F32)
        ybuf[slot] = (gate * jnp.dot(h.astype(BF16), w2b[...], preferred_element_type=F32)).astype(ybuf.dtype)
        for p in range(ppt):
            copy_out(t, slot, p).start()

    @pl.when(t == last)
    def _():
        @pl.when(jnp.logical_and(last >= 1, last - 1 < nt))
        def _():
            wait_out(last - 1, 1 - slot)

        @pl.when(last < nt)
        def _():
            wait_out(last, slot)


def _moe_unsort_kernel(ys_ref, g_ref, x_ref, g2_ref, *rest, final):
    if final:
        fg_ref, o_ref = rest
    else:
        (o_ref,) = rest
    nb, n_exp = g_ref.shape
    nbpad = ys_ref.shape[0]
    gates = g_ref[...]
    sel = gates > 0.0
    sel_b = jnp.where(sel, 1.0, 0.0).astype(BF16)
    earlier = lax.broadcasted_iota(jnp.int32, (nb, nb), 1) < lax.broadcasted_iota(jnp.int32, (nb, nb), 0)
    rank = jnp.dot(jnp.where(earlier, 1.0, 0.0).astype(BF16), sel_b, preferred_element_type=F32)
    total = jnp.sum(jnp.where(sel, 1.0, 0.0), axis=0, keepdims=True)
    padded = jnp.broadcast_to(_moe_slots_padded(total), (8, n_exp))
    below = lax.broadcasted_iota(jnp.int32, (n_exp, n_exp), 0) < lax.broadcasted_iota(jnp.int32, (n_exp, n_exp), 1)
    offs = jnp.dot(padded.astype(BF16), jnp.where(below, 1.0, 0.0).astype(BF16), preferred_element_type=F32)[0:1]
    dest = offs + rank
    d_lo = jnp.min(jnp.where(sel, dest, float(nbpad)), axis=1, keepdims=True).astype(jnp.int32)
    d_hi = jnp.max(jnp.where(sel, dest, -1.0), axis=1, keepdims=True).astype(jnp.int32)
    slot = lax.broadcasted_iota(jnp.int32, (nb, nbpad), 1)
    pick = jnp.where(jnp.logical_or(slot == d_lo, slot == d_hi), 1.0, 0.0).astype(BF16)
    moe = jnp.dot(pick, ys_ref[...], preferred_element_type=F32)
    half = nb // g2_ref.shape[0]
    d = x_ref.shape[1]
    for j in range(g2_ref.shape[0]):
        r = slice(j * half, (j + 1) * half)
        xn = x_ref[r, :] + g2_ref[j] * moe[r]
        if final:
            xn = _rms(xn, fg_ref[...])
            for c in range(half // SSD_CHUNK):
                col = (j * half) // SSD_CHUNK + c
                o_ref[:, col * d:(col + 1) * d] = xn[c * SSD_CHUNK:(c + 1) * SSD_CHUNK]
        else:
            o_ref[r, :] = xn


def _moe_schedule(counts, nbpad, n_tiles):
    nblk, n_exp = counts.shape
    ppt = MOE_TILE // MOE_PIECE
    pc = counts // MOE_PIECE
    loc = jnp.cumsum(pc, axis=1) - pc
    cum_b = jnp.cumsum(pc, axis=0)
    np_e = cum_b[-1]
    tiles_e = (np_e + ppt - 1) // ppt
    tile_end = jnp.cumsum(tiles_e)
    ntiles = tile_end[-1]
    t_idx = jnp.arange(n_tiles, dtype=jnp.int32)
    tile_e = jnp.minimum(jnp.sum((tile_end[None, :] <= t_idx[:, None]).astype(jnp.int32), axis=1), n_exp - 1)
    first = (tile_end - tiles_e)[tile_e]
    piece0 = (t_idx - first) * ppt
    npieces = jnp.where(t_idx < ntiles, jnp.clip(np_e[tile_e] - piece0, 0, ppt), 0)
    i = piece0[:, None] + jnp.arange(ppt, dtype=jnp.int32)[None, :]
    cum_t = cum_b.T[tile_e]
    blk = jnp.minimum(jnp.sum((cum_t[:, None, :] <= i[:, :, None]).astype(jnp.int32), axis=2), nblk - 1)
    before = jnp.take_along_axis(cum_t - pc.T[tile_e], blk, axis=1)
    within = i - before + jnp.take_along_axis(loc.T[tile_e], blk, axis=1)
    rows = blk * nbpad + within * MOE_PIECE
    real = jnp.arange(ppt)[None, :] < npieces[:, None]
    spare = nblk * nbpad
    piece = jnp.arange(ppt, dtype=jnp.int32)[None, :]
    src = jnp.where(real, rows, spare + piece * MOE_PIECE)
    dst = jnp.where(real, rows, spare + (ppt + (t_idx[:, None] % 2) * ppt + piece) * MOE_PIECE)
    return (tile_e.astype(jnp.int32), src.reshape(-1).astype(jnp.int32), dst.reshape(-1).astype(jnp.int32),
            ntiles.reshape(1).astype(jnp.int32))


def _moe(t, gates_t, xres, g2rows, w1, w3, w2, layer, *, final_g=None, blocks_per_batch=None):
    n, d = t.shape
    _, n_exp, _, f = w1.shape
    nb = MOE_BLOCK
    nblk = n // nb
    nbpad = TOP_K * nb + n_exp * MOE_PIECE
    final = final_g is not None

    ppt = MOE_TILE // MOE_PIECE
    assert 3 * ppt * MOE_PIECE <= nbpad
    last_blk = lambda j: jnp.minimum(j, nblk - 1)
    ts, cnt = pl.pallas_call(
        _moe_sort_kernel,
        grid=(nblk + 1,),
        in_specs=[pl.BlockSpec((n_exp, nb), lambda j: (0, last_blk(j))), pl.BlockSpec((nb, d), lambda j: (last_blk(j), 0))],
        out_specs=[pl.BlockSpec((nbpad, d + 128), lambda j: (j, 0)), pl.BlockSpec((None, n_exp, 128), lambda j: (j, 0, 0))],
        out_shape=[jax.ShapeDtypeStruct(((nblk + 1) * nbpad, d + 128), BF16), jax.ShapeDtypeStruct((nblk + 1, n_exp, 128), F32)],
        compiler_params=_cparams("parallel"),
        name="moe_sort",
    )(gates_t, t)

    n_tiles = nblk * nbpad // MOE_TILE + n_exp
    tile_e, src_rows, dst_rows, ntiles = _moe_schedule(cnt[:nblk, :, 0].astype(jnp.int32), nbpad, n_tiles)
    wspec = lambda shape: pl.BlockSpec((None, None) + shape, lambda i, te, sr, dr, nt: (layer, te[i], 0, 0))
    ys = pl.pallas_call(
        _moe_expert_kernel,
        grid_spec=pltpu.PrefetchScalarGridSpec(
            num_scalar_prefetch=4,
            grid=(n_tiles,),
            in_specs=[pl.BlockSpec(memory_space=pl.ANY), wspec((d, f)), wspec((d, f)), wspec((f, d))],
            out_specs=pl.BlockSpec(memory_space=pl.ANY),
            scratch_shapes=[pltpu.VMEM((2, MOE_TILE, d + 128), BF16), pltpu.VMEM((2, MOE_TILE, d), BF16),
                            pltpu.VMEM((d, f), BF16), pltpu.VMEM((d, f), BF16), pltpu.VMEM((f, d), BF16),
                            pltpu.SemaphoreType.DMA((2,)), pltpu.SemaphoreType.DMA((2,))],
        ),
        out_shape=jax.ShapeDtypeStruct(((nblk + 1) * nbpad, d + 128), BF16),
        input_output_aliases={4: 0},
        compiler_params=_cparams("arbitrary"),
        name="moe_experts",
    )(tile_e, src_rows, dst_rows, ntiles, ts, w1, w3, w2)

    halves = nb // TOKEN_TILE
    in_specs = [
        pl.BlockSpec((nbpad, d), lambda j: (j, 0)),
        pl.BlockSpec((nb, n_exp), lambda j: (j, 0)),
        pl.BlockSpec((nb, d), lambda j: (j, 0)),
        pl.BlockSpec((halves, 1, d), lambda j: (j, 0, 0)),
    ]
    args = [ys, gates_t.T, xres, g2rows]
    if final:
        in_specs.append(pl.BlockSpec((1, d), lambda j: (0, 0)))
        args.append(final_g.reshape(1, d))
        cols = nb // SSD_CHUNK
        out_spec = pl.BlockSpec((None, SSD_CHUNK, cols * d), lambda j: (j // blocks_per_batch, 0, j % blocks_per_batch))
        out_shape = jax.ShapeDtypeStruct((nblk // blocks_per_batch, SSD_CHUNK, GRID_W * d), F32)
    else:
        out_spec = pl.BlockSpec((nb, d), lambda j: (j, 0))
        out_shape = jax.ShapeDtypeStruct((n, d), F32)
    return pl.pallas_call(
        functools.partial(_moe_unsort_kernel, final=final),
        grid=(nblk,),
        in_specs=in_specs,
        out_specs=out_spec,
        out_shape=out_shape,
        compiler_params=_cparams("parallel"),
        name="moe_unsort_final" if final else "moe_unsort",
    )(*args)


SSD_COLS_PER_STEP = 8


def _ssd_inproj_kernel(xl_ref, xc_ref, g_ref, mod_ref, wz_ref, wx_ref, wdt_ref, wdtt_ref, bias_ref, biast_ref,
                       a_ref, at_ref, cw_ref, cb_ref, z_ref, xbc_ref, csdt_ref, cst_ref, xp_ref, slab, xt, ext, slabs, slabs_in,
                       *, n_lat_steps, n_ctx_chunks):
    step = pl.program_id(1)
    ip = pl.program_id(2)
    q = SSD_CHUNK
    ncols = SSD_COLS_PER_STEP
    d = xt.shape[1]
    is_lat = step < n_lat_steps
    c0 = step * ncols + 2 * ip
    n_lat_chunks = n_lat_steps * ncols
    n_chunks = n_lat_chunks + n_ctx_chunks
    halo = SSD_CONV // 2
    nsl = ext.shape[2]
    pitch = slabs.shape[1] // nsl

    def emit_conv(sub):
        w = [cw_ref[k] for k in range(SSD_CONV)]
        bias = cb_ref[...]
        for t in range(q):
            acc = bias
            for k in range(SSD_CONV):
                acc = acc + w[k] * ext[sub, t + k]
            y = _silu(acc)
            for a in range(nsl // 8):
                slabs[sub, pl.ds(a * 8 * pitch + t, 8, stride=pitch), :] = y[a * 8:(a + 1) * 8]
        for s in range(nsl):
            xbc_ref[sub * q:(sub + 1) * q, s * 128:(s + 1) * 128] = (
                slabs[sub, s * pitch:s * pitch + q, :].astype(xbc_ref.dtype))

    def project(sub):
        other = 1 - sub
        rows = slice(sub * q, (sub + 1) * q)
        m = mod_ref[...]
        x = xt[pl.ds(pl.multiple_of((2 * ip + sub) * q, q), q), :]
        xp_ref[rows, :] = x
        hn = (_rms(x, g_ref[...]) * (1.0 + m[1:2]) + m[0:1]).astype(BF16)
        z_ref[rows, :] = jnp.dot(hn, wz_ref[...], preferred_element_type=F32).astype(z_ref.dtype)
        xbc_new = jnp.dot(hn, wx_ref[...], preferred_element_type=F32)
        for s in range(nsl):
            slabs_in[sub, s * pitch:s * pitch + q, :] = xbc_new[:, s * 128:(s + 1) * 128]
        for t in range(q):
            for a in range(nsl // 8):
                ext[sub, halo + t, a * 8:(a + 1) * 8, :] = slabs_in[sub, pl.ds(a * 8 * pitch + t, 8, stride=pitch), :]
        if sub == 0:
            starts = jnp.logical_or(c0 == 0, c0 == n_lat_chunks)
            ext[other, halo + q:2 * halo + q] = jnp.where(starts, 0.0, ext[sub, halo:2 * halo])
            ext[sub, 0:halo] = jnp.where(starts, 0.0, ext[other, q:q + halo])
        else:
            ext[other, halo + q:2 * halo + q] = ext[sub, halo:2 * halo]
            ext[sub, 0:halo] = ext[other, q:q + halo]

        def softplus(v):
            return jnp.maximum(v, 0.0) + jnp.log(1.0 + jnp.exp(-jnp.abs(v)))

        r_i = lax.broadcasted_iota(jnp.int32, (q, q), 0)
        c_i = lax.broadcasted_iota(jnp.int32, (q, q), 1)
        lower = (r_i >= c_i).astype(F32)
        upper = (r_i <= c_i).astype(F32)
        dt = softplus(jnp.dot(hn, wdt_ref[...], preferred_element_type=F32) + bias_ref[...])
        da = dt * a_ref[...]
        half = da.shape[1] // 2
        cs = jnp.concatenate([jnp.dot(lower, da[:, :half], preferred_element_type=F32, precision=HIGHEST),
                              jnp.dot(upper, da[:, half:], preferred_element_type=F32, precision=HIGHEST)], axis=1)
        lane = lax.broadcasted_iota(jnp.int32, dt.shape, 1)
        csdt_ref[rows, :] = jnp.where(lane % 8 < 4, cs, dt)
        nh = at_ref.shape[0] // 2
        dtt = softplus(lax.dot_general(wdtt_ref[...], hn, (((1,), (1,)), ((), ())), preferred_element_type=F32)
                       + biast_ref[...])
        dat = dtt * at_ref[...]
        cst_f = jnp.dot(dat[:nh], upper, preferred_element_type=F32, precision=HIGHEST)
        cst_b = jnp.dot(dat[nh:], lower, preferred_element_type=F32, precision=HIGHEST)
        cst = jnp.concatenate([cst_f, cst_b], axis=0)
        for j in range(cst_ref.shape[1]):
            cst_ref[sub, j] = cst[j * 4:(j + 1) * 4, :]

    @pl.when(jnp.logical_and(ip == 0, is_lat))
    def _():
        for s in range(d // 128):
            slab[...] = xl_ref[:, :, s * 128:(s + 1) * 128].reshape(q * ncols, 128)
            for w in range(ncols):
                xt[w * q:(w + 1) * q, s * 128:(s + 1) * 128] = slab[pl.ds(w, q, stride=ncols), :]

    @pl.when(jnp.logical_and(ip == 0, jnp.logical_not(is_lat)))
    def _():
        xt[0:n_ctx_chunks * q, :] = xc_ref[...]

    @pl.when(c0 == 0)
    def _():
        ext[...] = jnp.zeros_like(ext)

    @pl.when(jnp.logical_or(is_lat, 2 * ip < n_ctx_chunks))
    def _():
        for sub in range(2):
            emit_conv(sub)
            project(sub)

    @pl.when(c0 == n_chunks)
    def _():
        ext[1, halo + q:2 * halo + q] = jnp.zeros((halo,) + ext.shape[2:], F32)
        for sub in range(2):
            emit_conv(sub)


def _ssd_inproj(xall, n_lat, norm_g, mods, in_w, dt_bias, a_log, conv_w, conv_b):
    bsz, lt, d = xall.shape
    q = SSD_CHUNK
    ncols = SSD_COLS_PER_STEP
    ncl = n_lat // q
    nc = lt // q
    ncc = nc - ncl
    n_ctx = lt - n_lat
    nh2 = dt_bias.size
    d_inner = (nh2 // 2) * SSD_HEAD_DIM
    conv_ch = in_w.shape[1] - d_inner - nh2
    wz = in_w[:, :d_inner].astype(BF16)
    wx = in_w[:, d_inner:d_inner + conv_ch].astype(BF16)
    wdt = in_w[:, d_inner + conv_ch:].astype(BF16)
    a = -jnp.exp(a_log.astype(F32)).reshape(nh2) * LOG2_E
    bias = dt_bias.astype(F32).reshape(nh2)
    ngr = nh2 // 4
    lanes = jnp.arange(2 * nh2)
    dup = (lanes // 8) * 4 + lanes % 4
    assert n_lat // GRID_W == q and GRID_W % ncols == 0 and lt % GRID_W == 0 and ncc + 2 <= ncols and ncc % 2 == 0 and n_lat % n_ctx == 0
    nls = ncl // ncols
    xgrid = xall.reshape(bsz, lt // GRID_W, GRID_W, d)
    kern = functools.partial(_ssd_inproj_kernel, n_lat_steps=nls, n_ctx_chunks=ncc)
    full = lambda s: pl.BlockSpec(s, lambda b, st, i: tuple(0 for _ in s), pipeline_mode=pl.Buffered(1))
    pps = ncols // 2
    npairs = nc // 2
    pair = lambda b, st, i: jnp.minimum(st * pps + i, npairs - 1)
    rows = lambda width: pl.BlockSpec((None, 2 * q, width), lambda b, st, i: (b, pair(b, st, i), 0))
    return pl.pallas_call(
        kern,
        grid=(bsz, nls + 1, pps),
        in_specs=[
            pl.BlockSpec((None, q, ncols, d), lambda b, st, i: (b, 0, jnp.minimum(st, nls - 1), 0)),
            pl.BlockSpec((None, n_ctx, d), lambda b, st, i: (b, n_lat // n_ctx, 0)),
            full((1, d)),
            pl.BlockSpec((None, 6, d), lambda b, st, i: (b * 2 + (st >= nls).astype(jnp.int32), 0, 0)),
            full((d, d_inner)), full((d, conv_ch)), full((d, 2 * nh2)), full((nh2, d)),
            full((1, 2 * nh2)), full((nh2, 1)), full((1, 2 * nh2)), full((nh2, 1)),
            full((SSD_CONV, conv_ch // 128, 128)), full((conv_ch // 128, 128)),
        ],
        out_specs=[
            rows(d_inner),
            pl.BlockSpec((None, 2 * q, conv_ch), lambda b, st, i: (b, jnp.clip(st * pps + i - 1, 0, npairs - 1), 0)),
            rows(2 * nh2),
            pl.BlockSpec((None, 2, ngr, 4, q), lambda b, st, i: (b, pair(b, st, i), 0, 0, 0)),
            rows(d),
        ],
        out_shape=[
            jax.ShapeDtypeStruct((bsz, lt, d_inner), BF16),
            jax.ShapeDtypeStruct((bsz, lt, conv_ch), BF16),
            jax.ShapeDtypeStruct((bsz, lt, 2 * nh2), F32),
            jax.ShapeDtypeStruct((bsz, nc, ngr, 4, q), F32),
            jax.ShapeDtypeStruct((bsz, lt, d), F32),
        ],
        scratch_shapes=[pltpu.VMEM((q * ncols, 128), F32), pltpu.VMEM((q * ncols, d), F32),
                        pltpu.VMEM((2, q + 2 * (SSD_CONV // 2), conv_ch // 128, 128), F32),
                        pltpu.VMEM((2, conv_ch // 128 * (q + 8), 128), F32),
                        pltpu.VMEM((2, conv_ch // 128 * (q + 8), 128), F32)],
        compiler_params=_cparams("parallel", "arbitrary", "arbitrary"),
        name="ssd_inproj",
    )(xgrid, xall, norm_g.reshape(1, d), mods, wz, wx, wdt[:, dup], wdt.T, bias[dup].reshape(1, -1),
      bias.reshape(nh2, 1), a[dup].reshape(1, -1), a.reshape(nh2, 1), conv_w.astype(F32).reshape(SSD_CONV, conv_ch // 128, 128),
      conv_b.astype(F32).reshape(conv_ch // 128, 128))


def _ssd_scan_dir(x, bm, cm, v, cst, state, reverse):
    q = SSD_CHUNK
    hp = SSD_HEAD_DIM
    gw = x.shape[1]
    r = gw // hp
    lane = lax.broadcasted_iota(jnp.int32, (q, 2 * hp), 1)
    first = lane < hp
    bc = [jnp.broadcast_to(v[:, j:j + 1], (q, 2 * hp)) for j in range(2 * r)]

    def head_lanes(cols):
        return jnp.concatenate([jnp.where(first, cols[2 * p], cols[2 * p + 1]) for p in range(r // 2)], axis=1)

    cs_x = head_lanes(bc[:r])
    dt_x = head_lanes(bc[r:])
    end = 0 if reverse else q - 1
    cs_end = cs_x[end:end + 1, :]
    xdt = x * dt_x
    xw = (xdt * jnp.exp2(cs_end - cs_x)).astype(BF16)
    cb = lax.dot_general(cm, bm, (((1,), (1,)), ((), ())), preferred_element_type=F32)
    y_off = jnp.dot(cm, state.astype(BF16), preferred_element_type=F32) * jnp.exp2(cs_x)
    r_i = lax.broadcasted_iota(jnp.int32, (q, q), 0)
    c_i = lax.broadcasted_iota(jnp.int32, (q, q), 1)
    mask = (r_i <= c_i) if reverse else (r_i >= c_i)
    ys = []
    for pair in range(r // 2):
        xp = xdt[:, pair * 2 * hp:(pair + 1) * 2 * hp]
        rhs = jnp.concatenate([jnp.where(first, xp, 0.0), jnp.where(first, 0.0, xp)], axis=0).astype(BF16)
        gmats = []
        for h in (2 * pair, 2 * pair + 1):
            seg = jnp.exp2(jnp.where(mask, bc[h] - cst[h:h + 1, :], -1e30))
            gmats.append((cb * seg).astype(BF16))
        ys.append(jnp.dot(jnp.concatenate(gmats, axis=1), rhs, preferred_element_type=F32))
    upd = lax.dot_general(bm, xw, (((0,), (0,)), ((), ())), preferred_element_type=F32)
    return jnp.concatenate(ys, axis=1) + y_off, state * jnp.exp2(cs_end) + upd


def _ssd_scan_kernel(xf, bf, cf, csdtf, cstf, xb, bb, cb, csdtb, cstb, yf_ref, yb_ref, state_f, state_b):
    @pl.when(pl.program_id(1) == 0)
    def _():
        state_f[...] = jnp.zeros_like(state_f)
        state_b[...] = jnp.zeros_like(state_b)

    ngr, n, gw = state_f.shape
    r = gw // SSD_HEAD_DIM
    dirs = ((xf, bf, cf, csdtf, cstf, yf_ref, state_f, False, 0), (xb, bb, cb, csdtb, cstb, yb_ref, state_b, True, ngr))
    for g in range(ngr):
        for x_ref, b_ref, c_ref, csdt_ref, cst_ref, y_ref, state, reverse, lane_group0 in dirs:
            j = lane_group0 + g
            y, new_state = _ssd_scan_dir(
                x_ref[:, g * gw:(g + 1) * gw].astype(F32), b_ref[:, g * n:(g + 1) * n], c_ref[:, g * n:(g + 1) * n],
                csdt_ref[:, 2 * r * j:2 * r * (j + 1)], cst_ref[g], state[g], reverse)
            y_ref[:, g * gw:(g + 1) * gw] = y.astype(y_ref.dtype)
            state[g] = new_state


def _ssd_scan(xbc, csdt, cst, n_lat):
    bsz, lt, ch = xbc.shape
    q = SSD_CHUNK
    nc, ncl = lt // q, n_lat // q
    ngr = SSD_GROUPS
    n = SSD_STATE
    d_inner = ch - 2 * ngr * n
    gw = d_inner // ngr
    assert d_inner % (ngr * n) == 0 and q == 2 * SSD_HEAD_DIM

    def specs(chunk, direction):
        return [
            pl.BlockSpec((None, q, d_inner), lambda b, k: (b, chunk(k), 0)),
            pl.BlockSpec((None, q, ngr * n), lambda b, k: (b, chunk(k), d_inner // (ngr * n))),
            pl.BlockSpec((None, q, ngr * n), lambda b, k: (b, chunk(k), d_inner // (ngr * n) + 1)),
            pl.BlockSpec((None, q, csdt.shape[2]), lambda b, k: (b, chunk(k), 0)),
            pl.BlockSpec((None, None, ngr, 4, q), lambda b, k: (b, chunk(k), direction, 0, 0)),
        ]

    fwd = lambda k: (k + ncl) % nc
    bwd = lambda k: nc - 1 - k
    out = jax.ShapeDtypeStruct((bsz, lt, d_inner), BF16)
    return pl.pallas_call(
        _ssd_scan_kernel,
        grid=(bsz, nc),
        in_specs=specs(fwd, 0) + specs(bwd, 1),
        out_specs=[pl.BlockSpec((None, q, d_inner), lambda b, k: (b, fwd(k), 0)),
                   pl.BlockSpec((None, q, d_inner), lambda b, k: (b, bwd(k), 0))],
        out_shape=[out, out],
        scratch_shapes=[pltpu.VMEM((ngr, n, gw), F32), pltpu.VMEM((ngr, n, gw), F32)],
        compiler_params=_cparams("parallel", "arbitrary"),
        name="ssd_scan",
    )(xbc, xbc, xbc, csdt, cst, xbc, xbc, xbc, csdt, cst)


def _ssd_finish_kernel(yf_ref, yb_ref, xs_ref, z_ref, x_ref, mod_ref, dsk_ref, ng_ref, w_ref, g2_ref,
                       rw_ref, rb_ref, x3_ref, hn2_ref, gates_ref):
    m = mod_ref[...]
    nsub = ROW_SUBTILES
    sub = x_ref.shape[0] // nsub
    for s in range(nsub):
        r = slice(s * sub, (s + 1) * sub)
        y = yf_ref[r, :].astype(F32) + yb_ref[r, :].astype(F32) + dsk_ref[...] * xs_ref[r, :].astype(F32)
        gated = y * _silu(z_ref[r, :].astype(F32))
        nrm = _rms(gated, ng_ref[...])
        out = jnp.dot(nrm.astype(BF16), w_ref[...], preferred_element_type=F32)
        x3 = x_ref[r, :] + m[2:3] * out
        x3_ref[r, :] = x3
        hn2 = _rms(x3, g2_ref[...]) * (1.0 + m[4:5]) + m[3:4]
        hn2_ref[r, :] = hn2.astype(hn2_ref.dtype)
        gates_ref[:, r] = _router_gates(hn2, rw_ref, rb_ref)


def _ssd_finish(yf, yb, xbc, z, xall, n_lat, mods, d_skip, norm_g, out_w, norm2_g, router_wt, router_b):
    bsz, lt, d_inner = z.shape
    d = xall.shape[-1]
    q = TOKEN_TILE
    ncl = n_lat // q
    n_exp = router_wt.shape[0]
    inner = pl.BlockSpec((None, q, d_inner), lambda b, c: (b, c, 0))
    tok = pl.BlockSpec((None, q, d), lambda b, c: (b, c, 0))
    full = lambda s: pl.BlockSpec(s, lambda b, c: tuple(0 for _ in s))
    dsk = jnp.repeat(d_skip.astype(F32), SSD_HEAD_DIM).reshape(1, d_inner)
    return pl.pallas_call(
        _ssd_finish_kernel,
        grid=(bsz, ncl),
        in_specs=[
            inner, inner, inner, inner,
            tok,
            pl.BlockSpec((None, 6, d), lambda b, c: (b * 2, 0, 0)),
            full((1, d_inner)), full((1, d_inner)), full((d_inner, d)), full((1, d)),
            full((n_exp, d)), full((n_exp, 1)),
        ],
        out_specs=[tok, tok, pl.BlockSpec((n_exp, q), lambda b, c: (0, b * ncl + c))],
        out_shape=[
            jax.ShapeDtypeStruct((bsz, n_lat, d), F32),
            jax.ShapeDtypeStruct((bsz, n_lat, d), BF16),
            jax.ShapeDtypeStruct((n_exp, bsz * n_lat), F32),
        ],
        compiler_params=_cparams("parallel", "parallel"),
        name="ssd_finish",
    )(yf, yb, xbc, z, xall, mods, dsk, norm_g.reshape(1, d_inner), out_w.astype(BF16),
      norm2_g.reshape(1, d), router_wt, router_b.reshape(n_exp, 1))


def kernel(x, c, ctx, c_ctx, mod_w, mod_b, norm1_g, norm2_g, final_g, s5_lam_re, s5_lam_im, s5_log_dt, s5_b_re, s5_b_im, s5_c_re, s5_c_im, s5_d, s5_glu_w, s5_glu_b, ssd_in_w, ssd_conv_w, ssd_conv_b, ssd_dt_bias, ssd_a_log, ssd_d, ssd_norm_g, ssd_out_w, router_w, router_b, moe_w1, moe_w3, moe_w2):
    bsz, n_lat, d = x.shape
    n_ctx = ctx.shape[1]
    lt = n_lat + n_ctx
    n_exp = router_w.shape[1]
    assert n_lat % TOKEN_TILE == 0 and n_ctx % TOKEN_TILE == 0
    assert (bsz * lt) % MOE_BLOCK == 0 and n_lat % MOE_BLOCK == 0 and MOE_BLOCK % TOKEN_TILE == 0
    assert TOKEN_TILE % SSD_CHUNK == 0

    mods = _modulation(c, c_ctx, mod_w, mod_b)
    router_wt = router_w.T.astype(F32)
    w1, w3, w2 = moe_w1, moe_w3, moe_w2
    nlt = n_lat // TOKEN_TILE
    tpb = lt // TOKEN_TILE

    hn = _prenorm(x, ctx, norm1_g[0], mods[0], nlt)
    s5w = _s5_weights(s5_lam_re[0], s5_lam_im[0], s5_log_dt[0], s5_b_re[0], s5_b_im[0], s5_c_re[0], s5_c_im[0])
    y = _s5_scan(hn, n_lat, s5w)
    x1, hn2, gates_t = _glu_head(y, hn, x, ctx, mods[0], s5_d[0], s5_glu_w[0], s5_glu_b[0], norm2_g[0],
                                 router_wt, router_b, nlt)
    g2_lat = jnp.broadcast_to(mods[0][0::2, None, 5], (bsz, nlt, d))
    g2_ctx = jnp.broadcast_to(mods[0][1::2, None, 5], (bsz, tpb - nlt, d))
    g2rows = jnp.concatenate([g2_lat, g2_ctx], axis=1).reshape(bsz * tpb, 1, d)
    x2 = _moe(hn2.reshape(bsz * lt, d), gates_t, x1.reshape(bsz * lt, d), g2rows, w1, w3, w2, 0).reshape(bsz, lt, d)

    z, xbc, csdt, cst, x2p = _ssd_inproj(x2, n_lat, norm1_g[1], mods[1], ssd_in_w[0], ssd_dt_bias[0], ssd_a_log[0],
                                             ssd_conv_w[0], ssd_conv_b[0])
    yf, yb = _ssd_scan(xbc, csdt, cst, n_lat)
    x3, hn3, gates3_t = _ssd_finish(yf, yb, xbc, z, x2p, n_lat, mods[1], ssd_d[0], ssd_norm_g[0], ssd_out_w[0],
                                    norm2_g[1], router_wt, router_b)
    g2rows = jnp.broadcast_to(mods[1][0::2, None, 5], (bsz, nlt, d)).reshape(bsz * nlt, 1, d)
    out = _moe(hn3.reshape(bsz * n_lat, d), gates3_t, x3.reshape(bsz * n_lat, d), g2rows, w1, w3, w2, 1,
               final_g=final_g, blocks_per_batch=n_lat // MOE_BLOCK)
    return out.reshape(bsz, n_lat, d)
```

```python
import functools

import jax
import jax.numpy as jnp
from jax import lax
from jax.experimental import pallas as pl
from jax.experimental.pallas import tpu as pltpu

F32 = jnp.float32
BF16 = jnp.bfloat16
HIGHEST = lax.Precision.HIGHEST

GRID_W = 64
RMS_EPS = 1e-6
LOG2_E = 1.4426950408889634

S5_GROUP = 16
S5_STATE = 64
S5_T = 16
S5_GB = 8

SSD_HEAD_DIM = 64
SSD_GROUPS = 8
SSD_STATE = 128
SSD_CONV = 5
SSD_CHUNK = 128

N_EXPERT_GROUPS = 4
TOP_K = 2

TOKEN_TILE = 256
ROW_SUBTILES = 2
MOE_BLOCK = 512
MOE_PIECE = 16
MOE_TILE = 512
VMEM_LIMIT_BYTES = 56 * 1024 * 1024


def _cparams(*sem):
    return pltpu.CompilerParams(dimension_semantics=sem, vmem_limit_bytes=VMEM_LIMIT_BYTES)


def _sigmoid(v):
    return 1.0 / (1.0 + jnp.exp(-v))


def _silu(v):
    return v * _sigmoid(v)


def _gelu_tanh(v):
    return 0.5 * v * (1.0 + jnp.tanh(0.7978845608028654 * (v + 0.044715 * (v * v * v))))


def _rms(v, g):
    return v * lax.rsqrt(jnp.mean(v * v, axis=-1, keepdims=True) + RMS_EPS) * g


def _mod_kernel(cc_ref, w_ref, b_ref, o_ref):
    a = _silu(cc_ref[...])
    o_ref[...] = jnp.dot(a, w_ref[...], preferred_element_type=F32, precision=HIGHEST) + b_ref[...]


def _modulation(c, c_ctx, mod_w, mod_b):
    depth, d, d6 = mod_w.shape
    bsz = c.shape[0]
    rows = 8
    cc = jnp.zeros((rows, d), F32).at[:bsz].set(c).at[bsz].set(c_ctx)
    tn = d6 // 4
    out = pl.pallas_call(
        _mod_kernel,
        grid=(depth, d6 // tn),
        in_specs=[
            pl.BlockSpec((rows, d), lambda i, j: (0, 0)),
            pl.BlockSpec((None, d, tn), lambda i, j: (i, 0, j)),
            pl.BlockSpec((None, 1, tn), lambda i, j: (i, 0, j)),
        ],
        out_specs=pl.BlockSpec((None, rows, tn), lambda i, j: (i, 0, j)),
        out_shape=jax.ShapeDtypeStruct((depth, rows, d6), F32),
        compiler_params=_cparams("parallel", "parallel"),
        name="modulation",
    )(cc, mod_w, mod_b.reshape(depth, 1, d6))
    lat = out[:, :bsz].reshape(depth, bsz, 1, 6, d)
    ctx = jnp.broadcast_to(out[:, bsz].reshape(depth, 1, 1, 6, d), (depth, bsz, 1, 6, d))
    return jnp.concatenate([lat, ctx], axis=2).reshape(depth, bsz * 2, 6, d)


def _prenorm_kernel(x_ref, c_ref, g_ref, mod_ref, o_ref, *, n_lat_tiles):
    m = mod_ref[...]
    x = jnp.where(pl.program_id(1) < n_lat_tiles, x_ref[...], c_ref[...])
    hn = _rms(x, g_ref[...]) * (1.0 + m[1:2]) + m[0:1]
    o_ref[...] = hn.astype(o_ref.dtype)


def _lat_ctx_specs(n_lat_tiles, d):
    return [pl.BlockSpec((None, TOKEN_TILE, d), lambda b, i: (b, jnp.minimum(i, n_lat_tiles - 1), 0)),
            pl.BlockSpec((None, TOKEN_TILE, d), lambda b, i: (b, jnp.maximum(i - n_lat_tiles, 0), 0))]


def _prenorm(x, ctx, g, mods, n_lat_tiles):
    bsz, n_lat, d = x.shape
    lt = n_lat + ctx.shape[1]
    nt = lt // TOKEN_TILE
    return pl.pallas_call(
        functools.partial(_prenorm_kernel, n_lat_tiles=n_lat_tiles),
        grid=(bsz, nt),
        in_specs=_lat_ctx_specs(n_lat_tiles, d) + [
            pl.BlockSpec((1, d), lambda b, i: (0, 0)),
            pl.BlockSpec((None, 6, d), lambda b, i: (b * 2 + (i >= n_lat_tiles).astype(jnp.int32), 0, 0)),
        ],
        out_specs=pl.BlockSpec((None, TOKEN_TILE, d), lambda b, i: (b, i, 0)),
        out_shape=jax.ShapeDtypeStruct((bsz, lt, d), F32),
        compiler_params=_cparams("parallel", "parallel"),
        name="prenorm",
    )(x, ctx, g.reshape(1, d), mods)


def _s5_weights(lam_re, lam_im, log_dt, b_re, b_im, c_re, c_im):
    t = S5_T
    k16 = b_re.shape[-1]

    def cmul(ar, ai, br, bi):
        return ar * br - ai * bi, ar * bi + ai * br

    def direction(k):
        lr, li = lam_re[k], lam_im[k]
        step = jnp.exp(log_dt[k])[:, None]
        mag = jnp.exp(lr * step)
        abar_r = mag * jnp.cos(li * step)
        abar_i = mag * jnp.sin(li * step)
        den = lr * lr + li * li
        q_r = ((abar_r - 1.0) * lr + abar_i * li) / den
        q_i = (abar_i * lr - (abar_r - 1.0) * li) / den
        bb_r, bb_i = cmul(q_r[..., None], q_i[..., None], b_re, b_im)

        def power(tau):
            tau = jnp.asarray(tau, F32)[None, :, None]
            m = jnp.exp((lr * step)[:, None, :] * tau)
            return m * jnp.cos((li * step)[:, None, :] * tau), m * jnp.sin((li * step)[:, None, :] * tau)

        return bb_r.transpose(0, 2, 1), bb_i.transpose(0, 2, 1), power

    rows = lambda v: jnp.repeat(v, k16, axis=1)
    row_tile = lambda v: jnp.tile(v, (1, t, 1))
    cols = lambda v: jnp.repeat(v.transpose(0, 2, 1), k16, axis=2)
    col_tile = lambda v: jnp.tile(v, (1, 1, t))
    ct_r, ct_i = c_re.transpose(0, 2, 1), c_im.transpose(0, 2, 1)
    steps = jnp.arange(t)

    def left(bt_r, bt_i, power, tau):
        pr, pi = power(tau)
        return cmul(row_tile(bt_r), row_tile(bt_i), rows(pr), rows(pi))

    def right(power, tau):
        pr, pi = power(tau)
        return cmul(col_tile(ct_r), col_tile(ct_i), cols(pr), cols(pi))

    bf_r, bf_i, pow_f = direction(0)
    bb_r, bb_i, pow_b = direction(1)
    lf_r, lf_i = left(bf_r, bf_i, pow_f, -steps)
    rf_r, rf_i = right(pow_f, steps)
    lb_r, lb_i = left(bb_r, bb_i, pow_b, steps)
    rb_r, rb_i = right(pow_b, -steps)
    lf = jnp.concatenate([lf_r, lf_i], axis=-1)
    lb = jnp.concatenate([lb_r, lb_i], axis=-1)
    rf = jnp.concatenate([rf_r, -rf_i], axis=1)
    rb = jnp.concatenate([rb_r, -rb_i], axis=1)

    sf_r, sf_i = left(bf_r, bf_i, pow_f, t - 1 - steps)
    ws = jnp.concatenate([sf_r, lb_r, sf_i, lb_i], axis=-1)

    of_r, of_i = right(pow_f, steps + 1)
    ob_r, ob_i = right(pow_b, t - steps)
    zero = jnp.zeros_like(of_r)
    w2 = jnp.concatenate([of_r, zero, -of_i, zero, zero, ob_r, zero, -ob_i], axis=1)

    af_r, af_i = pow_f([t])
    ab_r, ab_i = pow_b([t])
    ar = jnp.concatenate([af_r[:, 0], ab_r[:, 0]], axis=-1)
    ai = jnp.concatenate([af_i[:, 0], ab_i[:, 0]], axis=-1)
    return ws.astype(BF16), lf, rf, lb, rb, w2.astype(BF16), ar, ai


def _s5_row_block(nc):
    return max(rb for rb in range(16, min(nc, 176) + 1, 16) if nc % rb == 0)


def _dot_split3(a, b):
    a_hi = a.astype(BF16)
    a_lo = (a - a_hi.astype(F32)).astype(BF16)
    b_hi = b.astype(BF16)
    b_lo = (b - b_hi.astype(F32)).astype(BF16)
    dot = functools.partial(jnp.dot, preferred_element_type=F32)
    return dot(a_hi, b_hi) + dot(a_hi, b_lo) + dot(a_lo, b_hi)


def _s5_kernel(hn_ref, ws_ref, lf_ref, rf_ref, lb_ref, rb_ref, w2_ref, ar_ref, ai_ref, yo_ref,
               u_ref, y_ref, wm_ref, sre, sim, hre_f, him_f, hre_b, him_b, *, n_chunks, n_ctx_chunks, pitch):
    nc, ncc = n_chunks, n_ctx_chunks
    ncl = nc - ncc
    p = S5_STATE
    t_len = S5_T
    gl = S5_GROUP
    per_half = 128 // gl
    rb = _s5_row_block(nc)
    lane_slot = lax.broadcasted_iota(jnp.int32, (rb, 128), 1) // gl

    @pl.when(pl.program_id(1) == 0)
    def _():
        tk = t_len * gl
        src_tok = lax.broadcasted_iota(jnp.int32, (tk, tk), 0) // gl
        dst_tok = lax.broadcasted_iota(jnp.int32, (tk, tk), 1) // gl
        for g in range(S5_GB):
            causal = _dot_split3(lf_ref[g], rf_ref[g])
            anti = _dot_split3(lb_ref[g], rb_ref[g])
            wm = jnp.where(dst_tok >= src_tok, causal, 0.0) + jnp.where(src_tok >= dst_tok, anti, 0.0)
            wm_ref[g] = wm.astype(wm_ref.dtype)

    def slot_transpose(xs):
        xs = list(xs)
        bit = per_half // 2
        while bit >= 1:
            upper = (lane_slot // bit) % 2 == 1
            nxt = list(xs)
            for p in range(per_half):
                if p & bit:
                    continue
                lo, hi = xs[p], xs[p + bit]
                nxt[p] = jnp.where(upper, pltpu.roll(hi, bit * gl, axis=1), lo)
                nxt[p + bit] = jnp.where(upper, hi, pltpu.roll(lo, 128 - bit * gl, axis=1))
            xs = nxt
            bit //= 2
        return xs

    def gather_u(blk, carry):
        r0 = pl.multiple_of(blk * rb, 16)
        halves = []
        for hb in range(t_len // per_half):
            a = [hn_ref[pl.ds(r0 * t_len + hb * per_half + j, rb, stride=t_len), :] for j in range(per_half)]
            halves.append(slot_transpose(a))
        for i in range(S5_GB):
            u_ref[i, pl.ds(r0, rb), :] = jnp.concatenate([h[i] for h in halves], axis=1).astype(u_ref.dtype)
        return carry

    lax.fori_loop(0, nc // rb, gather_u, 0)

    for g in range(S5_GB):
        s = jnp.dot(u_ref[g], ws_ref[g], preferred_element_type=F32)
        sre[pl.ds(g * pitch, nc), :] = s[:, : 2 * p]
        sim[pl.ds(g * pitch, nc), :] = s[:, 2 * p:]

    ar = ar_ref[...]
    ai = ai_ref[...]
    fwd_lane = lax.broadcasted_iota(jnp.int32, (S5_GB, 2 * p), 1) < p

    def step(k, carry):
        h_r, h_i = carry
        cf = jnp.where(k < ncc, ncl + k, k - ncc)
        cb = nc - 1 - k
        rows_f = pl.ds(cf, S5_GB, stride=pitch)
        rows_b = pl.ds(cb, S5_GB, stride=pitch)
        hre_f[rows_f, :] = h_r
        him_f[rows_f, :] = h_i
        hre_b[rows_b, :] = h_r
        him_b[rows_b, :] = h_i
        s_r = jnp.where(fwd_lane, sre[rows_f, :], sre[rows_b, :])
        s_i = jnp.where(fwd_lane, sim[rows_f, :], sim[rows_b, :])
        n_r = ar * h_r - ai * h_i + s_r
        n_i = ar * h_i + ai * h_r + s_i
        return n_r, n_i

    zero = jnp.zeros((S5_GB, 2 * p), F32)
    lax.fori_loop(0, nc, step, (zero, zero), unroll=2)

    for g in range(S5_GB):
        rows = pl.ds(g * pitch, nc)
        hin = jnp.concatenate([hre_f[rows, :], him_f[rows, :], hre_b[rows, :], him_b[rows, :]], axis=1)
        out = jnp.dot(u_ref[g], wm_ref[g], preferred_element_type=F32)
        out = out + jnp.dot(hin.astype(BF16), w2_ref[g], preferred_element_type=F32)
        y_ref[g] = out

    def scatter_y(blk, carry):
        r0 = pl.multiple_of(blk * rb, 16)
        for hb in range(t_len // per_half):
            yv = [y_ref[i, pl.ds(r0, rb), hb * 128:(hb + 1) * 128] for i in range(S5_GB)]
            for j, tok in enumerate(slot_transpose(yv)):
                yo_ref[pl.ds(r0 * t_len + hb * per_half + j, rb, stride=t_len), :] = tok
        return carry

    lax.fori_loop(0, nc // rb, scatter_y, 0)


def _s5_scan(hn, n_lat, weights):
    bsz, lt, d = hn.shape
    ngrp = d // S5_GROUP
    t = S5_T
    tk = t * S5_GROUP
    nc = lt // t
    ncc = (lt - n_lat) // t
    gb = S5_GB
    assert gb * S5_GROUP == 128 and (128 // S5_GROUP) == gb and t % gb == 0
    ws, lf, rf, lb, rb, w2, ar, ai = weights
    p2 = 2 * S5_STATE
    pitch = nc + 8 if (nc // 8) % 2 == 0 else nc
    kern = functools.partial(_s5_kernel, n_chunks=nc, n_ctx_chunks=ncc, pitch=pitch)
    per_group = lambda *s: pl.BlockSpec((gb,) + s, lambda gi, b: (gi,) + tuple(0 for _ in s))
    return pl.pallas_call(
        kern,
        grid=(ngrp // gb, bsz),
        in_specs=[
            pl.BlockSpec((None, lt, 128), lambda gi, b: (b, 0, gi)),
            per_group(tk, 2 * p2),
            per_group(tk, p2), per_group(p2, tk), per_group(tk, p2), per_group(p2, tk),
            per_group(4 * p2, tk),
            per_group(p2), per_group(p2),
        ],
        out_specs=pl.BlockSpec((None, lt, 128), lambda gi, b: (b, 0, gi)),
        out_shape=jax.ShapeDtypeStruct((bsz, lt, d), F32),
        scratch_shapes=[pltpu.VMEM((gb, nc, tk), BF16), pltpu.VMEM((gb, nc, tk), F32), pltpu.VMEM((gb, tk, tk), BF16)]
        + [pltpu.VMEM((gb * pitch, p2), F32) for _ in range(6)],
        compiler_params=_cparams("parallel", "arbitrary"),
        name="s5_scan",
    )(hn, ws, lf, rf, lb, rb, w2, ar, ai)


def _router_gates(hn2, rw_ref, rb_ref):
    n_exp = rw_ref.shape[0]
    epg = n_exp // N_EXPERT_GROUPS
    logits = lax.dot_general(rw_ref[...], hn2, (((1,), (1,)), ((), ())),
                             preferred_element_type=F32, precision=HIGHEST)
    s = _sigmoid(logits)
    sel = s + rb_ref[...]
    row = [sel[e:e + 1] for e in range(n_exp)]
    gscore = []
    for gi in range(N_EXPERT_GROUPS):
        a, b, c, dd = row[gi * epg: gi * epg + epg]
        hi1, lo1 = jnp.maximum(a, b), jnp.minimum(a, b)
        hi2, lo2 = jnp.maximum(c, dd), jnp.minimum(c, dd)
        gscore.append(jnp.maximum(hi1, hi2) + jnp.maximum(jnp.minimum(hi1, hi2), jnp.maximum(lo1, lo2)))
    gmax = functools.reduce(jnp.maximum, gscore)
    gates = []
    taken = None
    for gi in range(N_EXPERT_GROUPS):
        is_max = gscore[gi] == gmax
        best = is_max if taken is None else jnp.logical_and(is_max, jnp.logical_not(taken))
        taken = is_max if taken is None else jnp.logical_or(taken, is_max)
        for e in range(gi * epg, gi * epg + epg):
            rank = jnp.zeros_like(row[e])
            for j in range(gi * epg, gi * epg + epg):
                if j == e:
                    continue
                ahead = (row[j] >= row[e]) if j < e else (row[j] > row[e])
                rank = rank + ahead.astype(F32)
            chosen = jnp.logical_and(best, rank < float(TOP_K))
            gates.append(jnp.where(chosen, s[e:e + 1], 0.0))
    g = jnp.concatenate(gates, axis=0)
    return g / jnp.sum(g, axis=0, keepdims=True)


def _glu_kernel(y_ref, u_ref, x_ref, c_ref, mod_ref, d_ref, w_ref, b_ref, g2_ref, rw_ref, rb_ref,
                x1_ref, hn2_ref, gates_ref, *, n_lat_tiles):
    d = x_ref.shape[-1]
    is_lat = pl.program_id(1) < n_lat_tiles
    m = mod_ref[...]
    nsub = ROW_SUBTILES
    sub = x_ref.shape[0] // nsub
    for s in range(nsub):
        r = slice(s * sub, (s + 1) * sub)
        u = u_ref[r, :].astype(F32)
        a = _gelu_tanh(y_ref[r, :].astype(F32) + d_ref[...] * u)
        z = jnp.dot(a.astype(BF16), w_ref[...], preferred_element_type=F32) + b_ref[...]
        out = z[:, :d] * _sigmoid(z[:, d:])
        x1 = jnp.where(is_lat, x_ref[r, :], c_ref[r, :]) + m[2:3] * out
        x1_ref[r, :] = x1
        hn2 = _rms(x1, g2_ref[...]) * (1.0 + m[4:5]) + m[3:4]
        hn2_ref[r, :] = hn2.astype(hn2_ref.dtype)
        gates_ref[:, r] = _router_gates(hn2, rw_ref, rb_ref)


def _glu_head(y, hn, x, ctx, mods, d_skip, glu_w, glu_b, norm2_g, router_wt, router_b, n_lat_tiles):
    bsz, lt, d = hn.shape
    nt = lt // TOKEN_TILE
    n_exp = router_wt.shape[0]
    tok = pl.BlockSpec((None, TOKEN_TILE, d), lambda b, i: (b, i, 0))
    vec = lambda n: pl.BlockSpec((1, n), lambda b, i: (0, 0))
    return pl.pallas_call(
        functools.partial(_glu_kernel, n_lat_tiles=n_lat_tiles),
        grid=(bsz, nt),
        in_specs=[tok, tok] + _lat_ctx_specs(n_lat_tiles, d) + [
            pl.BlockSpec((None, 6, d), lambda b, i: (b * 2 + (i >= n_lat_tiles).astype(jnp.int32), 0, 0)),
            vec(d),
            pl.BlockSpec((d, 2 * d), lambda b, i: (0, 0)),
            vec(2 * d),
            vec(d),
            pl.BlockSpec((n_exp, d), lambda b, i: (0, 0)),
            pl.BlockSpec((n_exp, 1), lambda b, i: (0, 0)),
        ],
        out_specs=[
            tok, tok,
            pl.BlockSpec((n_exp, TOKEN_TILE), lambda b, i: (0, b * nt + i)),
        ],
        out_shape=[
            jax.ShapeDtypeStruct((bsz, lt, d), F32),
            jax.ShapeDtypeStruct((bsz, lt, d), BF16),
            jax.ShapeDtypeStruct((n_exp, bsz * lt), F32),
        ],
        compiler_params=_cparams("parallel", "parallel"),
        name="s5_glu_head",
    )(y, hn, x, ctx, mods, d_skip.reshape(1, d), glu_w.astype(BF16), glu_b.reshape(1, 2 * d),
      norm2_g.reshape(1, d), router_wt, router_b.reshape(n_exp, 1))


def _moe_slots_padded(total):
    return jnp.floor((total + float(MOE_PIECE - 1)) * (1.0 / MOE_PIECE)) * float(MOE_PIECE)


def _moe_sort_kernel(gt_ref, t_ref, ts_ref, cnt_ref, *, n_blocks):
    n_exp, nb = gt_ref.shape
    nbpad = ts_ref.shape[0]
    sel = jnp.logical_and(gt_ref[...] > 0.0, pl.program_id(0) < n_blocks)
    sel_b = jnp.where(sel, 1.0, 0.0).astype(BF16)
    earlier = lax.broadcasted_iota(jnp.int32, (nb, nb), 0) < lax.broadcasted_iota(jnp.int32, (nb, nb), 1)
    rank = jnp.dot(sel_b, jnp.where(earlier, 1.0, 0.0).astype(BF16), preferred_element_type=F32)
    total = jnp.sum(jnp.where(sel, 1.0, 0.0), axis=1, keepdims=True)
    padded = jnp.broadcast_to(_moe_slots_padded(total), (n_exp, 128))
    below = lax.broadcasted_iota(jnp.int32, (n_exp, n_exp), 1) < lax.broadcasted_iota(jnp.int32, (n_exp, n_exp), 0)
    offs = jnp.dot(jnp.where(below, 1.0, 0.0).astype(BF16), padded.astype(BF16), preferred_element_type=F32)[:, 0:1]
    dest = offs + rank
    d_lo_f = jnp.min(jnp.where(sel, dest, float(nbpad)), axis=0, keepdims=True)
    d_hi_f = jnp.max(jnp.where(sel, dest, -1.0), axis=0, keepdims=True)
    slot = lax.broadcasted_iota(jnp.int32, (nbpad, nb), 0)
    at_lo = jnp.where(slot == d_lo_f.astype(jnp.int32), 1.0, 0.0).astype(BF16)
    at_hi = jnp.where(slot == d_hi_f.astype(jnp.int32), 1.0, 0.0).astype(BF16)
    d = t_ref.shape[1]
    ts_ref[:, :d] = jnp.dot(at_lo + at_hi, t_ref[...], preferred_element_type=F32).astype(ts_ref.dtype)
    gts = gt_ref[...]

    def gate_rows(d_f):
        g = jnp.sum(jnp.where(jnp.logical_and(sel, dest == d_f), gts, 0.0), axis=0, keepdims=True)
        g_hi = g.astype(BF16).astype(F32)
        g_mid = (g - g_hi).astype(BF16).astype(F32)
        g_lo = g - g_hi - g_mid
        row = lax.broadcasted_iota(jnp.int32, (128, nb), 0)
        terms = jnp.where(row == 0, g_hi, jnp.where(row == 1, g_mid, jnp.where(row == 2, g_lo, 0.0)))
        return terms.astype(BF16)

    gate_cols = lax.dot_general(jnp.concatenate([at_lo, at_hi], axis=1),
                                jnp.concatenate([gate_rows(d_lo_f), gate_rows(d_hi_f)], axis=1),
                                (((1,), (1,)), ((), ())), preferred_element_type=F32)
    ts_ref[:, d:] = gate_cols.astype(ts_ref.dtype)
    cnt_ref[...] = padded


def _moe_expert_kernel(tile_e, src_rows, dst_rows, ntiles, ts_hbm, w1_ref, w3_ref, w2_ref, ys_hbm,
                       tbuf, ybuf, w1b, w3b, w2b, sem_in, sem_out):
    t = pl.program_id(0)
    nt = ntiles[0]
    last = pl.num_programs(0) - 1
    ppt = MOE_TILE // MOE_PIECE
    slot = t % 2

    def rows_at(table, tt, p):
        return pl.ds(pl.multiple_of(table[tt * ppt + p], MOE_PIECE), MOE_PIECE)

    def copy_in(tt, sl, p):
        return pltpu.make_async_copy(ts_hbm.at[rows_at(src_rows, tt, p), :],
                                     tbuf.at[sl, pl.ds(p * MOE_PIECE, MOE_PIECE), :], sem_in.at[sl])

    def copy_out(tt, sl, p):
        return pltpu.make_async_copy(ybuf.at[sl, pl.ds(p * MOE_PIECE, MOE_PIECE), :],
                                     ys_hbm.at[rows_at(dst_rows, tt, p), pl.ds(0, ybuf.shape[2])], sem_out.at[sl])

    def start_in(tt, sl):
        for p in range(ppt):
            copy_in(tt, sl, p).start()

    def wait_out(tt, sl):
        for p in range(ppt):
            copy_out(tt, sl, p).wait()

    @pl.when(jnp.logical_and(t == 0, nt > 0))
    def _():
        start_in(0, 0)

    @pl.when(t + 1 < nt)
    def _():
        start_in(t + 1, 1 - slot)

    @pl.when(jnp.logical_and(t >= 2, t - 2 < nt))
    def _():
        wait_out(t - 2, slot)

    @pl.when(t < nt)
    def _():
        @pl.when(jnp.logical_or(t == 0, tile_e[t] != tile_e[jnp.maximum(t - 1, 0)]))
        def _():
            w1b[...] = w1_ref[...].astype(BF16)
            w3b[...] = w3_ref[...].astype(BF16)
            w2b[...] = w2_ref[...].astype(BF16)

        for p in range(ppt):
            copy_in(t, slot, p).wait()
        d = ybuf.shape[2]
        x = tbuf[slot, :, :d]
        gate = jnp.sum(tbuf[slot, :, d:].astype(F32), axis=1, keepdims=True)
        h = _silu(jnp.dot(x, w1b[...], preferred_element_type=F32)) * jnp.dot(x, w3b[...], preferred_element_type=F32)
        ybuf[slot] = (gate * jnp.dot(h.astype(BF16), w2b[...], preferred_element_type=F32)).astype(ybuf.dtype)
        for p in range(ppt):
            copy_out(t, slot, p).start()

    @pl.when(t == last)
    def _():
        @pl.when(jnp.logical_and(last >= 1, last - 1 < nt))
        def _():
            wait_out(last - 1, 1 - slot)

        @pl.when(last < nt)
        def _():
            wait_out(last, slot)


def _moe_unsort_kernel(ys_ref, g_ref, x_ref, g2_ref, *rest, final):
    if final:
        fg_ref, o_ref = rest
    else:
        (o_ref,) = rest
    nb, n_exp = g_ref.shape
    nbpad = ys_ref.shape[0]
    gates = g_ref[...]
    sel = gates > 0.0
    sel_b = jnp.where(sel, 1.0, 0.0).astype(BF16)
    earlier = lax.broadcasted_iota(jnp.int32, (nb, nb), 1) < lax.broadcasted_iota(jnp.int32, (nb, nb), 0)
    rank = jnp.dot(jnp.where(earlier, 1.0, 0.0).astype(BF16), sel_b, preferred_element_type=F32)
    total = jnp.sum(jnp.where(sel, 1.0, 0.0), axis=0, keepdims=True)
    padded = jnp.broadcast_to(_moe_slots_padded(total), (8, n_exp))
    below = lax.broadcasted_iota(jnp.int32, (n_exp, n_exp), 0) < lax.broadcasted_iota(jnp.int32, (n_exp, n_exp), 1)
    offs = jnp.dot(padded.astype(BF16), jnp.where(below, 1.0, 0.0).astype(BF16), preferred_element_type=F32)[0:1]
    dest = offs + rank
    d_lo = jnp.min(jnp.where(sel, dest, float(nbpad)), axis=1, keepdims=True).astype(jnp.int32)
    d_hi = jnp.max(jnp.where(sel, dest, -1.0), axis=1, keepdims=True).astype(jnp.int32)
    slot = lax.broadcasted_iota(jnp.int32, (nb, nbpad), 1)
    pick = jnp.where(jnp.logical_or(slot == d_lo, slot == d_hi), 1.0, 0.0).astype(BF16)
    moe = jnp.dot(pick, ys_ref[...], preferred_element_type=F32)
    half = nb // g2_ref.shape[0]
    d = x_ref.shape[1]
    for j in range(g2_ref.shape[0]):
        r = slice(j * half, (j + 1) * half)
        xn = x_ref[r, :] + g2_ref[j] * moe[r]
        if final:
            xn = _rms(xn, fg_ref[...])
            for c in range(half // SSD_CHUNK):
                col = (j * half) // SSD_CHUNK + c
                o_ref[:, col * d:(col + 1) * d] = xn[c * SSD_CHUNK:(c + 1) * SSD_CHUNK]
        else:
            o_ref[r, :] = xn


def _moe_schedule(counts, nbpad, n_tiles):
    nblk, n_exp = counts.shape
    ppt = MOE_TILE // MOE_PIECE
    pc = counts // MOE_PIECE
    loc = jnp.cumsum(pc, axis=1) - pc
    cum_b = jnp.cumsum(pc, axis=0)
    np_e = cum_b[-1]
    tiles_e = (np_e + ppt - 1) // ppt
    tile_end = jnp.cumsum(tiles_e)
    ntiles = tile_end[-1]
    t_idx = jnp.arange(n_tiles, dtype=jnp.int32)
    tile_e = jnp.minimum(jnp.sum((tile_end[None, :] <= t_idx[:, None]).astype(jnp.int32), axis=1), n_exp - 1)
    first = (tile_end - tiles_e)[tile_e]
    piece0 = (t_idx - first) * ppt
    npieces = jnp.where(t_idx < ntiles, jnp.clip(np_e[tile_e] - piece0, 0, ppt), 0)
    i = piece0[:, None] + jnp.arange(ppt, dtype=jnp.int32)[None, :]
    cum_t = cum_b.T[tile_e]
    blk = jnp.minimum(jnp.sum((cum_t[:, None, :] <= i[:, :, None]).astype(jnp.int32), axis=2), nblk - 1)
    before = jnp.take_along_axis(cum_t - pc.T[tile_e], blk, axis=1)
    within = i - before + jnp.take_along_axis(loc.T[tile_e], blk, axis=1)
    rows = blk * nbpad + within * MOE_PIECE
    real = jnp.arange(ppt)[None, :] < npieces[:, None]
    spare = nblk * nbpad
    piece = jnp.arange(ppt, dtype=jnp.int32)[None, :]
    src = jnp.where(real, rows, spare + piece * MOE_PIECE)
    dst = jnp.where(real, rows, spare + (ppt + (t_idx[:, None] % 2) * ppt + piece) * MOE_PIECE)
    return (tile_e.astype(jnp.int32), src.reshape(-1).astype(jnp.int32), dst.reshape(-1).astype(jnp.int32),
            ntiles.reshape(1).astype(jnp.int32))


def _moe(t, gates_t, xres, g2rows, w1, w3, w2, layer, *, final_g=None, blocks_per_batch=None):
    n, d = t.shape
    _, n_exp, _, f = w1.shape
    nb = MOE_BLOCK
    nblk = n // nb
    nbpad = TOP_K * nb + n_exp * MOE_PIECE
    final = final_g is not None

    ppt = MOE_TILE // MOE_PIECE
    nspare = -(-3 * ppt * MOE_PIECE // nbpad)
    last_blk = lambda j: jnp.minimum(j, nblk - 1)
    ts, cnt = pl.pallas_call(
        functools.partial(_moe_sort_kernel, n_blocks=nblk),
        grid=(nblk + nspare,),
        in_specs=[pl.BlockSpec((n_exp, nb), lambda j: (0, last_blk(j))), pl.BlockSpec((nb, d), lambda j: (last_blk(j), 0))],
        out_specs=[pl.BlockSpec((nbpad, d + 128), lambda j: (j, 0)), pl.BlockSpec((None, n_exp, 128), lambda j: (j, 0, 0))],
        out_shape=[jax.ShapeDtypeStruct(((nblk + nspare) * nbpad, d + 128), BF16),
                   jax.ShapeDtypeStruct((nblk + nspare, n_exp, 128), F32)],
        compiler_params=_cparams("parallel"),
        name="moe_sort",
    )(gates_t, t)

    n_tiles = nblk * nbpad // MOE_TILE + n_exp
    tile_e, src_rows, dst_rows, ntiles = _moe_schedule(cnt[:nblk, :, 0].astype(jnp.int32), nbpad, n_tiles)
    wspec = lambda shape: pl.BlockSpec((None, None) + shape, lambda i, te, sr, dr, nt: (layer, te[i], 0, 0))
    ys = pl.pallas_call(
        _moe_expert_kernel,
        grid_spec=pltpu.PrefetchScalarGridSpec(
            num_scalar_prefetch=4,
            grid=(n_tiles,),
            in_specs=[pl.BlockSpec(memory_space=pl.ANY), wspec((d, f)), wspec((d, f)), wspec((f, d))],
            out_specs=pl.BlockSpec(memory_space=pl.ANY),
            scratch_shapes=[pltpu.VMEM((2, MOE_TILE, d + 128), BF16), pltpu.VMEM((2, MOE_TILE, d), BF16),
                            pltpu.VMEM((d, f), BF16), pltpu.VMEM((d, f), BF16), pltpu.VMEM((f, d), BF16),
                            pltpu.SemaphoreType.DMA((2,)), pltpu.SemaphoreType.DMA((2,))],
        ),
        out_shape=jax.ShapeDtypeStruct(((nblk + nspare) * nbpad, d + 128), BF16),
        input_output_aliases={4: 0},
        compiler_params=_cparams("arbitrary"),
        name="moe_experts",
    )(tile_e, src_rows, dst_rows, ntiles, ts, w1, w3, w2)

    halves = nb // TOKEN_TILE
    in_specs = [
        pl.BlockSpec((nbpad, d), lambda j: (j, 0)),
        pl.BlockSpec((nb, n_exp), lambda j: (j, 0)),
        pl.BlockSpec((nb, d), lambda j: (j, 0)),
        pl.BlockSpec((halves, 1, d), lambda j: (j, 0, 0)),
    ]
    args = [ys, gates_t.T, xres, g2rows]
    if final:
        in_specs.append(pl.BlockSpec((1, d), lambda j: (0, 0)))
        args.append(final_g.reshape(1, d))
        cols = nb // SSD_CHUNK
        out_spec = pl.BlockSpec((None, SSD_CHUNK, cols * d), lambda j: (j // blocks_per_batch, 0, j % blocks_per_batch))
        out_shape = jax.ShapeDtypeStruct((nblk // blocks_per_batch, SSD_CHUNK, GRID_W * d), F32)
    else:
        out_spec = pl.BlockSpec((nb, d), lambda j: (j, 0))
        out_shape = jax.ShapeDtypeStruct((n, d), F32)
    return pl.pallas_call(
        functools.partial(_moe_unsort_kernel, final=final),
        grid=(nblk,),
        in_specs=in_specs,
        out_specs=out_spec,
        out_shape=out_shape,
        compiler_params=_cparams("parallel"),
        name="moe_unsort_final" if final else "moe_unsort",
    )(*args)


SSD_COLS_PER_STEP = 8


def _ssd_inproj_kernel(xl_ref, xc_ref, g_ref, mod_ref, wz_ref, wx_ref, wdt_ref, wdtt_ref, bias_ref, biast_ref,
                       a_ref, at_ref, cw_ref, cb_ref, z_ref, xbc_ref, csdt_ref, cst_ref, xp_ref, slab, xt, ext, slabs, slabs_in,
                       *, n_lat_steps, n_ctx_chunks):
    step = pl.program_id(1)
    ip = pl.program_id(2)
    q = SSD_CHUNK
    ncols = SSD_COLS_PER_STEP
    d = xt.shape[1]
    is_lat = step < n_lat_steps
    c0 = step * ncols + 2 * ip
    n_lat_chunks = n_lat_steps * ncols
    n_chunks = n_lat_chunks + n_ctx_chunks
    halo = SSD_CONV // 2
    nsl = ext.shape[2]
    pitch = slabs.shape[1] // nsl

    def emit_conv(sub):
        w = [cw_ref[k] for k in range(SSD_CONV)]
        bias = cb_ref[...]
        for t in range(q):
            acc = bias
            for k in range(SSD_CONV):
                acc = acc + w[k] * ext[sub, t + k]
            y = _silu(acc)
            for a in range(nsl // 8):
                slabs[sub, pl.ds(a * 8 * pitch + t, 8, stride=pitch), :] = y[a * 8:(a + 1) * 8]
        for s in range(nsl):
            xbc_ref[sub * q:(sub + 1) * q, s * 128:(s + 1) * 128] = (
                slabs[sub, s * pitch:s * pitch + q, :].astype(xbc_ref.dtype))

    def project(sub):
        other = 1 - sub
        rows = slice(sub * q, (sub + 1) * q)
        m = mod_ref[...]
        x = xt[pl.ds(pl.multiple_of((2 * ip + sub) * q, q), q), :]
        xp_ref[rows, :] = x
        hn = (_rms(x, g_ref[...]) * (1.0 + m[1:2]) + m[0:1]).astype(BF16)
        z_ref[rows, :] = jnp.dot(hn, wz_ref[...], preferred_element_type=F32).astype(z_ref.dtype)
        xbc_new = jnp.dot(hn, wx_ref[...], preferred_element_type=F32)
        for s in range(nsl):
            slabs_in[sub, s * pitch:s * pitch + q, :] = xbc_new[:, s * 128:(s + 1) * 128]
        for t in range(q):
            for a in range(nsl // 8):
                ext[sub, halo + t, a * 8:(a + 1) * 8, :] = slabs_in[sub, pl.ds(a * 8 * pitch + t, 8, stride=pitch), :]
        if sub == 0:
            starts = jnp.logical_or(c0 == 0, c0 == n_lat_chunks)
            ext[other, halo + q:2 * halo + q] = jnp.where(starts, 0.0, ext[sub, halo:2 * halo])
            ext[sub, 0:halo] = jnp.where(starts, 0.0, ext[other, q:q + halo])
        else:
            ext[other, halo + q:2 * halo + q] = ext[sub, halo:2 * halo]
            ext[sub, 0:halo] = ext[other, q:q + halo]

        def softplus(v):
            return jnp.maximum(v, 0.0) + jnp.log(1.0 + jnp.exp(-jnp.abs(v)))

        r_i = lax.broadcasted_iota(jnp.int32, (q, q), 0)
        c_i = lax.broadcasted_iota(jnp.int32, (q, q), 1)
        lower = (r_i >= c_i).astype(F32)
        upper = (r_i <= c_i).astype(F32)
        dt = softplus(jnp.dot(hn, wdt_ref[...], preferred_element_type=F32) + bias_ref[...])
        da = dt * a_ref[...]
        half = da.shape[1] // 2
        cs = jnp.concatenate([jnp.dot(lower, da[:, :half], preferred_element_type=F32, precision=HIGHEST),
                              jnp.dot(upper, da[:, half:], preferred_element_type=F32, precision=HIGHEST)], axis=1)
        lane = lax.broadcasted_iota(jnp.int32, dt.shape, 1)
        csdt_ref[rows, :] = jnp.where(lane % 8 < 4, cs, dt)
        nh = at_ref.shape[0] // 2
        dtt = softplus(lax.dot_general(wdtt_ref[...], hn, (((1,), (1,)), ((), ())), preferred_element_type=F32)
                       + biast_ref[...])
        dat = dtt * at_ref[...]
        cst_f = jnp.dot(dat[:nh], upper, preferred_element_type=F32, precision=HIGHEST)
        cst_b = jnp.dot(dat[nh:], lower, preferred_element_type=F32, precision=HIGHEST)
        cst = jnp.concatenate([cst_f, cst_b], axis=0)
        for j in range(cst_ref.shape[1]):
            cst_ref[sub, j] = cst[j * 4:(j + 1) * 4, :]

    @pl.when(jnp.logical_and(ip == 0, is_lat))
    def _():
        for s in range(d // 128):
            slab[...] = xl_ref[:, :, s * 128:(s + 1) * 128].reshape(q * ncols, 128)
            for w in range(ncols):
                xt[w * q:(w + 1) * q, s * 128:(s + 1) * 128] = slab[pl.ds(w, q, stride=ncols), :]

    @pl.when(jnp.logical_and(ip == 0, jnp.logical_not(is_lat)))
    def _():
        xt[0:n_ctx_chunks * q, :] = xc_ref[...]

    @pl.when(c0 == 0)
    def _():
        ext[...] = jnp.zeros_like(ext)

    @pl.when(jnp.logical_or(is_lat, 2 * ip < n_ctx_chunks))
    def _():
        for sub in range(2):
            emit_conv(sub)
            project(sub)

    @pl.when(c0 == n_chunks)
    def _():
        ext[1, halo + q:2 * halo + q] = jnp.zeros((halo,) + ext.shape[2:], F32)
        for sub in range(2):
            emit_conv(sub)


def _ssd_inproj(xall, n_lat, norm_g, mods, in_w, dt_bias, a_log, conv_w, conv_b):
    bsz, lt, d = xall.shape
    q = SSD_CHUNK
    ncols = SSD_COLS_PER_STEP
    ncl = n_lat // q
    nc = lt // q
    ncc = nc - ncl
    n_ctx = lt - n_lat
    nh2 = dt_bias.size
    d_inner = (nh2 // 2) * SSD_HEAD_DIM
    conv_ch = in_w.shape[1] - d_inner - nh2
    wz = in_w[:, :d_inner].astype(BF16)
    wx = in_w[:, d_inner:d_inner + conv_ch].astype(BF16)
    wdt = in_w[:, d_inner + conv_ch:].astype(BF16)
    a = -jnp.exp(a_log.astype(F32)).reshape(nh2) * LOG2_E
    bias = dt_bias.astype(F32).reshape(nh2)
    ngr = nh2 // 4
    lanes = jnp.arange(2 * nh2)
    dup = (lanes // 8) * 4 + lanes % 4
    assert n_lat // GRID_W == q and GRID_W % ncols == 0 and lt % GRID_W == 0 and ncc + 2 <= ncols and ncc % 2 == 0 and n_lat % n_ctx == 0
    nls = ncl // ncols
    xgrid = xall.reshape(bsz, lt // GRID_W, GRID_W, d)
    kern = functools.partial(_ssd_inproj_kernel, n_lat_steps=nls, n_ctx_chunks=ncc)
    full = lambda s: pl.BlockSpec(s, lambda b, st, i: tuple(0 for _ in s), pipeline_mode=pl.Buffered(1))
    pps = ncols // 2
    npairs = nc // 2
    pair = lambda b, st, i: jnp.minimum(st * pps + i, npairs - 1)
    rows = lambda width: pl.BlockSpec((None, 2 * q, width), lambda b, st, i: (b, pair(b, st, i), 0))
    return pl.pallas_call(
        kern,
        grid=(bsz, nls + 1, pps),
        in_specs=[
            pl.BlockSpec((None, q, ncols, d), lambda b, st, i: (b, 0, jnp.minimum(st, nls - 1), 0)),
            pl.BlockSpec((None, n_ctx, d), lambda b, st, i: (b, n_lat // n_ctx, 0)),
            full((1, d)),
            pl.BlockSpec((None, 6, d), lambda b, st, i: (b * 2 + (st >= nls).astype(jnp.int32), 0, 0)),
            full((d, d_inner)), full((d, conv_ch)), full((d, 2 * nh2)), full((nh2, d)),
            full((1, 2 * nh2)), full((nh2, 1)), full((1, 2 * nh2)), full((nh2, 1)),
            full((SSD_CONV, conv_ch // 128, 128)), full((conv_ch // 128, 128)),
        ],
        out_specs=[
            rows(d_inner),
            pl.BlockSpec((None, 2 * q, conv_ch), lambda b, st, i: (b, jnp.clip(st * pps + i - 1, 0, npairs - 1), 0)),
            rows(2 * nh2),
            pl.BlockSpec((None, 2, ngr, 4, q), lambda b, st, i: (b, pair(b, st, i), 0, 0, 0)),
            rows(d),
        ],
        out_shape=[
            jax.ShapeDtypeStruct((bsz, lt, d_inner), BF16),
            jax.ShapeDtypeStruct((bsz, lt, conv_ch), BF16),
            jax.ShapeDtypeStruct((bsz, lt, 2 * nh2), F32),
            jax.ShapeDtypeStruct((bsz, nc, ngr, 4, q), F32),
            jax.ShapeDtypeStruct((bsz, lt, d), F32),
        ],
        scratch_shapes=[pltpu.VMEM((q * ncols, 128), F32), pltpu.VMEM((q * ncols, d), F32),
                        pltpu.VMEM((2, q + 2 * (SSD_CONV // 2), conv_ch // 128, 128), F32),
                        pltpu.VMEM((2, conv_ch // 128 * (q + 8), 128), F32),
                        pltpu.VMEM((2, conv_ch // 128 * (q + 8), 128), F32)],
        compiler_params=_cparams("parallel", "arbitrary", "arbitrary"),
        name="ssd_inproj",
    )(xgrid, xall, norm_g.reshape(1, d), mods, wz, wx, wdt[:, dup], wdt.T, bias[dup].reshape(1, -1),
      bias.reshape(nh2, 1), a[dup].reshape(1, -1), a.reshape(nh2, 1), conv_w.astype(F32).reshape(SSD_CONV, conv_ch // 128, 128),
      conv_b.astype(F32).reshape(conv_ch // 128, 128))


def _ssd_scan_dir(x, bm, cm, v, cst, state, reverse):
    q = SSD_CHUNK
    hp = SSD_HEAD_DIM
    gw = x.shape[1]
    r = gw // hp
    lane = lax.broadcasted_iota(jnp.int32, (q, 2 * hp), 1)
    first = lane < hp
    bc = [jnp.broadcast_to(v[:, j:j + 1], (q, 2 * hp)) for j in range(2 * r)]

    def head_lanes(cols):
        return jnp.concatenate([jnp.where(first, cols[2 * p], cols[2 * p + 1]) for p in range(r // 2)], axis=1)

    cs_x = head_lanes(bc[:r])
    dt_x = head_lanes(bc[r:])
    end = 0 if reverse else q - 1
    cs_end = cs_x[end:end + 1, :]
    xdt = x * dt_x
    xw = (xdt * jnp.exp2(cs_end - cs_x)).astype(BF16)
    cb = lax.dot_general(cm, bm, (((1,), (1,)), ((), ())), preferred_element_type=F32)
    y_off = jnp.dot(cm, state.astype(BF16), preferred_element_type=F32) * jnp.exp2(cs_x)
    r_i = lax.broadcasted_iota(jnp.int32, (q, q), 0)
    c_i = lax.broadcasted_iota(jnp.int32, (q, q), 1)
    mask = (r_i <= c_i) if reverse else (r_i >= c_i)
    ys = []
    for pair in range(r // 2):
        xp = xdt[:, pair * 2 * hp:(pair + 1) * 2 * hp]
        rhs = jnp.concatenate([jnp.where(first, xp, 0.0), jnp.where(first, 0.0, xp)], axis=0).astype(BF16)
        gmats = []
        for h in (2 * pair, 2 * pair + 1):
            seg = jnp.exp2(jnp.where(mask, bc[h] - cst[h:h + 1, :], -1e30))
            gmats.append((cb * seg).astype(BF16))
        ys.append(jnp.dot(jnp.concatenate(gmats, axis=1), rhs, preferred_element_type=F32))
    upd = lax.dot_general(bm, xw, (((0,), (0,)), ((), ())), preferred_element_type=F32)
    return jnp.concatenate(ys, axis=1) + y_off, state * jnp.exp2(cs_end) + upd


def _ssd_scan_kernel(xf, bf, cf, csdtf, cstf, xb, bb, cb, csdtb, cstb, yf_ref, yb_ref, state_f, state_b):
    @pl.when(pl.program_id(1) == 0)
    def _():
        state_f[...] = jnp.zeros_like(state_f)
        state_b[...] = jnp.zeros_like(state_b)

    ngr, n, gw = state_f.shape
    r = gw // SSD_HEAD_DIM
    dirs = ((xf, bf, cf, csdtf, cstf, yf_ref, state_f, False, 0), (xb, bb, cb, csdtb, cstb, yb_ref, state_b, True, ngr))
    for g in range(ngr):
        for x_ref, b_ref, c_ref, csdt_ref, cst_ref, y_ref, state, reverse, lane_group0 in dirs:
            j = lane_group0 + g
            y, new_state = _ssd_scan_dir(
                x_ref[:, g * gw:(g + 1) * gw].astype(F32), b_ref[:, g * n:(g + 1) * n], c_ref[:, g * n:(g + 1) * n],
                csdt_ref[:, 2 * r * j:2 * r * (j + 1)], cst_ref[g], state[g], reverse)
            y_ref[:, g * gw:(g + 1) * gw] = y.astype(y_ref.dtype)
            state[g] = new_state


def _ssd_scan(xbc, csdt, cst, n_lat):
    bsz, lt, ch = xbc.shape
    q = SSD_CHUNK
    nc, ncl = lt // q, n_lat // q
    ngr = SSD_GROUPS
    n = SSD_STATE
    d_inner = ch - 2 * ngr * n
    gw = d_inner // ngr
    assert d_inner % (ngr * n) == 0 and q == 2 * SSD_HEAD_DIM

    def specs(chunk, direction):
        return [
            pl.BlockSpec((None, q, d_inner), lambda b, k: (b, chunk(k), 0)),
            pl.BlockSpec((None, q, ngr * n), lambda b, k: (b, chunk(k), d_inner // (ngr * n))),
            pl.BlockSpec((None, q, ngr * n), lambda b, k: (b, chunk(k), d_inner // (ngr * n) + 1)),
            pl.BlockSpec((None, q, csdt.shape[2]), lambda b, k: (b, chunk(k), 0)),
            pl.BlockSpec((None, None, ngr, 4, q), lambda b, k: (b, chunk(k), direction, 0, 0)),
        ]

    fwd = lambda k: (k + ncl) % nc
    bwd = lambda k: nc - 1 - k
    out = jax.ShapeDtypeStruct((bsz, lt, d_inner), BF16)
    return pl.pallas_call(
        _ssd_scan_kernel,
        grid=(bsz, nc),
        in_specs=specs(fwd, 0) + specs(bwd, 1),
        out_specs=[pl.BlockSpec((None, q, d_inner), lambda b, k: (b, fwd(k), 0)),
                   pl.BlockSpec((None, q, d_inner), lambda b, k: (b, bwd(k), 0))],
        out_shape=[out, out],
        scratch_shapes=[pltpu.VMEM((ngr, n, gw), F32), pltpu.VMEM((ngr, n, gw), F32)],
        compiler_params=_cparams("parallel", "arbitrary"),
        name="ssd_scan",
    )(xbc, xbc, xbc, csdt, cst, xbc, xbc, xbc, csdt, cst)


def _ssd_finish_kernel(yf_ref, yb_ref, xs_ref, z_ref, x_ref, mod_ref, dsk_ref, ng_ref, w_ref, g2_ref,
                       rw_ref, rb_ref, x3_ref, hn2_ref, gates_ref):
    m = mod_ref[...]
    nsub = ROW_SUBTILES
    sub = x_ref.shape[0] // nsub
    for s in range(nsub):
        r = slice(s * sub, (s + 1) * sub)
        y = yf_ref[r, :].astype(F32) + yb_ref[r, :].astype(F32) + dsk_ref[...] * xs_ref[r, :].astype(F32)
        gated = y * _silu(z_ref[r, :].astype(F32))
        nrm = _rms(gated, ng_ref[...])
        out = jnp.dot(nrm.astype(BF16), w_ref[...], preferred_element_type=F32)
        x3 = x_ref[r, :] + m[2:3] * out
        x3_ref[r, :] = x3
        hn2 = _rms(x3, g2_ref[...]) * (1.0 + m[4:5]) + m[3:4]
        hn2_ref[r, :] = hn2.astype(hn2_ref.dtype)
        gates_ref[:, r] = _router_gates(hn2, rw_ref, rb_ref)


def _ssd_finish(yf, yb, xbc, z, xall, n_lat, mods, d_skip, norm_g, out_w, norm2_g, router_wt, router_b):
    bsz, lt, d_inner = z.shape
    d = xall.shape[-1]
    q = TOKEN_TILE
    ncl = n_lat // q
    n_exp = router_wt.shape[0]
    inner = pl.BlockSpec((None, q, d_inner), lambda b, c: (b, c, 0))
    tok = pl.BlockSpec((None, q, d), lambda b, c: (b, c, 0))
    full = lambda s: pl.BlockSpec(s, lambda b, c: tuple(0 for _ in s))
    dsk = jnp.repeat(d_skip.astype(F32), SSD_HEAD_DIM).reshape(1, d_inner)
    return pl.pallas_call(
        _ssd_finish_kernel,
        grid=(bsz, ncl),
        in_specs=[
            inner, inner, inner, inner,
            tok,
            pl.BlockSpec((None, 6, d), lambda b, c: (b * 2, 0, 0)),
            full((1, d_inner)), full((1, d_inner)), full((d_inner, d)), full((1, d)),
            full((n_exp, d)), full((n_exp, 1)),
        ],
        out_specs=[tok, tok, pl.BlockSpec((n_exp, q), lambda b, c: (0, b * ncl + c))],
        out_shape=[
            jax.ShapeDtypeStruct((bsz, n_lat, d), F32),
            jax.ShapeDtypeStruct((bsz, n_lat, d), BF16),
            jax.ShapeDtypeStruct((n_exp, bsz * n_lat), F32),
        ],
        compiler_params=_cparams("parallel", "parallel"),
        name="ssd_finish",
    )(yf, yb, xbc, z, xall, mods, dsk, norm_g.reshape(1, d_inner), out_w.astype(BF16),
      norm2_g.reshape(1, d), router_wt, router_b.reshape(n_exp, 1))


def kernel(x, c, ctx, c_ctx, mod_w, mod_b, norm1_g, norm2_g, final_g, s5_lam_re, s5_lam_im, s5_log_dt, s5_b_re, s5_b_im, s5_c_re, s5_c_im, s5_d, s5_glu_w, s5_glu_b, ssd_in_w, ssd_conv_w, ssd_conv_b, ssd_dt_bias, ssd_a_log, ssd_d, ssd_norm_g, ssd_out_w, router_w, router_b, moe_w1, moe_w3, moe_w2):
    bsz, n_lat, d = x.shape
    n_ctx = ctx.shape[1]
    lt = n_lat + n_ctx
    n_exp = router_w.shape[1]
    assert n_lat % TOKEN_TILE == 0 and n_ctx % TOKEN_TILE == 0
    assert (bsz * lt) % MOE_BLOCK == 0 and n_lat % MOE_BLOCK == 0 and MOE_BLOCK % TOKEN_TILE == 0
    assert TOKEN_TILE % SSD_CHUNK == 0

    mods = _modulation(c, c_ctx, mod_w, mod_b)
    router_wt = router_w.T.astype(F32)
    w1, w3, w2 = moe_w1, moe_w3, moe_w2
    nlt = n_lat // TOKEN_TILE
    tpb = lt // TOKEN_TILE

    hn = _prenorm(x, ctx, norm1_g[0], mods[0], nlt)
    s5w = _s5_weights(s5_lam_re[0], s5_lam_im[0], s5_log_dt[0], s5_b_re[0], s5_b_im[0], s5_c_re[0], s5_c_im[0])
    y = _s5_scan(hn, n_lat, s5w)
    x1, hn2, gates_t = _glu_head(y, hn, x, ctx, mods[0], s5_d[0], s5_glu_w[0], s5_glu_b[0], norm2_g[0],
                                 router_wt, router_b, nlt)
    g2_lat = jnp.broadcast_to(mods[0][0::2, None, 5], (bsz, nlt, d))
    g2_ctx = jnp.broadcast_to(mods[0][1::2, None, 5], (bsz, tpb - nlt, d))
    g2rows = jnp.concatenate([g2_lat, g2_ctx], axis=1).reshape(bsz * tpb, 1, d)
    x2 = _moe(hn2.reshape(bsz * lt, d), gates_t, x1.reshape(bsz * lt, d), g2rows, w1, w3, w2, 0).reshape(bsz, lt, d)

    z, xbc, csdt, cst, x2p = _ssd_inproj(x2, n_lat, norm1_g[1], mods[1], ssd_in_w[0], ssd_dt_bias[0], ssd_a_log[0],
                                             ssd_conv_w[0], ssd_conv_b[0])
    yf, yb = _ssd_scan(xbc, csdt, cst, n_lat)
    x3, hn3, gates3_t = _ssd_finish(yf, yb, xbc, z, x2p, n_lat, mods[1], ssd_d[0], ssd_norm_g[0], ssd_out_w[0],
                                    norm2_g[1], router_wt, router_b)
    g2rows = jnp.broadcast_to(mods[1][0::2, None, 5], (bsz, nlt, d)).reshape(bsz * nlt, 1, d)
    out = _moe(hn3.reshape(bsz * n_lat, d), gates3_t, x3.reshape(bsz * n_lat, d), g2rows, w1, w3, w2, 1,
               final_g=final_g, blocks_per_batch=n_lat // MOE_BLOCK)
    return out.reshape(bsz, n_lat, d)
```

```python
import functools

import jax
import jax.numpy as jnp
from jax import lax
from jax.experimental import pallas as pl
from jax.experimental.pallas import tpu as pltpu

F32 = jnp.float32
BF16 = jnp.bfloat16
HIGHEST = lax.Precision.HIGHEST

GRID_W = 64
RMS_EPS = 1e-6
LOG2_E = 1.4426950408889634

S5_GROUP = 16
S5_STATE = 64
S5_T = 16
S5_GB = 8

SSD_HEAD_DIM = 64
SSD_GROUPS = 8
SSD_STATE = 128
SSD_CONV = 5
SSD_CHUNK = 128

N_EXPERT_GROUPS = 4
TOP_K = 2

TOKEN_TILE = 256
ROW_SUBTILES = 2
MOE_BLOCK = 512
MOE_PIECE = 16
MOE_TILE = 512
VMEM_LIMIT_BYTES = 56 * 1024 * 1024


def _cparams(*sem):
    return pltpu.CompilerParams(dimension_semantics=sem, vmem_limit_bytes=VMEM_LIMIT_BYTES)


def _sigmoid(v):
    return 1.0 / (1.0 + jnp.exp(-v))


def _silu(v):
    return v * _sigmoid(v)


def _gelu_tanh(v):
    return 0.5 * v * (1.0 + jnp.tanh(0.7978845608028654 * (v + 0.044715 * (v * v * v))))


def _rms(v, g):
    return v * lax.rsqrt(jnp.mean(v * v, axis=-1, keepdims=True) + RMS_EPS) * g


def _mod_kernel(cc_ref, w_ref, b_ref, o_ref):
    a = _silu(cc_ref[...])
    o_ref[...] = jnp.dot(a, w_ref[...], preferred_element_type=F32, precision=HIGHEST) + b_ref[...]


def _modulation(c, c_ctx, mod_w, mod_b):
    depth, d, d6 = mod_w.shape
    bsz = c.shape[0]
    rows = 8
    cc = jnp.zeros((rows, d), F32).at[:bsz].set(c).at[bsz].set(c_ctx)
    tn = d6 // 4
    out = pl.pallas_call(
        _mod_kernel,
        grid=(depth, d6 // tn),
        in_specs=[
            pl.BlockSpec((rows, d), lambda i, j: (0, 0)),
            pl.BlockSpec((None, d, tn), lambda i, j: (i, 0, j)),
            pl.BlockSpec((None, 1, tn), lambda i, j: (i, 0, j)),
        ],
        out_specs=pl.BlockSpec((None, rows, tn), lambda i, j: (i, 0, j)),
        out_shape=jax.ShapeDtypeStruct((depth, rows, d6), F32),
        compiler_params=_cparams("parallel", "parallel"),
        name="modulation",
    )(cc, mod_w, mod_b.reshape(depth, 1, d6))
    lat = out[:, :bsz].reshape(depth, bsz, 1, 6, d)
    ctx = jnp.broadcast_to(out[:, bsz].reshape(depth, 1, 1, 6, d), (depth, bsz, 1, 6, d))
    return jnp.concatenate([lat, ctx], axis=2).reshape(depth, bsz * 2, 6, d)


def _prenorm_kernel(x_ref, c_ref, g_ref, mod_ref, o_ref, *, n_lat_tiles):
    m = mod_ref[...]
    x = jnp.where(pl.program_id(1) < n_lat_tiles, x_ref[...], c_ref[...])
    hn = _rms(x, g_ref[...]) * (1.0 + m[1:2]) + m[0:1]
    o_ref[...] = hn.astype(o_ref.dtype)


def _lat_ctx_specs(n_lat_tiles, d):
    return [pl.BlockSpec((None, TOKEN_TILE, d), lambda b, i: (b, jnp.minimum(i, n_lat_tiles - 1), 0)),
            pl.BlockSpec((None, TOKEN_TILE, d), lambda b, i: (b, jnp.maximum(i - n_lat_tiles, 0), 0))]


def _prenorm(x, ctx, g, mods, n_lat_tiles):
    bsz, n_lat, d = x.shape
    lt = n_lat + ctx.shape[1]
    nt = lt // TOKEN_TILE
    return pl.pallas_call(
        functools.partial(_prenorm_kernel, n_lat_tiles=n_lat_tiles),
        grid=(bsz, nt),
        in_specs=_lat_ctx_specs(n_lat_tiles, d) + [
            pl.BlockSpec((1, d), lambda b, i: (0, 0)),
            pl.BlockSpec((None, 6, d), lambda b, i: (b * 2 + (i >= n_lat_tiles).astype(jnp.int32), 0, 0)),
        ],
        out_specs=pl.BlockSpec((None, TOKEN_TILE, d), lambda b, i: (b, i, 0)),
        out_shape=jax.ShapeDtypeStruct((bsz, lt, d), F32),
        compiler_params=_cparams("parallel", "parallel"),
        name="prenorm",
    )(x, ctx, g.reshape(1, d), mods)


def _s5_weights(lam_re, lam_im, log_dt, b_re, b_im, c_re, c_im):
    t = S5_T
    k16 = b_re.shape[-1]

    def cmul(ar, ai, br, bi):
        return ar * br - ai * bi, ar * bi + ai * br

    def direction(k):
        lr, li = lam_re[k], lam_im[k]
        step = jnp.exp(log_dt[k])[:, None]
        mag = jnp.exp(lr * step)
        abar_r = mag * jnp.cos(li * step)
        abar_i = mag * jnp.sin(li * step)
        den = lr * lr + li * li
        q_r = ((abar_r - 1.0) * lr + abar_i * li) / den
        q_i = (abar_i * lr - (abar_r - 1.0) * li) / den
        bb_r, bb_i = cmul(q_r[..., None], q_i[..., None], b_re, b_im)

        def power(tau):
            tau = jnp.asarray(tau, F32)[None, :, None]
            m = jnp.exp((lr * step)[:, None, :] * tau)
            return m * jnp.cos((li * step)[:, None, :] * tau), m * jnp.sin((li * step)[:, None, :] * tau)

        return bb_r.transpose(0, 2, 1), bb_i.transpose(0, 2, 1), power

    rows = lambda v: jnp.repeat(v, k16, axis=1)
    row_tile = lambda v: jnp.tile(v, (1, t, 1))
    cols = lambda v: jnp.repeat(v.transpose(0, 2, 1), k16, axis=2)
    col_tile = lambda v: jnp.tile(v, (1, 1, t))
    ct_r, ct_i = c_re.transpose(0, 2, 1), c_im.transpose(0, 2, 1)
    steps = jnp.arange(t)

    def left(bt_r, bt_i, power, tau):
        pr, pi = power(tau)
        return cmul(row_tile(bt_r), row_tile(bt_i), rows(pr), rows(pi))

    def right(power, tau):
        pr, pi = power(tau)
        return cmul(col_tile(ct_r), col_tile(ct_i), cols(pr), cols(pi))

    bf_r, bf_i, pow_f = direction(0)
    bb_r, bb_i, pow_b = direction(1)
    lf_r, lf_i = left(bf_r, bf_i, pow_f, -steps)
    rf_r, rf_i = right(pow_f, steps)
    lb_r, lb_i = left(bb_r, bb_i, pow_b, steps)
    rb_r, rb_i = right(pow_b, -steps)
    lf = jnp.concatenate([lf_r, lf_i], axis=-1)
    lb = jnp.concatenate([lb_r, lb_i], axis=-1)
    rf = jnp.concatenate([rf_r, -rf_i], axis=1)
    rb = jnp.concatenate([rb_r, -rb_i], axis=1)

    sf_r, sf_i = left(bf_r, bf_i, pow_f, t - 1 - steps)
    ws = jnp.concatenate([sf_r, lb_r, sf_i, lb_i], axis=-1)

    of_r, of_i = right(pow_f, steps + 1)
    ob_r, ob_i = right(pow_b, t - steps)
    zero = jnp.zeros_like(of_r)
    w2 = jnp.concatenate([of_r, zero, -of_i, zero, zero, ob_r, zero, -ob_i], axis=1)

    af_r, af_i = pow_f([t])
    ab_r, ab_i = pow_b([t])
    ar = jnp.concatenate([af_r[:, 0], ab_r[:, 0]], axis=-1)
    ai = jnp.concatenate([af_i[:, 0], ab_i[:, 0]], axis=-1)
    return ws.astype(BF16), lf, rf, lb, rb, w2.astype(BF16), ar, ai


def _s5_row_block(nc):
    return max(rb for rb in range(16, min(nc, 176) + 1, 16) if nc % rb == 0)


def _dot_split3(a, b):
    a_hi = a.astype(BF16)
    a_lo = (a - a_hi.astype(F32)).astype(BF16)
    b_hi = b.astype(BF16)
    b_lo = (b - b_hi.astype(F32)).astype(BF16)
    dot = functools.partial(jnp.dot, preferred_element_type=F32)
    return dot(a_hi, b_hi) + dot(a_hi, b_lo) + dot(a_lo, b_hi)


def _s5_kernel(hn_ref, ws_ref, lf_ref, rf_ref, lb_ref, rb_ref, w2_ref, ar_ref, ai_ref, yo_ref,
               u_ref, y_ref, wm_ref, sre, sim, hre_f, him_f, hre_b, him_b, *, n_chunks, n_ctx_chunks, pitch):
    nc, ncc = n_chunks, n_ctx_chunks
    ncl = nc - ncc
    p = S5_STATE
    t_len = S5_T
    gl = S5_GROUP
    per_half = 128 // gl
    rb = _s5_row_block(nc)
    lane_slot = lax.broadcasted_iota(jnp.int32, (rb, 128), 1) // gl

    @pl.when(pl.program_id(1) == 0)
    def _():
        tk = t_len * gl
        src_tok = lax.broadcasted_iota(jnp.int32, (tk, tk), 0) // gl
        dst_tok = lax.broadcasted_iota(jnp.int32, (tk, tk), 1) // gl
        for g in range(S5_GB):
            causal = _dot_split3(lf_ref[g], rf_ref[g])
            anti = _dot_split3(lb_ref[g], rb_ref[g])
            wm = jnp.where(dst_tok >= src_tok, causal, 0.0) + jnp.where(src_tok >= dst_tok, anti, 0.0)
            wm_ref[g] = wm.astype(wm_ref.dtype)

    def slot_transpose(xs):
        xs = list(xs)
        bit = per_half // 2
        while bit >= 1:
            upper = (lane_slot // bit) % 2 == 1
            nxt = list(xs)
            for p in range(per_half):
                if p & bit:
                    continue
                lo, hi = xs[p], xs[p + bit]
                nxt[p] = jnp.where(upper, pltpu.roll(hi, bit * gl, axis=1), lo)
                nxt[p + bit] = jnp.where(upper, hi, pltpu.roll(lo, 128 - bit * gl, axis=1))
            xs = nxt
            bit //= 2
        return xs

    def gather_u(blk, carry):
        r0 = pl.multiple_of(blk * rb, 16)
        halves = []
        for hb in range(t_len // per_half):
            a = [hn_ref[pl.ds(r0 * t_len + hb * per_half + j, rb, stride=t_len), :] for j in range(per_half)]
            halves.append(slot_transpose(a))
        for i in range(S5_GB):
            u_ref[i, pl.ds(r0, rb), :] = jnp.concatenate([h[i] for h in halves], axis=1).astype(u_ref.dtype)
        return carry

    lax.fori_loop(0, nc // rb, gather_u, 0)

    for g in range(S5_GB):
        s = jnp.dot(u_ref[g], ws_ref[g], preferred_element_type=F32)
        sre[pl.ds(g * pitch, nc), :] = s[:, : 2 * p]
        sim[pl.ds(g * pitch, nc), :] = s[:, 2 * p:]

    ar = ar_ref[...]
    ai = ai_ref[...]
    fwd_lane = lax.broadcasted_iota(jnp.int32, (S5_GB, 2 * p), 1) < p

    def step(k, carry):
        h_r, h_i = carry
        cf = jnp.where(k < ncc, ncl + k, k - ncc)
        cb = nc - 1 - k
        rows_f = pl.ds(cf, S5_GB, stride=pitch)
        rows_b = pl.ds(cb, S5_GB, stride=pitch)
        hre_f[rows_f, :] = h_r
        him_f[rows_f, :] = h_i
        hre_b[rows_b, :] = h_r
        him_b[rows_b, :] = h_i
        s_r = jnp.where(fwd_lane, sre[rows_f, :], sre[rows_b, :])
        s_i = jnp.where(fwd_lane, sim[rows_f, :], sim[rows_b, :])
        n_r = ar * h_r - ai * h_i + s_r
        n_i = ar * h_i + ai * h_r + s_i
        return n_r, n_i

    zero = jnp.zeros((S5_GB, 2 * p), F32)
    lax.fori_loop(0, nc, step, (zero, zero), unroll=2)

    for g in range(S5_GB):
        rows = pl.ds(g * pitch, nc)
        hin = jnp.concatenate([hre_f[rows, :], him_f[rows, :], hre_b[rows, :], him_b[rows, :]], axis=1)
        out = jnp.dot(u_ref[g], wm_ref[g], preferred_element_type=F32)
        out = out + jnp.dot(hin.astype(BF16), w2_ref[g], preferred_element_type=F32)
        y_ref[g] = out

    def scatter_y(blk, carry):
        r0 = pl.multiple_of(blk * rb, 16)
        for hb in range(t_len // per_half):
            yv = [y_ref[i, pl.ds(r0, rb), hb * 128:(hb + 1) * 128] for i in range(S5_GB)]
            for j, tok in enumerate(slot_transpose(yv)):
                yo_ref[pl.ds(r0 * t_len + hb * per_half + j, rb, stride=t_len), :] = tok
        return carry

    lax.fori_loop(0, nc // rb, scatter_y, 0)


def _s5_scan(hn, n_lat, weights):
    bsz, lt, d = hn.shape
    ngrp = d // S5_GROUP
    t = S5_T
    tk = t * S5_GROUP
    nc = lt // t
    ncc = (lt - n_lat) // t
    gb = S5_GB
    assert gb * S5_GROUP == 128 and (128 // S5_GROUP) == gb and t % gb == 0
    ws, lf, rf, lb, rb, w2, ar, ai = weights
    p2 = 2 * S5_STATE
    pitch = nc + 8 if (nc // 8) % 2 == 0 else nc
    kern = functools.partial(_s5_kernel, n_chunks=nc, n_ctx_chunks=ncc, pitch=pitch)
    per_group = lambda *s: pl.BlockSpec((gb,) + s, lambda gi, b: (gi,) + tuple(0 for _ in s))
    return pl.pallas_call(
        kern,
        grid=(ngrp // gb, bsz),
        in_specs=[
            pl.BlockSpec((None, lt, 128), lambda gi, b: (b, 0, gi)),
            per_group(tk, 2 * p2),
            per_group(tk, p2), per_group(p2, tk), per_group(tk, p2), per_group(p2, tk),
            per_group(4 * p2, tk),
            per_group(p2), per_group(p2),
        ],
        out_specs=pl.BlockSpec((None, lt, 128), lambda gi, b: (b, 0, gi)),
        out_shape=jax.ShapeDtypeStruct((bsz, lt, d), F32),
        scratch_shapes=[pltpu.VMEM((gb, nc, tk), BF16), pltpu.VMEM((gb, nc, tk), F32), pltpu.VMEM((gb, tk, tk), BF16)]
        + [pltpu.VMEM((gb * pitch, p2), F32) for _ in range(6)],
        compiler_params=_cparams("parallel", "arbitrary"),
        name="s5_scan",
    )(hn, ws, lf, rf, lb, rb, w2, ar, ai)


def _router_gates(hn2, rw_ref, rb_ref):
    n_exp = rw_ref.shape[0]
    epg = n_exp // N_EXPERT_GROUPS
    logits = lax.dot_general(rw_ref[...], hn2, (((1,), (1,)), ((), ())),
                             preferred_element_type=F32, precision=HIGHEST)
    s = _sigmoid(logits)
    sel = s + rb_ref[...]
    row = [sel[e:e + 1] for e in range(n_exp)]
    gscore = []
    for gi in range(N_EXPERT_GROUPS):
        a, b, c, dd = row[gi * epg: gi * epg + epg]
        hi1, lo1 = jnp.maximum(a, b), jnp.minimum(a, b)
        hi2, lo2 = jnp.maximum(c, dd), jnp.minimum(c, dd)
        gscore.append(jnp.maximum(hi1, hi2) + jnp.maximum(jnp.minimum(hi1, hi2), jnp.maximum(lo1, lo2)))
    gmax = functools.reduce(jnp.maximum, gscore)
    gates = []
    taken = None
    for gi in range(N_EXPERT_GROUPS):
        is_max = gscore[gi] == gmax
        best = is_max if taken is None else jnp.logical_and(is_max, jnp.logical_not(taken))
        taken = is_max if taken is None else jnp.logical_or(taken, is_max)
        for e in range(gi * epg, gi * epg + epg):
            rank = jnp.zeros_like(row[e])
            for j in range(gi * epg, gi * epg + epg):
                if j == e:
                    continue
                ahead = (row[j] >= row[e]) if j < e else (row[j] > row[e])
                rank = rank + ahead.astype(F32)
            chosen = jnp.logical_and(best, rank < float(TOP_K))
            gates.append(jnp.where(chosen, s[e:e + 1], 0.0))
    g = jnp.concatenate(gates, axis=0)
    return g / jnp.sum(g, axis=0, keepdims=True)


def _glu_kernel(y_ref, u_ref, x_ref, c_ref, mod_ref, d_ref, w_ref, b_ref, g2_ref, rw_ref, rb_ref,
                x1_ref, hn2_ref, gates_ref, *, n_lat_tiles):
    d = x_ref.shape[-1]
    is_lat = pl.program_id(1) < n_lat_tiles
    m = mod_ref[...]
    nsub = ROW_SUBTILES
    sub = x_ref.shape[0] // nsub
    for s in range(nsub):
        r = slice(s * sub, (s + 1) * sub)
        u = u_ref[r, :].astype(F32)
        a = _gelu_tanh(y_ref[r, :].astype(F32) + d_ref[...] * u)
        z = jnp.dot(a.astype(BF16), w_ref[...], preferred_element_type=F32) + b_ref[...]
        out = z[:, :d] * _sigmoid(z[:, d:])
        x1 = jnp.where(is_lat, x_ref[r, :], c_ref[r, :]) + m[2:3] * out
        x1_ref[r, :] = x1
        hn2 = _rms(x1, g2_ref[...]) * (1.0 + m[4:5]) + m[3:4]
        hn2_ref[r, :] = hn2.astype(hn2_ref.dtype)
        gates_ref[:, r] = _router_gates(hn2, rw_ref, rb_ref)


def _glu_head(y, hn, x, ctx, mods, d_skip, glu_w, glu_b, norm2_g, router_wt, router_b, n_lat_tiles):
    bsz, lt, d = hn.shape
    nt = lt // TOKEN_TILE
    n_exp = router_wt.shape[0]
    tok = pl.BlockSpec((None, TOKEN_TILE, d), lambda b, i: (b, i, 0))
    vec = lambda n: pl.BlockSpec((1, n), lambda b, i: (0, 0))
    return pl.pallas_call(
        functools.partial(_glu_kernel, n_lat_tiles=n_lat_tiles),
        grid=(bsz, nt),
        in_specs=[tok, tok] + _lat_ctx_specs(n_lat_tiles, d) + [
            pl.BlockSpec((None, 6, d), lambda b, i: (b * 2 + (i >= n_lat_tiles).astype(jnp.int32), 0, 0)),
            vec(d),
            pl.BlockSpec((d, 2 * d), lambda b, i: (0, 0)),
            vec(2 * d),
            vec(d),
            pl.BlockSpec((n_exp, d), lambda b, i: (0, 0)),
            pl.BlockSpec((n_exp, 1), lambda b, i: (0, 0)),
        ],
        out_specs=[
            tok, tok,
            pl.BlockSpec((n_exp, TOKEN_TILE), lambda b, i: (0, b * nt + i)),
        ],
        out_shape=[
            jax.ShapeDtypeStruct((bsz, lt, d), F32),
            jax.ShapeDtypeStruct((bsz, lt, d), BF16),
            jax.ShapeDtypeStruct((n_exp, bsz * lt), F32),
        ],
        compiler_params=_cparams("parallel", "parallel"),
        name="s5_glu_head",
    )(y, hn, x, ctx, mods, d_skip.reshape(1, d), glu_w.astype(BF16), glu_b.reshape(1, 2 * d),
      norm2_g.reshape(1, d), router_wt, router_b.reshape(n_exp, 1))


def _moe_slots_padded(total):
    return jnp.floor((total + float(MOE_PIECE - 1)) * (1.0 / MOE_PIECE)) * float(MOE_PIECE)


def _moe_sort_kernel(gt_ref, t_ref, ts_ref, cnt_ref, *, n_blocks):
    n_exp, nb = gt_ref.shape
    nbpad = ts_ref.shape[0]
    sel = jnp.logical_and(gt_ref[...] > 0.0, pl.program_id(0) < n_blocks)
    sel_b = jnp.where(sel, 1.0, 0.0).astype(BF16)
    earlier = lax.broadcasted_iota(jnp.int32, (nb, nb), 0) < lax.broadcasted_iota(jnp.int32, (nb, nb), 1)
    rank = jnp.dot(sel_b, jnp.where(earlier, 1.0, 0.0).astype(BF16), preferred_element_type=F32)
    total = jnp.sum(jnp.where(sel, 1.0, 0.0), axis=1, keepdims=True)
    padded = jnp.broadcast_to(_moe_slots_padded(total), (n_exp, 128))
    below = lax.broadcasted_iota(jnp.int32, (n_exp, n_exp), 1) < lax.broadcasted_iota(jnp.int32, (n_exp, n_exp), 0)
    offs = jnp.dot(jnp.where(below, 1.0, 0.0).astype(BF16), padded.astype(BF16), preferred_element_type=F32)[:, 0:1]
    dest = offs + rank
    d_lo_f = jnp.min(jnp.where(sel, dest, float(nbpad)), axis=0, keepdims=True)
    d_hi_f = jnp.max(jnp.where(sel, dest, -1.0), axis=0, keepdims=True)
    slot = lax.broadcasted_iota(jnp.int32, (nbpad, nb), 0)
    at_lo = jnp.where(slot == d_lo_f.astype(jnp.int32), 1.0, 0.0).astype(BF16)
    at_hi = jnp.where(slot == d_hi_f.astype(jnp.int32), 1.0, 0.0).astype(BF16)
    d = t_ref.shape[1]
    ts_ref[:, :d] = jnp.dot(at_lo + at_hi, t_ref[...], preferred_element_type=F32).astype(ts_ref.dtype)
    gts = gt_ref[...]

    def gate_rows(d_f):
        g = jnp.sum(jnp.where(jnp.logical_and(sel, dest == d_f), gts, 0.0), axis=0, keepdims=True)
        g_hi = g.astype(BF16).astype(F32)
        g_mid = (g - g_hi).astype(BF16).astype(F32)
        g_lo = g - g_hi - g_mid
        row = lax.broadcasted_iota(jnp.int32, (128, nb), 0)
        terms = jnp.where(row == 0, g_hi, jnp.where(row == 1, g_mid, jnp.where(row == 2, g_lo, 0.0)))
        return terms.astype(BF16)

    gate_cols = lax.dot_general(jnp.concatenate([at_lo, at_hi], axis=1),
                                jnp.concatenate([gate_rows(d_lo_f), gate_rows(d_hi_f)], axis=1),
                                (((1,), (1,)), ((), ())), preferred_element_type=F32)
    ts_ref[:, d:] = gate_cols.astype(ts_ref.dtype)
    cnt_ref[...] = padded


def _moe_expert_kernel(tile_e, src_rows, dst_rows, ntiles, ts_hbm, w1_ref, w3_ref, w2_ref, ys_hbm,
                       tbuf, ybuf, w1b, w3b, w2b, sem_in, sem_out):
    t = pl.program_id(0)
    nt = ntiles[0]
    last = pl.num_programs(0) - 1
    ppt = MOE_TILE // MOE_PIECE
    slot = t % 2

    def rows_at(table, tt, p):
        return pl.ds(pl.multiple_of(table[tt * ppt + p], MOE_PIECE), MOE_PIECE)

    def copy_in(tt, sl, p):
        return pltpu.make_async_copy(ts_hbm.at[rows_at(src_rows, tt, p), :],
                                     tbuf.at[sl, pl.ds(p * MOE_PIECE, MOE_PIECE), :], sem_in.at[sl])

    def copy_out(tt, sl, p):
        return pltpu.make_async_copy(ybuf.at[sl, pl.ds(p * MOE_PIECE, MOE_PIECE), :],
                                     ys_hbm.at[rows_at(dst_rows, tt, p), pl.ds(0, ybuf.shape[2])], sem_out.at[sl])

    def start_in(tt, sl):
        for p in range(ppt):
            copy_in(tt, sl, p).start()

    def wait_out(tt, sl):
        for p in range(ppt):
            copy_out(tt, sl, p).wait()

    @pl.when(jnp.logical_and(t == 0, nt > 0))
    def _():
        start_in(0, 0)

    @pl.when(t + 1 < nt)
    def _():
        start_in(t + 1, 1 - slot)

    @pl.when(jnp.logical_and(t >= 2, t - 2 < nt))
    def _():
        wait_out(t - 2, slot)

    @pl.when(t < nt)
    def _():
        @pl.when(jnp.logical_or(t == 0, tile_e[t] != tile_e[jnp.maximum(t - 1, 0)]))
        def _():
            w1b[...] = w1_ref[...].astype(BF16)
            w3b[...] = w3_ref[...].astype(BF16)
            w2b[...] = w2_ref[...].astype(BF16)

        for p in range(ppt):
            copy_in(t, slot, p).wait()
        d = ybuf.shape[2]
        x = tbuf[slot, :, :d]
        gate = jnp.sum(tbuf[slot, :, d:].astype(F32), axis=1, keepdims=True)
        h = _silu(jnp.dot(x, w1b[...], preferred_element_type=F32)) * jnp.dot(x, w3b[...], preferred_element_type=F32)
        ybuf[slot] = (gate * jnp.dot(h.astype(BF16), w2b[...], preferred_element_type=F32)).astype(ybuf.dtype)
        for p in range(ppt):
            copy_out(t, slot, p).start()

    @pl.when(t == last)
    def _():
        @pl.when(jnp.logical_and(last >= 1, last - 1 < nt))
        def _():
            wait_out(last - 1, 1 - slot)

        @pl.when(last < nt)
        def _():
            wait_out(last, slot)


def _moe_unsort_kernel(ys_ref, g_ref, x_ref, g2_ref, *rest, final):
    if final:
        fg_ref, o_ref = rest
    else:
        (o_ref,) = rest
    n_exp = g_ref.shape[1]
    nb = MOE_BLOCK
    nsub = g_ref.shape[0] // nb
    nbpad = ys_ref.shape[0] // nsub
    halves = g2_ref.shape[0] // nsub
    for sb in range(nsub):
        _moe_unsort_block(ys_ref.at[sb * nbpad:(sb + 1) * nbpad], g_ref.at[sb * nb:(sb + 1) * nb],
                          x_ref.at[sb * nb:(sb + 1) * nb], g2_ref.at[sb * halves:(sb + 1) * halves],
                          fg_ref if final else None, o_ref, sb, final)


def _moe_unsort_block(ys_ref, g_ref, x_ref, g2_ref, fg_ref, o_ref, sb, final):
    nb, n_exp = g_ref.shape
    nbpad = ys_ref.shape[0]
    gates = g_ref[...]
    sel = gates > 0.0
    sel_b = jnp.where(sel, 1.0, 0.0).astype(BF16)
    earlier = lax.broadcasted_iota(jnp.int32, (nb, nb), 1) < lax.broadcasted_iota(jnp.int32, (nb, nb), 0)
    rank = jnp.dot(jnp.where(earlier, 1.0, 0.0).astype(BF16), sel_b, preferred_element_type=F32)
    total = jnp.sum(jnp.where(sel, 1.0, 0.0), axis=0, keepdims=True)
    padded = jnp.broadcast_to(_moe_slots_padded(total), (8, n_exp))
    below = lax.broadcasted_iota(jnp.int32, (n_exp, n_exp), 0) < lax.broadcasted_iota(jnp.int32, (n_exp, n_exp), 1)
    offs = jnp.dot(padded.astype(BF16), jnp.where(below, 1.0, 0.0).astype(BF16), preferred_element_type=F32)[0:1]
    dest = offs + rank
    d_lo = jnp.min(jnp.where(sel, dest, float(nbpad)), axis=1, keepdims=True).astype(jnp.int32)
    d_hi = jnp.max(jnp.where(sel, dest, -1.0), axis=1, keepdims=True).astype(jnp.int32)
    slot = lax.broadcasted_iota(jnp.int32, (nb, nbpad), 1)
    pick = jnp.where(jnp.logical_or(slot == d_lo, slot == d_hi), 1.0, 0.0).astype(BF16)
    moe = jnp.dot(pick, ys_ref[...], preferred_element_type=F32)
    half = nb // g2_ref.shape[0]
    d = x_ref.shape[1]
    for j in range(g2_ref.shape[0]):
        r = slice(j * half, (j + 1) * half)
        xn = x_ref[r, :] + g2_ref[j] * moe[r]
        if final:
            xn = _rms(xn, fg_ref[...])
            for c in range(half // SSD_CHUNK):
                col = (sb * nb + j * half) // SSD_CHUNK + c
                o_ref[:, col, :] = xn[c * SSD_CHUNK:(c + 1) * SSD_CHUNK]
        else:
            o_ref[pl.ds(sb * nb + j * half, half), :] = xn


def _moe_schedule(counts, nbpad, n_tiles):
    nblk, n_exp = counts.shape
    ppt = MOE_TILE // MOE_PIECE
    pc = counts // MOE_PIECE
    loc = jnp.cumsum(pc, axis=1) - pc
    cum_b = jnp.cumsum(pc, axis=0)
    np_e = cum_b[-1]
    tiles_e = (np_e + ppt - 1) // ppt
    tile_end = jnp.cumsum(tiles_e)
    ntiles = tile_end[-1]
    t_idx = jnp.arange(n_tiles, dtype=jnp.int32)
    tile_e = jnp.minimum(jnp.sum((tile_end[None, :] <= t_idx[:, None]).astype(jnp.int32), axis=1), n_exp - 1)
    first = (tile_end - tiles_e)[tile_e]
    piece0 = (t_idx - first) * ppt
    npieces = jnp.where(t_idx < ntiles, jnp.clip(np_e[tile_e] - piece0, 0, ppt), 0)
    i = piece0[:, None] + jnp.arange(ppt, dtype=jnp.int32)[None, :]
    cum_t = cum_b.T[tile_e]
    blk = jnp.minimum(jnp.sum((cum_t[:, None, :] <= i[:, :, None]).astype(jnp.int32), axis=2), nblk - 1)
    before = jnp.take_along_axis(cum_t - pc.T[tile_e], blk, axis=1)
    within = i - before + jnp.take_along_axis(loc.T[tile_e], blk, axis=1)
    rows = blk * nbpad + within * MOE_PIECE
    real = jnp.arange(ppt)[None, :] < npieces[:, None]
    spare = nblk * nbpad
    piece = jnp.arange(ppt, dtype=jnp.int32)[None, :]
    src = jnp.where(real, rows, spare + piece * MOE_PIECE)
    dst = jnp.where(real, rows, spare + (ppt + (t_idx[:, None] % 2) * ppt + piece) * MOE_PIECE)
    return (tile_e.astype(jnp.int32), src.reshape(-1).astype(jnp.int32), dst.reshape(-1).astype(jnp.int32),
            ntiles.reshape(1).astype(jnp.int32))


def _moe(t, gates_t, xres, g2rows, w1, w3, w2, layer, *, final_g=None, blocks_per_batch=None):
    n, d = t.shape
    _, n_exp, _, f = w1.shape
    nb = MOE_BLOCK
    nblk = n // nb
    nbpad = TOP_K * nb + n_exp * MOE_PIECE
    final = final_g is not None

    ppt = MOE_TILE // MOE_PIECE
    nspare = -(-3 * ppt * MOE_PIECE // nbpad)
    last_blk = lambda j: jnp.minimum(j, nblk - 1)
    ts, cnt = pl.pallas_call(
        functools.partial(_moe_sort_kernel, n_blocks=nblk),
        grid=(nblk + nspare,),
        in_specs=[pl.BlockSpec((n_exp, nb), lambda j: (0, last_blk(j))), pl.BlockSpec((nb, d), lambda j: (last_blk(j), 0))],
        out_specs=[pl.BlockSpec((nbpad, d + 128), lambda j: (j, 0)), pl.BlockSpec((None, n_exp, 128), lambda j: (j, 0, 0))],
        out_shape=[jax.ShapeDtypeStruct(((nblk + nspare) * nbpad, d + 128), BF16),
                   jax.ShapeDtypeStruct((nblk + nspare, n_exp, 128), F32)],
        compiler_params=_cparams("parallel"),
        name="moe_sort",
    )(gates_t, t)

    n_tiles = nblk * nbpad // MOE_TILE + n_exp
    tile_e, src_rows, dst_rows, ntiles = _moe_schedule(cnt[:nblk, :, 0].astype(jnp.int32), nbpad, n_tiles)
    wspec = lambda shape: pl.BlockSpec((None, None) + shape, lambda i, te, sr, dr, nt: (layer, te[i], 0, 0))
    ys = pl.pallas_call(
        _moe_expert_kernel,
        grid_spec=pltpu.PrefetchScalarGridSpec(
            num_scalar_prefetch=4,
            grid=(n_tiles,),
            in_specs=[pl.BlockSpec(memory_space=pl.ANY), wspec((d, f)), wspec((d, f)), wspec((f, d))],
            out_specs=pl.BlockSpec(memory_space=pl.ANY),
            scratch_shapes=[pltpu.VMEM((2, MOE_TILE, d + 128), BF16), pltpu.VMEM((2, MOE_TILE, d), BF16),
                            pltpu.VMEM((d, f), BF16), pltpu.VMEM((d, f), BF16), pltpu.VMEM((f, d), BF16),
                            pltpu.SemaphoreType.DMA((2,)), pltpu.SemaphoreType.DMA((2,))],
        ),
        out_shape=jax.ShapeDtypeStruct(((nblk + nspare) * nbpad, d + 128), BF16),
        input_output_aliases={4: 0},
        compiler_params=_cparams("arbitrary"),
        name="moe_experts",
    )(tile_e, src_rows, dst_rows, ntiles, ts, w1, w3, w2)

    nsub = SSD_COLS_PER_STEP * SSD_CHUNK // nb if final else 1
    halves = nb // TOKEN_TILE
    in_specs = [
        pl.BlockSpec((nsub * nbpad, d), lambda j: (j, 0)),
        pl.BlockSpec((nsub * nb, n_exp), lambda j: (j, 0)),
        pl.BlockSpec((nsub * nb, d), lambda j: (j, 0)),
        pl.BlockSpec((nsub * halves, 1, d), lambda j: (j, 0, 0)),
    ]
    args = [ys, gates_t.T, xres, g2rows]
    if final:
        in_specs.append(pl.BlockSpec((1, d), lambda j: (0, 0)))
        args.append(final_g.reshape(1, d))
        assert blocks_per_batch % nsub == 0 and nsub * nb == SSD_COLS_PER_STEP * SSD_CHUNK
        spb = blocks_per_batch // nsub
        out_spec = pl.BlockSpec((None, SSD_CHUNK, SSD_COLS_PER_STEP, d), lambda j: (j // spb, 0, j % spb, 0))
        out_shape = jax.ShapeDtypeStruct((nblk // blocks_per_batch, SSD_CHUNK, GRID_W, d), F32)
    else:
        out_spec = pl.BlockSpec((nb, d), lambda j: (j, 0))
        out_shape = jax.ShapeDtypeStruct((n, d), F32)
    return pl.pallas_call(
        functools.partial(_moe_unsort_kernel, final=final),
        grid=(nblk // nsub,),
        in_specs=in_specs,
        out_specs=out_spec,
        out_shape=out_shape,
        compiler_params=_cparams("parallel"),
        name="moe_unsort_final" if final else "moe_unsort",
    )(*args)


SSD_COLS_PER_STEP = 8


def _ssd_inproj_kernel(xl_ref, xc_ref, g_ref, mod_ref, wz_ref, wx_ref, wdt_ref, wdtt_ref, bias_ref, biast_ref,
                       a_ref, at_ref, cw_ref, cb_ref, z_ref, xbc_ref, csdt_ref, cst_ref, xp_ref, slab, xt, ext, slabs, slabs_in,
                       *, n_lat_steps, n_ctx_chunks):
    step = pl.program_id(1)
    ip = pl.program_id(2)
    q = SSD_CHUNK
    ncols = SSD_COLS_PER_STEP
    d = xt.shape[1]
    is_lat = step < n_lat_steps
    c0 = step * ncols + 2 * ip
    n_lat_chunks = n_lat_steps * ncols
    n_chunks = n_lat_chunks + n_ctx_chunks
    halo = SSD_CONV // 2
    nsl = ext.shape[2]
    pitch = slabs.shape[1] // nsl

    def emit_conv(sub):
        w = [cw_ref[k] for k in range(SSD_CONV)]
        bias = cb_ref[...]
        for t in range(q):
            acc = bias
            for k in range(SSD_CONV):
                acc = acc + w[k] * ext[sub, t + k]
            y = _silu(acc)
            for a in range(nsl // 8):
                slabs[sub, pl.ds(a * 8 * pitch + t, 8, stride=pitch), :] = y[a * 8:(a + 1) * 8]
        for s in range(nsl):
            xbc_ref[sub * q:(sub + 1) * q, s * 128:(s + 1) * 128] = (
                slabs[sub, s * pitch:s * pitch + q, :].astype(xbc_ref.dtype))

    def project(sub):
        other = 1 - sub
        rows = slice(sub * q, (sub + 1) * q)
        m = mod_ref[...]
        x = xt[pl.ds(pl.multiple_of((2 * ip + sub) * q, q), q), :]
        xp_ref[rows, :] = x
        hn = (_rms(x, g_ref[...]) * (1.0 + m[1:2]) + m[0:1]).astype(BF16)
        z_ref[rows, :] = jnp.dot(hn, wz_ref[...], preferred_element_type=F32).astype(z_ref.dtype)
        xbc_new = jnp.dot(hn, wx_ref[...], preferred_element_type=F32)
        for s in range(nsl):
            slabs_in[sub, s * pitch:s * pitch + q, :] = xbc_new[:, s * 128:(s + 1) * 128]
        for t in range(q):
            for a in range(nsl // 8):
                ext[sub, halo + t, a * 8:(a + 1) * 8, :] = slabs_in[sub, pl.ds(a * 8 * pitch + t, 8, stride=pitch), :]
        if sub == 0:
            starts = jnp.logical_or(c0 == 0, c0 == n_lat_chunks)
            ext[other, halo + q:2 * halo + q] = jnp.where(starts, 0.0, ext[sub, halo:2 * halo])
            ext[sub, 0:halo] = jnp.where(starts, 0.0, ext[other, q:q + halo])
        else:
            ext[other, halo + q:2 * halo + q] = ext[sub, halo:2 * halo]
            ext[sub, 0:halo] = ext[other, q:q + halo]

        def softplus(v):
            return jnp.maximum(v, 0.0) + jnp.log(1.0 + jnp.exp(-jnp.abs(v)))

        r_i = lax.broadcasted_iota(jnp.int32, (q, q), 0)
        c_i = lax.broadcasted_iota(jnp.int32, (q, q), 1)
        lower = (r_i >= c_i).astype(F32)
        upper = (r_i <= c_i).astype(F32)
        dt = softplus(jnp.dot(hn, wdt_ref[...], preferred_element_type=F32) + bias_ref[...])
        da = dt * a_ref[...]
        half = da.shape[1] // 2
        cs = jnp.concatenate([jnp.dot(lower, da[:, :half], preferred_element_type=F32, precision=HIGHEST),
                              jnp.dot(upper, da[:, half:], preferred_element_type=F32, precision=HIGHEST)], axis=1)
        lane = lax.broadcasted_iota(jnp.int32, dt.shape, 1)
        csdt_ref[rows, :] = jnp.where(lane % 8 < 4, cs, dt)
        nh = at_ref.shape[0] // 2
        dtt = softplus(lax.dot_general(wdtt_ref[...], hn, (((1,), (1,)), ((), ())), preferred_element_type=F32)
                       + biast_ref[...])
        dat = dtt * at_ref[...]
        cst_f = jnp.dot(dat[:nh], upper, preferred_element_type=F32, precision=HIGHEST)
        cst_b = jnp.dot(dat[nh:], lower, preferred_element_type=F32, precision=HIGHEST)
        cst = jnp.concatenate([cst_f, cst_b], axis=0)
        for j in range(cst_ref.shape[1]):
            cst_ref[sub, j] = cst[j * 4:(j + 1) * 4, :]

    @pl.when(jnp.logical_and(ip == 0, is_lat))
    def _():
        for s in range(d // 128):
            slab[...] = xl_ref[:, :, s * 128:(s + 1) * 128].reshape(q * ncols, 128)
            for w in range(ncols):
                xt[w * q:(w + 1) * q, s * 128:(s + 1) * 128] = slab[pl.ds(w, q, stride=ncols), :]

    @pl.when(jnp.logical_and(ip == 0, jnp.logical_not(is_lat)))
    def _():
        xt[0:n_ctx_chunks * q, :] = xc_ref[...]

    @pl.when(c0 == 0)
    def _():
        ext[...] = jnp.zeros_like(ext)

    @pl.when(jnp.logical_or(is_lat, 2 * ip < n_ctx_chunks))
    def _():
        for sub in range(2):
            emit_conv(sub)
            project(sub)

    @pl.when(c0 == n_chunks)
    def _():
        ext[1, halo + q:2 * halo + q] = jnp.zeros((halo,) + ext.shape[2:], F32)
        for sub in range(2):
            emit_conv(sub)


def _ssd_inproj(xall, n_lat, norm_g, mods, in_w, dt_bias, a_log, conv_w, conv_b):
    bsz, lt, d = xall.shape
    q = SSD_CHUNK
    ncols = SSD_COLS_PER_STEP
    ncl = n_lat // q
    nc = lt // q
    ncc = nc - ncl
    n_ctx = lt - n_lat
    nh2 = dt_bias.size
    d_inner = (nh2 // 2) * SSD_HEAD_DIM
    conv_ch = in_w.shape[1] - d_inner - nh2
    wz = in_w[:, :d_inner].astype(BF16)
    wx = in_w[:, d_inner:d_inner + conv_ch].astype(BF16)
    wdt = in_w[:, d_inner + conv_ch:].astype(BF16)
    a = -jnp.exp(a_log.astype(F32)).reshape(nh2) * LOG2_E
    bias = dt_bias.astype(F32).reshape(nh2)
    ngr = nh2 // 4
    lanes = jnp.arange(2 * nh2)
    dup = (lanes // 8) * 4 + lanes % 4
    assert n_lat // GRID_W == q and GRID_W % ncols == 0 and lt % GRID_W == 0 and ncc + 2 <= ncols and ncc % 2 == 0 and n_lat % n_ctx == 0
    nls = ncl // ncols
    xgrid = xall.reshape(bsz, lt // GRID_W, GRID_W, d)
    kern = functools.partial(_ssd_inproj_kernel, n_lat_steps=nls, n_ctx_chunks=ncc)
    full = lambda s: pl.BlockSpec(s, lambda b, st, i: tuple(0 for _ in s), pipeline_mode=pl.Buffered(1))
    pps = ncols // 2
    npairs = nc // 2
    pair = lambda b, st, i: jnp.minimum(st * pps + i, npairs - 1)
    rows = lambda width: pl.BlockSpec((None, 2 * q, width), lambda b, st, i: (b, pair(b, st, i), 0))
    return pl.pallas_call(
        kern,
        grid=(bsz, nls + 1, pps),
        in_specs=[
            pl.BlockSpec((None, q, ncols, d), lambda b, st, i: (b, 0, jnp.minimum(st, nls - 1), 0)),
            pl.BlockSpec((None, n_ctx, d), lambda b, st, i: (b, n_lat // n_ctx, 0)),
            full((1, d)),
            pl.BlockSpec((None, 6, d), lambda b, st, i: (b * 2 + (st >= nls).astype(jnp.int32), 0, 0)),
            full((d, d_inner)), full((d, conv_ch)), full((d, 2 * nh2)), full((nh2, d)),
            full((1, 2 * nh2)), full((nh2, 1)), full((1, 2 * nh2)), full((nh2, 1)),
            full((SSD_CONV, conv_ch // 128, 128)), full((conv_ch // 128, 128)),
        ],
        out_specs=[
            rows(d_inner),
            pl.BlockSpec((None, 2 * q, conv_ch), lambda b, st, i: (b, jnp.clip(st * pps + i - 1, 0, npairs - 1), 0)),
            rows(2 * nh2),
            pl.BlockSpec((None, 2, ngr, 4, q), lambda b, st, i: (b, pair(b, st, i), 0, 0, 0)),
            rows(d),
        ],
        out_shape=[
            jax.ShapeDtypeStruct((bsz, lt, d_inner), BF16),
            jax.ShapeDtypeStruct((bsz, lt, conv_ch), BF16),
            jax.ShapeDtypeStruct((bsz, lt, 2 * nh2), F32),
            jax.ShapeDtypeStruct((bsz, nc, ngr, 4, q), F32),
            jax.ShapeDtypeStruct((bsz, lt, d), F32),
        ],
        scratch_shapes=[pltpu.VMEM((q * ncols, 128), F32), pltpu.VMEM((q * ncols, d), F32),
                        pltpu.VMEM((2, q + 2 * (SSD_CONV // 2), conv_ch // 128, 128), F32),
                        pltpu.VMEM((2, conv_ch // 128 * (q + 8), 128), F32),
                        pltpu.VMEM((2, conv_ch // 128 * (q + 8), 128), F32)],
        compiler_params=_cparams("parallel", "arbitrary", "arbitrary"),
        name="ssd_inproj",
    )(xgrid, xall, norm_g.reshape(1, d), mods, wz, wx, wdt[:, dup], wdt.T, bias[dup].reshape(1, -1),
      bias.reshape(nh2, 1), a[dup].reshape(1, -1), a.reshape(nh2, 1), conv_w.astype(F32).reshape(SSD_CONV, conv_ch // 128, 128),
      conv_b.astype(F32).reshape(conv_ch // 128, 128))


def _ssd_scan_dir(x, bm, cm, v, cst, state, reverse):
    q = SSD_CHUNK
    hp = SSD_HEAD_DIM
    gw = x.shape[1]
    r = gw // hp
    lane = lax.broadcasted_iota(jnp.int32, (q, 2 * hp), 1)
    first = lane < hp
    bc = [jnp.broadcast_to(v[:, j:j + 1], (q, 2 * hp)) for j in range(2 * r)]

    def head_lanes(cols):
        return jnp.concatenate([jnp.where(first, cols[2 * p], cols[2 * p + 1]) for p in range(r // 2)], axis=1)

    cs_x = head_lanes(bc[:r])
    dt_x = head_lanes(bc[r:])
    end = 0 if reverse else q - 1
    cs_end = cs_x[end:end + 1, :]
    xdt = x * dt_x
    xw = (xdt * jnp.exp2(cs_end - cs_x)).astype(BF16)
    cb = lax.dot_general(cm, bm, (((1,), (1,)), ((), ())), preferred_element_type=F32)
    y_off = jnp.dot(cm, state.astype(BF16), preferred_element_type=F32) * jnp.exp2(cs_x)
    r_i = lax.broadcasted_iota(jnp.int32, (q, q), 0)
    c_i = lax.broadcasted_iota(jnp.int32, (q, q), 1)
    mask = (r_i <= c_i) if reverse else (r_i >= c_i)
    ys = []
    for pair in range(r // 2):
        xp = xdt[:, pair * 2 * hp:(pair + 1) * 2 * hp]
        rhs = jnp.concatenate([jnp.where(first, xp, 0.0), jnp.where(first, 0.0, xp)], axis=0).astype(BF16)
        gmats = []
        for h in (2 * pair, 2 * pair + 1):
            seg = jnp.exp2(jnp.where(mask, bc[h] - cst[h:h + 1, :], -1e30))
            gmats.append((cb * seg).astype(BF16))
        ys.append(jnp.dot(jnp.concatenate(gmats, axis=1), rhs, preferred_element_type=F32))
    upd = lax.dot_general(bm, xw, (((0,), (0,)), ((), ())), preferred_element_type=F32)
    return jnp.concatenate(ys, axis=1) + y_off, state * jnp.exp2(cs_end) + upd


def _ssd_scan_kernel(xf, bf, cf, csdtf, cstf, xb, bb, cb, csdtb, cstb, yf_ref, yb_ref, state_f, state_b):
    @pl.when(pl.program_id(1) == 0)
    def _():
        state_f[...] = jnp.zeros_like(state_f)
        state_b[...] = jnp.zeros_like(state_b)

    ngr, n, gw = state_f.shape
    r = gw // SSD_HEAD_DIM
    dirs = ((xf, bf, cf, csdtf, cstf, yf_ref, state_f, False, 0), (xb, bb, cb, csdtb, cstb, yb_ref, state_b, True, ngr))
    for g in range(ngr):
        for x_ref, b_ref, c_ref, csdt_ref, cst_ref, y_ref, state, reverse, lane_group0 in dirs:
            j = lane_group0 + g
            y, new_state = _ssd_scan_dir(
                x_ref[:, g * gw:(g + 1) * gw].astype(F32), b_ref[:, g * n:(g + 1) * n], c_ref[:, g * n:(g + 1) * n],
                csdt_ref[:, 2 * r * j:2 * r * (j + 1)], cst_ref[g], state[g], reverse)
            y_ref[:, g * gw:(g + 1) * gw] = y.astype(y_ref.dtype)
            state[g] = new_state


def _ssd_scan(xbc, csdt, cst, n_lat):
    bsz, lt, ch = xbc.shape
    q = SSD_CHUNK
    nc, ncl = lt // q, n_lat // q
    ngr = SSD_GROUPS
    n = SSD_STATE
    d_inner = ch - 2 * ngr * n
    gw = d_inner // ngr
    assert d_inner % (ngr * n) == 0 and q == 2 * SSD_HEAD_DIM

    def specs(chunk, direction):
        return [
            pl.BlockSpec((None, q, d_inner), lambda b, k: (b, chunk(k), 0)),
            pl.BlockSpec((None, q, ngr * n), lambda b, k: (b, chunk(k), d_inner // (ngr * n))),
            pl.BlockSpec((None, q, ngr * n), lambda b, k: (b, chunk(k), d_inner // (ngr * n) + 1)),
            pl.BlockSpec((None, q, csdt.shape[2]), lambda b, k: (b, chunk(k), 0)),
            pl.BlockSpec((None, None, ngr, 4, q), lambda b, k: (b, chunk(k), direction, 0, 0)),
        ]

    fwd = lambda k: (k + ncl) % nc
    bwd = lambda k: nc - 1 - k
    out = jax.ShapeDtypeStruct((bsz, lt, d_inner), BF16)
    return pl.pallas_call(
        _ssd_scan_kernel,
        grid=(bsz, nc),
        in_specs=specs(fwd, 0) + specs(bwd, 1),
        out_specs=[pl.BlockSpec((None, q, d_inner), lambda b, k: (b, fwd(k), 0)),
                   pl.BlockSpec((None, q, d_inner), lambda b, k: (b, bwd(k), 0))],
        out_shape=[out, out],
        scratch_shapes=[pltpu.VMEM((ngr, n, gw), F32), pltpu.VMEM((ngr, n, gw), F32)],
        compiler_params=_cparams("parallel", "arbitrary"),
        name="ssd_scan",
    )(xbc, xbc, xbc, csdt, cst, xbc, xbc, xbc, csdt, cst)


def _ssd_finish_kernel(yf_ref, yb_ref, xs_ref, z_ref, x_ref, mod_ref, dsk_ref, ng_ref, w_ref, g2_ref,
                       rw_ref, rb_ref, x3_ref, hn2_ref, gates_ref):
    m = mod_ref[...]
    nsub = ROW_SUBTILES
    sub = x_ref.shape[0] // nsub
    for s in range(nsub):
        r = slice(s * sub, (s + 1) * sub)
        y = yf_ref[r, :].astype(F32) + yb_ref[r, :].astype(F32) + dsk_ref[...] * xs_ref[r, :].astype(F32)
        gated = y * _silu(z_ref[r, :].astype(F32))
        nrm = _rms(gated, ng_ref[...])
        out = jnp.dot(nrm.astype(BF16), w_ref[...], preferred_element_type=F32)
        x3 = x_ref[r, :] + m[2:3] * out
        x3_ref[r, :] = x3
        hn2 = _rms(x3, g2_ref[...]) * (1.0 + m[4:5]) + m[3:4]
        hn2_ref[r, :] = hn2.astype(hn2_ref.dtype)
        gates_ref[:, r] = _router_gates(hn2, rw_ref, rb_ref)


def _ssd_finish(yf, yb, xbc, z, xall, n_lat, mods, d_skip, norm_g, out_w, norm2_g, router_wt, router_b):
    bsz, lt, d_inner = z.shape
    d = xall.shape[-1]
    q = TOKEN_TILE
    ncl = n_lat // q
    n_exp = router_wt.shape[0]
    inner = pl.BlockSpec((None, q, d_inner), lambda b, c: (b, c, 0))
    tok = pl.BlockSpec((None, q, d), lambda b, c: (b, c, 0))
    full = lambda s: pl.BlockSpec(s, lambda b, c: tuple(0 for _ in s))
    dsk = jnp.repeat(d_skip.astype(F32), SSD_HEAD_DIM).reshape(1, d_inner)
    return pl.pallas_call(
        _ssd_finish_kernel,
        grid=(bsz, ncl),
        in_specs=[
            inner, inner, inner, inner,
            tok,
            pl.BlockSpec((None, 6, d), lambda b, c: (b * 2, 0, 0)),
            full((1, d_inner)), full((1, d_inner)), full((d_inner, d)), full((1, d)),
            full((n_exp, d)), full((n_exp, 1)),
        ],
        out_specs=[tok, tok, pl.BlockSpec((n_exp, q), lambda b, c: (0, b * ncl + c))],
        out_shape=[
            jax.ShapeDtypeStruct((bsz, n_lat, d), F32),
            jax.ShapeDtypeStruct((bsz, n_lat, d), BF16),
            jax.ShapeDtypeStruct((n_exp, bsz * n_lat), F32),
        ],
        compiler_params=_cparams("parallel", "parallel"),
        name="ssd_finish",
    )(yf, yb, xbc, z, xall, mods, dsk, norm_g.reshape(1, d_inner), out_w.astype(BF16),
      norm2_g.reshape(1, d), router_wt, router_b.reshape(n_exp, 1))


def kernel(x, c, ctx, c_ctx, mod_w, mod_b, norm1_g, norm2_g, final_g, s5_lam_re, s5_lam_im, s5_log_dt, s5_b_re, s5_b_im, s5_c_re, s5_c_im, s5_d, s5_glu_w, s5_glu_b, ssd_in_w, ssd_conv_w, ssd_conv_b, ssd_dt_bias, ssd_a_log, ssd_d, ssd_norm_g, ssd_out_w, router_w, router_b, moe_w1, moe_w3, moe_w2):
    bsz, n_lat, d = x.shape
    n_ctx = ctx.shape[1]
    lt = n_lat + n_ctx
    n_exp = router_w.shape[1]
    assert n_lat % TOKEN_TILE == 0 and n_ctx % TOKEN_TILE == 0
    assert (bsz * lt) % MOE_BLOCK == 0 and n_lat % MOE_BLOCK == 0 and MOE_BLOCK % TOKEN_TILE == 0
    assert TOKEN_TILE % SSD_CHUNK == 0

    mods = _modulation(c, c_ctx, mod_w, mod_b)
    router_wt = router_w.T.astype(F32)
    w1, w3, w2 = moe_w1, moe_w3, moe_w2
    nlt = n_lat // TOKEN_TILE
    tpb = lt // TOKEN_TILE

    hn = _prenorm(x, ctx, norm1_g[0], mods[0], nlt)
    s5w = _s5_weights(s5_lam_re[0], s5_lam_im[0], s5_log_dt[0], s5_b_re[0], s5_b_im[0], s5_c_re[0], s5_c_im[0])
    y = _s5_scan(hn, n_lat, s5w)
    x1, hn2, gates_t = _glu_head(y, hn, x, ctx, mods[0], s5_d[0], s5_glu_w[0], s5_glu_b[0], norm2_g[0],
                                 router_wt, router_b, nlt)
    g2_lat = jnp.broadcast_to(mods[0][0::2, None, 5], (bsz, nlt, d))
    g2_ctx = jnp.broadcast_to(mods[0][1::2, None, 5], (bsz, tpb - nlt, d))
    g2rows = jnp.concatenate([g2_lat, g2_ctx], axis=1).reshape(bsz * tpb, 1, d)
    x2 = _moe(hn2.reshape(bsz * lt, d), gates_t, x1.reshape(bsz * lt, d), g2rows, w1, w3, w2, 0).reshape(bsz, lt, d)

    z, xbc, csdt, cst, x2p = _ssd_inproj(x2, n_lat, norm1_g[1], mods[1], ssd_in_w[0], ssd_dt_bias[0], ssd_a_log[0],
                                             ssd_conv_w[0], ssd_conv_b[0])
    yf, yb = _ssd_scan(xbc, csdt, cst, n_lat)
    x3, hn3, gates3_t = _ssd_finish(yf, yb, xbc, z, x2p, n_lat, mods[1], ssd_d[0], ssd_norm_g[0], ssd_out_w[0],
                                    norm2_g[1], router_wt, router_b)
    g2rows = jnp.broadcast_to(mods[1][0::2, None, 5], (bsz, nlt, d)).reshape(bsz * nlt, 1, d)
    out = _moe(hn3.reshape(bsz * n_lat, d), gates3_t, x3.reshape(bsz * n_lat, d), g2rows, w1, w3, w2, 1,
               final_g=final_g, blocks_per_batch=n_lat // MOE_BLOCK)
    return out.reshape(bsz, n_lat, d)
```

```python
import functools

import jax
import jax.numpy as jnp
from jax import lax
from jax.experimental import pallas as pl
from jax.experimental.pallas import tpu as pltpu

F32 = jnp.float32
BF16 = jnp.bfloat16
HIGHEST = lax.Precision.HIGHEST

GRID_W = 64
RMS_EPS = 1e-6
LOG2_E = 1.4426950408889634

S5_GROUP = 16
S5_STATE = 64
S5_T = 16
S5_GB = 8

SSD_HEAD_DIM = 64
SSD_GROUPS = 8
SSD_STATE = 128
SSD_CONV = 5
SSD_CHUNK = 128

N_EXPERT_GROUPS = 4
TOP_K = 2

TOKEN_TILE = 256
ROW_SUBTILES = 2
MOE_BLOCK = 512
MOE_PIECE = 16
MOE_TILE = 512
VMEM_LIMIT_BYTES = 56 * 1024 * 1024


def _cparams(*sem):
    return pltpu.CompilerParams(dimension_semantics=sem, vmem_limit_bytes=VMEM_LIMIT_BYTES)


def _sigmoid(v):
    return 1.0 / (1.0 + jnp.exp(-v))


def _silu(v):
    return v * _sigmoid(v)


def _gelu_tanh(v):
    return 0.5 * v * (1.0 + jnp.tanh(0.7978845608028654 * (v + 0.044715 * (v * v * v))))


def _rms(v, g):
    return v * lax.rsqrt(jnp.mean(v * v, axis=-1, keepdims=True) + RMS_EPS) * g


def _mod_kernel(cc_ref, w_ref, b_ref, o_ref):
    a = _silu(cc_ref[...])
    o_ref[...] = jnp.dot(a, w_ref[...], preferred_element_type=F32, precision=HIGHEST) + b_ref[...]


def _modulation(c, c_ctx, mod_w, mod_b):
    depth, d, d6 = mod_w.shape
    bsz = c.shape[0]
    rows = 8
    cc = jnp.zeros((rows, d), F32).at[:bsz].set(c).at[bsz].set(c_ctx)
    tn = d6 // 4
    out = pl.pallas_call(
        _mod_kernel,
        grid=(depth, d6 // tn),
        in_specs=[
            pl.BlockSpec((rows, d), lambda i, j: (0, 0)),
            pl.BlockSpec((None, d, tn), lambda i, j: (i, 0, j)),
            pl.BlockSpec((None, 1, tn), lambda i, j: (i, 0, j)),
        ],
        out_specs=pl.BlockSpec((None, rows, tn), lambda i, j: (i, 0, j)),
        out_shape=jax.ShapeDtypeStruct((depth, rows, d6), F32),
        compiler_params=_cparams("parallel", "parallel"),
        name="modulation",
    )(cc, mod_w, mod_b.reshape(depth, 1, d6))
    lat = out[:, :bsz].reshape(depth, bsz, 1, 6, d)
    ctx = jnp.broadcast_to(out[:, bsz].reshape(depth, 1, 1, 6, d), (depth, bsz, 1, 6, d))
    return jnp.concatenate([lat, ctx], axis=2).reshape(depth, bsz * 2, 6, d)


def _prenorm_kernel(x_ref, c_ref, g_ref, mod_ref, o_ref, *, n_lat_tiles):
    m = mod_ref[...]
    x = jnp.where(pl.program_id(1) < n_lat_tiles, x_ref[...], c_ref[...])
    hn = _rms(x, g_ref[...]) * (1.0 + m[1:2]) + m[0:1]
    o_ref[...] = hn.astype(o_ref.dtype)


def _lat_ctx_specs(n_lat_tiles, d):
    return [pl.BlockSpec((None, TOKEN_TILE, d), lambda b, i: (b, jnp.minimum(i, n_lat_tiles - 1), 0)),
            pl.BlockSpec((None, TOKEN_TILE, d), lambda b, i: (b, jnp.maximum(i - n_lat_tiles, 0), 0))]


def _prenorm(x, ctx, g, mods, n_lat_tiles):
    bsz, n_lat, d = x.shape
    lt = n_lat + ctx.shape[1]
    nt = lt // TOKEN_TILE
    return pl.pallas_call(
        functools.partial(_prenorm_kernel, n_lat_tiles=n_lat_tiles),
        grid=(bsz, nt),
        in_specs=_lat_ctx_specs(n_lat_tiles, d) + [
            pl.BlockSpec((1, d), lambda b, i: (0, 0)),
            pl.BlockSpec((None, 6, d), lambda b, i: (b * 2 + (i >= n_lat_tiles).astype(jnp.int32), 0, 0)),
        ],
        out_specs=pl.BlockSpec((None, TOKEN_TILE, d), lambda b, i: (b, i, 0)),
        out_shape=jax.ShapeDtypeStruct((bsz, lt, d), F32),
        compiler_params=_cparams("parallel", "parallel"),
        name="prenorm",
    )(x, ctx, g.reshape(1, d), mods)


def _s5_weights(lam_re, lam_im, log_dt, b_re, b_im, c_re, c_im):
    t = S5_T
    k16 = b_re.shape[-1]

    def cmul(ar, ai, br, bi):
        return ar * br - ai * bi, ar * bi + ai * br

    def direction(k):
        lr, li = lam_re[k], lam_im[k]
        step = jnp.exp(log_dt[k])[:, None]
        mag = jnp.exp(lr * step)
        abar_r = mag * jnp.cos(li * step)
        abar_i = mag * jnp.sin(li * step)
        den = lr * lr + li * li
        q_r = ((abar_r - 1.0) * lr + abar_i * li) / den
        q_i = (abar_i * lr - (abar_r - 1.0) * li) / den
        bb_r, bb_i = cmul(q_r[..., None], q_i[..., None], b_re, b_im)

        def power(tau):
            tau = jnp.asarray(tau, F32)[None, :, None]
            m = jnp.exp((lr * step)[:, None, :] * tau)
            return m * jnp.cos((li * step)[:, None, :] * tau), m * jnp.sin((li * step)[:, None, :] * tau)

        return bb_r.transpose(0, 2, 1), bb_i.transpose(0, 2, 1), power

    rows = lambda v: jnp.repeat(v, k16, axis=1)
    row_tile = lambda v: jnp.tile(v, (1, t, 1))
    cols = lambda v: jnp.repeat(v.transpose(0, 2, 1), k16, axis=2)
    col_tile = lambda v: jnp.tile(v, (1, 1, t))
    ct_r, ct_i = c_re.transpose(0, 2, 1), c_im.transpose(0, 2, 1)
    steps = jnp.arange(t)

    def left(bt_r, bt_i, power, tau):
        pr, pi = power(tau)
        return cmul(row_tile(bt_r), row_tile(bt_i), rows(pr), rows(pi))

    def right(power, tau):
        pr, pi = power(tau)
        return cmul(col_tile(ct_r), col_tile(ct_i), cols(pr), cols(pi))

    bf_r, bf_i, pow_f = direction(0)
    bb_r, bb_i, pow_b = direction(1)
    lf_r, lf_i = left(bf_r, bf_i, pow_f, -steps)
    rf_r, rf_i = right(pow_f, steps)
    lb_r, lb_i = left(bb_r, bb_i, pow_b, steps)
    rb_r, rb_i = right(pow_b, -steps)
    lf = jnp.concatenate([lf_r, lf_i], axis=-1)
    lb = jnp.concatenate([lb_r, lb_i], axis=-1)
    rf = jnp.concatenate([rf_r, -rf_i], axis=1)
    rb = jnp.concatenate([rb_r, -rb_i], axis=1)

    sf_r, sf_i = left(bf_r, bf_i, pow_f, t - 1 - steps)
    ws = jnp.concatenate([sf_r, lb_r, sf_i, lb_i], axis=-1)

    of_r, of_i = right(pow_f, steps + 1)
    ob_r, ob_i = right(pow_b, t - steps)
    zero = jnp.zeros_like(of_r)
    w2 = jnp.concatenate([of_r, zero, -of_i, zero, zero, ob_r, zero, -ob_i], axis=1)

    af_r, af_i = pow_f([t])
    ab_r, ab_i = pow_b([t])
    ar = jnp.concatenate([af_r[:, 0], ab_r[:, 0]], axis=-1)
    ai = jnp.concatenate([af_i[:, 0], ab_i[:, 0]], axis=-1)
    return ws.astype(BF16), lf, rf, lb, rb, w2.astype(BF16), ar, ai


def _s5_row_block(nc):
    return max(rb for rb in range(16, min(nc, 176) + 1, 16) if nc % rb == 0)


def _dot_split3(a, b):
    a_hi = a.astype(BF16)
    a_lo = (a - a_hi.astype(F32)).astype(BF16)
    b_hi = b.astype(BF16)
    b_lo = (b - b_hi.astype(F32)).astype(BF16)
    dot = functools.partial(jnp.dot, preferred_element_type=F32)
    return dot(a_hi, b_hi) + dot(a_hi, b_lo) + dot(a_lo, b_hi)


def _s5_kernel(hn_ref, ws_ref, lf_ref, rf_ref, lb_ref, rb_ref, w2_ref, ar_ref, ai_ref, yo_ref,
               u_ref, y_ref, wm_ref, sre, sim, hre_f, him_f, hre_b, him_b, *, n_chunks, n_ctx_chunks, pitch):
    nc, ncc = n_chunks, n_ctx_chunks
    ncl = nc - ncc
    p = S5_STATE
    t_len = S5_T
    gl = S5_GROUP
    per_half = 128 // gl
    rb = _s5_row_block(nc)
    lane_slot = lax.broadcasted_iota(jnp.int32, (rb, 128), 1) // gl

    @pl.when(pl.program_id(1) == 0)
    def _():
        tk = t_len * gl
        src_tok = lax.broadcasted_iota(jnp.int32, (tk, tk), 0) // gl
        dst_tok = lax.broadcasted_iota(jnp.int32, (tk, tk), 1) // gl
        for g in range(S5_GB):
            causal = _dot_split3(lf_ref[g], rf_ref[g])
            anti = _dot_split3(lb_ref[g], rb_ref[g])
            wm = jnp.where(dst_tok >= src_tok, causal, 0.0) + jnp.where(src_tok >= dst_tok, anti, 0.0)
            wm_ref[g] = wm.astype(wm_ref.dtype)

    def slot_transpose(xs):
        xs = list(xs)
        bit = per_half // 2
        while bit >= 1:
            upper = (lane_slot // bit) % 2 == 1
            nxt = list(xs)
            for p in range(per_half):
                if p & bit:
                    continue
                lo, hi = xs[p], xs[p + bit]
                nxt[p] = jnp.where(upper, pltpu.roll(hi, bit * gl, axis=1), lo)
                nxt[p + bit] = jnp.where(upper, hi, pltpu.roll(lo, 128 - bit * gl, axis=1))
            xs = nxt
            bit //= 2
        return xs

    def gather_u(blk, carry):
        r0 = pl.multiple_of(blk * rb, 16)
        halves = []
        for hb in range(t_len // per_half):
            a = [hn_ref[pl.ds(r0 * t_len + hb * per_half + j, rb, stride=t_len), :] for j in range(per_half)]
            halves.append(slot_transpose(a))
        for i in range(S5_GB):
            u_ref[i, pl.ds(r0, rb), :] = jnp.concatenate([h[i] for h in halves], axis=1).astype(u_ref.dtype)
        return carry

    lax.fori_loop(0, nc // rb, gather_u, 0)

    for g in range(S5_GB):
        s = jnp.dot(u_ref[g], ws_ref[g], preferred_element_type=F32)
        sre[pl.ds(g * pitch, nc), :] = s[:, : 2 * p]
        sim[pl.ds(g * pitch, nc), :] = s[:, 2 * p:]

    ar = ar_ref[...]
    ai = ai_ref[...]
    fwd_lane = lax.broadcasted_iota(jnp.int32, (S5_GB, 2 * p), 1) < p

    def step(k, carry):
        h_r, h_i = carry
        cf = jnp.where(k < ncc, ncl + k, k - ncc)
        cb = nc - 1 - k
        rows_f = pl.ds(cf, S5_GB, stride=pitch)
        rows_b = pl.ds(cb, S5_GB, stride=pitch)
        hre_f[rows_f, :] = h_r
        him_f[rows_f, :] = h_i
        hre_b[rows_b, :] = h_r
        him_b[rows_b, :] = h_i
        s_r = jnp.where(fwd_lane, sre[rows_f, :], sre[rows_b, :])
        s_i = jnp.where(fwd_lane, sim[rows_f, :], sim[rows_b, :])
        n_r = ar * h_r - ai * h_i + s_r
        n_i = ar * h_i + ai * h_r + s_i
        return n_r, n_i

    zero = jnp.zeros((S5_GB, 2 * p), F32)
    lax.fori_loop(0, nc, step, (zero, zero), unroll=2)

    for g in range(S5_GB):
        rows = pl.ds(g * pitch, nc)
        hin = jnp.concatenate([hre_f[rows, :], him_f[rows, :], hre_b[rows, :], him_b[rows, :]], axis=1)
        out = jnp.dot(u_ref[g], wm_ref[g], preferred_element_type=F32)
        out = out + jnp.dot(hin.astype(BF16), w2_ref[g], preferred_element_type=F32)
        y_ref[g] = out

    def scatter_y(blk, carry):
        r0 = pl.multiple_of(blk * rb, 16)
        for hb in range(t_len // per_half):
            yv = [y_ref[i, pl.ds(r0, rb), hb * 128:(hb + 1) * 128] for i in range(S5_GB)]
            for j, tok in enumerate(slot_transpose(yv)):
                yo_ref[pl.ds(r0 * t_len + hb * per_half + j, rb, stride=t_len), :] = tok
        return carry

    lax.fori_loop(0, nc // rb, scatter_y, 0)


def _s5_scan(hn, n_lat, weights):
    bsz, lt, d = hn.shape
    ngrp = d // S5_GROUP
    t = S5_T
    tk = t * S5_GROUP
    nc = lt // t
    ncc = (lt - n_lat) // t
    gb = S5_GB
    assert gb * S5_GROUP == 128 and (128 // S5_GROUP) == gb and t % gb == 0
    ws, lf, rf, lb, rb, w2, ar, ai = weights
    p2 = 2 * S5_STATE
    pitch = nc + 8 if (nc // 8) % 2 == 0 else nc
    kern = functools.partial(_s5_kernel, n_chunks=nc, n_ctx_chunks=ncc, pitch=pitch)
    per_group = lambda *s: pl.BlockSpec((gb,) + s, lambda gi, b: (gi,) + tuple(0 for _ in s))
    return pl.pallas_call(
        kern,
        grid=(ngrp // gb, bsz),
        in_specs=[
            pl.BlockSpec((None, lt, 128), lambda gi, b: (b, 0, gi)),
            per_group(tk, 2 * p2),
            per_group(tk, p2), per_group(p2, tk), per_group(tk, p2), per_group(p2, tk),
            per_group(4 * p2, tk),
            per_group(p2), per_group(p2),
        ],
        out_specs=pl.BlockSpec((None, lt, 128), lambda gi, b: (b, 0, gi)),
        out_shape=jax.ShapeDtypeStruct((bsz, lt, d), F32),
        scratch_shapes=[pltpu.VMEM((gb, nc, tk), BF16), pltpu.VMEM((gb, nc, tk), F32), pltpu.VMEM((gb, tk, tk), BF16)]
        + [pltpu.VMEM((gb * pitch, p2), F32) for _ in range(6)],
        compiler_params=_cparams("parallel", "arbitrary"),
        name="s5_scan",
    )(hn, ws, lf, rf, lb, rb, w2, ar, ai)


def _router_gates(hn2, rw_ref, rb_ref):
    n_exp = rw_ref.shape[0]
    epg = n_exp // N_EXPERT_GROUPS
    logits = lax.dot_general(rw_ref[...], hn2, (((1,), (1,)), ((), ())),
                             preferred_element_type=F32, precision=HIGHEST)
    s = _sigmoid(logits)
    sel = s + rb_ref[...]
    row = [sel[e:e + 1] for e in range(n_exp)]
    gscore = []
    for gi in range(N_EXPERT_GROUPS):
        a, b, c, dd = row[gi * epg: gi * epg + epg]
        hi1, lo1 = jnp.maximum(a, b), jnp.minimum(a, b)
        hi2, lo2 = jnp.maximum(c, dd), jnp.minimum(c, dd)
        gscore.append(jnp.maximum(hi1, hi2) + jnp.maximum(jnp.minimum(hi1, hi2), jnp.maximum(lo1, lo2)))
    gmax = functools.reduce(jnp.maximum, gscore)
    gates = []
    taken = None
    for gi in range(N_EXPERT_GROUPS):
        is_max = gscore[gi] == gmax
        best = is_max if taken is None else jnp.logical_and(is_max, jnp.logical_not(taken))
        taken = is_max if taken is None else jnp.logical_or(taken, is_max)
        for e in range(gi * epg, gi * epg + epg):
            rank = jnp.zeros_like(row[e])
            for j in range(gi * epg, gi * epg + epg):
                if j == e:
                    continue
                ahead = (row[j] >= row[e]) if j < e else (row[j] > row[e])
                rank = rank + ahead.astype(F32)
            chosen = jnp.logical_and(best, rank < float(TOP_K))
            gates.append(jnp.where(chosen, s[e:e + 1], 0.0))
    g = jnp.concatenate(gates, axis=0)
    return g / jnp.sum(g, axis=0, keepdims=True)


def _glu_kernel(y_ref, u_ref, x_ref, c_ref, mod_ref, d_ref, w_ref, b_ref, g2_ref, rw_ref, rb_ref,
                x1_ref, hn2_ref, gates_ref, *, n_lat_tiles):
    d = x_ref.shape[-1]
    is_lat = pl.program_id(1) < n_lat_tiles
    m = mod_ref[...]
    nsub = ROW_SUBTILES
    sub = x_ref.shape[0] // nsub
    for s in range(nsub):
        r = slice(s * sub, (s + 1) * sub)
        u = u_ref[r, :].astype(F32)
        a = _gelu_tanh(y_ref[r, :].astype(F32) + d_ref[...] * u)
        z = jnp.dot(a.astype(BF16), w_ref[...], preferred_element_type=F32) + b_ref[...]
        out = z[:, :d] * _sigmoid(z[:, d:])
        x1 = jnp.where(is_lat, x_ref[r, :], c_ref[r, :]) + m[2:3] * out
        x1_ref[r, :] = x1
        hn2 = _rms(x1, g2_ref[...]) * (1.0 + m[4:5]) + m[3:4]
        hn2_ref[r, :] = hn2.astype(hn2_ref.dtype)
        gates_ref[:, r] = _router_gates(hn2, rw_ref, rb_ref)


def _glu_head(y, hn, x, ctx, mods, d_skip, glu_w, glu_b, norm2_g, router_wt, router_b, n_lat_tiles):
    bsz, lt, d = hn.shape
    nt = lt // TOKEN_TILE
    n_exp = router_wt.shape[0]
    tok = pl.BlockSpec((None, TOKEN_TILE, d), lambda b, i: (b, i, 0))
    vec = lambda n: pl.BlockSpec((1, n), lambda b, i: (0, 0))
    return pl.pallas_call(
        functools.partial(_glu_kernel, n_lat_tiles=n_lat_tiles),
        grid=(bsz, nt),
        in_specs=[tok, tok] + _lat_ctx_specs(n_lat_tiles, d) + [
            pl.BlockSpec((None, 6, d), lambda b, i: (b * 2 + (i >= n_lat_tiles).astype(jnp.int32), 0, 0)),
            vec(d),
            pl.BlockSpec((d, 2 * d), lambda b, i: (0, 0)),
            vec(2 * d),
            vec(d),
            pl.BlockSpec((n_exp, d), lambda b, i: (0, 0)),
            pl.BlockSpec((n_exp, 1), lambda b, i: (0, 0)),
        ],
        out_specs=[
            tok, tok,
            pl.BlockSpec((n_exp, TOKEN_TILE), lambda b, i: (0, b * nt + i)),
        ],
        out_shape=[
            jax.ShapeDtypeStruct((bsz, lt, d), F32),
            jax.ShapeDtypeStruct((bsz, lt, d), BF16),
            jax.ShapeDtypeStruct((n_exp, bsz * lt), F32),
        ],
        compiler_params=_cparams("parallel", "parallel"),
        name="s5_glu_head",
    )(y, hn, x, ctx, mods, d_skip.reshape(1, d), glu_w.astype(BF16), glu_b.reshape(1, 2 * d),
      norm2_g.reshape(1, d), router_wt, router_b.reshape(n_exp, 1))


def _moe_slots_padded(total):
    return jnp.floor((total + float(MOE_PIECE - 1)) * (1.0 / MOE_PIECE)) * float(MOE_PIECE)


def _moe_sort_kernel(gt_ref, t_ref, ts_ref, cnt_ref, *, n_blocks):
    n_exp, nb = gt_ref.shape
    nbpad = ts_ref.shape[0]
    sel = jnp.logical_and(gt_ref[...] > 0.0, pl.program_id(0) < n_blocks)
    sel_b = jnp.where(sel, 1.0, 0.0).astype(BF16)
    earlier = lax.broadcasted_iota(jnp.int32, (nb, nb), 0) < lax.broadcasted_iota(jnp.int32, (nb, nb), 1)
    rank = jnp.dot(sel_b, jnp.where(earlier, 1.0, 0.0).astype(BF16), preferred_element_type=F32)
    total = jnp.sum(jnp.where(sel, 1.0, 0.0), axis=1, keepdims=True)
    padded = jnp.broadcast_to(_moe_slots_padded(total), (n_exp, 128))
    below = lax.broadcasted_iota(jnp.int32, (n_exp, n_exp), 1) < lax.broadcasted_iota(jnp.int32, (n_exp, n_exp), 0)
    offs = jnp.dot(jnp.where(below, 1.0, 0.0).astype(BF16), padded.astype(BF16), preferred_element_type=F32)[:, 0:1]
    dest = offs + rank
    d_lo_f = jnp.min(jnp.where(sel, dest, float(nbpad)), axis=0, keepdims=True)
    d_hi_f = jnp.max(jnp.where(sel, dest, -1.0), axis=0, keepdims=True)
    slot = lax.broadcasted_iota(jnp.int32, (nbpad, nb), 0)
    at_lo = jnp.where(slot == d_lo_f.astype(jnp.int32), 1.0, 0.0).astype(BF16)
    at_hi = jnp.where(slot == d_hi_f.astype(jnp.int32), 1.0, 0.0).astype(BF16)
    d = t_ref.shape[1]
    ts_ref[:, :d] = jnp.dot(at_lo + at_hi, t_ref[...], preferred_element_type=F32).astype(ts_ref.dtype)
    gts = gt_ref[...]

    def gate_rows(d_f):
        g = jnp.sum(jnp.where(jnp.logical_and(sel, dest == d_f), gts, 0.0), axis=0, keepdims=True)
        g_hi = g.astype(BF16).astype(F32)
        g_mid = (g - g_hi).astype(BF16).astype(F32)
        g_lo = g - g_hi - g_mid
        row = lax.broadcasted_iota(jnp.int32, (128, nb), 0)
        terms = jnp.where(row == 0, g_hi, jnp.where(row == 1, g_mid, jnp.where(row == 2, g_lo, 0.0)))
        return terms.astype(BF16)

    gate_cols = lax.dot_general(jnp.concatenate([at_lo, at_hi], axis=1),
                                jnp.concatenate([gate_rows(d_lo_f), gate_rows(d_hi_f)], axis=1),
                                (((1,), (1,)), ((), ())), preferred_element_type=F32)
    ts_ref[:, d:] = gate_cols.astype(ts_ref.dtype)
    cnt_ref[...] = padded


def _moe_expert_kernel(tile_e, src_rows, dst_rows, ntiles, ts_hbm, w1_ref, w3_ref, w2_ref, ys_hbm,
                       tbuf, ybuf, w1b, w3b, w2b, sem_in, sem_out):
    t = pl.program_id(0)
    nt = ntiles[0]
    last = pl.num_programs(0) - 1
    ppt = MOE_TILE // MOE_PIECE
    slot = t % 2

    def rows_at(table, tt, p):
        return pl.ds(pl.multiple_of(table[tt * ppt + p], MOE_PIECE), MOE_PIECE)

    def copy_in(tt, sl, p):
        return pltpu.make_async_copy(ts_hbm.at[rows_at(src_rows, tt, p), :],
                                     tbuf.at[sl, pl.ds(p * MOE_PIECE, MOE_PIECE), :], sem_in.at[sl])

    def copy_out(tt, sl, p):
        return pltpu.make_async_copy(ybuf.at[sl, pl.ds(p * MOE_PIECE, MOE_PIECE), :],
                                     ys_hbm.at[rows_at(dst_rows, tt, p), pl.ds(0, ybuf.shape[2])], sem_out.at[sl])

    def start_in(tt, sl):
        for p in range(ppt):
            copy_in(tt, sl, p).start()

    def wait_out(tt, sl):
        for p in range(ppt):
            copy_out(tt, sl, p).wait()

    @pl.when(jnp.logical_and(t == 0, nt > 0))
    def _():
        start_in(0, 0)

    @pl.when(t + 1 < nt)
    def _():
        start_in(t + 1, 1 - slot)

    @pl.when(jnp.logical_and(t >= 2, t - 2 < nt))
    def _():
        wait_out(t - 2, slot)

    @pl.when(t < nt)
    def _():
        @pl.when(jnp.logical_or(t == 0, tile_e[t] != tile_e[jnp.maximum(t - 1, 0)]))
        def _():
            w1b[...] = w1_ref[...].astype(BF16)
            w3b[...] = w3_ref[...].astype(BF16)
            w2b[...] = w2_ref[...].astype(BF16)

        for p in range(ppt):
            copy_in(t, slot, p).wait()
        d = ybuf.shape[2]
        x = tbuf[slot, :, :d]
        gate = jnp.sum(tbuf[slot, :, d:].astype(F32), axis=1, keepdims=True)
        h = _silu(jnp.dot(x, w1b[...], preferred_element_type=F32)) * jnp.dot(x, w3b[...], preferred_element_type=F32)
        ybuf[slot] = (gate * jnp.dot(h.astype(BF16), w2b[...], preferred_element_type=F32)).astype(ybuf.dtype)
        for p in range(ppt):
            copy_out(t, slot, p).start()

    @pl.when(t == last)
    def _():
        @pl.when(jnp.logical_and(last >= 1, last - 1 < nt))
        def _():
            wait_out(last - 1, 1 - slot)

        @pl.when(last < nt)
        def _():
            wait_out(last, slot)


def _moe_unsort_kernel(ys_ref, g_ref, x_ref, g2_ref, *rest, final):
    if final:
        fg_ref, o_ref = rest
    else:
        (o_ref,) = rest
    n_exp = g_ref.shape[1]
    nb = MOE_BLOCK
    nsub = g_ref.shape[0] // nb
    nbpad = ys_ref.shape[0] // nsub
    halves = g2_ref.shape[0] // nsub
    for sb in range(nsub):
        _moe_unsort_block(ys_ref.at[sb * nbpad:(sb + 1) * nbpad], g_ref.at[sb * nb:(sb + 1) * nb],
                          x_ref.at[sb * nb:(sb + 1) * nb], g2_ref.at[sb * halves:(sb + 1) * halves],
                          fg_ref if final else None, o_ref, sb, final)


def _moe_unsort_block(ys_ref, g_ref, x_ref, g2_ref, fg_ref, o_ref, sb, final):
    nb, n_exp = g_ref.shape
    nbpad = ys_ref.shape[0]
    gates = g_ref[...]
    sel = gates > 0.0
    sel_b = jnp.where(sel, 1.0, 0.0).astype(BF16)
    earlier = lax.broadcasted_iota(jnp.int32, (nb, nb), 1) < lax.broadcasted_iota(jnp.int32, (nb, nb), 0)
    rank = jnp.dot(jnp.where(earlier, 1.0, 0.0).astype(BF16), sel_b, preferred_element_type=F32)
    total = jnp.sum(jnp.where(sel, 1.0, 0.0), axis=0, keepdims=True)
    padded = jnp.broadcast_to(_moe_slots_padded(total), (8, n_exp))
    below = lax.broadcasted_iota(jnp.int32, (n_exp, n_exp), 0) < lax.broadcasted_iota(jnp.int32, (n_exp, n_exp), 1)
    offs = jnp.dot(padded.astype(BF16), jnp.where(below, 1.0, 0.0).astype(BF16), preferred_element_type=F32)[0:1]
    dest = offs + rank
    d_lo = jnp.min(jnp.where(sel, dest, float(nbpad)), axis=1, keepdims=True).astype(jnp.int32)
    d_hi = jnp.max(jnp.where(sel, dest, -1.0), axis=1, keepdims=True).astype(jnp.int32)
    slot = lax.broadcasted_iota(jnp.int32, (nb, nbpad), 1)
    pick = jnp.where(jnp.logical_or(slot == d_lo, slot == d_hi), 1.0, 0.0).astype(BF16)
    moe = jnp.dot(pick, ys_ref[...], preferred_element_type=F32)
    half = nb // g2_ref.shape[0]
    d = x_ref.shape[1]
    for j in range(g2_ref.shape[0]):
        r = slice(j * half, (j + 1) * half)
        xn = x_ref[r, :] + g2_ref[j] * moe[r]
        if final:
            xn = _rms(xn, fg_ref[...])
            for c in range(half // SSD_CHUNK):
                col = (sb * nb + j * half) // SSD_CHUNK + c
                o_ref[:, col, :] = xn[c * SSD_CHUNK:(c + 1) * SSD_CHUNK]
        else:
            o_ref[pl.ds(sb * nb + j * half, half), :] = xn


def _moe_schedule(counts, nbpad, n_tiles):
    nblk, n_exp = counts.shape
    ppt = MOE_TILE // MOE_PIECE
    pc = counts // MOE_PIECE
    loc = jnp.cumsum(pc, axis=1) - pc
    cum_b = jnp.cumsum(pc, axis=0)
    np_e = cum_b[-1]
    tiles_e = (np_e + ppt - 1) // ppt
    tile_end = jnp.cumsum(tiles_e)
    ntiles = tile_end[-1]
    t_idx = jnp.arange(n_tiles, dtype=jnp.int32)
    tile_e = jnp.minimum(jnp.sum((tile_end[None, :] <= t_idx[:, None]).astype(jnp.int32), axis=1), n_exp - 1)
    first = (tile_end - tiles_e)[tile_e]
    piece0 = (t_idx - first) * ppt
    npieces = jnp.where(t_idx < ntiles, jnp.clip(np_e[tile_e] - piece0, 0, ppt), 0)
    slot_t = jnp.repeat(t_idx, ppt)
    piece = jnp.tile(jnp.arange(ppt, dtype=jnp.int32), n_tiles)
    slot_e = jnp.repeat(tile_e, ppt)
    i = jnp.repeat(piece0, ppt) + piece
    cum_s = cum_b[:, slot_e]
    blk = jnp.minimum(jnp.sum((cum_s <= i[None, :]).astype(jnp.int32), axis=0), nblk - 1)
    in_blk = jnp.arange(nblk, dtype=jnp.int32)[:, None] == blk[None, :]
    before = jnp.sum(jnp.where(in_blk, cum_s - pc[:, slot_e], 0), axis=0)
    within = i - before + jnp.sum(jnp.where(in_blk, loc[:, slot_e], 0), axis=0)
    rows = blk * nbpad + within * MOE_PIECE
    real = piece < jnp.repeat(npieces, ppt)
    spare = nblk * nbpad
    src = jnp.where(real, rows, spare + piece * MOE_PIECE)
    dst = jnp.where(real, rows, spare + (ppt + (slot_t % 2) * ppt + piece) * MOE_PIECE)
    return tile_e.astype(jnp.int32), src.astype(jnp.int32), dst.astype(jnp.int32), ntiles.reshape(1).astype(jnp.int32)


def _moe(t, gates_t, xres, g2rows, w1, w3, w2, layer, *, final_g=None, blocks_per_batch=None):
    n, d = t.shape
    _, n_exp, _, f = w1.shape
    nb = MOE_BLOCK
    nblk = n // nb
    nbpad = TOP_K * nb + n_exp * MOE_PIECE
    final = final_g is not None

    ppt = MOE_TILE // MOE_PIECE
    nspare = -(-3 * ppt * MOE_PIECE // nbpad)
    last_blk = lambda j: jnp.minimum(j, nblk - 1)
    ts, cnt = pl.pallas_call(
        functools.partial(_moe_sort_kernel, n_blocks=nblk),
        grid=(nblk + nspare,),
        in_specs=[pl.BlockSpec((n_exp, nb), lambda j: (0, last_blk(j))), pl.BlockSpec((nb, d), lambda j: (last_blk(j), 0))],
        out_specs=[pl.BlockSpec((nbpad, d + 128), lambda j: (j, 0)), pl.BlockSpec((None, n_exp, 128), lambda j: (j, 0, 0))],
        out_shape=[jax.ShapeDtypeStruct(((nblk + nspare) * nbpad, d + 128), BF16),
                   jax.ShapeDtypeStruct((nblk + nspare, n_exp, 128), F32)],
        compiler_params=_cparams("parallel"),
        name="moe_sort",
    )(gates_t, t)

    n_tiles = nblk * nbpad // MOE_TILE + n_exp
    tile_e, src_rows, dst_rows, ntiles = _moe_schedule(cnt[:nblk, :, 0].astype(jnp.int32), nbpad, n_tiles)
    wspec = lambda shape: pl.BlockSpec((None, None) + shape, lambda i, te, sr, dr, nt: (layer, te[i], 0, 0))
    ys = pl.pallas_call(
        _moe_expert_kernel,
        grid_spec=pltpu.PrefetchScalarGridSpec(
            num_scalar_prefetch=4,
            grid=(n_tiles,),
            in_specs=[pl.BlockSpec(memory_space=pl.ANY), wspec((d, f)), wspec((d, f)), wspec((f, d))],
            out_specs=pl.BlockSpec(memory_space=pl.ANY),
            scratch_shapes=[pltpu.VMEM((2, MOE_TILE, d + 128), BF16), pltpu.VMEM((2, MOE_TILE, d), BF16),
                            pltpu.VMEM((d, f), BF16), pltpu.VMEM((d, f), BF16), pltpu.VMEM((f, d), BF16),
                            pltpu.SemaphoreType.DMA((2,)), pltpu.SemaphoreType.DMA((2,))],
        ),
        out_shape=jax.ShapeDtypeStruct(((nblk + nspare) * nbpad, d + 128), BF16),
        input_output_aliases={4: 0},
        compiler_params=_cparams("arbitrary"),
        name="moe_experts",
    )(tile_e, src_rows, dst_rows, ntiles, ts, w1, w3, w2)

    nsub = SSD_COLS_PER_STEP * SSD_CHUNK // nb if final else 1
    halves = nb // TOKEN_TILE
    in_specs = [
        pl.BlockSpec((nsub * nbpad, d), lambda j: (j, 0)),
        pl.BlockSpec((nsub * nb, n_exp), lambda j: (j, 0)),
        pl.BlockSpec((nsub * nb, d), lambda j: (j, 0)),
        pl.BlockSpec((nsub * halves, 1, d), lambda j: (j, 0, 0)),
    ]
    args = [ys, gates_t.T, xres, g2rows]
    if final:
        in_specs.append(pl.BlockSpec((1, d), lambda j: (0, 0)))
        args.append(final_g.reshape(1, d))
        assert blocks_per_batch % nsub == 0 and nsub * nb == SSD_COLS_PER_STEP * SSD_CHUNK
        spb = blocks_per_batch // nsub
        out_spec = pl.BlockSpec((None, SSD_CHUNK, SSD_COLS_PER_STEP, d), lambda j: (j // spb, 0, j % spb, 0))
        out_shape = jax.ShapeDtypeStruct((nblk // blocks_per_batch, SSD_CHUNK, GRID_W, d), F32)
    else:
        out_spec = pl.BlockSpec((nb, d), lambda j: (j, 0))
        out_shape = jax.ShapeDtypeStruct((n, d), F32)
    return pl.pallas_call(
        functools.partial(_moe_unsort_kernel, final=final),
        grid=(nblk // nsub,),
        in_specs=in_specs,
        out_specs=out_spec,
        out_shape=out_shape,
        compiler_params=_cparams("parallel"),
        name="moe_unsort_final" if final else "moe_unsort",
    )(*args)


SSD_COLS_PER_STEP = 8


def _ssd_inproj_kernel(xl_ref, xc_ref, g_ref, mod_ref, wz_ref, wx_ref, wdt_ref, wdtt_ref, bias_ref, biast_ref,
                       a_ref, at_ref, cw_ref, cb_ref, z_ref, xbc_ref, csdt_ref, cst_ref, xp_ref, slab, xt, ext, slabs, slabs_in,
                       *, n_lat_steps, n_ctx_chunks):
    step = pl.program_id(1)
    ip = pl.program_id(2)
    q = SSD_CHUNK
    ncols = SSD_COLS_PER_STEP
    d = xt.shape[1]
    is_lat = step < n_lat_steps
    c0 = step * ncols + 2 * ip
    n_lat_chunks = n_lat_steps * ncols
    n_chunks = n_lat_chunks + n_ctx_chunks
    halo = SSD_CONV // 2
    nsl = ext.shape[2]
    pitch = slabs.shape[1] // nsl

    def emit_conv(sub):
        w = [cw_ref[k] for k in range(SSD_CONV)]
        bias = cb_ref[...]
        for t in range(q):
            acc = bias
            for k in range(SSD_CONV):
                acc = acc + w[k] * ext[sub, t + k]
            y = _silu(acc)
            for a in range(nsl // 8):
                slabs[sub, pl.ds(a * 8 * pitch + t, 8, stride=pitch), :] = y[a * 8:(a + 1) * 8]
        for s in range(nsl):
            xbc_ref[sub * q:(sub + 1) * q, s * 128:(s + 1) * 128] = (
                slabs[sub, s * pitch:s * pitch + q, :].astype(xbc_ref.dtype))

    def project(sub):
        other = 1 - sub
        rows = slice(sub * q, (sub + 1) * q)
        m = mod_ref[...]
        x = xt[pl.ds(pl.multiple_of((2 * ip + sub) * q, q), q), :]
        xp_ref[rows, :] = x
        hn = (_rms(x, g_ref[...]) * (1.0 + m[1:2]) + m[0:1]).astype(BF16)
        z_ref[rows, :] = jnp.dot(hn, wz_ref[...], preferred_element_type=F32).astype(z_ref.dtype)
        xbc_new = jnp.dot(hn, wx_ref[...], preferred_element_type=F32)
        for s in range(nsl):
            slabs_in[sub, s * pitch:s * pitch + q, :] = xbc_new[:, s * 128:(s + 1) * 128]
        for t in range(q):
            for a in range(nsl // 8):
                ext[sub, halo + t, a * 8:(a + 1) * 8, :] = slabs_in[sub, pl.ds(a * 8 * pitch + t, 8, stride=pitch), :]
        if sub == 0:
            starts = jnp.logical_or(c0 == 0, c0 == n_lat_chunks)
            ext[other, halo + q:2 * halo + q] = jnp.where(starts, 0.0, ext[sub, halo:2 * halo])
            ext[sub, 0:halo] = jnp.where(starts, 0.0, ext[other, q:q + halo])
        else:
            ext[other, halo + q:2 * halo + q] = ext[sub, halo:2 * halo]
            ext[sub, 0:halo] = ext[other, q:q + halo]

        def softplus(v):
            return jnp.maximum(v, 0.0) + jnp.log(1.0 + jnp.exp(-jnp.abs(v)))

        r_i = lax.broadcasted_iota(jnp.int32, (q, q), 0)
        c_i = lax.broadcasted_iota(jnp.int32, (q, q), 1)
        lower = (r_i >= c_i).astype(F32)
        upper = (r_i <= c_i).astype(F32)
        dt = softplus(jnp.dot(hn, wdt_ref[...], preferred_element_type=F32) + bias_ref[...])
        da = dt * a_ref[...]
        half = da.shape[1] // 2
        cs = jnp.concatenate([jnp.dot(lower, da[:, :half], preferred_element_type=F32, precision=HIGHEST),
                              jnp.dot(upper, da[:, half:], preferred_element_type=F32, precision=HIGHEST)], axis=1)
        lane = lax.broadcasted_iota(jnp.int32, dt.shape, 1)
        csdt_ref[rows, :] = jnp.where(lane % 8 < 4, cs, dt)
        nh = at_ref.shape[0] // 2
        dtt = softplus(lax.dot_general(wdtt_ref[...], hn, (((1,), (1,)), ((), ())), preferred_element_type=F32)
                       + biast_ref[...])
        dat = dtt * at_ref[...]
        cst_f = jnp.dot(dat[:nh], upper, preferred_element_type=F32, precision=HIGHEST)
        cst_b = jnp.dot(dat[nh:], lower, preferred_element_type=F32, precision=HIGHEST)
        cst = jnp.concatenate([cst_f, cst_b], axis=0)
        for j in range(cst_ref.shape[1]):
            cst_ref[sub, j] = cst[j * 4:(j + 1) * 4, :]

    @pl.when(jnp.logical_and(ip == 0, is_lat))
    def _():
        for s in range(d // 128):
            slab[...] = xl_ref[:, :, s * 128:(s + 1) * 128].reshape(q * ncols, 128)
            for w in range(ncols):
                xt[w * q:(w + 1) * q, s * 128:(s + 1) * 128] = slab[pl.ds(w, q, stride=ncols), :]

    @pl.when(jnp.logical_and(ip == 0, jnp.logical_not(is_lat)))
    def _():
        xt[0:n_ctx_chunks * q, :] = xc_ref[...]

    @pl.when(c0 == 0)
    def _():
        ext[...] = jnp.zeros_like(ext)

    @pl.when(jnp.logical_or(is_lat, 2 * ip < n_ctx_chunks))
    def _():
        for sub in range(2):
            emit_conv(sub)
            project(sub)

    @pl.when(c0 == n_chunks)
    def _():
        ext[1, halo + q:2 * halo + q] = jnp.zeros((halo,) + ext.shape[2:], F32)
        for sub in range(2):
            emit_conv(sub)


def _ssd_inproj(xall, n_lat, norm_g, mods, in_w, dt_bias, a_log, conv_w, conv_b):
    bsz, lt, d = xall.shape
    q = SSD_CHUNK
    ncols = SSD_COLS_PER_STEP
    ncl = n_lat // q
    nc = lt // q
    ncc = nc - ncl
    n_ctx = lt - n_lat
    nh2 = dt_bias.size
    d_inner = (nh2 // 2) * SSD_HEAD_DIM
    conv_ch = in_w.shape[1] - d_inner - nh2
    wz = in_w[:, :d_inner].astype(BF16)
    wx = in_w[:, d_inner:d_inner + conv_ch].astype(BF16)
    wdt = in_w[:, d_inner + conv_ch:].astype(BF16)
    a = -jnp.exp(a_log.astype(F32)).reshape(nh2) * LOG2_E
    bias = dt_bias.astype(F32).reshape(nh2)
    ngr = nh2 // 4
    lanes = jnp.arange(2 * nh2)
    dup = (lanes // 8) * 4 + lanes % 4
    assert n_lat // GRID_W == q and GRID_W % ncols == 0 and lt % GRID_W == 0 and ncc + 2 <= ncols and ncc % 2 == 0 and n_lat % n_ctx == 0
    nls = ncl // ncols
    xgrid = xall.reshape(bsz, lt // GRID_W, GRID_W, d)
    kern = functools.partial(_ssd_inproj_kernel, n_lat_steps=nls, n_ctx_chunks=ncc)
    full = lambda s: pl.BlockSpec(s, lambda b, st, i: tuple(0 for _ in s), pipeline_mode=pl.Buffered(1))
    pps = ncols // 2
    npairs = nc // 2
    pair = lambda b, st, i: jnp.minimum(st * pps + i, npairs - 1)
    rows = lambda width: pl.BlockSpec((None, 2 * q, width), lambda b, st, i: (b, pair(b, st, i), 0))
    return pl.pallas_call(
        kern,
        grid=(bsz, nls + 1, pps),
        in_specs=[
            pl.BlockSpec((None, q, ncols, d), lambda b, st, i: (b, 0, jnp.minimum(st, nls - 1), 0)),
            pl.BlockSpec((None, n_ctx, d), lambda b, st, i: (b, n_lat // n_ctx, 0)),
            full((1, d)),
            pl.BlockSpec((None, 6, d), lambda b, st, i: (b * 2 + (st >= nls).astype(jnp.int32), 0, 0)),
            full((d, d_inner)), full((d, conv_ch)), full((d, 2 * nh2)), full((nh2, d)),
            full((1, 2 * nh2)), full((nh2, 1)), full((1, 2 * nh2)), full((nh2, 1)),
            full((SSD_CONV, conv_ch // 128, 128)), full((conv_ch // 128, 128)),
        ],
        out_specs=[
            rows(d_inner),
            pl.BlockSpec((None, 2 * q, conv_ch), lambda b, st, i: (b, jnp.clip(st * pps + i - 1, 0, npairs - 1), 0)),
            rows(2 * nh2),
            pl.BlockSpec((None, 2, ngr, 4, q), lambda b, st, i: (b, pair(b, st, i), 0, 0, 0)),
            rows(d),
        ],
        out_shape=[
            jax.ShapeDtypeStruct((bsz, lt, d_inner), BF16),
            jax.ShapeDtypeStruct((bsz, lt, conv_ch), BF16),
            jax.ShapeDtypeStruct((bsz, lt, 2 * nh2), F32),
            jax.ShapeDtypeStruct((bsz, nc, ngr, 4, q), F32),
            jax.ShapeDtypeStruct((bsz, lt, d), F32),
        ],
        scratch_shapes=[pltpu.VMEM((q * ncols, 128), F32), pltpu.VMEM((q * ncols, d), F32),
                        pltpu.VMEM((2, q + 2 * (SSD_CONV // 2), conv_ch // 128, 128), F32),
                        pltpu.VMEM((2, conv_ch // 128 * (q + 8), 128), F32),
                        pltpu.VMEM((2, conv_ch // 128 * (q + 8), 128), F32)],
        compiler_params=_cparams("parallel", "arbitrary", "arbitrary"),
        name="ssd_inproj",
    )(xgrid, xall, norm_g.reshape(1, d), mods, wz, wx, wdt[:, dup], wdt.T, bias[dup].reshape(1, -1),
      bias.reshape(nh2, 1), a[dup].reshape(1, -1), a.reshape(nh2, 1), conv_w.astype(F32).reshape(SSD_CONV, conv_ch // 128, 128),
      conv_b.astype(F32).reshape(conv_ch // 128, 128))


def _ssd_scan_dir(x, bm, cm, v, cst, state, reverse):
    q = SSD_CHUNK
    hp = SSD_HEAD_DIM
    gw = x.shape[1]
    r = gw // hp
    lane = lax.broadcasted_iota(jnp.int32, (q, 2 * hp), 1)
    first = lane < hp
    bc = [jnp.broadcast_to(v[:, j:j + 1], (q, 2 * hp)) for j in range(2 * r)]

    def head_lanes(cols):
        return jnp.concatenate([jnp.where(first, cols[2 * p], cols[2 * p + 1]) for p in range(r // 2)], axis=1)

    cs_x = head_lanes(bc[:r])
    dt_x = head_lanes(bc[r:])
    end = 0 if reverse else q - 1
    cs_end = cs_x[end:end + 1, :]
    xdt = x * dt_x
    xw = (xdt * jnp.exp2(cs_end - cs_x)).astype(BF16)
    cb = lax.dot_general(cm, bm, (((1,), (1,)), ((), ())), preferred_element_type=F32)
    y_off = jnp.dot(cm, state.astype(BF16), preferred_element_type=F32) * jnp.exp2(cs_x)
    r_i = lax.broadcasted_iota(jnp.int32, (q, q), 0)
    c_i = lax.broadcasted_iota(jnp.int32, (q, q), 1)
    mask = (r_i <= c_i) if reverse else (r_i >= c_i)
    ys = []
    for pair in range(r // 2):
        xp = xdt[:, pair * 2 * hp:(pair + 1) * 2 * hp]
        rhs = jnp.concatenate([jnp.where(first, xp, 0.0), jnp.where(first, 0.0, xp)], axis=0).astype(BF16)
        gmats = []
        for h in (2 * pair, 2 * pair + 1):
            seg = jnp.exp2(jnp.where(mask, bc[h] - cst[h:h + 1, :], -1e30))
            gmats.append((cb * seg).astype(BF16))
        ys.append(jnp.dot(jnp.concatenate(gmats, axis=1), rhs, preferred_element_type=F32))
    upd = lax.dot_general(bm, xw, (((0,), (0,)), ((), ())), preferred_element_type=F32)
    return jnp.concatenate(ys, axis=1) + y_off, state * jnp.exp2(cs_end) + upd


def _ssd_scan_kernel(xf, bf, cf, csdtf, cstf, xb, bb, cb, csdtb, cstb, yf_ref, yb_ref, state_f, state_b):
    @pl.when(pl.program_id(1) == 0)
    def _():
        state_f[...] = jnp.zeros_like(state_f)
        state_b[...] = jnp.zeros_like(state_b)

    ngr, n, gw = state_f.shape
    r = gw // SSD_HEAD_DIM
    dirs = ((xf, bf, cf, csdtf, cstf, yf_ref, state_f, False, 0), (xb, bb, cb, csdtb, cstb, yb_ref, state_b, True, ngr))
    for g in range(ngr):
        for x_ref, b_ref, c_ref, csdt_ref, cst_ref, y_ref, state, reverse, lane_group0 in dirs:
            j = lane_group0 + g
            y, new_state = _ssd_scan_dir(
                x_ref[:, g * gw:(g + 1) * gw].astype(F32), b_ref[:, g * n:(g + 1) * n], c_ref[:, g * n:(g + 1) * n],
                csdt_ref[:, 2 * r * j:2 * r * (j + 1)], cst_ref[g], state[g], reverse)
            y_ref[:, g * gw:(g + 1) * gw] = y.astype(y_ref.dtype)
            state[g] = new_state


def _ssd_scan(xbc, csdt, cst, n_lat):
    bsz, lt, ch = xbc.shape
    q = SSD_CHUNK
    nc, ncl = lt // q, n_lat // q
    ngr = SSD_GROUPS
    n = SSD_STATE
    d_inner = ch - 2 * ngr * n
    gw = d_inner // ngr
    assert d_inner % (ngr * n) == 0 and q == 2 * SSD_HEAD_DIM

    def specs(chunk, direction):
        return [
            pl.BlockSpec((None, q, d_inner), lambda b, k: (b, chunk(k), 0)),
            pl.BlockSpec((None, q, ngr * n), lambda b, k: (b, chunk(k), d_inner // (ngr * n))),
            pl.BlockSpec((None, q, ngr * n), lambda b, k: (b, chunk(k), d_inner // (ngr * n) + 1)),
            pl.BlockSpec((None, q, csdt.shape[2]), lambda b, k: (b, chunk(k), 0)),
            pl.BlockSpec((None, None, ngr, 4, q), lambda b, k: (b, chunk(k), direction, 0, 0)),
        ]

    fwd = lambda k: (k + ncl) % nc
    bwd = lambda k: nc - 1 - k
    out = jax.ShapeDtypeStruct((bsz, lt, d_inner), BF16)
    return pl.pallas_call(
        _ssd_scan_kernel,
        grid=(bsz, nc),
        in_specs=specs(fwd, 0) + specs(bwd, 1),
        out_specs=[pl.BlockSpec((None, q, d_inner), lambda b, k: (b, fwd(k), 0)),
                   pl.BlockSpec((None, q, d_inner), lambda b, k: (b, bwd(k), 0))],
        out_shape=[out, out],
        scratch_shapes=[pltpu.VMEM((ngr, n, gw), F32), pltpu.VMEM((ngr, n, gw), F32)],
        compiler_params=_cparams("parallel", "arbitrary"),
        name="ssd_scan",
    )(xbc, xbc, xbc, csdt, cst, xbc, xbc, xbc, csdt, cst)


def _ssd_finish_kernel(yf_ref, yb_ref, xs_ref, z_ref, x_ref, mod_ref, dsk_ref, ng_ref, w_ref, g2_ref,
                       rw_ref, rb_ref, x3_ref, hn2_ref, gates_ref):
    m = mod_ref[...]
    nsub = ROW_SUBTILES
    sub = x_ref.shape[0] // nsub
    for s in range(nsub):
        r = slice(s * sub, (s + 1) * sub)
        y = yf_ref[r, :].astype(F32) + yb_ref[r, :].astype(F32) + dsk_ref[...] * xs_ref[r, :].astype(F32)
        gated = y * _silu(z_ref[r, :].astype(F32))
        nrm = _rms(gated, ng_ref[...])
        out = jnp.dot(nrm.astype(BF16), w_ref[...], preferred_element_type=F32)
        x3 = x_ref[r, :] + m[2:3] * out
        x3_ref[r, :] = x3
        hn2 = _rms(x3, g2_ref[...]) * (1.0 + m[4:5]) + m[3:4]
        hn2_ref[r, :] = hn2.astype(hn2_ref.dtype)
        gates_ref[:, r] = _router_gates(hn2, rw_ref, rb_ref)


def _ssd_finish(yf, yb, xbc, z, xall, n_lat, mods, d_skip, norm_g, out_w, norm2_g, router_wt, router_b):
    bsz, lt, d_inner = z.shape
    d = xall.shape[-1]
    q = TOKEN_TILE
    ncl = n_lat // q
    n_exp = router_wt.shape[0]
    inner = pl.BlockSpec((None, q, d_inner), lambda b, c: (b, c, 0))
    tok = pl.BlockSpec((None, q, d), lambda b, c: (b, c, 0))
    full = lambda s: pl.BlockSpec(s, lambda b, c: tuple(0 for _ in s))
    dsk = jnp.repeat(d_skip.astype(F32), SSD_HEAD_DIM).reshape(1, d_inner)
    return pl.pallas_call(
        _ssd_finish_kernel,
        grid=(bsz, ncl),
        in_specs=[
            inner, inner, inner, inner,
            tok,
            pl.BlockSpec((None, 6, d), lambda b, c: (b * 2, 0, 0)),
            full((1, d_inner)), full((1, d_inner)), full((d_inner, d)), full((1, d)),
            full((n_exp, d)), full((n_exp, 1)),
        ],
        out_specs=[tok, tok, pl.BlockSpec((n_exp, q), lambda b, c: (0, b * ncl + c))],
        out_shape=[
            jax.ShapeDtypeStruct((bsz, n_lat, d), F32),
            jax.ShapeDtypeStruct((bsz, n_lat, d), BF16),
            jax.ShapeDtypeStruct((n_exp, bsz * n_lat), F32),
        ],
        compiler_params=_cparams("parallel", "parallel"),
        name="ssd_finish",
    )(yf, yb, xbc, z, xall, mods, dsk, norm_g.reshape(1, d_inner), out_w.astype(BF16),
      norm2_g.reshape(1, d), router_wt, router_b.reshape(n_exp, 1))


def kernel(x, c, ctx, c_ctx, mod_w, mod_b, norm1_g, norm2_g, final_g, s5_lam_re, s5_lam_im, s5_log_dt, s5_b_re, s5_b_im, s5_c_re, s5_c_im, s5_d, s5_glu_w, s5_glu_b, ssd_in_w, ssd_conv_w, ssd_conv_b, ssd_dt_bias, ssd_a_log, ssd_d, ssd_norm_g, ssd_out_w, router_w, router_b, moe_w1, moe_w3, moe_w2):
    bsz, n_lat, d = x.shape
    n_ctx = ctx.shape[1]
    lt = n_lat + n_ctx
    n_exp = router_w.shape[1]
    assert n_lat % TOKEN_TILE == 0 and n_ctx % TOKEN_TILE == 0
    assert (bsz * lt) % MOE_BLOCK == 0 and n_lat % MOE_BLOCK == 0 and MOE_BLOCK % TOKEN_TILE == 0
    assert TOKEN_TILE % SSD_CHUNK == 0

    mods = _modulation(c, c_ctx, mod_w, mod_b)
    router_wt = router_w.T.astype(F32)
    w1, w3, w2 = moe_w1, moe_w3, moe_w2
    nlt = n_lat // TOKEN_TILE
    tpb = lt // TOKEN_TILE

    hn = _prenorm(x, ctx, norm1_g[0], mods[0], nlt)
    s5w = _s5_weights(s5_lam_re[0], s5_lam_im[0], s5_log_dt[0], s5_b_re[0], s5_b_im[0], s5_c_re[0], s5_c_im[0])
    y = _s5_scan(hn, n_lat, s5w)
    x1, hn2, gates_t = _glu_head(y, hn, x, ctx, mods[0], s5_d[0], s5_glu_w[0], s5_glu_b[0], norm2_g[0],
                                 router_wt, router_b, nlt)
    g2_lat = jnp.broadcast_to(mods[0][0::2, None, 5], (bsz, nlt, d))
    g2_ctx = jnp.broadcast_to(mods[0][1::2, None, 5], (bsz, tpb - nlt, d))
    g2rows = jnp.concatenate([g2_lat, g2_ctx], axis=1).reshape(bsz * tpb, 1, d)
    x2 = _moe(hn2.reshape(bsz * lt, d), gates_t, x1.reshape(bsz * lt, d), g2rows, w1, w3, w2, 0).reshape(bsz, lt, d)

    z, xbc, csdt, cst, x2p = _ssd_inproj(x2, n_lat, norm1_g[1], mods[1], ssd_in_w[0], ssd_dt_bias[0], ssd_a_log[0],
                                             ssd_conv_w[0], ssd_conv_b[0])
    yf, yb = _ssd_scan(xbc, csdt, cst, n_lat)
    x3, hn3, gates3_t = _ssd_finish(yf, yb, xbc, z, x2p, n_lat, mods[1], ssd_d[0], ssd_norm_g[0], ssd_out_w[0],
                                    norm2_g[1], router_wt, router_b)
    g2rows = jnp.broadcast_to(mods[1][0::2, None, 5], (bsz, nlt, d)).reshape(bsz * nlt, 1, d)
    out = _moe(hn3.reshape(bsz * n_lat, d), gates3_t, x3.reshape(bsz * n_lat, d), g2rows, w1, w3, w2, 1,
               final_g=final_g, blocks_per_batch=n_lat // MOE_BLOCK)
    return out.reshape(bsz, n_lat, d)
```
